```python
import jax, jax.numpy as jnp
from jax import lax
import numpy as np

D_MODEL = 1024
BATCH = 16
SEQ = 2048
DEPTH = 1

HEAD_DIM = 64
ATTN_PAIRS = ((128, 1), (512, 4), (2048, 16))
ATTN_GROUPS = 3
ATTN_HEADS_PER_GROUP = 8
ATTN_BLOCK = 64
ROPE_THETA = 10000.0
RET_HEADS = 8
RET_QK_DIM = 64
RET_V_DIM = 128
RET_CHUNK = 128
MOE_GROUPS = 8
MOE_EXPERTS_PER_GROUP = 8
MOE_N_EXPERTS = MOE_GROUPS * MOE_EXPERTS_PER_GROUP
MOE_TOP_K = 2
MOE_HIDDEN = 512
MOE_BLOCK = 128
NORM_EPS = 1e-6

ATTN_W = ATTN_GROUPS * ATTN_HEADS_PER_GROUP * HEAD_DIM
ATTN_OUT_W = ATTN_HEADS_PER_GROUP * HEAD_DIM
RET_QK_W = RET_HEADS * RET_QK_DIM
RET_V_W = RET_HEADS * RET_V_DIM
IN_SPLIT_POINTS = (ATTN_W, 2 * ATTN_W, 3 * ATTN_W,
                   3 * ATTN_W + RET_QK_W, 3 * ATTN_W + 2 * RET_QK_W,
                   3 * ATTN_W + 2 * RET_QK_W + RET_V_W,
                   3 * ATTN_W + 2 * RET_QK_W + 2 * RET_V_W)
IN_W = IN_SPLIT_POINTS[-1] + 2 * D_MODEL

kernel_name = 'gated_hybrid_dilated_attn_retention_hmoe'


def rms_norm(x, w):
    xf = x.astype(jnp.float32)
    y = xf * lax.rsqrt(jnp.mean(xf * xf, axis=-1, keepdims=True) + NORM_EPS)
    return (y * w.astype(jnp.float32)).astype(x.dtype)


def rotary_tables(seq_len):
    inv_freq = 1.0 / (ROPE_THETA ** (jnp.arange(0, HEAD_DIM, 2, dtype=jnp.float32) / HEAD_DIM))
    ang = jnp.arange(seq_len, dtype=jnp.float32)[:, None] * inv_freq[None, :]
    return jnp.cos(ang), jnp.sin(ang)


def apply_rotary(t, cos, sin):
    t1, t2 = jnp.split(t.astype(jnp.float32), 2, axis=-1)
    c = cos[None, :, None, :]
    s = sin[None, :, None, :]
    return jnp.concatenate([t1 * c - t2 * s, t1 * s + t2 * c], axis=-1)


def dilated_window_attention(q, k, v, dilation, half):
    B, S, H, Dh = q.shape
    L = S // dilation
    nb = -(-L // ATTN_BLOCK)
    Lp = nb * ATTN_BLOCK
    N = B * dilation

    def strided(t):
        t = t.reshape(B, L, dilation, H, Dh).transpose(0, 2, 3, 1, 4)
        return t.reshape(N, H, L, Dh)

    qb = jnp.pad(strided(q), ((0, 0), (0, 0), (0, Lp - L), (0, 0))).reshape(N, H, nb, ATTN_BLOCK, Dh)

    def band(t):
        tp = jnp.pad(strided(t), ((0, 0), (0, 0), (ATTN_BLOCK, Lp - L + ATTN_BLOCK), (0, 0)))
        tb = tp.reshape(N, H, nb + 2, ATTN_BLOCK, Dh)
        return jnp.concatenate([tb[:, :, :-2], tb[:, :, 1:-1], tb[:, :, 2:]], axis=3)

    kw = band(k)
    vw = band(v)
    scores = jnp.einsum('nhbqd,nhbkd->nhbqk', qb, kw) * (Dh ** -0.5)
    blk = jnp.arange(nb)[:, None] * ATTN_BLOCK
    q_pos = blk + jnp.arange(ATTN_BLOCK)[None, :]
    k_pos = blk - ATTN_BLOCK + jnp.arange(3 * ATTN_BLOCK)[None, :]
    rel = k_pos[:, None, :] - q_pos[:, :, None]
    in_range = (k_pos[:, None, :] >= 0) & (k_pos[:, None, :] < L)
    valid = (jnp.abs(rel) <= half) & (in_range | (rel == 0))
    scores = jnp.where(valid[None, None], scores, -jnp.inf)
    m = jnp.max(scores, axis=-1, keepdims=True)
    p = jnp.exp(scores - m)
    denom = jnp.sum(p, axis=-1, keepdims=True)
    out = jnp.einsum('nhbqk,nhbkd->nhbqd', p, vw) / denom
    lse = (m + jnp.log(denom))[..., 0]
    out = out.reshape(N, H, Lp, Dh)[:, :, :L].reshape(B, dilation, H, L, Dh)
    out = out.transpose(0, 3, 1, 2, 4).reshape(B, S, H, Dh)
    lse = lse.reshape(N, H, Lp)[:, :, :L].reshape(B, dilation, H, L)
    lse = lse.transpose(0, 3, 1, 2).reshape(B, S, H)
    return out, lse


def retention_chunkwise(q, k, v, log_g):
    B, H, S, Dk = q.shape
    Dv = v.shape[-1]
    C = RET_CHUNK
    nc = S // C
    qc = q.reshape(B, H, nc, C, Dk)
    kc = k.reshape(B, H, nc, C, Dk)
    vc = v.reshape(B, H, nc, C, Dv)
    idx = jnp.arange(C, dtype=jnp.float32)
    rel = idx[:, None] - idx[None, :]
    decay = jnp.where(rel[None] >= 0, jnp.exp(jnp.maximum(rel[None], 0.0) * log_g[:, None, None]), 0.0)
    scores = jnp.einsum('bhnqd,bhnkd->bhnqk', qc, kc) * decay[None, :, None]
    inner = jnp.einsum('bhnqk,bhnkv->bhnqv', scores, vc)
    zeta = jnp.exp((C - 1 - idx)[None, :] * log_g[:, None])
    kv = jnp.einsum('bhnkd,bhnkv->nbhdv', kc * zeta[None, :, None, :, None], vc)
    chunk_decay = jnp.exp(C * log_g)[None, :, None, None]

    def step(state, kv_chunk):
        return chunk_decay * state + kv_chunk, state

    _, prev_states = lax.scan(step, jnp.zeros((B, H, Dk, Dv), jnp.float32), kv)
    xi = jnp.exp((idx + 1.0)[None, :] * log_g[:, None])
    cross = jnp.einsum('bhnqd,nbhdv->bhnqv', qc * xi[None, :, None, :, None], prev_states)
    return (inner + cross).reshape(B, H, S, Dv)


def hybrid_mixer(xn, w_in, b_branch_gate, ret_decay_fwd, ret_decay_bwd, ret_gn_w,
                 w_attn_branch, w_ret_branch, w_out, cos, sin):
    B, S, _ = xn.shape
    proj = jnp.einsum('bsd,de->bse', xn, w_in)
    q_a, k_a, v_a, q_r, k_r, v_r, g_r, gate_logits = jnp.split(proj, IN_SPLIT_POINTS, axis=-1)

    n_a = ATTN_GROUPS * ATTN_HEADS_PER_GROUP
    q_a = apply_rotary(q_a.reshape(B, S, n_a, HEAD_DIM), cos, sin).reshape(B, S, ATTN_GROUPS, ATTN_HEADS_PER_GROUP, HEAD_DIM)
    k_a = apply_rotary(k_a.reshape(B, S, n_a, HEAD_DIM), cos, sin).reshape(B, S, ATTN_GROUPS, ATTN_HEADS_PER_GROUP, HEAD_DIM)
    v_a = v_a.astype(jnp.float32).reshape(B, S, ATTN_GROUPS, ATTN_HEADS_PER_GROUP, HEAD_DIM)
    outs = []
    lses = []
    for g, (window, dilation) in enumerate(ATTN_PAIRS):
        o, l = dilated_window_attention(q_a[:, :, g], k_a[:, :, g], v_a[:, :, g], dilation, window // (2 * dilation))
        outs.append(o)
        lses.append(l)
    mix_w = jax.nn.softmax(jnp.stack(lses, axis=0), axis=0)
    y_att = jnp.einsum('gbsh,gbshd->bshd', mix_w, jnp.stack(outs, axis=0)).reshape(B, S, ATTN_OUT_W)

    q_r = apply_rotary(q_r.reshape(B, S, RET_HEADS, RET_QK_DIM), cos, sin).transpose(0, 2, 1, 3)
    k_r = (apply_rotary(k_r.reshape(B, S, RET_HEADS, RET_QK_DIM), cos, sin) * (RET_QK_DIM ** -0.5)).transpose(0, 2, 1, 3)
    v_r = v_r.astype(jnp.float32).reshape(B, S, RET_HEADS, RET_V_DIM).transpose(0, 2, 1, 3)
    log_g_fwd = jax.nn.log_sigmoid(ret_decay_fwd.astype(jnp.float32))
    log_g_bwd = jax.nn.log_sigmoid(ret_decay_bwd.astype(jnp.float32))
    ret_f = retention_chunkwise(q_r, k_r, v_r, log_g_fwd)
    ret_b = jnp.flip(retention_chunkwise(jnp.flip(q_r, 2), jnp.flip(k_r, 2), jnp.flip(v_r, 2), log_g_bwd), 2)
    ret = ret_f + ret_b
    mu = jnp.mean(ret, axis=-1, keepdims=True)
    var = jnp.mean(jnp.square(ret - mu), axis=-1, keepdims=True)
    ret = (ret - mu) * lax.rsqrt(var + NORM_EPS)
    ret = ret.transpose(0, 2, 1, 3).reshape(B, S, RET_V_W) * ret_gn_w.astype(jnp.float32)
    y_ret = jax.nn.silu(g_r.astype(jnp.float32)) * ret

    gates = jax.nn.sigmoid((gate_logits + b_branch_gate).astype(jnp.float32))
    g_att, g_ret = jnp.split(gates, 2, axis=-1)
    merged = (g_att * jnp.einsum('bse,ed->bsd', y_att, w_attn_branch.astype(jnp.float32))
              + g_ret * jnp.einsum('bse,ed->bsd', y_ret, w_ret_branch.astype(jnp.float32)))
    return jnp.einsum('bsd,de->bse', merged.astype(xn.dtype), w_out)


def hierarchical_moe(x2, w_group, b_group, w_expert, b_expert, w1, w3, w2):
    T, D = x2.shape
    xf = x2.astype(jnp.float32)
    group_p = jax.nn.softmax(xf @ w_group.astype(jnp.float32) + b_group.astype(jnp.float32), axis=-1)
    g_w, g_idx = lax.top_k(group_p, 1)
    exp_logits = (xf @ w_expert.astype(jnp.float32) + b_expert.astype(jnp.float32)).reshape(T, MOE_GROUPS, MOE_EXPERTS_PER_GROUP)
    sel_logits = jnp.take_along_axis(exp_logits, g_idx[:, :, None], axis=1)[:, 0]
    e_w, e_idx = lax.top_k(jax.nn.softmax(sel_logits, axis=-1), MOE_TOP_K)
    e_w = e_w / jnp.sum(e_w, axis=-1, keepdims=True)
    gate = (g_w * e_w).reshape(-1)
    eid = (g_idx * MOE_EXPERTS_PER_GROUP + e_idx).reshape(-1).astype(jnp.int32)
    tok = jnp.repeat(jnp.arange(T, dtype=jnp.int32), MOE_TOP_K)
    A = T * MOE_TOP_K

    order = jnp.argsort(eid)
    s_eid = eid[order]
    s_tok = tok[order]
    s_gate = gate[order]
    counts = jnp.bincount(eid, length=MOE_N_EXPERTS)
    starts = jnp.cumsum(counts) - counts
    padded = (counts + MOE_BLOCK - 1) // MOE_BLOCK * MOE_BLOCK
    p_ends = jnp.cumsum(padded)
    p_starts = p_ends - padded
    dest = p_starts[s_eid] + (jnp.arange(A, dtype=jnp.int32) - starts[s_eid])
    n_blocks = -(-A // MOE_BLOCK) + MOE_N_EXPERTS
    slot_tok = jnp.full((n_blocks * MOE_BLOCK,), T, jnp.int32).at[dest].set(s_tok)
    blk_eid = jnp.clip(jnp.searchsorted(p_ends, jnp.arange(n_blocks) * MOE_BLOCK, side='right'), 0, MOE_N_EXPERTS - 1)
    x_pad = jnp.concatenate([x2, jnp.zeros((1, D), x2.dtype)], axis=0)
    x_slots = x_pad[slot_tok].reshape(n_blocks, MOE_BLOCK, D)

    def expert_block(args):
        xb, e = args
        hid = jax.nn.silu(xb @ w1[e]) * (xb @ w3[e])
        return hid @ w2[e]

    y_slots = lax.map(expert_block, (x_slots, blk_eid)).reshape(n_blocks * MOE_BLOCK, D)
    y_assign = y_slots[dest].astype(jnp.float32) * s_gate[:, None]
    return jax.ops.segment_sum(y_assign, s_tok, num_segments=T).astype(x2.dtype)


def setup_inputs(seed: int = 0) -> dict:
    key = jax.random.key(seed)
    ks = jax.random.split(key, 20)
    L = DEPTH
    f32 = jnp.float32

    def nrm(k, shape, scale):
        return jax.random.normal(k, shape, f32) * scale

    heads = jnp.arange(RET_HEADS, dtype=f32)
    decay_logit = jnp.log(2.0 ** (5.0 + heads) - 1.0)
    return {
        'x': nrm(ks[0], (BATCH, SEQ, D_MODEL), 1.0),
        'norm_mix_w': 1.0 + nrm(ks[1], (L, D_MODEL), 0.02),
        'w_in': nrm(ks[2], (L, D_MODEL, IN_W), D_MODEL ** -0.5),
        'b_branch_gate': nrm(ks[3], (L, 2 * D_MODEL), 0.02),
        'ret_decay_fwd': decay_logit[None] + nrm(ks[4], (L, RET_HEADS), 0.05),
        'ret_decay_bwd': decay_logit[None] + nrm(ks[5], (L, RET_HEADS), 0.05),
        'ret_gn_w': 1.0 + nrm(ks[6], (L, RET_V_W), 0.02),
        'w_attn_branch': nrm(ks[7], (L, ATTN_OUT_W, D_MODEL), ATTN_OUT_W ** -0.5),
        'w_ret_branch': nrm(ks[8], (L, RET_V_W, D_MODEL), RET_V_W ** -0.5),
        'w_out': nrm(ks[9], (L, D_MODEL, D_MODEL), D_MODEL ** -0.5),
        'norm_moe_w': 1.0 + nrm(ks[10], (L, D_MODEL), 0.02),
        'moe_w_group': nrm(ks[11], (L, D_MODEL, MOE_GROUPS), D_MODEL ** -0.5),
        'moe_b_group': nrm(ks[12], (L, MOE_GROUPS), 0.01),
        'moe_w_expert': nrm(ks[13], (L, D_MODEL, MOE_N_EXPERTS), D_MODEL ** -0.5),
        'moe_b_expert': nrm(ks[14], (L, MOE_N_EXPERTS), 0.01),
        'moe_w1': nrm(ks[15], (L, MOE_N_EXPERTS, D_MODEL, MOE_HIDDEN), D_MODEL ** -0.5),
        'moe_w3': nrm(ks[16], (L, MOE_N_EXPERTS, D_MODEL, MOE_HIDDEN), D_MODEL ** -0.5),
        'moe_w2': nrm(ks[17], (L, MOE_N_EXPERTS, MOE_HIDDEN, D_MODEL), MOE_HIDDEN ** -0.5),
        'norm_final_w': 1.0 + nrm(ks[18], (D_MODEL,), 0.02),
    }


def reference(x, norm_mix_w, w_in, b_branch_gate, ret_decay_fwd, ret_decay_bwd, ret_gn_w,
              w_attn_branch, w_ret_branch, w_out, norm_moe_w, moe_w_group, moe_b_group,
              moe_w_expert, moe_b_expert, moe_w1, moe_w3, moe_w2, norm_final_w):
    B, S, D = x.shape
    cos, sin = rotary_tables(S)
    h = x
    for l in range(DEPTH):
        xn = rms_norm(h, norm_mix_w[l])
        mix = hybrid_mixer(xn, w_in[l], b_branch_gate[l], ret_decay_fwd[l], ret_decay_bwd[l], ret_gn_w[l],
                           w_attn_branch[l], w_ret_branch[l], w_out[l], cos, sin)
        h = h + mix.astype(h.dtype)
        hn = rms_norm(h, norm_moe_w[l])
        ffn = hierarchical_moe(hn.reshape(B * S, D), moe_w_group[l], moe_b_group[l], moe_w_expert[l],
                               moe_b_expert[l], moe_w1[l], moe_w3[l], moe_w2[l])
        h = h + ffn.reshape(B, S, D).astype(h.dtype)
    return rms_norm(h, norm_final_w)
```

```python
import functools

import jax
import jax.numpy as jnp
from jax import lax
from jax.experimental import pallas as pl
from jax.experimental.pallas import tpu as pltpu

F32 = jnp.float32
BF16 = jnp.bfloat16

D_MODEL = 1024
HEAD_DIM = 64
ATTN_PAIRS = ((128, 1), (512, 4), (2048, 16))
ATTN_HEADS_PER_GROUP = 8
ATTN_GROUP_W = ATTN_HEADS_PER_GROUP * HEAD_DIM
ATTN_HALF = 64
ROPE_THETA = 10000.0
RET_HEADS = 8
RET_QK_DIM = 64
RET_V_DIM = 128
RET_CHUNK = 128
RET_QK_W = RET_HEADS * RET_QK_DIM
RET_V_W = RET_HEADS * RET_V_DIM
MOE_GROUPS = 8
MOE_EXPERTS_PER_GROUP = 8
MOE_N_EXPERTS = MOE_GROUPS * MOE_EXPERTS_PER_GROUP
MOE_HIDDEN = 512
NORM_EPS = 1e-6

LANES = 128
NEG_BIG = -1e30

TM_INPROJ = 512
TM_MERGE = 256
TM_COMBINE = 256
MOE_BM = 256
DISPATCH_CHUNK = 512
ATTN_QB = 128

VMEM_LIMIT = 56 * 1024 * 1024

_A = 3 * ATTN_GROUP_W
OFF_QA, OFF_KA, OFF_VA = 0, _A, 2 * _A
OFF_QR = 3 * _A
OFF_KR = OFF_QR + RET_QK_W
OFF_VR = OFF_KR + RET_QK_W
OFF_GR = OFF_VR + RET_V_W
OFF_GL = OFF_GR + RET_V_W
IN_W = OFF_GL + 2 * D_MODEL


def _cparams(sem, vmem=VMEM_LIMIT):
    return pltpu.CompilerParams(dimension_semantics=sem, vmem_limit_bytes=vmem)


def _inproj_body(x_ref, nw_ref, w_ref, bg_ref, cos_ref, sin_ref,
                 qa0, ka0, va0, qa1, ka1, va1, qa2, ka2, va2, qr, kr, vr, gr, gt,
                 stage_ref):
    tm = x_ref.shape[0]
    x = x_ref[...]
    ms = jnp.mean(x * x, axis=-1, keepdims=True)
    xn = (x * lax.rsqrt(ms + NORM_EPS) * nw_ref[...]).astype(BF16)
    cos = cos_ref[...]
    sin = sin_ref[...]
    lane = lax.broadcasted_iota(jnp.int32, (tm, LANES), 1)
    first_half = (lane & (HEAD_DIM - 1)) < (HEAD_DIM // 2)

    def proj(c0, width):
        return jnp.dot(xn, w_ref[:, c0:c0 + width], preferred_element_type=F32)

    def rotary(a, scale):
        partner = jnp.where(first_half, pltpu.roll(a, LANES - HEAD_DIM // 2, 1),
                            pltpu.roll(a, HEAD_DIM // 2, 1))
        r = a * cos + partner * sin
        return r * scale if scale != 1.0 else r

    def chunks(acc):
        return [acc[:, c * LANES:(c + 1) * LANES] for c in range(acc.shape[1] // LANES)]

    def store_natural(out_ref, acc, fn):
        for c, a in enumerate(chunks(acc)):
            out_ref[:, c * LANES:(c + 1) * LANES] = fn(a).astype(out_ref.dtype)

    def store_strided(out_ref, acc, fn, d):
        for c, a in enumerate(chunks(acc)):
            stage_ref[c] = fn(a)
        for r in range(d):
            for c in range(acc.shape[1] // LANES):
                out_ref[0, r, :, c * LANES:(c + 1) * LANES] = (
                    stage_ref[c, pl.ds(r, tm // d, stride=d), :].astype(out_ref.dtype))

    ident = lambda a: a
    rot_q = lambda a: rotary(a, HEAD_DIM ** -0.5)
    rot_1 = lambda a: rotary(a, 1.0)
    rot_k = lambda a: rotary(a, RET_QK_DIM ** -0.5)

    W = ATTN_GROUP_W
    store_natural(qa0, proj(OFF_QA, W), rot_q)
    store_natural(ka0, proj(OFF_KA, W), rot_1)
    store_natural(va0, proj(OFF_VA, W), ident)
    for g, (qo, ko, vo) in ((1, (qa1, ka1, va1)), (2, (qa2, ka2, va2))):
        d = ATTN_PAIRS[g][1]
        store_strided(qo, proj(OFF_QA + g * W, W), rot_q, d)
        store_strided(ko, proj(OFF_KA + g * W, W), rot_1, d)
        store_strided(vo, proj(OFF_VA + g * W, W), ident, d)
    store_natural(qr, proj(OFF_QR, RET_QK_W), rot_1)
    store_natural(kr, proj(OFF_KR, RET_QK_W), rot_k)
    for h in range(RET_V_W // W):
        vr[:, h * W:(h + 1) * W] = proj(OFF_VR + h * W, W).astype(vr.dtype)
        gr[:, h * W:(h + 1) * W] = proj(OFF_GR + h * W, W).astype(gr.dtype)
    for h in range(2 * D_MODEL // W):
        z = proj(OFF_GL + h * W, W) + bg_ref[:, h * W:(h + 1) * W]
        gt[:, h * W:(h + 1) * W] = jax.nn.sigmoid(z).astype(gt.dtype)


def _in_projection(x2, norm_w, w_bf, b_gate, cos_t, sin_t, B, S):
    T = B * S
    tm = TM_INPROJ
    nt = S // tm
    W = ATTN_GROUP_W
    row = lambda i: (i, 0)
    const = lambda i: (0, 0)
    nat = lambda width: pl.BlockSpec((tm, width), row)

    def strided_spec(d):
        return pl.BlockSpec((1, d, tm // d, W), lambda i: (i // nt, 0, i % nt, 0))

    def strided_shape(d):
        return jax.ShapeDtypeStruct((B, d, S // d, W), BF16)

    nat_shape = lambda width: jax.ShapeDtypeStruct((T, width), BF16)
    d1, d2 = ATTN_PAIRS[1][1], ATTN_PAIRS[2][1]
    out_shape = ([nat_shape(W)] * 3 + [strided_shape(d1)] * 3 + [strided_shape(d2)] * 3
                 + [nat_shape(RET_QK_W)] * 2 + [nat_shape(RET_V_W)] * 2 + [nat_shape(2 * D_MODEL)])
    out_specs = ([nat(W)] * 3 + [strided_spec(d1)] * 3 + [strided_spec(d2)] * 3
                 + [nat(RET_QK_W)] * 2 + [nat(RET_V_W)] * 2 + [nat(2 * D_MODEL)])
    in_specs = [
        pl.BlockSpec((tm, D_MODEL), row),
        pl.BlockSpec((1, D_MODEL), const),
        pl.BlockSpec((D_MODEL, IN_W), const, pipeline_mode=pl.Buffered(1)),
        pl.BlockSpec((1, 2 * D_MODEL), const),
        pl.BlockSpec((tm, LANES), lambda i: (i % nt, 0)),
        pl.BlockSpec((tm, LANES), lambda i: (i % nt, 0)),
    ]
    return pl.pallas_call(
        _inproj_body,
        grid=(T // tm,),
        in_specs=in_specs,
        out_specs=out_specs,
        out_shape=out_shape,
        scratch_shapes=[pltpu.VMEM((W // LANES, tm, LANES), F32)],
        compiler_params=_cparams(("parallel",)),
        name="in_projection",
    )(x2, norm_w, w_bf, b_gate, cos_t, sin_t)


def _attn_body(q_ref, k_ref, v_ref, o_ref, lse_ref, *, L):
    QB = ATTN_QB
    KW = min(L, QB + 2 * ATTN_HALF)
    lane = lax.broadcasted_iota(jnp.int32, (QB, LANES), 1)
    head0 = lane < HEAD_DIM
    q_idx = lax.broadcasted_iota(jnp.int32, (QB, KW), 0)
    k_idx = lax.broadcasted_iota(jnp.int32, (QB, KW), 1)

    def block(j, carry):
        q0 = pl.multiple_of(j * QB, QB)
        ws = pl.multiple_of(jnp.clip(q0 - ATTN_HALF, 0, L - KW), ATTN_HALF)
        valid = jnp.abs((k_idx + ws) - (q_idx + q0)) <= ATTN_HALF
        for hp in range(ATTN_GROUP_W // LANES):
            cols = slice(hp * LANES, (hp + 1) * LANES)
            qp = q_ref[0, 0, pl.ds(q0, QB), cols]
            kw = k_ref[0, 0, pl.ds(ws, KW), cols]
            vw = v_ref[0, 0, pl.ds(ws, KW), cols]
            outs, lses = [], []
            for h in range(2):
                qm = jnp.where(head0 if h == 0 else jnp.logical_not(head0), qp, jnp.zeros_like(qp))
                s = lax.dot_general(qm, kw, (((1,), (1,)), ((), ())), preferred_element_type=F32)
                s = jnp.where(valid, s, NEG_BIG)
                m = jnp.max(s, axis=-1, keepdims=True)
                p = jnp.exp(s - m)
                l = jnp.sum(p, axis=-1, keepdims=True)
                o = jnp.dot(p.astype(BF16), vw, preferred_element_type=F32) / l
                outs.append(o)
                lses.append(m + jnp.log(l))
            o_ref[0, pl.ds(q0, QB), cols] = jnp.where(head0, outs[0], outs[1]).astype(o_ref.dtype)
            lse_ref[0, pl.ds(q0, QB), cols] = jnp.where(head0, lses[0], lses[1])
        return carry

    lax.fori_loop(0, L // QB, block, 0)


def _attention_group(q, k, v, B, S, d):
    L = S // d
    W = ATTN_GROUP_W
    in_spec = pl.BlockSpec((1, 1, L, W), lambda b, r: (b, r, 0, 0))
    out_spec = pl.BlockSpec((1, L, W), lambda b, r: (b, 0, r))
    o, lse = pl.pallas_call(
        functools.partial(_attn_body, L=L),
        grid=(B, d),
        in_specs=[in_spec] * 3,
        out_specs=[out_spec] * 2,
        out_shape=[jax.ShapeDtypeStruct((B, L, d * W), BF16),
                   jax.ShapeDtypeStruct((B, L, d * W), F32)],
        compiler_params=_cparams(("parallel", "parallel")),
        name=f"attention_d{d}",
    )(q, k, v)
    return o.reshape(B * S, W), lse.reshape(B * S, W)


def _log_sigmoid(z):
    return jnp.minimum(z, 0.0) - jnp.log(1.0 + jnp.exp(-jnp.abs(z)))


def _ret_body(dec_ref, q_ref, k_ref, v_ref, g_ref, gnw_ref, o_ref, sb_ref, sf_ref, *, S):
    C = RET_CHUNK
    nc = S // C
    lg = _log_sigmoid(dec_ref[0])
    a_row = lax.broadcasted_iota(jnp.int32, (C, LANES), 0).astype(F32)
    lane = lax.broadcasted_iota(jnp.int32, (C, LANES), 1)
    rel = (lax.broadcasted_iota(jnp.int32, (C, C), 0) - lax.broadcasted_iota(jnp.int32, (C, C), 1)).astype(F32)

    heads = []
    for h in range(2):
        lgf = lg[h:h + 1, :]
        lgb = lg[2 + h:3 + h, :]
        in_head = (lane < RET_QK_DIM) if h == 0 else (lane >= RET_QK_DIM)
        heads.append(dict(
            in_head=in_head,
            xi_f=jnp.where(in_head, jnp.exp((a_row + 1.0) * lgf), 0.0),
            xi_b=jnp.where(in_head, jnp.exp((C - a_row) * lgb), 0.0),
            zeta_f=jnp.exp((C - 1.0 - a_row) * lgf),
            zeta_b=jnp.exp(a_row * lgb),
            dloc=jnp.where(rel > 0, jnp.exp(rel * lgf), jnp.where(rel < 0, jnp.exp(-rel * lgb), 2.0)),
            cd_f=jnp.exp(C * lgf),
            cd_b=jnp.exp(C * lgb),
        ))

    def v_head(rows, h):
        return v_ref[0, rows, h * RET_V_DIM:(h + 1) * RET_V_DIM].astype(F32)

    def k_t(rows):
        return k_ref[0, rows, :].astype(F32).T.astype(BF16)

    sb_ref[nc - 1] = jnp.zeros(sb_ref.shape[1:], F32)

    def bwd(i, carry):
        n = nc - 1 - i
        rows = pl.ds(pl.multiple_of(n * C, C), C)
        kt = k_t(rows)
        for h, hd in enumerate(heads):
            kv = jnp.dot(kt, (v_head(rows, h) * hd["zeta_b"]).astype(BF16), preferred_element_type=F32)
            sb_ref[n - 1, h] = hd["cd_b"] * sb_ref[n, h] + kv
        return carry

    lax.fori_loop(0, nc - 1, bwd, 0)

    sf_ref[...] = jnp.zeros(sf_ref.shape, F32)

    def fwd(n, carry):
        rows = pl.ds(pl.multiple_of(n * C, C), C)
        qp = q_ref[0, rows, :]
        qf = qp.astype(F32)
        kt = k_t(rows)
        for h, hd in enumerate(heads):
            vh = v_head(rows, h)
            qm = jnp.where(hd["in_head"], qp, jnp.zeros_like(qp))
            s = jnp.dot(qm, kt, preferred_element_type=F32) * hd["dloc"]
            inner = jnp.dot(s.astype(BF16), vh.astype(BF16), preferred_element_type=F32)
            qx = jnp.concatenate([(qf * hd["xi_f"]).astype(BF16), (qf * hd["xi_b"]).astype(BF16)], axis=1)
            st = jnp.concatenate([sf_ref[h].astype(BF16), sb_ref[n, h].astype(BF16)], axis=0)
            ret = inner + jnp.dot(qx, st, preferred_element_type=F32)
            sf_ref[h] = hd["cd_f"] * sf_ref[h] + jnp.dot(
                kt, (vh * hd["zeta_f"]).astype(BF16), preferred_element_type=F32)
            mu = jnp.mean(ret, axis=-1, keepdims=True)
            xc = ret - mu
            var = jnp.mean(xc * xc, axis=-1, keepdims=True)
            cols = slice(h * RET_V_DIM, (h + 1) * RET_V_DIM)
            gate = g_ref[0, rows, cols].astype(F32)
            y = xc * lax.rsqrt(var + NORM_EPS) * gnw_ref[:, cols] * (gate * jax.nn.sigmoid(gate))
            o_ref[0, rows, cols] = y.astype(o_ref.dtype)
        return carry

    lax.fori_loop(0, nc, fwd, 0)


def _retention(dec, qr, kr, vr, gr, gn_w, B, S):
    nc = S // RET_CHUNK
    npairs = RET_HEADS // 2
    qk_spec = pl.BlockSpec((1, S, 2 * RET_QK_DIM), lambda b, p: (b, 0, p))
    v_spec = pl.BlockSpec((1, S, 2 * RET_V_DIM), lambda b, p: (b, 0, p))
    return pl.pallas_call(
        functools.partial(_ret_body, S=S),
        grid=(B, npairs),
        in_specs=[pl.BlockSpec((1, 4, LANES), lambda b, p: (p, 0, 0)),
                  qk_spec, qk_spec, v_spec, v_spec,
                  pl.BlockSpec((1, 2 * RET_V_DIM), lambda b, p: (0, p))],
        out_specs=v_spec,
        out_shape=jax.ShapeDtypeStruct((B, S, RET_V_W), BF16),
        scratch_shapes=[pltpu.VMEM((nc, 2, 2 * RET_QK_DIM, RET_V_DIM), F32),
                        pltpu.VMEM((2, 2 * RET_QK_DIM, RET_V_DIM), F32)],
        compiler_params=_cparams(("parallel", "parallel")),
        name="retention",
    )(dec, qr.reshape(B, S, RET_QK_W), kr.reshape(B, S, RET_QK_W),
      vr.reshape(B, S, RET_V_W), gr.reshape(B, S, RET_V_W), gn_w)


ROUTE_EID, ROUTE_RANK, ROUTE_GATE = 0, 2, 4
ROUTER_EXPERT_LANE0 = MOE_GROUPS


def _merge_body(o0, o1, o2, l0, l1, l2, yret_ref, gt_ref, x_ref, wa_ref, wb_ref, wo_ref,
                nw_ref, wrh_ref, wrl_ref, br_ref,
                h_ref, hn_ref, route_ref, cnt_ref):
    tm = x_ref.shape[0]
    i = pl.program_id(0)

    ls = [l0[...], l1[...], l2[...]]
    mx = jnp.maximum(jnp.maximum(ls[0], ls[1]), ls[2])
    es = [jnp.exp(l - mx) for l in ls]
    num = es[0] * o0[...].astype(F32) + es[1] * o1[...].astype(F32) + es[2] * o2[...].astype(F32)
    y_att = (num / (es[0] + es[1] + es[2])).astype(BF16)

    a = jnp.dot(y_att, wa_ref[...], preferred_element_type=F32)
    b = jnp.dot(yret_ref[...], wb_ref[...], preferred_element_type=F32)
    merged = gt_ref[:, :D_MODEL].astype(F32) * a + gt_ref[:, D_MODEL:].astype(F32) * b
    mix = jnp.dot(merged.astype(BF16), wo_ref[...], preferred_element_type=F32)
    h = x_ref[...] + mix
    h_ref[...] = h
    ms = jnp.mean(h * h, axis=-1, keepdims=True)
    hn = h * lax.rsqrt(ms + NORM_EPS) * nw_ref[...]
    hn_ref[...] = hn

    hi = hn.astype(BF16)
    lo = (hn - hi.astype(F32)).astype(BF16)
    logits = (jnp.dot(hi, wrh_ref[...], preferred_element_type=F32)
              + jnp.dot(hi, wrl_ref[...], preferred_element_type=F32)
              + jnp.dot(lo, wrh_ref[...], preferred_element_type=F32)) + br_ref[...]
    lane = lax.broadcasted_iota(jnp.int32, (tm, LANES), 1)
    far = jnp.int32(4 * LANES)

    def first_argmax(vals, vmax):
        return jnp.min(jnp.where(vals == vmax, lane, far), axis=-1, keepdims=True)

    is_group = lane < MOE_GROUPS
    gl = jnp.where(is_group, logits, NEG_BIG)
    gmax = jnp.max(gl, axis=-1, keepdims=True)
    g_w = 1.0 / jnp.sum(jnp.where(is_group, jnp.exp(gl - gmax), 0.0), axis=-1, keepdims=True)
    g_idx = first_argmax(gl, gmax)
    e_lane = lane - ROUTER_EXPERT_LANE0
    in_group = (e_lane >= 0) & (e_lane < MOE_N_EXPERTS) & (jnp.right_shift(e_lane, 3) == g_idx)
    el = jnp.where(in_group, logits, NEG_BIG)
    m1 = jnp.max(el, axis=-1, keepdims=True)
    i1 = first_argmax(el, m1)
    el2 = jnp.where(lane == i1, NEG_BIG, el)
    m2 = jnp.max(el2, axis=-1, keepdims=True)
    i2 = first_argmax(el2, m2)
    ex = jnp.exp(m2 - m1)
    gate1 = g_w / (1.0 + ex)
    gate2 = g_w * ex / (1.0 + ex)

    @pl.when(i == 0)
    def _():
        cnt_ref[...] = jnp.zeros(cnt_ref.shape, F32)

    hot1 = lane == i1
    hot2 = lane == i2
    onehot = jnp.where(hot1 | hot2, 1.0, 0.0)
    r_idx = lax.broadcasted_iota(jnp.int32, (tm, tm), 0)
    c_idx = lax.broadcasted_iota(jnp.int32, (tm, tm), 1)
    lower = jnp.where(c_idx < r_idx, 1.0, 0.0).astype(BF16)
    before = jnp.dot(lower, onehot.astype(BF16), preferred_element_type=F32) + cnt_ref[...]
    rank1 = jnp.sum(jnp.where(hot1, before, 0.0), axis=-1, keepdims=True)
    rank2 = jnp.sum(jnp.where(hot2, before, 0.0), axis=-1, keepdims=True)
    cnt_ref[...] = cnt_ref[...] + jnp.sum(onehot, axis=0, keepdims=True)

    rec = jnp.zeros((tm, LANES), F32)
    for pos, val in ((ROUTE_EID, (i1 - ROUTER_EXPERT_LANE0).astype(F32)),
                     (ROUTE_EID + 1, (i2 - ROUTER_EXPERT_LANE0).astype(F32)),
                     (ROUTE_RANK, rank1), (ROUTE_RANK + 1, rank2),
                     (ROUTE_GATE, gate1), (ROUTE_GATE + 1, gate2)):
        rec = jnp.where(lane == pos, val, rec)
    route_ref[...] = rec


def _merge_route(os_, ls_, y_ret, gates, x2, wa, wb, wo, nw, wr_hi, wr_lo, b_r, T):
    tm = TM_MERGE
    row = lambda i: (i, 0)
    const = lambda i: (0, 0)
    W = ATTN_GROUP_W
    full = lambda arr: pl.BlockSpec(arr.shape, const)
    in_specs = ([pl.BlockSpec((tm, W), row)] * 6
                + [pl.BlockSpec((tm, RET_V_W), row), pl.BlockSpec((tm, 2 * D_MODEL), row),
                   pl.BlockSpec((tm, D_MODEL), row),
                   full(wa), full(wb), full(wo), full(nw), full(wr_hi), full(wr_lo), full(b_r)])
    return pl.pallas_call(
        _merge_body,
        grid=(T // tm,),
        in_specs=in_specs,
        out_specs=[pl.BlockSpec((tm, D_MODEL), row), pl.BlockSpec((tm, D_MODEL), row),
                   pl.BlockSpec((tm, LANES), row), pl.BlockSpec((1, LANES), const)],
        out_shape=[jax.ShapeDtypeStruct((T, D_MODEL), F32), jax.ShapeDtypeStruct((T, D_MODEL), F32),
                   jax.ShapeDtypeStruct((T, LANES), F32), jax.ShapeDtypeStruct((1, LANES), F32)],
        compiler_params=_cparams(("arbitrary",)),
        name="merge_route",
    )(*os_, *ls_, y_ret, gates, x2, wa, wb, wo, nw, wr_hi, wr_lo, b_r)


CODE_SHIFT = 16


def _slot_of(code, pstart_ref):
    return pstart_ref[code >> CODE_SHIFT] + (code & ((1 << CODE_SHIFT) - 1))


def _dispatch_body(code_ref, pstart_ref, hn_ref, xs_ref, sem, *, T):
    i = pl.program_id(0)
    n = pl.num_programs(0)
    ch = DISPATCH_CHUNK

    def row_copy(t, slot):
        return pltpu.make_async_copy(hn_ref.at[pl.ds(t, 1)], xs_ref.at[pl.ds(slot, 1)], sem)

    def issue(j, carry):
        t = i * ch + j
        row_copy(t, _slot_of(code_ref[t], pstart_ref)).start()
        row_copy(t, _slot_of(code_ref[T + t], pstart_ref)).start()
        return carry

    lax.fori_loop(0, ch, issue, 0)

    def drain():
        pltpu.make_async_copy(hn_ref.at[pl.ds(0, 2 * ch)], xs_ref.at[pl.ds(0, 2 * ch)], sem).wait()

    @pl.when(i > 0)
    def _():
        drain()

    @pl.when(i == n - 1)
    def _():
        drain()


def _dispatch(code, pstart, hn, n_slots, T):
    grid_spec = pltpu.PrefetchScalarGridSpec(
        num_scalar_prefetch=2,
        grid=(T // DISPATCH_CHUNK,),
        in_specs=[pl.BlockSpec(memory_space=pl.ANY)],
        out_specs=pl.BlockSpec(memory_space=pl.ANY),
        scratch_shapes=[pltpu.SemaphoreType.DMA(())],
    )
    return pl.pallas_call(
        functools.partial(_dispatch_body, T=T),
        grid_spec=grid_spec,
        out_shape=jax.ShapeDtypeStruct((n_slots, D_MODEL), F32),
        compiler_params=_cparams(("arbitrary",)),
        name="moe_dispatch",
    )(code, pstart, hn)


def _expert_body(blk_ref, eid_ref, valid_ref, fresh_ref, x_ref, w1_ref, w3_ref, w2_ref, y_ref,
                 w1b, w3b, w2b):
    i = pl.program_id(0)
    valid = valid_ref[i]

    @pl.when(valid > 0)
    def _():
        @pl.when(fresh_ref[i] == 1)
        def _():
            w1b[...] = w1_ref[0].astype(BF16)
            w3b[...] = w3_ref[0].astype(BF16)
            w2b[...] = w2_ref[0].astype(BF16)

        rows = lax.broadcasted_iota(jnp.int32, x_ref.shape, 0)
        x = jnp.where(rows < valid, x_ref[...], 0.0).astype(BF16)
        a = jnp.dot(x, w1b[...], preferred_element_type=F32)
        b = jnp.dot(x, w3b[...], preferred_element_type=F32)
        hid = (a * jax.nn.sigmoid(a) * b).astype(BF16)
        y_ref[...] = jnp.dot(hid, w2b[...], preferred_element_type=F32)


def _experts(blk, blk_eid, blk_valid, blk_fresh, x_slots, w1, w3, w2, n_blocks):
    bm = MOE_BM
    grid_spec = pltpu.PrefetchScalarGridSpec(
        num_scalar_prefetch=4,
        grid=(n_blocks,),
        in_specs=[
            pl.BlockSpec((bm, D_MODEL), lambda i, blk, eid, val, fr: (blk[i], 0)),
            pl.BlockSpec((1, D_MODEL, MOE_HIDDEN), lambda i, blk, eid, val, fr: (eid[i], 0, 0)),
            pl.BlockSpec((1, D_MODEL, MOE_HIDDEN), lambda i, blk, eid, val, fr: (eid[i], 0, 0)),
            pl.BlockSpec((1, MOE_HIDDEN, D_MODEL), lambda i, blk, eid, val, fr: (eid[i], 0, 0)),
        ],
        out_specs=pl.BlockSpec((bm, D_MODEL), lambda i, blk, eid, val, fr: (blk[i], 0)),
        scratch_shapes=[pltpu.VMEM((D_MODEL, MOE_HIDDEN), BF16), pltpu.VMEM((D_MODEL, MOE_HIDDEN), BF16),
                        pltpu.VMEM((MOE_HIDDEN, D_MODEL), BF16)],
    )
    return pl.pallas_call(
        _expert_body,
        grid_spec=grid_spec,
        out_shape=jax.ShapeDtypeStruct(x_slots.shape, F32),
        compiler_params=_cparams(("arbitrary",)),
        name="moe_experts",
    )(blk, blk_eid, blk_valid, blk_fresh, x_slots, w1, w3, w2)


def _combine_body(code_ref, pstart_ref, ys_ref, h_ref, route_ref, nw_ref, o_ref, ybuf, sem, *, T):
    i = pl.program_id(0)
    n = pl.num_programs(0)
    tm = h_ref.shape[0]

    def row_copy(slot, buf, k, j):
        return pltpu.make_async_copy(ys_ref.at[pl.ds(slot, 1)], ybuf.at[buf, k, pl.ds(j, 1)], sem.at[buf])

    def issue(tile, buf):
        def one(j, carry):
            t = tile * tm + j
            row_copy(_slot_of(code_ref[t], pstart_ref), buf, 0, j).start()
            row_copy(_slot_of(code_ref[T + t], pstart_ref), buf, 1, j).start()
            return carry
        lax.fori_loop(0, tm, one, 0)

    @pl.when(i == 0)
    def _():
        issue(0, 0)

    @pl.when(i + 1 < n)
    def _():
        issue(i + 1, (i + 1) % 2)

    buf = i % 2
    for k in range(2):
        pltpu.make_async_copy(ys_ref.at[pl.ds(0, tm)], ybuf.at[buf, k], sem.at[buf]).wait()
    route = route_ref[...]
    g1 = route[:, ROUTE_GATE:ROUTE_GATE + 1]
    g2 = route[:, ROUTE_GATE + 1:ROUTE_GATE + 2]
    h = h_ref[...] + (ybuf[buf, 0] * g1 + ybuf[buf, 1] * g2)
    ms = jnp.mean(h * h, axis=-1, keepdims=True)
    o_ref[...] = h * lax.rsqrt(ms + NORM_EPS) * nw_ref[...]


def _combine(code, pstart, y_slots, h, route, nw, T):
    tm = TM_COMBINE
    row = lambda i, c, p: (i, 0)
    grid_spec = pltpu.PrefetchScalarGridSpec(
        num_scalar_prefetch=2,
        grid=(T // tm,),
        in_specs=[pl.BlockSpec(memory_space=pl.ANY),
                  pl.BlockSpec((tm, D_MODEL), row),
                  pl.BlockSpec((tm, LANES), row),
                  pl.BlockSpec((1, D_MODEL), lambda i, c, p: (0, 0))],
        out_specs=pl.BlockSpec((tm, D_MODEL), row),
        scratch_shapes=[pltpu.VMEM((2, 2, tm, D_MODEL), F32), pltpu.SemaphoreType.DMA((2,))],
    )
    return pl.pallas_call(
        functools.partial(_combine_body, T=T),
        grid_spec=grid_spec,
        out_shape=jax.ShapeDtypeStruct((T, D_MODEL), F32),
        compiler_params=_cparams(("arbitrary",)),
        name="moe_combine",
    )(code, pstart, y_slots, h, route, nw)


def _rotary_tables(S):
    inv_freq = 1.0 / (ROPE_THETA ** (jnp.arange(0, HEAD_DIM, 2, dtype=F32) / HEAD_DIM))
    ang = jnp.arange(S, dtype=F32)[:, None] * inv_freq[None, :]
    cos, sin = jnp.cos(ang), jnp.sin(ang)
    reps = LANES // HEAD_DIM
    cos_t = jnp.tile(jnp.concatenate([cos, cos], axis=1), (1, reps))
    sin_t = jnp.tile(jnp.concatenate([-sin, sin], axis=1), (1, reps))
    return cos_t, sin_t


def _layer(h_in, norm_mix_w, w_in, b_branch_gate, ret_decay_fwd, ret_decay_bwd, ret_gn_w, w_attn_branch,
           w_ret_branch, w_out, norm_moe_w, moe_w_group, moe_b_group, moe_w_expert, moe_b_expert,
           moe_w1, moe_w3, moe_w2, next_norm_w, B, S, cos_t, sin_t):
    T = B * S
    (qa0, ka0, va0, qa1, ka1, va1, qa2, ka2, va2, qr, kr, vr, gr, gates) = _in_projection(
        h_in, norm_mix_w[None, :], w_in.astype(BF16), b_branch_gate[None, :], cos_t, sin_t, B, S)

    os_, ls_ = [], []
    for (q, k, v), (_, d) in zip(((qa0.reshape(B, 1, S, ATTN_GROUP_W), ka0.reshape(B, 1, S, ATTN_GROUP_W),
                                   va0.reshape(B, 1, S, ATTN_GROUP_W)), (qa1, ka1, va1), (qa2, ka2, va2)),
                                 ATTN_PAIRS):
        o, lse = _attention_group(q, k, v, B, S, d)
        os_.append(o)
        ls_.append(lse)

    dec = jnp.stack([ret_decay_fwd.reshape(RET_HEADS // 2, 2), ret_decay_bwd.reshape(RET_HEADS // 2, 2)], axis=1)
    dec = jnp.broadcast_to(dec.reshape(RET_HEADS // 2, 4, 1), (RET_HEADS // 2, 4, LANES)).astype(F32)
    y_ret = _retention(dec, qr, kr, vr, gr, ret_gn_w[None, :], B, S).reshape(T, RET_V_W)

    pad = LANES - MOE_GROUPS - MOE_N_EXPERTS
    w_r = jnp.concatenate([moe_w_group, moe_w_expert, jnp.zeros((D_MODEL, pad), F32)], axis=1)
    w_r_hi = w_r.astype(BF16)
    w_r_lo = (w_r - w_r_hi.astype(F32)).astype(BF16)
    b_r = jnp.concatenate([moe_b_group, moe_b_expert, jnp.zeros((pad,), F32)])[None, :]

    h_mid, hn, route, cnt = _merge_route(
        os_, ls_, y_ret, gates, h_in, w_attn_branch.astype(BF16), w_ret_branch.astype(BF16),
        w_out.astype(BF16), norm_moe_w[None, :], w_r_hi, w_r_lo, b_r, T)

    bm = MOE_BM
    counts = cnt[0, ROUTER_EXPERT_LANE0:ROUTER_EXPERT_LANE0 + MOE_N_EXPERTS].astype(jnp.int32)
    nblk = (counts + bm - 1) // bm
    blk_end = jnp.cumsum(nblk)
    pstart = (blk_end - nblk) * bm
    n_blocks = (2 * T) // bm + MOE_N_EXPERTS
    n_active = blk_end[-1]
    bidx = jnp.minimum(jnp.arange(n_blocks, dtype=jnp.int32), n_active - 1)
    blk_eid = jnp.sum(bidx[:, None] >= blk_end[None, :], axis=1).astype(jnp.int32)
    blk_valid = jnp.clip(counts[blk_eid] - (bidx * bm - pstart[blk_eid]), 0, bm)
    blk_valid = jnp.where(jnp.arange(n_blocks) < n_active, blk_valid, 0).astype(jnp.int32)
    blk_fresh = jnp.concatenate([jnp.ones((1,), jnp.int32), (blk_eid[1:] != blk_eid[:-1]).astype(jnp.int32)])
    eid = route[:, ROUTE_EID:ROUTE_EID + 2].astype(jnp.int32)
    rank = route[:, ROUTE_RANK:ROUTE_RANK + 2].astype(jnp.int32)
    code = ((eid << CODE_SHIFT) | rank).T.reshape(2 * T)
    pstart = pstart.astype(jnp.int32)

    x_slots = _dispatch(code, pstart, hn, n_blocks * bm, T)
    y_slots = _experts(bidx, blk_eid, blk_valid, blk_fresh, x_slots, moe_w1, moe_w3, moe_w2, n_blocks)
    return _combine(code, pstart, y_slots, h_mid, route, next_norm_w[None, :], T)


def kernel(x, norm_mix_w, w_in, b_branch_gate, ret_decay_fwd, ret_decay_bwd, ret_gn_w, w_attn_branch,
           w_ret_branch, w_out, norm_moe_w, moe_w_group, moe_b_group, moe_w_expert, moe_b_expert, moe_w1,
           moe_w3, moe_w2, norm_final_w):
    B, S, D = x.shape
    depth = norm_mix_w.shape[0]
    assert depth == 1, "the final norm is fused into the layer's combine stage"
    assert D == D_MODEL and S % TM_INPROJ == 0 and (B * S) < (1 << CODE_SHIFT)
    cos_t, sin_t = _rotary_tables(S)
    out = _layer(x.reshape(B * S, D), norm_mix_w[0], w_in[0], b_branch_gate[0], ret_decay_fwd[0],
                 ret_decay_bwd[0], ret_gn_w[0], w_attn_branch[0], w_ret_branch[0], w_out[0], norm_moe_w[0],
                 moe_w_group[0], moe_b_group[0], moe_w_expert[0], moe_b_expert[0], moe_w1[0], moe_w3[0],
                 moe_w2[0], norm_final_w, B, S, cos_t, sin_t)
    return out.reshape(B, S, D)
```

```python
import functools

import jax
import jax.numpy as jnp
from jax import lax
from jax.experimental import pallas as pl
from jax.experimental.pallas import tpu as pltpu

F32 = jnp.float32
BF16 = jnp.bfloat16

D_MODEL = 1024
HEAD_DIM = 64
ATTN_PAIRS = ((128, 1), (512, 4), (2048, 16))
ATTN_HEADS_PER_GROUP = 8
ATTN_GROUP_W = ATTN_HEADS_PER_GROUP * HEAD_DIM
ATTN_HALF = 64
ROPE_THETA = 10000.0
RET_HEADS = 8
RET_QK_DIM = 64
RET_V_DIM = 128
RET_CHUNK = 128
RET_QK_W = RET_HEADS * RET_QK_DIM
RET_V_W = RET_HEADS * RET_V_DIM
MOE_GROUPS = 8
MOE_EXPERTS_PER_GROUP = 8
MOE_N_EXPERTS = MOE_GROUPS * MOE_EXPERTS_PER_GROUP
MOE_HIDDEN = 512
NORM_EPS = 1e-6

LANES = 128
ROW_TILES = D_MODEL // LANES
NEG_BIG = -1e30

TM_INPROJ = 512
TM_MERGE = 256
TM_COMBINE = 256
MOE_BM = 256
DISPATCH_CHUNK = 512
ATTN_QB = 128

VMEM_LIMIT = 56 * 1024 * 1024

_A = 3 * ATTN_GROUP_W
OFF_QA, OFF_KA, OFF_VA = 0, _A, 2 * _A
OFF_QR = 3 * _A
OFF_KR = OFF_QR + RET_QK_W
OFF_VR = OFF_KR + RET_QK_W
OFF_GR = OFF_VR + RET_V_W
OFF_GL = OFF_GR + RET_V_W
IN_W = OFF_GL + 2 * D_MODEL


def _cparams(sem, vmem=VMEM_LIMIT):
    return pltpu.CompilerParams(dimension_semantics=sem, vmem_limit_bytes=vmem)


def _inproj_body(x_ref, nw_ref, w_ref, bg_ref, cos_ref, sin_ref,
                 qa0, ka0, va0, qa1, ka1, va1, qa2, ka2, va2, qr, kr, vr, gr, gt,
                 stage_ref):
    tm = x_ref.shape[0]
    x = x_ref[...]
    ms = jnp.mean(x * x, axis=-1, keepdims=True)
    xn = (x * lax.rsqrt(ms + NORM_EPS) * nw_ref[...]).astype(BF16)
    cos = cos_ref[...]
    sin = sin_ref[...]
    lane = lax.broadcasted_iota(jnp.int32, (tm, LANES), 1)
    first_half = (lane & (HEAD_DIM - 1)) < (HEAD_DIM // 2)

    def proj(c0, width):
        return jnp.dot(xn, w_ref[:, c0:c0 + width], preferred_element_type=F32)

    def rotary(a, scale):
        partner = jnp.where(first_half, pltpu.roll(a, LANES - HEAD_DIM // 2, 1),
                            pltpu.roll(a, HEAD_DIM // 2, 1))
        r = a * cos + partner * sin
        return r * scale if scale != 1.0 else r

    def chunks(acc):
        return [acc[:, c * LANES:(c + 1) * LANES] for c in range(acc.shape[1] // LANES)]

    def store_natural(out_ref, acc, fn):
        for c, a in enumerate(chunks(acc)):
            out_ref[:, c * LANES:(c + 1) * LANES] = fn(a).astype(out_ref.dtype)

    def store_strided(out_ref, acc, fn, d):
        for c, a in enumerate(chunks(acc)):
            stage_ref[c] = fn(a)
        for r in range(d):
            for c in range(acc.shape[1] // LANES):
                out_ref[0, r, :, c * LANES:(c + 1) * LANES] = (
                    stage_ref[c, pl.ds(r, tm // d, stride=d), :].astype(out_ref.dtype))

    ident = lambda a: a
    rot_q = lambda a: rotary(a, HEAD_DIM ** -0.5)
    rot_1 = lambda a: rotary(a, 1.0)
    rot_k = lambda a: rotary(a, RET_QK_DIM ** -0.5)

    W = ATTN_GROUP_W
    store_natural(qa0, proj(OFF_QA, W), rot_q)
    store_natural(ka0, proj(OFF_KA, W), rot_1)
    store_natural(va0, proj(OFF_VA, W), ident)
    for g, (qo, ko, vo) in ((1, (qa1, ka1, va1)), (2, (qa2, ka2, va2))):
        d = ATTN_PAIRS[g][1]
        store_strided(qo, proj(OFF_QA + g * W, W), rot_q, d)
        store_strided(ko, proj(OFF_KA + g * W, W), rot_1, d)
        store_strided(vo, proj(OFF_VA + g * W, W), ident, d)
    store_natural(qr, proj(OFF_QR, RET_QK_W), rot_1)
    store_natural(kr, proj(OFF_KR, RET_QK_W), rot_k)
    for h in range(RET_V_W // W):
        vr[:, h * W:(h + 1) * W] = proj(OFF_VR + h * W, W).astype(vr.dtype)
        gr[:, h * W:(h + 1) * W] = proj(OFF_GR + h * W, W).astype(gr.dtype)
    for h in range(2 * D_MODEL // W):
        z = proj(OFF_GL + h * W, W) + bg_ref[:, h * W:(h + 1) * W]
        gt[:, h * W:(h + 1) * W] = jax.nn.sigmoid(z).astype(gt.dtype)


def _in_projection(x2, norm_w, w_bf, b_gate, cos_t, sin_t, B, S):
    T = B * S
    tm = TM_INPROJ
    nt = S // tm
    W = ATTN_GROUP_W
    row = lambda i: (i, 0)
    const = lambda i: (0, 0)
    nat = lambda width: pl.BlockSpec((tm, width), row)

    def strided_spec(d):
        return pl.BlockSpec((1, d, tm // d, W), lambda i: (i // nt, 0, i % nt, 0))

    def strided_shape(d):
        return jax.ShapeDtypeStruct((B, d, S // d, W), BF16)

    nat_shape = lambda width: jax.ShapeDtypeStruct((T, width), BF16)
    d1, d2 = ATTN_PAIRS[1][1], ATTN_PAIRS[2][1]
    out_shape = ([nat_shape(W)] * 3 + [strided_shape(d1)] * 3 + [strided_shape(d2)] * 3
                 + [nat_shape(RET_QK_W)] * 2 + [nat_shape(RET_V_W)] * 2 + [nat_shape(2 * D_MODEL)])
    out_specs = ([nat(W)] * 3 + [strided_spec(d1)] * 3 + [strided_spec(d2)] * 3
                 + [nat(RET_QK_W)] * 2 + [nat(RET_V_W)] * 2 + [nat(2 * D_MODEL)])
    in_specs = [
        pl.BlockSpec((tm, D_MODEL), row),
        pl.BlockSpec((1, D_MODEL), const),
        pl.BlockSpec((D_MODEL, IN_W), const, pipeline_mode=pl.Buffered(1)),
        pl.BlockSpec((1, 2 * D_MODEL), const),
        pl.BlockSpec((tm, LANES), lambda i: (i % nt, 0)),
        pl.BlockSpec((tm, LANES), lambda i: (i % nt, 0)),
    ]
    return pl.pallas_call(
        _inproj_body,
        grid=(T // tm,),
        in_specs=in_specs,
        out_specs=out_specs,
        out_shape=out_shape,
        scratch_shapes=[pltpu.VMEM((W // LANES, tm, LANES), F32)],
        compiler_params=_cparams(("parallel",)),
        name="in_projection",
    )(x2, norm_w, w_bf, b_gate, cos_t, sin_t)


def _attn_body(q_ref, k_ref, v_ref, o_ref, lse_ref, *, L):
    QB = ATTN_QB
    KW = min(L, QB + 2 * ATTN_HALF)
    lane = lax.broadcasted_iota(jnp.int32, (QB, LANES), 1)
    head0 = lane < HEAD_DIM
    q_idx = lax.broadcasted_iota(jnp.int32, (QB, KW), 0)
    k_idx = lax.broadcasted_iota(jnp.int32, (QB, KW), 1)

    def block(j, carry):
        q0 = pl.multiple_of(j * QB, QB)
        ws = pl.multiple_of(jnp.clip(q0 - ATTN_HALF, 0, L - KW), ATTN_HALF)
        valid = jnp.abs((k_idx + ws) - (q_idx + q0)) <= ATTN_HALF
        for hp in range(ATTN_GROUP_W // LANES):
            cols = slice(hp * LANES, (hp + 1) * LANES)
            qp = q_ref[0, 0, pl.ds(q0, QB), cols]
            kw = k_ref[0, 0, pl.ds(ws, KW), cols]
            vw = v_ref[0, 0, pl.ds(ws, KW), cols]
            outs, lses = [], []
            for h in range(2):
                qm = jnp.where(head0 if h == 0 else jnp.logical_not(head0), qp, jnp.zeros_like(qp))
                s = lax.dot_general(qm, kw, (((1,), (1,)), ((), ())), preferred_element_type=F32)
                s = jnp.where(valid, s, NEG_BIG)
                m = jnp.max(s, axis=-1, keepdims=True)
                p = jnp.exp(s - m)
                l = jnp.sum(p, axis=-1, keepdims=True)
                o = jnp.dot(p.astype(BF16), vw, preferred_element_type=F32) / l
                outs.append(o)
                lses.append(m + jnp.log(l))
            o_ref[0, pl.ds(q0, QB), cols] = jnp.where(head0, outs[0], outs[1]).astype(o_ref.dtype)
            lse_ref[0, pl.ds(q0, QB), cols] = jnp.where(head0, lses[0], lses[1])
        return carry

    lax.fori_loop(0, L // QB, block, 0)


def _attention_group(q, k, v, B, S, d):
    L = S // d
    W = ATTN_GROUP_W
    in_spec = pl.BlockSpec((1, 1, L, W), lambda b, r: (b, r, 0, 0))
    out_spec = pl.BlockSpec((1, L, W), lambda b, r: (b, 0, r))
    o, lse = pl.pallas_call(
        functools.partial(_attn_body, L=L),
        grid=(B, d),
        in_specs=[in_spec] * 3,
        out_specs=[out_spec] * 2,
        out_shape=[jax.ShapeDtypeStruct((B, L, d * W), BF16),
                   jax.ShapeDtypeStruct((B, L, d * W), F32)],
        compiler_params=_cparams(("parallel", "parallel")),
        name=f"attention_d{d}",
    )(q, k, v)
    return o.reshape(B * S, W), lse.reshape(B * S, W)


def _log_sigmoid(z):
    return jnp.minimum(z, 0.0) - jnp.log(1.0 + jnp.exp(-jnp.abs(z)))


def _ret_body(dec_ref, q_ref, k_ref, v_ref, g_ref, gnw_ref, o_ref, sb_ref, sf_ref, *, S):
    C = RET_CHUNK
    nc = S // C
    lg = _log_sigmoid(dec_ref[0])
    a_row = lax.broadcasted_iota(jnp.int32, (C, LANES), 0).astype(F32)
    lane = lax.broadcasted_iota(jnp.int32, (C, LANES), 1)
    rel = (lax.broadcasted_iota(jnp.int32, (C, C), 0) - lax.broadcasted_iota(jnp.int32, (C, C), 1)).astype(F32)

    heads = []
    for h in range(2):
        lgf = lg[h:h + 1, :]
        lgb = lg[2 + h:3 + h, :]
        in_head = (lane < RET_QK_DIM) if h == 0 else (lane >= RET_QK_DIM)
        heads.append(dict(
            in_head=in_head,
            xi_f=jnp.where(in_head, jnp.exp((a_row + 1.0) * lgf), 0.0),
            xi_b=jnp.where(in_head, jnp.exp((C - a_row) * lgb), 0.0),
            zeta_f=jnp.exp((C - 1.0 - a_row) * lgf),
            zeta_b=jnp.exp(a_row * lgb),
            dloc=jnp.where(rel > 0, jnp.exp(rel * lgf), jnp.where(rel < 0, jnp.exp(-rel * lgb), 2.0)),
            cd_f=jnp.exp(C * lgf),
            cd_b=jnp.exp(C * lgb),
        ))

    def v_head(rows, h):
        return v_ref[0, rows, h * RET_V_DIM:(h + 1) * RET_V_DIM].astype(F32)

    def k_t(rows):
        return k_ref[0, rows, :].astype(F32).T.astype(BF16)

    sb_ref[nc - 1] = jnp.zeros(sb_ref.shape[1:], F32)

    def bwd(i, carry):
        n = nc - 1 - i
        rows = pl.ds(pl.multiple_of(n * C, C), C)
        kt = k_t(rows)
        for h, hd in enumerate(heads):
            kv = jnp.dot(kt, (v_head(rows, h) * hd["zeta_b"]).astype(BF16), preferred_element_type=F32)
            sb_ref[n - 1, h] = hd["cd_b"] * sb_ref[n, h] + kv
        return carry

    lax.fori_loop(0, nc - 1, bwd, 0)

    sf_ref[...] = jnp.zeros(sf_ref.shape, F32)

    def fwd(n, carry):
        rows = pl.ds(pl.multiple_of(n * C, C), C)
        qp = q_ref[0, rows, :]
        qf = qp.astype(F32)
        kt = k_t(rows)
        for h, hd in enumerate(heads):
            vh = v_head(rows, h)
            qm = jnp.where(hd["in_head"], qp, jnp.zeros_like(qp))
            s = jnp.dot(qm, kt, preferred_element_type=F32) * hd["dloc"]
            inner = jnp.dot(s.astype(BF16), vh.astype(BF16), preferred_element_type=F32)
            qx = jnp.concatenate([(qf * hd["xi_f"]).astype(BF16), (qf * hd["xi_b"]).astype(BF16)], axis=1)
            st = jnp.concatenate([sf_ref[h].astype(BF16), sb_ref[n, h].astype(BF16)], axis=0)
            ret = inner + jnp.dot(qx, st, preferred_element_type=F32)
            sf_ref[h] = hd["cd_f"] * sf_ref[h] + jnp.dot(
                kt, (vh * hd["zeta_f"]).astype(BF16), preferred_element_type=F32)
            mu = jnp.mean(ret, axis=-1, keepdims=True)
            xc = ret - mu
            var = jnp.mean(xc * xc, axis=-1, keepdims=True)
            cols = slice(h * RET_V_DIM, (h + 1) * RET_V_DIM)
            gate = g_ref[0, rows, cols].astype(F32)
            y = xc * lax.rsqrt(var + NORM_EPS) * gnw_ref[:, cols] * (gate * jax.nn.sigmoid(gate))
            o_ref[0, rows, cols] = y.astype(o_ref.dtype)
        return carry

    lax.fori_loop(0, nc, fwd, 0)


def _retention(dec, qr, kr, vr, gr, gn_w, B, S):
    nc = S // RET_CHUNK
    npairs = RET_HEADS // 2
    qk_spec = pl.BlockSpec((1, S, 2 * RET_QK_DIM), lambda b, p: (b, 0, p))
    v_spec = pl.BlockSpec((1, S, 2 * RET_V_DIM), lambda b, p: (b, 0, p))
    return pl.pallas_call(
        functools.partial(_ret_body, S=S),
        grid=(B, npairs),
        in_specs=[pl.BlockSpec((1, 4, LANES), lambda b, p: (p, 0, 0)),
                  qk_spec, qk_spec, v_spec, v_spec,
                  pl.BlockSpec((1, 2 * RET_V_DIM), lambda b, p: (0, p))],
        out_specs=v_spec,
        out_shape=jax.ShapeDtypeStruct((B, S, RET_V_W), BF16),
        scratch_shapes=[pltpu.VMEM((nc, 2, 2 * RET_QK_DIM, RET_V_DIM), F32),
                        pltpu.VMEM((2, 2 * RET_QK_DIM, RET_V_DIM), F32)],
        compiler_params=_cparams(("parallel", "parallel")),
        name="retention",
    )(dec, qr.reshape(B, S, RET_QK_W), kr.reshape(B, S, RET_QK_W),
      vr.reshape(B, S, RET_V_W), gr.reshape(B, S, RET_V_W), gn_w)


ROUTE_EID, ROUTE_RANK, ROUTE_GATE = 0, 2, 4
ROUTER_EXPERT_LANE0 = MOE_GROUPS


def _merge_body(o0, o1, o2, l0, l1, l2, yret_ref, gt_ref, x_ref, wa_ref, wb_ref, wo_ref,
                nw_ref, wrh_ref, wrl_ref, br_ref,
                h_ref, hn_ref, route_ref, cnt_ref):
    tm = x_ref.shape[0]
    i = pl.program_id(0)

    ls = [l0[...], l1[...], l2[...]]
    mx = jnp.maximum(jnp.maximum(ls[0], ls[1]), ls[2])
    es = [jnp.exp(l - mx) for l in ls]
    num = es[0] * o0[...].astype(F32) + es[1] * o1[...].astype(F32) + es[2] * o2[...].astype(F32)
    y_att = (num / (es[0] + es[1] + es[2])).astype(BF16)

    a = jnp.dot(y_att, wa_ref[...], preferred_element_type=F32)
    b = jnp.dot(yret_ref[...], wb_ref[...], preferred_element_type=F32)
    merged = gt_ref[:, :D_MODEL].astype(F32) * a + gt_ref[:, D_MODEL:].astype(F32) * b
    mix = jnp.dot(merged.astype(BF16), wo_ref[...], preferred_element_type=F32)
    h = x_ref[...] + mix
    h_ref[...] = h
    ms = jnp.mean(h * h, axis=-1, keepdims=True)
    hn = h * lax.rsqrt(ms + NORM_EPS) * nw_ref[...]
    for j in range(ROW_TILES):
        hn_ref[:, j, :] = hn[:, j * LANES:(j + 1) * LANES]

    hi = hn.astype(BF16)
    lo = (hn - hi.astype(F32)).astype(BF16)
    logits = (jnp.dot(hi, wrh_ref[...], preferred_element_type=F32)
              + jnp.dot(hi, wrl_ref[...], preferred_element_type=F32)
              + jnp.dot(lo, wrh_ref[...], preferred_element_type=F32)) + br_ref[...]
    lane = lax.broadcasted_iota(jnp.int32, (tm, LANES), 1)
    far = jnp.int32(4 * LANES)

    def first_argmax(vals, vmax):
        return jnp.min(jnp.where(vals == vmax, lane, far), axis=-1, keepdims=True)

    is_group = lane < MOE_GROUPS
    gl = jnp.where(is_group, logits, NEG_BIG)
    gmax = jnp.max(gl, axis=-1, keepdims=True)
    g_w = 1.0 / jnp.sum(jnp.where(is_group, jnp.exp(gl - gmax), 0.0), axis=-1, keepdims=True)
    g_idx = first_argmax(gl, gmax)
    e_lane = lane - ROUTER_EXPERT_LANE0
    in_group = (e_lane >= 0) & (e_lane < MOE_N_EXPERTS) & (jnp.right_shift(e_lane, 3) == g_idx)
    el = jnp.where(in_group, logits, NEG_BIG)
    m1 = jnp.max(el, axis=-1, keepdims=True)
    i1 = first_argmax(el, m1)
    el2 = jnp.where(lane == i1, NEG_BIG, el)
    m2 = jnp.max(el2, axis=-1, keepdims=True)
    i2 = first_argmax(el2, m2)
    ex = jnp.exp(m2 - m1)
    gate1 = g_w / (1.0 + ex)
    gate2 = g_w * ex / (1.0 + ex)

    @pl.when(i == 0)
    def _():
        cnt_ref[...] = jnp.zeros(cnt_ref.shape, F32)

    hot1 = lane == i1
    hot2 = lane == i2
    onehot = jnp.where(hot1 | hot2, 1.0, 0.0)
    r_idx = lax.broadcasted_iota(jnp.int32, (tm, tm), 0)
    c_idx = lax.broadcasted_iota(jnp.int32, (tm, tm), 1)
    lower = jnp.where(c_idx < r_idx, 1.0, 0.0).astype(BF16)
    before = jnp.dot(lower, onehot.astype(BF16), preferred_element_type=F32) + cnt_ref[...]
    rank1 = jnp.sum(jnp.where(hot1, before, 0.0), axis=-1, keepdims=True)
    rank2 = jnp.sum(jnp.where(hot2, before, 0.0), axis=-1, keepdims=True)
    cnt_ref[...] = cnt_ref[...] + jnp.sum(onehot, axis=0, keepdims=True)

    rec = jnp.zeros((tm, LANES), F32)
    for pos, val in ((ROUTE_EID, (i1 - ROUTER_EXPERT_LANE0).astype(F32)),
                     (ROUTE_EID + 1, (i2 - ROUTER_EXPERT_LANE0).astype(F32)),
                     (ROUTE_RANK, rank1), (ROUTE_RANK + 1, rank2),
                     (ROUTE_GATE, gate1), (ROUTE_GATE + 1, gate2)):
        rec = jnp.where(lane == pos, val, rec)
    route_ref[...] = rec


def _merge_route(os_, ls_, y_ret, gates, x2, wa, wb, wo, nw, wr_hi, wr_lo, b_r, T):
    tm = TM_MERGE
    row = lambda i: (i, 0)
    const = lambda i: (0, 0)
    W = ATTN_GROUP_W
    full = lambda arr: pl.BlockSpec(arr.shape, const)
    in_specs = ([pl.BlockSpec((tm, W), row)] * 6
                + [pl.BlockSpec((tm, RET_V_W), row), pl.BlockSpec((tm, 2 * D_MODEL), row),
                   pl.BlockSpec((tm, D_MODEL), row),
                   full(wa), full(wb), full(wo), full(nw), full(wr_hi), full(wr_lo), full(b_r)])
    return pl.pallas_call(
        _merge_body,
        grid=(T // tm,),
        in_specs=in_specs,
        out_specs=[pl.BlockSpec((tm, D_MODEL), row),
                   pl.BlockSpec((tm, ROW_TILES, LANES), lambda i: (i, 0, 0)),
                   pl.BlockSpec((tm, LANES), row), pl.BlockSpec((1, LANES), const)],
        out_shape=[jax.ShapeDtypeStruct((T, D_MODEL), F32), jax.ShapeDtypeStruct((T, ROW_TILES, LANES), F32),
                   jax.ShapeDtypeStruct((T, LANES), F32), jax.ShapeDtypeStruct((1, LANES), F32)],
        compiler_params=_cparams(("arbitrary",)),
        name="merge_route",
    )(*os_, *ls_, y_ret, gates, x2, wa, wb, wo, nw, wr_hi, wr_lo, b_r)


ISSUE_UNROLL = 8


def _dispatch_body(slot_ref, hn_ref, xs_ref, sem, *, T):
    i = pl.program_id(0)
    ch = hn_ref.shape[0]

    def row_copy(j, slot):
        return pltpu.make_async_copy(hn_ref.at[j], xs_ref.at[slot], sem)

    def issue(j, carry):
        t = i * ch + j
        row_copy(j, slot_ref[t]).start()
        row_copy(j, slot_ref[T + t]).start()
        return carry

    lax.fori_loop(0, ch, issue, 0, unroll=ISSUE_UNROLL)
    for _ in range(2):
        pltpu.make_async_copy(hn_ref, xs_ref.at[pl.ds(0, ch)], sem).wait()


def _dispatch(slots, hn, n_slots, T):
    ch = DISPATCH_CHUNK
    grid_spec = pltpu.PrefetchScalarGridSpec(
        num_scalar_prefetch=1,
        grid=(T // ch,),
        in_specs=[pl.BlockSpec((ch, ROW_TILES, LANES), lambda i, s: (i, 0, 0))],
        out_specs=pl.BlockSpec(memory_space=pl.ANY),
        scratch_shapes=[pltpu.SemaphoreType.DMA(())],
    )
    return pl.pallas_call(
        functools.partial(_dispatch_body, T=T),
        grid_spec=grid_spec,
        out_shape=jax.ShapeDtypeStruct((n_slots, ROW_TILES, LANES), F32),
        compiler_params=_cparams(("arbitrary",)),
        name="moe_dispatch",
    )(slots, hn)


def _expert_body(blk_ref, eid_ref, valid_ref, fresh_ref, x_ref, w1_ref, w3_ref, w2_ref, y_ref,
                 w1b, w3b, w2b):
    i = pl.program_id(0)
    valid = valid_ref[i]

    @pl.when(valid > 0)
    def _():
        @pl.when(fresh_ref[i] == 1)
        def _():
            w1b[...] = w1_ref[0].astype(BF16)
            w3b[...] = w3_ref[0].astype(BF16)
            w2b[...] = w2_ref[0].astype(BF16)

        bm = x_ref.shape[0]
        live = lax.broadcasted_iota(jnp.int32, (bm, LANES), 0) < valid
        x = jnp.concatenate(
            [jnp.where(live, x_ref[:, j, :], 0.0).astype(BF16) for j in range(ROW_TILES)], axis=1)
        a = jnp.dot(x, w1b[...], preferred_element_type=F32)
        b = jnp.dot(x, w3b[...], preferred_element_type=F32)
        hid = (a * jax.nn.sigmoid(a) * b).astype(BF16)
        y = jnp.dot(hid, w2b[...], preferred_element_type=F32)
        for j in range(ROW_TILES):
            y_ref[:, j, :] = y[:, j * LANES:(j + 1) * LANES]


def _experts(blk, blk_eid, blk_valid, blk_fresh, x_slots, w1, w3, w2, n_blocks):
    bm = MOE_BM
    grid_spec = pltpu.PrefetchScalarGridSpec(
        num_scalar_prefetch=4,
        grid=(n_blocks,),
        in_specs=[
            pl.BlockSpec((bm, ROW_TILES, LANES), lambda i, blk, eid, val, fr: (blk[i], 0, 0)),
            pl.BlockSpec((1, D_MODEL, MOE_HIDDEN), lambda i, blk, eid, val, fr: (eid[i], 0, 0)),
            pl.BlockSpec((1, D_MODEL, MOE_HIDDEN), lambda i, blk, eid, val, fr: (eid[i], 0, 0)),
            pl.BlockSpec((1, MOE_HIDDEN, D_MODEL), lambda i, blk, eid, val, fr: (eid[i], 0, 0)),
        ],
        out_specs=pl.BlockSpec((bm, ROW_TILES, LANES), lambda i, blk, eid, val, fr: (blk[i], 0, 0)),
        scratch_shapes=[pltpu.VMEM((D_MODEL, MOE_HIDDEN), BF16), pltpu.VMEM((D_MODEL, MOE_HIDDEN), BF16),
                        pltpu.VMEM((MOE_HIDDEN, D_MODEL), BF16)],
    )
    return pl.pallas_call(
        _expert_body,
        grid_spec=grid_spec,
        out_shape=jax.ShapeDtypeStruct(x_slots.shape, F32),
        compiler_params=_cparams(("arbitrary",)),
        name="moe_experts",
    )(blk, blk_eid, blk_valid, blk_fresh, x_slots, w1, w3, w2)


def _combine_body(slot_ref, ys_ref, h_ref, route_ref, nw_ref, o_ref, ybuf, sem, *, T):
    i = pl.program_id(0)
    n = pl.num_programs(0)
    tm = h_ref.shape[0]

    def row_copy(slot, buf, k, j):
        return pltpu.make_async_copy(ys_ref.at[slot], ybuf.at[buf, k, j], sem.at[buf])

    def issue(tile, buf):
        def one(j, carry):
            t = tile * tm + j
            row_copy(slot_ref[t], buf, 0, j).start()
            row_copy(slot_ref[T + t], buf, 1, j).start()
            return carry
        lax.fori_loop(0, tm, one, 0, unroll=ISSUE_UNROLL)

    @pl.when(i == 0)
    def _():
        issue(0, 0)

    @pl.when(i + 1 < n)
    def _():
        issue(i + 1, (i + 1) % 2)

    buf = i % 2
    for k in range(2):
        pltpu.make_async_copy(ys_ref.at[pl.ds(0, tm)], ybuf.at[buf, k], sem.at[buf]).wait()
    route = route_ref[...]
    g1 = route[:, ROUTE_GATE:ROUTE_GATE + 1]
    g2 = route[:, ROUTE_GATE + 1:ROUTE_GATE + 2]
    hs = []
    ss = jnp.zeros((tm, 1), F32)
    for j in range(ROW_TILES):
        hj = h_ref[:, j * LANES:(j + 1) * LANES] + (ybuf[buf, 0, :, j, :] * g1 + ybuf[buf, 1, :, j, :] * g2)
        hs.append(hj)
        ss = ss + jnp.sum(hj * hj, axis=-1, keepdims=True)
    inv = lax.rsqrt(ss * (1.0 / D_MODEL) + NORM_EPS)
    for j, hj in enumerate(hs):
        cols = slice(j * LANES, (j + 1) * LANES)
        o_ref[:, cols] = hj * inv * nw_ref[:, cols]


def _combine(slots, y_slots, h, route, nw, T):
    tm = TM_COMBINE
    row = lambda i, s: (i, 0)
    grid_spec = pltpu.PrefetchScalarGridSpec(
        num_scalar_prefetch=1,
        grid=(T // tm,),
        in_specs=[pl.BlockSpec(memory_space=pl.ANY),
                  pl.BlockSpec((tm, D_MODEL), row),
                  pl.BlockSpec((tm, LANES), row),
                  pl.BlockSpec((1, D_MODEL), lambda i, s: (0, 0))],
        out_specs=pl.BlockSpec((tm, D_MODEL), row),
        scratch_shapes=[pltpu.VMEM((2, 2, tm, ROW_TILES, LANES), F32), pltpu.SemaphoreType.DMA((2,))],
    )
    return pl.pallas_call(
        functools.partial(_combine_body, T=T),
        grid_spec=grid_spec,
        out_shape=jax.ShapeDtypeStruct((T, D_MODEL), F32),
        compiler_params=_cparams(("arbitrary",)),
        name="moe_combine",
    )(slots, y_slots, h, route, nw)


def _rotary_tables(S):
    inv_freq = 1.0 / (ROPE_THETA ** (jnp.arange(0, HEAD_DIM, 2, dtype=F32) / HEAD_DIM))
    ang = jnp.arange(S, dtype=F32)[:, None] * inv_freq[None, :]
    cos, sin = jnp.cos(ang), jnp.sin(ang)
    reps = LANES // HEAD_DIM
    cos_t = jnp.tile(jnp.concatenate([cos, cos], axis=1), (1, reps))
    sin_t = jnp.tile(jnp.concatenate([-sin, sin], axis=1), (1, reps))
    return cos_t, sin_t


def _layer(h_in, norm_mix_w, w_in, b_branch_gate, ret_decay_fwd, ret_decay_bwd, ret_gn_w, w_attn_branch,
           w_ret_branch, w_out, norm_moe_w, moe_w_group, moe_b_group, moe_w_expert, moe_b_expert,
           moe_w1, moe_w3, moe_w2, next_norm_w, B, S, cos_t, sin_t):
    T = B * S
    (qa0, ka0, va0, qa1, ka1, va1, qa2, ka2, va2, qr, kr, vr, gr, gates) = _in_projection(
        h_in, norm_mix_w[None, :], w_in.astype(BF16), b_branch_gate[None, :], cos_t, sin_t, B, S)

    os_, ls_ = [], []
    for (q, k, v), (_, d) in zip(((qa0.reshape(B, 1, S, ATTN_GROUP_W), ka0.reshape(B, 1, S, ATTN_GROUP_W),
                                   va0.reshape(B, 1, S, ATTN_GROUP_W)), (qa1, ka1, va1), (qa2, ka2, va2)),
                                 ATTN_PAIRS):
        o, lse = _attention_group(q, k, v, B, S, d)
        os_.append(o)
        ls_.append(lse)

    dec = jnp.stack([ret_decay_fwd.reshape(RET_HEADS // 2, 2), ret_decay_bwd.reshape(RET_HEADS // 2, 2)], axis=1)
    dec = jnp.broadcast_to(dec.reshape(RET_HEADS // 2, 4, 1), (RET_HEADS // 2, 4, LANES)).astype(F32)
    y_ret = _retention(dec, qr, kr, vr, gr, ret_gn_w[None, :], B, S).reshape(T, RET_V_W)

    pad = LANES - MOE_GROUPS - MOE_N_EXPERTS
    w_r = jnp.concatenate([moe_w_group, moe_w_expert, jnp.zeros((D_MODEL, pad), F32)], axis=1)
    w_r_hi = w_r.astype(BF16)
    w_r_lo = (w_r - w_r_hi.astype(F32)).astype(BF16)
    b_r = jnp.concatenate([moe_b_group, moe_b_expert, jnp.zeros((pad,), F32)])[None, :]

    h_mid, hn, route, cnt = _merge_route(
        os_, ls_, y_ret, gates, h_in, w_attn_branch.astype(BF16), w_ret_branch.astype(BF16),
        w_out.astype(BF16), norm_moe_w[None, :], w_r_hi, w_r_lo, b_r, T)

    bm = MOE_BM
    counts = cnt[0, ROUTER_EXPERT_LANE0:ROUTER_EXPERT_LANE0 + MOE_N_EXPERTS].astype(jnp.int32)
    nblk = (counts + bm - 1) // bm
    blk_end = jnp.cumsum(nblk)
    pstart = (blk_end - nblk) * bm
    n_blocks = (2 * T) // bm + MOE_N_EXPERTS
    n_active = blk_end[-1]
    bidx = jnp.minimum(jnp.arange(n_blocks, dtype=jnp.int32), n_active - 1)
    blk_eid = jnp.sum(bidx[:, None] >= blk_end[None, :], axis=1).astype(jnp.int32)
    blk_valid = jnp.clip(counts[blk_eid] - (bidx * bm - pstart[blk_eid]), 0, bm)
    blk_valid = jnp.where(jnp.arange(n_blocks) < n_active, blk_valid, 0).astype(jnp.int32)
    blk_fresh = jnp.concatenate([jnp.ones((1,), jnp.int32), (blk_eid[1:] != blk_eid[:-1]).astype(jnp.int32)])
    eid = route[:, ROUTE_EID:ROUTE_EID + 2].astype(jnp.int32)
    rank = route[:, ROUTE_RANK:ROUTE_RANK + 2].astype(jnp.int32)
    slots = (pstart.astype(jnp.int32)[eid] + rank).T.reshape(2 * T)

    x_slots = _dispatch(slots, hn, n_blocks * bm, T)
    y_slots = _experts(bidx, blk_eid, blk_valid, blk_fresh, x_slots, moe_w1, moe_w3, moe_w2, n_blocks)
    return _combine(slots, y_slots, h_mid, route, next_norm_w[None, :], T)


def kernel(x, norm_mix_w, w_in, b_branch_gate, ret_decay_fwd, ret_decay_bwd, ret_gn_w, w_attn_branch,
           w_ret_branch, w_out, norm_moe_w, moe_w_group, moe_b_group, moe_w_expert, moe_b_expert, moe_w1,
           moe_w3, moe_w2, norm_final_w):
    B, S, D = x.shape
    depth = norm_mix_w.shape[0]
    assert depth == 1, "the final norm is fused into the layer's combine stage"
    assert D == D_MODEL and S % TM_INPROJ == 0 and (B * S) < (1 << 24)
    cos_t, sin_t = _rotary_tables(S)
    out = _layer(x.reshape(B * S, D), norm_mix_w[0], w_in[0], b_branch_gate[0], ret_decay_fwd[0],
                 ret_decay_bwd[0], ret_gn_w[0], w_attn_branch[0], w_ret_branch[0], w_out[0], norm_moe_w[0],
                 moe_w_group[0], moe_b_group[0], moe_w_expert[0], moe_b_expert[0], moe_w1[0], moe_w3[0],
                 moe_w2[0], norm_final_w, B, S, cos_t, sin_t)
    return out.reshape(B, S, D)
```

```python
import functools

import jax
import jax.numpy as jnp
from jax import lax
from jax.experimental import pallas as pl
from jax.experimental.pallas import tpu as pltpu

F32 = jnp.float32
BF16 = jnp.bfloat16

D_MODEL = 1024
HEAD_DIM = 64
ATTN_PAIRS = ((128, 1), (512, 4), (2048, 16))
ATTN_HEADS_PER_GROUP = 8
ATTN_GROUP_W = ATTN_HEADS_PER_GROUP * HEAD_DIM
ATTN_HALF = 64
ROPE_THETA = 10000.0
RET_HEADS = 8
RET_QK_DIM = 64
RET_V_DIM = 128
RET_CHUNK = 128
RET_QK_W = RET_HEADS * RET_QK_DIM
RET_V_W = RET_HEADS * RET_V_DIM
MOE_GROUPS = 8
MOE_EXPERTS_PER_GROUP = 8
MOE_N_EXPERTS = MOE_GROUPS * MOE_EXPERTS_PER_GROUP
MOE_HIDDEN = 512
NORM_EPS = 1e-6

LANES = 128
ROW_TILES = D_MODEL // LANES
NEG_BIG = -1e30

TM_INPROJ = 512
TM_MERGE = 256
TM_COMBINE = 256
MOE_BM = 256
DISPATCH_CHUNK = 512
ATTN_QB = 128
ATTN_UNROLL = 8
RET_UNROLL = 4

VMEM_LIMIT = 56 * 1024 * 1024

_A = 3 * ATTN_GROUP_W
OFF_QA, OFF_KA, OFF_VA = 0, _A, 2 * _A
OFF_QR = 3 * _A
OFF_KR = OFF_QR + RET_QK_W
OFF_VR = OFF_KR + RET_QK_W
OFF_GR = OFF_VR + RET_V_W
OFF_GL = OFF_GR + RET_V_W
IN_W = OFF_GL + 2 * D_MODEL


def _cparams(sem, vmem=VMEM_LIMIT):
    return pltpu.CompilerParams(dimension_semantics=sem, vmem_limit_bytes=vmem)


def _inproj_body(x_ref, nw_ref, w_ref, bg_ref, cos_ref, sin_ref,
                 qa0, ka0, va0, qa1, ka1, va1, qa2, ka2, va2, qr, kr, vr, gr, gt,
                 stage_ref):
    tm = x_ref.shape[0]
    x = x_ref[...]
    ms = jnp.mean(x * x, axis=-1, keepdims=True)
    xn = (x * lax.rsqrt(ms + NORM_EPS) * nw_ref[...]).astype(BF16)
    cos = cos_ref[...]
    sin = sin_ref[...]
    lane = lax.broadcasted_iota(jnp.int32, (tm, LANES), 1)
    first_half = (lane & (HEAD_DIM - 1)) < (HEAD_DIM // 2)

    def proj(c0, width):
        return jnp.dot(xn, w_ref[:, c0:c0 + width], preferred_element_type=F32)

    def rotary(a, scale):
        partner = jnp.where(first_half, pltpu.roll(a, LANES - HEAD_DIM // 2, 1),
                            pltpu.roll(a, HEAD_DIM // 2, 1))
        r = a * cos + partner * sin
        return r * scale if scale != 1.0 else r

    def chunks(acc):
        return [acc[:, c * LANES:(c + 1) * LANES] for c in range(acc.shape[1] // LANES)]

    def store_natural(out_ref, acc, fn):
        for c, a in enumerate(chunks(acc)):
            out_ref[:, c * LANES:(c + 1) * LANES] = fn(a).astype(out_ref.dtype)

    def store_pairs(out_ref, acc, fn):
        for c, a in enumerate(chunks(acc)):
            out_ref[0, c] = fn(a).astype(out_ref.dtype)

    def store_strided(out_ref, acc, fn, d):
        for c, a in enumerate(chunks(acc)):
            stage_ref[c] = fn(a)
        for c in range(acc.shape[1] // LANES):
            for r in range(d):
                out_ref[0, c, r] = stage_ref[c, pl.ds(r, tm // d, stride=d), :].astype(out_ref.dtype)

    ident = lambda a: a
    rot_q = lambda a: rotary(a, HEAD_DIM ** -0.5)
    rot_1 = lambda a: rotary(a, 1.0)
    rot_k = lambda a: rotary(a, RET_QK_DIM ** -0.5)

    W = ATTN_GROUP_W
    store_pairs(qa0, proj(OFF_QA, W), rot_q)
    store_pairs(ka0, proj(OFF_KA, W), rot_1)
    store_pairs(va0, proj(OFF_VA, W), ident)
    for g, (qo, ko, vo) in ((1, (qa1, ka1, va1)), (2, (qa2, ka2, va2))):
        d = ATTN_PAIRS[g][1]
        store_strided(qo, proj(OFF_QA + g * W, W), rot_q, d)
        store_strided(ko, proj(OFF_KA + g * W, W), rot_1, d)
        store_strided(vo, proj(OFF_VA + g * W, W), ident, d)
    store_natural(qr, proj(OFF_QR, RET_QK_W), rot_1)
    store_natural(kr, proj(OFF_KR, RET_QK_W), rot_k)
    for h in range(RET_V_W // W):
        vr[:, h * W:(h + 1) * W] = proj(OFF_VR + h * W, W).astype(vr.dtype)
        gr[:, h * W:(h + 1) * W] = proj(OFF_GR + h * W, W).astype(gr.dtype)
    for h in range(2 * D_MODEL // W):
        z = proj(OFF_GL + h * W, W) + bg_ref[:, h * W:(h + 1) * W]
        gt[:, h * W:(h + 1) * W] = jax.nn.sigmoid(z).astype(gt.dtype)


def _in_projection(x2, norm_w, w_bf, b_gate, cos_t, sin_t, B, S):
    T = B * S
    tm = TM_INPROJ
    nt = S // tm
    W = ATTN_GROUP_W
    row = lambda i: (i, 0)
    const = lambda i: (0, 0)
    nat = lambda width: pl.BlockSpec((tm, width), row)

    P = W // LANES

    def strided_spec(d):
        return pl.BlockSpec((1, P, d, tm // d, LANES), lambda i: (i // nt, 0, 0, i % nt, 0))

    def strided_shape(d):
        return jax.ShapeDtypeStruct((B, P, d, S // d, LANES), BF16)

    pair_spec = pl.BlockSpec((1, P, tm, LANES), lambda i: (i // nt, 0, i % nt, 0))
    pair_shape = jax.ShapeDtypeStruct((B, P, S, LANES), BF16)
    nat_shape = lambda width: jax.ShapeDtypeStruct((T, width), BF16)
    d1, d2 = ATTN_PAIRS[1][1], ATTN_PAIRS[2][1]
    out_shape = ([pair_shape] * 3 + [strided_shape(d1)] * 3 + [strided_shape(d2)] * 3
                 + [nat_shape(RET_QK_W)] * 2 + [nat_shape(RET_V_W)] * 2 + [nat_shape(2 * D_MODEL)])
    out_specs = ([pair_spec] * 3 + [strided_spec(d1)] * 3 + [strided_spec(d2)] * 3
                 + [nat(RET_QK_W)] * 2 + [nat(RET_V_W)] * 2 + [nat(2 * D_MODEL)])
    in_specs = [
        pl.BlockSpec((tm, D_MODEL), row),
        pl.BlockSpec((1, D_MODEL), const),
        pl.BlockSpec((D_MODEL, IN_W), const, pipeline_mode=pl.Buffered(1)),
        pl.BlockSpec((1, 2 * D_MODEL), const),
        pl.BlockSpec((tm, LANES), lambda i: (i % nt, 0)),
        pl.BlockSpec((tm, LANES), lambda i: (i % nt, 0)),
    ]
    return pl.pallas_call(
        _inproj_body,
        grid=(T // tm,),
        in_specs=in_specs,
        out_specs=out_specs,
        out_shape=out_shape,
        scratch_shapes=[pltpu.VMEM((W // LANES, tm, LANES), F32)],
        compiler_params=_cparams(("parallel",)),
        name="in_projection",
    )(x2, norm_w, w_bf, b_gate, cos_t, sin_t)


def _attn_body(q0_ref, k0_ref, v0_ref, q1_ref, k1_ref, v1_ref, q2_ref, k2_ref, v2_ref, o_ref,
               va_ref, vb_ref, part_ref, bias_ref, *, S):
    QB, H = ATTN_QB, ATTN_HALF
    lane = lax.broadcasted_iota(jnp.int32, (QB, LANES), 1)
    head0 = lane < HEAD_DIM
    ones = jnp.ones((QB, LANES), BF16)

    qi = lax.broadcasted_iota(jnp.int32, (QB, QB + 2 * H), 0)
    ki = lax.broadcasted_iota(jnp.int32, (QB, QB + 2 * H), 1)
    for n in range(3):
        bias_ref[n] = jnp.where(jnp.abs(ki - qi - n * H) <= H, 0.0, NEG_BIG).astype(F32)

    def scores(q_rows, k_rows, bias, h):
        qm = jnp.where(head0 if h == 0 else jnp.logical_not(head0), q_rows, jnp.zeros_like(q_rows))
        return lax.dot_general(qm, k_rows, (((1,), (1,)), ((), ())), preferred_element_type=F32) + bias

    def weights(s):
        m = jnp.max(s, axis=-1, keepdims=True)
        return m, jnp.exp(s - m).astype(BF16)

    def heads_to_lanes(m0, a, m1, b):
        num = jnp.where(head0, a, b)
        den = pltpu.roll(jnp.where(head0, b, a), HEAD_DIM, 1)
        mx = jnp.where(head0, m0, m1)
        return num, den, mx

    def run_group(q_ref, k_ref, v_ref, d, prepare, store):
        L = S // d
        KW = min(L, QB + 2 * H)
        nb = L // QB

        def fill(t, carry):
            r = t // nb
            rows = pl.ds(pl.multiple_of((t % nb) * QB, QB), QB)
            v = v_ref[0, 0, r, rows, :]
            dst = pl.ds(pl.multiple_of(t * QB, QB), QB)
            va_ref[dst, :] = jnp.where(head0, v, ones)
            vb_ref[dst, :] = jnp.where(head0, ones, v)
            return carry

        lax.fori_loop(0, S // QB, fill, 0)

        def trip(i, carry):
            blocks = []
            for u in range(ATTN_UNROLL):
                t = i * ATTN_UNROLL + u
                r = t // nb
                q0 = pl.multiple_of((t % nb) * QB, QB)
                ws = pl.multiple_of(jnp.clip(q0 - H, 0, L - KW), H)
                bias = bias_ref[(q0 - ws) // H][:, :KW]
                q_rows = q_ref[0, 0, r, pl.ds(q0, QB), :]
                k_rows = k_ref[0, 0, r, pl.ds(ws, KW), :]
                vrows = pl.ds(pl.multiple_of(r * L + ws, H), KW)
                blocks.append((r, q0, vrows, [scores(q_rows, k_rows, bias, h) for h in range(2)]))
            blocks = [(r, q0, vrows, [weights(s) for s in ss]) for r, q0, vrows, ss in blocks]
            done = []
            for r, q0, vrows, ((m0, p0), (m1, p1)) in blocks:
                a = jnp.dot(p0, va_ref[vrows, :], preferred_element_type=F32)
                b = jnp.dot(p1, vb_ref[vrows, :], preferred_element_type=F32)
                done.append((r, q0, m0, a, m1, b))
            done = [(r, q0, prepare(q0, *heads_to_lanes(m0, a, m1, b))) for r, q0, m0, a, m1, b in done]
            for r, q0, vals in done:
                store(r, q0, vals)
            return carry

        lax.fori_loop(0, S // QB // ATTN_UNROLL, trip, 0)

    def store_partial(g):
        d = ATTN_PAIRS[g][1]

        def store(r, q0, vals):
            rows = pl.ds(r + q0 * d, QB, stride=d)
            for n, val in enumerate(vals):
                part_ref[3 * (g - 1) + n, rows, :] = val
        return store

    keep = lambda q0, num, den, mx: (num, den, mx)
    run_group(q1_ref, k1_ref, v1_ref, ATTN_PAIRS[1][1], keep, store_partial(1))
    run_group(q2_ref, k2_ref, v2_ref, ATTN_PAIRS[2][1], keep, store_partial(2))

    def merge(q0, num, den, mx):
        rows = pl.ds(q0, QB)
        nums = [num, part_ref[0, rows, :], part_ref[3, rows, :]]
        dens = [den, part_ref[1, rows, :], part_ref[4, rows, :]]
        mxs = [mx, part_ref[2, rows, :], part_ref[5, rows, :]]
        top = jnp.maximum(jnp.maximum(mxs[0], mxs[1]), mxs[2])
        ws = [jnp.exp(m - top) for m in mxs]
        n = ws[0] * nums[0] + ws[1] * nums[1] + ws[2] * nums[2]
        dn = ws[0] * dens[0] + ws[1] * dens[1] + ws[2] * dens[2]
        return (n / dn).astype(o_ref.dtype)

    def store_out(r, q0, y):
        o_ref[0, 0, pl.ds(q0, QB), :] = y

    run_group(q0_ref, k0_ref, v0_ref, ATTN_PAIRS[0][1], merge, store_out)


def _attention(qkv, B, S):
    P = ATTN_GROUP_W // LANES
    in_specs = []
    for _, d in ATTN_PAIRS:
        in_specs += [pl.BlockSpec((1, 1, d, S // d, LANES), lambda b, p: (b, p, 0, 0, 0))] * 3
    return pl.pallas_call(
        functools.partial(_attn_body, S=S),
        grid=(B, P),
        in_specs=in_specs,
        out_specs=pl.BlockSpec((1, 1, S, LANES), lambda b, p: (b, p, 0, 0)),
        out_shape=jax.ShapeDtypeStruct((B, P, S, LANES), BF16),
        scratch_shapes=[pltpu.VMEM((S, LANES), BF16), pltpu.VMEM((S, LANES), BF16),
                        pltpu.VMEM((6, S, LANES), F32),
                        pltpu.VMEM((3, ATTN_QB, ATTN_QB + 2 * ATTN_HALF), F32)],
        compiler_params=_cparams(("parallel", "parallel")),
        name="attention",
    )(*qkv)


def _log_sigmoid(z):
    return jnp.minimum(z, 0.0) - jnp.log(1.0 + jnp.exp(-jnp.abs(z)))


def _ret_body(dec_ref, q_ref, k_ref, v_ref, g_ref, gnw_ref, o_ref, kt_ref, sf_ref, sb_ref, *, S):
    C = RET_CHUNK
    nc = S // C
    lg = _log_sigmoid(dec_ref[0])
    a_row = lax.broadcasted_iota(jnp.int32, (C, LANES), 0).astype(F32)
    lane = lax.broadcasted_iota(jnp.int32, (C, LANES), 1)
    rel = (lax.broadcasted_iota(jnp.int32, (C, C), 0) - lax.broadcasted_iota(jnp.int32, (C, C), 1)).astype(F32)

    heads = []
    for h in range(2):
        lgf = lg[h:h + 1, :]
        lgb = lg[2 + h:3 + h, :]
        in_head = (lane < RET_QK_DIM) if h == 0 else (lane >= RET_QK_DIM)
        heads.append(dict(
            in_head=in_head,
            xi_f=jnp.where(in_head, jnp.exp((a_row + 1.0) * lgf), 0.0),
            xi_b=jnp.where(in_head, jnp.exp((C - a_row) * lgb), 0.0),
            zeta_f=jnp.exp((C - 1.0 - a_row) * lgf),
            zeta_b=jnp.exp(a_row * lgb),
            dloc=jnp.where(rel > 0, jnp.exp(rel * lgf), jnp.where(rel < 0, jnp.exp(-rel * lgb), 2.0)),
            cd_f=jnp.exp(C * lgf),
            cd_b=jnp.exp(C * lgb),
        ))

    def chunk_rows(n):
        return pl.ds(pl.multiple_of(n * C, C), C)

    def v_head(rows, h):
        return v_ref[0, rows, h * RET_V_DIM:(h + 1) * RET_V_DIM]

    def products(i, carry):
        items = []
        for u in range(RET_UNROLL):
            n = i * RET_UNROLL + u
            rows = chunk_rows(n)
            kt = k_ref[0, rows, :].astype(F32).T.astype(BF16)
            kt_ref[n] = kt
            for h, hd in enumerate(heads):
                vh = v_head(rows, h).astype(F32)
                items.append((n, h, kt, (vh * hd["zeta_f"]).astype(BF16), (vh * hd["zeta_b"]).astype(BF16)))
        outs = [(n, h, jnp.dot(kt, vf, preferred_element_type=F32), jnp.dot(kt, vb, preferred_element_type=F32))
                for n, h, kt, vf, vb in items]
        for n, h, f, b in outs:
            sf_ref[n, h] = f
            sb_ref[n, h] = b
        return carry

    lax.fori_loop(0, nc // RET_UNROLL, products, 0)

    def scan(ref, key, order):
        def step(i, state):
            n = order(i)
            new = []
            for h, hd in enumerate(heads):
                kv = ref[n, h]
                ref[n, h] = state[h]
                new.append(hd[key] * state[h] + kv)
            return tuple(new)
        zero = jnp.zeros(ref.shape[2:], F32)
        lax.fori_loop(0, nc, step, (zero, zero))

    scan(sf_ref, "cd_f", lambda i: i)
    scan(sb_ref, "cd_b", lambda i: nc - 1 - i)

    def outputs(i, carry):
        items = []
        for u in range(RET_UNROLL):
            n = i * RET_UNROLL + u
            rows = chunk_rows(n)
            qp = q_ref[0, rows, :]
            qf = qp.astype(F32)
            kt = kt_ref[n]
            for h, hd in enumerate(heads):
                qm = jnp.where(hd["in_head"], qp, jnp.zeros_like(qp))
                qx = jnp.concatenate([(qf * hd["xi_f"]).astype(BF16), (qf * hd["xi_b"]).astype(BF16)], axis=1)
                st = jnp.concatenate([sf_ref[n, h].astype(BF16), sb_ref[n, h].astype(BF16)], axis=0)
                items.append((rows, h, hd, jnp.dot(qm, kt, preferred_element_type=F32),
                              jnp.dot(qx, st, preferred_element_type=F32)))
        items = [(rows, h, hd, (s * hd["dloc"]).astype(BF16), cross) for rows, h, hd, s, cross in items]
        items = [(rows, h, cross + jnp.dot(p, v_head(rows, h), preferred_element_type=F32))
                 for rows, h, hd, p, cross in items]
        for rows, h, ret in items:
            mu = jnp.mean(ret, axis=-1, keepdims=True)
            xc = ret - mu
            var = jnp.mean(xc * xc, axis=-1, keepdims=True)
            cols = slice(h * RET_V_DIM, (h + 1) * RET_V_DIM)
            gate = g_ref[0, rows, cols].astype(F32)
            y = xc * lax.rsqrt(var + NORM_EPS) * gnw_ref[:, cols] * (gate * jax.nn.sigmoid(gate))
            o_ref[0, rows, cols] = y.astype(o_ref.dtype)
        return carry

    lax.fori_loop(0, nc // RET_UNROLL, outputs, 0)


def _retention(dec, qr, kr, vr, gr, gn_w, B, S):
    nc = S // RET_CHUNK
    npairs = RET_HEADS // 2
    qk_spec = pl.BlockSpec((1, S, 2 * RET_QK_DIM), lambda b, p: (b, 0, p))
    v_spec = pl.BlockSpec((1, S, 2 * RET_V_DIM), lambda b, p: (b, 0, p))
    return pl.pallas_call(
        functools.partial(_ret_body, S=S),
        grid=(B, npairs),
        in_specs=[pl.BlockSpec((1, 4, LANES), lambda b, p: (p, 0, 0)),
                  qk_spec, qk_spec, v_spec, v_spec,
                  pl.BlockSpec((1, 2 * RET_V_DIM), lambda b, p: (0, p))],
        out_specs=v_spec,
        out_shape=jax.ShapeDtypeStruct((B, S, RET_V_W), BF16),
        scratch_shapes=[pltpu.VMEM((nc, 2 * RET_QK_DIM, RET_CHUNK), BF16),
                        pltpu.VMEM((nc, 2, 2 * RET_QK_DIM, RET_V_DIM), F32),
                        pltpu.VMEM((nc, 2, 2 * RET_QK_DIM, RET_V_DIM), F32)],
        compiler_params=_cparams(("parallel", "parallel")),
        name="retention",
    )(dec, qr.reshape(B, S, RET_QK_W), kr.reshape(B, S, RET_QK_W),
      vr.reshape(B, S, RET_V_W), gr.reshape(B, S, RET_V_W), gn_w)


ROUTE_EID, ROUTE_RANK, ROUTE_GATE = 0, 2, 4
ROUTE_FIELDS = 8
ROUTER_EXPERT_LANE0 = MOE_GROUPS


def _merge_body(yatt_ref, yret_ref, gt_ref, x_ref, wa_ref, wb_ref, wo_ref,
                nw_ref, wrh_ref, wrl_ref, br_ref,
                h_ref, hn_ref, route_ref, route_t_ref, cnt_ref):
    tm = x_ref.shape[0]
    i = pl.program_id(0)

    y_att = jnp.concatenate([yatt_ref[0, c] for c in range(yatt_ref.shape[1])], axis=1)
    a = jnp.dot(y_att, wa_ref[...], preferred_element_type=F32)
    b = jnp.dot(yret_ref[...], wb_ref[...], preferred_element_type=F32)
    merged = gt_ref[:, :D_MODEL].astype(F32) * a + gt_ref[:, D_MODEL:].astype(F32) * b
    mix = jnp.dot(merged.astype(BF16), wo_ref[...], preferred_element_type=F32)
    h = x_ref[...] + mix
    h_ref[...] = h
    ms = jnp.mean(h * h, axis=-1, keepdims=True)
    hn = h * lax.rsqrt(ms + NORM_EPS) * nw_ref[...]
    for j in range(ROW_TILES):
        hn_ref[:, j, :] = hn[:, j * LANES:(j + 1) * LANES]

    hi = hn.astype(BF16)
    lo = (hn - hi.astype(F32)).astype(BF16)
    logits = (jnp.dot(hi, wrh_ref[...], preferred_element_type=F32)
              + jnp.dot(hi, wrl_ref[...], preferred_element_type=F32)
              + jnp.dot(lo, wrh_ref[...], preferred_element_type=F32)) + br_ref[...]
    lane = lax.broadcasted_iota(jnp.int32, (tm, LANES), 1)
    far = jnp.int32(4 * LANES)

    def first_argmax(vals, vmax):
        return jnp.min(jnp.where(vals == vmax, lane, far), axis=-1, keepdims=True)

    is_group = lane < MOE_GROUPS
    gl = jnp.where(is_group, logits, NEG_BIG)
    gmax = jnp.max(gl, axis=-1, keepdims=True)
    g_w = 1.0 / jnp.sum(jnp.where(is_group, jnp.exp(gl - gmax), 0.0), axis=-1, keepdims=True)
    g_idx = first_argmax(gl, gmax)
    e_lane = lane - ROUTER_EXPERT_LANE0
    in_group = (e_lane >= 0) & (e_lane < MOE_N_EXPERTS) & (jnp.right_shift(e_lane, 3) == g_idx)
    el = jnp.where(in_group, logits, NEG_BIG)
    m1 = jnp.max(el, axis=-1, keepdims=True)
    i1 = first_argmax(el, m1)
    el2 = jnp.where(lane == i1, NEG_BIG, el)
    m2 = jnp.max(el2, axis=-1, keepdims=True)
    i2 = first_argmax(el2, m2)
    ex = jnp.exp(m2 - m1)
    gate1 = g_w / (1.0 + ex)
    gate2 = g_w * ex / (1.0 + ex)

    @pl.when(i == 0)
    def _():
        cnt_ref[...] = jnp.zeros(cnt_ref.shape, F32)

    hot1 = lane == i1
    hot2 = lane == i2
    onehot = jnp.where(hot1 | hot2, 1.0, 0.0)
    r_idx = lax.broadcasted_iota(jnp.int32, (tm, tm), 0)
    c_idx = lax.broadcasted_iota(jnp.int32, (tm, tm), 1)
    lower = jnp.where(c_idx < r_idx, 1.0, 0.0).astype(BF16)
    before = jnp.dot(lower, onehot.astype(BF16), preferred_element_type=F32) + cnt_ref[...]
    rank1 = jnp.sum(jnp.where(hot1, before, 0.0), axis=-1, keepdims=True)
    rank2 = jnp.sum(jnp.where(hot2, before, 0.0), axis=-1, keepdims=True)
    cnt_ref[...] = cnt_ref[...] + jnp.sum(onehot, axis=0, keepdims=True)

    rec = jnp.zeros((tm, LANES), F32)
    for pos, val in ((ROUTE_EID, (i1 - ROUTER_EXPERT_LANE0).astype(F32)),
                     (ROUTE_EID + 1, (i2 - ROUTER_EXPERT_LANE0).astype(F32)),
                     (ROUTE_RANK, rank1), (ROUTE_RANK + 1, rank2),
                     (ROUTE_GATE, gate1), (ROUTE_GATE + 1, gate2)):
        rec = jnp.where(lane == pos, val, rec)
    route_ref[...] = rec
    route_t_ref[...] = rec.T[:route_t_ref.shape[0], :]


def _merge_route(y_att, y_ret, gates, x2, wa, wb, wo, nw, wr_hi, wr_lo, b_r, T, S):
    tm = TM_MERGE
    nt = S // tm
    row = lambda i: (i, 0)
    const = lambda i: (0, 0)
    full = lambda arr: pl.BlockSpec(arr.shape, const)
    in_specs = ([pl.BlockSpec((1, y_att.shape[1], tm, LANES), lambda i: (i // nt, 0, i % nt, 0)),
                 pl.BlockSpec((tm, RET_V_W), row), pl.BlockSpec((tm, 2 * D_MODEL), row),
                 pl.BlockSpec((tm, D_MODEL), row),
                 full(wa), full(wb), full(wo), full(nw), full(wr_hi), full(wr_lo), full(b_r)])
    return pl.pallas_call(
        _merge_body,
        grid=(T // tm,),
        in_specs=in_specs,
        out_specs=[pl.BlockSpec((tm, D_MODEL), row),
                   pl.BlockSpec((tm, ROW_TILES, LANES), lambda i: (i, 0, 0)),
                   pl.BlockSpec((tm, LANES), row), pl.BlockSpec((ROUTE_FIELDS, tm), lambda i: (0, i)),
                   pl.BlockSpec((1, LANES), const)],
        out_shape=[jax.ShapeDtypeStruct((T, D_MODEL), F32), jax.ShapeDtypeStruct((T, ROW_TILES, LANES), F32),
                   jax.ShapeDtypeStruct((T, LANES), F32), jax.ShapeDtypeStruct((ROUTE_FIELDS, T), F32),
                   jax.ShapeDtypeStruct((1, LANES), F32)],
        compiler_params=_cparams(("arbitrary",)),
        name="merge_route",
    )(y_att, y_ret, gates, x2, wa, wb, wo, nw, wr_hi, wr_lo, b_r)


ISSUE_UNROLL = 8


def _dispatch_body(slot_ref, hn_ref, xs_ref, sem, *, T):
    i = pl.program_id(0)
    ch = hn_ref.shape[0]

    def row_copy(j, slot):
        return pltpu.make_async_copy(hn_ref.at[j], xs_ref.at[slot], sem)

    def issue(j, carry):
        t = i * ch + j
        row_copy(j, slot_ref[t]).start()
        row_copy(j, slot_ref[T + t]).start()
        return carry

    lax.fori_loop(0, ch, issue, 0, unroll=ISSUE_UNROLL)
    for _ in range(2):
        pltpu.make_async_copy(hn_ref, xs_ref.at[pl.ds(0, ch)], sem).wait()


def _dispatch(slots, hn, n_slots, T):
    ch = DISPATCH_CHUNK
    grid_spec = pltpu.PrefetchScalarGridSpec(
        num_scalar_prefetch=1,
        grid=(T // ch,),
        in_specs=[pl.BlockSpec((ch, ROW_TILES, LANES), lambda i, s: (i, 0, 0))],
        out_specs=pl.BlockSpec(memory_space=pl.ANY),
        scratch_shapes=[pltpu.SemaphoreType.DMA(())],
    )
    return pl.pallas_call(
        functools.partial(_dispatch_body, T=T),
        grid_spec=grid_spec,
        out_shape=jax.ShapeDtypeStruct((n_slots, ROW_TILES, LANES), F32),
        compiler_params=_cparams(("arbitrary",)),
        name="moe_dispatch",
    )(slots, hn)


def _expert_body(blk_ref, eid_ref, valid_ref, fresh_ref, x_ref, w1_ref, w3_ref, w2_ref, y_ref,
                 w1b, w3b, w2b):
    i = pl.program_id(0)
    valid = valid_ref[i]

    @pl.when(valid > 0)
    def _():
        @pl.when(fresh_ref[i] == 1)
        def _():
            w1b[...] = w1_ref[0].astype(BF16)
            w3b[...] = w3_ref[0].astype(BF16)
            w2b[...] = w2_ref[0].astype(BF16)

        bm = x_ref.shape[0]
        live = lax.broadcasted_iota(jnp.int32, (bm, LANES), 0) < valid
        x = jnp.concatenate(
            [jnp.where(live, x_ref[:, j, :], 0.0).astype(BF16) for j in range(ROW_TILES)], axis=1)
        a = jnp.dot(x, w1b[...], preferred_element_type=F32)
        b = jnp.dot(x, w3b[...], preferred_element_type=F32)
        hid = (a * jax.nn.sigmoid(a) * b).astype(BF16)
        y = jnp.dot(hid, w2b[...], preferred_element_type=F32)
        for j in range(ROW_TILES):
            y_ref[:, j, :] = y[:, j * LANES:(j + 1) * LANES]


def _experts(blk, blk_eid, blk_valid, blk_fresh, x_slots, w1, w3, w2, n_blocks):
    bm = MOE_BM
    grid_spec = pltpu.PrefetchScalarGridSpec(
        num_scalar_prefetch=4,
        grid=(n_blocks,),
        in_specs=[
            pl.BlockSpec((bm, ROW_TILES, LANES), lambda i, blk, eid, val, fr: (blk[i], 0, 0)),
            pl.BlockSpec((1, D_MODEL, MOE_HIDDEN), lambda i, blk, eid, val, fr: (eid[i], 0, 0)),
            pl.BlockSpec((1, D_MODEL, MOE_HIDDEN), lambda i, blk, eid, val, fr: (eid[i], 0, 0)),
            pl.BlockSpec((1, MOE_HIDDEN, D_MODEL), lambda i, blk, eid, val, fr: (eid[i], 0, 0)),
        ],
        out_specs=pl.BlockSpec((bm, ROW_TILES, LANES), lambda i, blk, eid, val, fr: (blk[i], 0, 0)),
        scratch_shapes=[pltpu.VMEM((D_MODEL, MOE_HIDDEN), BF16), pltpu.VMEM((D_MODEL, MOE_HIDDEN), BF16),
                        pltpu.VMEM((MOE_HIDDEN, D_MODEL), BF16)],
    )
    return pl.pallas_call(
        _expert_body,
        grid_spec=grid_spec,
        out_shape=jax.ShapeDtypeStruct(x_slots.shape, F32),
        compiler_params=_cparams(("arbitrary",)),
        name="moe_experts",
    )(blk, blk_eid, blk_valid, blk_fresh, x_slots, w1, w3, w2)


def _combine_body(slot_ref, ys_ref, h_ref, route_ref, nw_ref, o_ref, ybuf, sem, *, T):
    i = pl.program_id(0)
    n = pl.num_programs(0)
    tm = h_ref.shape[0]

    def row_copy(slot, buf, k, j):
        return pltpu.make_async_copy(ys_ref.at[slot], ybuf.at[buf, k, j], sem.at[buf])

    def issue(tile, buf):
        def one(j, carry):
            t = tile * tm + j
            row_copy(slot_ref[t], buf, 0, j).start()
            row_copy(slot_ref[T + t], buf, 1, j).start()
            return carry
        lax.fori_loop(0, tm, one, 0, unroll=ISSUE_UNROLL)

    @pl.when(i == 0)
    def _():
        issue(0, 0)

    @pl.when(i + 1 < n)
    def _():
        issue(i + 1, (i + 1) % 2)

    buf = i % 2
    for k in range(2):
        pltpu.make_async_copy(ys_ref.at[pl.ds(0, tm)], ybuf.at[buf, k], sem.at[buf]).wait()
    route = route_ref[...]
    g1 = route[:, ROUTE_GATE:ROUTE_GATE + 1]
    g2 = route[:, ROUTE_GATE + 1:ROUTE_GATE + 2]
    hs = []
    ss = jnp.zeros((tm, 1), F32)
    for j in range(ROW_TILES):
        hj = h_ref[:, j * LANES:(j + 1) * LANES] + (ybuf[buf, 0, :, j, :] * g1 + ybuf[buf, 1, :, j, :] * g2)
        hs.append(hj)
        ss = ss + jnp.sum(hj * hj, axis=-1, keepdims=True)
    inv = lax.rsqrt(ss * (1.0 / D_MODEL) + NORM_EPS)
    for j, hj in enumerate(hs):
        cols = slice(j * LANES, (j + 1) * LANES)
        o_ref[:, cols] = hj * inv * nw_ref[:, cols]


def _combine(slots, y_slots, h, route, nw, T):
    tm = TM_COMBINE
    row = lambda i, s: (i, 0)
    grid_spec = pltpu.PrefetchScalarGridSpec(
        num_scalar_prefetch=1,
        grid=(T // tm,),
        in_specs=[pl.BlockSpec(memory_space=pl.ANY),
                  pl.BlockSpec((tm, D_MODEL), row),
                  pl.BlockSpec((tm, LANES), row),
                  pl.BlockSpec((1, D_MODEL), lambda i, s: (0, 0))],
        out_specs=pl.BlockSpec((tm, D_MODEL), row),
        scratch_shapes=[pltpu.VMEM((2, 2, tm, ROW_TILES, LANES), F32), pltpu.SemaphoreType.DMA((2,))],
    )
    return pl.pallas_call(
        functools.partial(_combine_body, T=T),
        grid_spec=grid_spec,
        out_shape=jax.ShapeDtypeStruct((T, D_MODEL), F32),
        compiler_params=_cparams(("arbitrary",)),
        name="moe_combine",
    )(slots, y_slots, h, route, nw)


def _rotary_tables(S):
    inv_freq = 1.0 / (ROPE_THETA ** (jnp.arange(0, HEAD_DIM, 2, dtype=F32) / HEAD_DIM))
    ang = jnp.arange(S, dtype=F32)[:, None] * inv_freq[None, :]
    cos, sin = jnp.cos(ang), jnp.sin(ang)
    reps = LANES // HEAD_DIM
    cos_t = jnp.tile(jnp.concatenate([cos, cos], axis=1), (1, reps))
    sin_t = jnp.tile(jnp.concatenate([-sin, sin], axis=1), (1, reps))
    return cos_t, sin_t


def _layer(h_in, norm_mix_w, w_in, b_branch_gate, ret_decay_fwd, ret_decay_bwd, ret_gn_w, w_attn_branch,
           w_ret_branch, w_out, norm_moe_w, moe_w_group, moe_b_group, moe_w_expert, moe_b_expert,
           moe_w1, moe_w3, moe_w2, next_norm_w, B, S, cos_t, sin_t):
    T = B * S
    (qa0, ka0, va0, qa1, ka1, va1, qa2, ka2, va2, qr, kr, vr, gr, gates) = _in_projection(
        h_in, norm_mix_w[None, :], w_in.astype(BF16), b_branch_gate[None, :], cos_t, sin_t, B, S)

    unit = lambda a: a[:, :, None]
    y_att = _attention((unit(qa0), unit(ka0), unit(va0), qa1, ka1, va1, qa2, ka2, va2), B, S)

    dec = jnp.stack([ret_decay_fwd.reshape(RET_HEADS // 2, 2), ret_decay_bwd.reshape(RET_HEADS // 2, 2)], axis=1)
    dec = jnp.broadcast_to(dec.reshape(RET_HEADS // 2, 4, 1), (RET_HEADS // 2, 4, LANES)).astype(F32)
    y_ret = _retention(dec, qr, kr, vr, gr, ret_gn_w[None, :], B, S).reshape(T, RET_V_W)

    pad = LANES - MOE_GROUPS - MOE_N_EXPERTS
    w_r = jnp.concatenate([moe_w_group, moe_w_expert, jnp.zeros((D_MODEL, pad), F32)], axis=1)
    w_r_hi = w_r.astype(BF16)
    w_r_lo = (w_r - w_r_hi.astype(F32)).astype(BF16)
    b_r = jnp.concatenate([moe_b_group, moe_b_expert, jnp.zeros((pad,), F32)])[None, :]

    h_mid, hn, route, route_t, cnt = _merge_route(
        y_att, y_ret, gates, h_in, w_attn_branch.astype(BF16), w_ret_branch.astype(BF16),
        w_out.astype(BF16), norm_moe_w[None, :], w_r_hi, w_r_lo, b_r, T, S)

    bm = MOE_BM
    counts = cnt[0, ROUTER_EXPERT_LANE0:ROUTER_EXPERT_LANE0 + MOE_N_EXPERTS].astype(jnp.int32)
    nblk = (counts + bm - 1) // bm
    blk_end = jnp.cumsum(nblk)
    pstart = (blk_end - nblk) * bm
    n_blocks = (2 * T) // bm + MOE_N_EXPERTS
    n_active = blk_end[-1]
    bidx = jnp.minimum(jnp.arange(n_blocks, dtype=jnp.int32), n_active - 1)
    blk_eid = jnp.sum(bidx[:, None] >= blk_end[None, :], axis=1).astype(jnp.int32)
    blk_valid = jnp.clip(counts[blk_eid] - (bidx * bm - pstart[blk_eid]), 0, bm)
    blk_valid = jnp.where(jnp.arange(n_blocks) < n_active, blk_valid, 0).astype(jnp.int32)
    blk_fresh = jnp.concatenate([jnp.ones((1,), jnp.int32), (blk_eid[1:] != blk_eid[:-1]).astype(jnp.int32)])
    eid = route_t[ROUTE_EID:ROUTE_EID + 2].astype(jnp.int32)
    rank = route_t[ROUTE_RANK:ROUTE_RANK + 2].astype(jnp.int32)
    start = jnp.sum(jnp.where(eid[..., None] == jnp.arange(MOE_N_EXPERTS, dtype=jnp.int32),
                              pstart.astype(jnp.int32), 0), axis=-1)
    slots = (start + rank).reshape(2 * T)

    x_slots = _dispatch(slots, hn, n_blocks * bm, T)
    y_slots = _experts(bidx, blk_eid, blk_valid, blk_fresh, x_slots, moe_w1, moe_w3, moe_w2, n_blocks)
    return _combine(slots, y_slots, h_mid, route, next_norm_w[None, :], T)


def kernel(x, norm_mix_w, w_in, b_branch_gate, ret_decay_fwd, ret_decay_bwd, ret_gn_w, w_attn_branch,
           w_ret_branch, w_out, norm_moe_w, moe_w_group, moe_b_group, moe_w_expert, moe_b_expert, moe_w1,
           moe_w3, moe_w2, norm_final_w):
    B, S, D = x.shape
    depth = norm_mix_w.shape[0]
    assert depth == 1, "the final norm is fused into the layer's combine stage"
    assert D == D_MODEL and S % TM_INPROJ == 0 and (B * S) < (1 << 24)
    cos_t, sin_t = _rotary_tables(S)
    out = _layer(x.reshape(B * S, D), norm_mix_w[0], w_in[0], b_branch_gate[0], ret_decay_fwd[0],
                 ret_decay_bwd[0], ret_gn_w[0], w_attn_branch[0], w_ret_branch[0], w_out[0], norm_moe_w[0],
                 moe_w_group[0], moe_b_group[0], moe_w_expert[0], moe_b_expert[0], moe_w1[0], moe_w3[0],
                 moe_w2[0], norm_final_w, B, S, cos_t, sin_t)
    return out.reshape(B, S, D)
```

```python
import functools

import numpy as np
import jax
import jax.numpy as jnp
from jax import lax
from jax.experimental import pallas as pl
from jax.experimental.pallas import tpu as pltpu

F32 = jnp.float32
BF16 = jnp.bfloat16

D_MODEL = 1024
HEAD_DIM = 64
ATTN_PAIRS = ((128, 1), (512, 4), (2048, 16))
ATTN_HEADS_PER_GROUP = 8
ATTN_GROUP_W = ATTN_HEADS_PER_GROUP * HEAD_DIM
ATTN_HALF = 64
ROPE_THETA = 10000.0
RET_HEADS = 8
RET_QK_DIM = 64
RET_V_DIM = 128
RET_CHUNK = 128
RET_QK_W = RET_HEADS * RET_QK_DIM
RET_V_W = RET_HEADS * RET_V_DIM
MOE_GROUPS = 8
MOE_EXPERTS_PER_GROUP = 8
MOE_N_EXPERTS = MOE_GROUPS * MOE_EXPERTS_PER_GROUP
MOE_HIDDEN = 512
NORM_EPS = 1e-6

LANES = 128
ROW_TILES = D_MODEL // LANES
NEG_BIG = -1e30

TM_INPROJ = 512
TM_MERGE = 512
TM_COMBINE = 256
MOE_BM = 256
DISPATCH_CHUNK = 512
ATTN_QB = 128
ATTN_UNROLL = 8
RET_UNROLL = 4

VMEM_LIMIT = 56 * 1024 * 1024

_A = 3 * ATTN_GROUP_W
OFF_QA, OFF_KA, OFF_VA = 0, _A, 2 * _A
OFF_QR = 3 * _A
OFF_KR = OFF_QR + RET_QK_W
OFF_VR = OFF_KR + RET_QK_W
OFF_GR = OFF_VR + RET_V_W
OFF_GL = OFF_GR + RET_V_W
IN_W = OFF_GL + 2 * D_MODEL


def _cparams(sem, vmem=VMEM_LIMIT):
    return pltpu.CompilerParams(dimension_semantics=sem, vmem_limit_bytes=vmem)


def _inproj_body(x_ref, nw_ref, w_ref, bg_ref, cos_ref, sin_ref,
                 qa0, ka0, va0, qa1, ka1, va1, qa2, ka2, va2, qr, kr, vr, gr, gt,
                 stage_ref):
    tm = x_ref.shape[0]
    x = x_ref[...]
    ms = jnp.mean(x * x, axis=-1, keepdims=True)
    xn = (x * lax.rsqrt(ms + NORM_EPS) * nw_ref[...]).astype(BF16)
    cos = cos_ref[...]
    sin = sin_ref[...]
    lane = lax.broadcasted_iota(jnp.int32, (tm, LANES), 1)
    first_half = (lane & (HEAD_DIM - 1)) < (HEAD_DIM // 2)

    def proj(c0, width):
        return jnp.dot(xn, w_ref[:, c0:c0 + width], preferred_element_type=F32)

    def rotary(a, scale):
        partner = jnp.where(first_half, pltpu.roll(a, LANES - HEAD_DIM // 2, 1),
                            pltpu.roll(a, HEAD_DIM // 2, 1))
        r = a * cos + partner * sin
        return r * scale if scale != 1.0 else r

    def chunks(acc):
        return [acc[:, c * LANES:(c + 1) * LANES] for c in range(acc.shape[1] // LANES)]

    def store_natural(out_ref, acc, fn):
        for c, a in enumerate(chunks(acc)):
            out_ref[:, c * LANES:(c + 1) * LANES] = fn(a).astype(out_ref.dtype)

    def store_pairs(out_ref, acc, fn):
        for c, a in enumerate(chunks(acc)):
            out_ref[0, c] = fn(a).astype(out_ref.dtype)

    def store_strided(out_ref, acc, fn, d):
        for c, a in enumerate(chunks(acc)):
            stage_ref[c] = fn(a)
        for c in range(acc.shape[1] // LANES):
            for r in range(d):
                out_ref[0, c, r] = stage_ref[c, pl.ds(r, tm // d, stride=d), :].astype(out_ref.dtype)

    ident = lambda a: a
    rot_q = lambda a: rotary(a, HEAD_DIM ** -0.5)
    rot_1 = lambda a: rotary(a, 1.0)
    rot_k = lambda a: rotary(a, RET_QK_DIM ** -0.5)

    W = ATTN_GROUP_W
    store_pairs(qa0, proj(OFF_QA, W), rot_q)
    store_pairs(ka0, proj(OFF_KA, W), rot_1)
    store_pairs(va0, proj(OFF_VA, W), ident)
    for g, (qo, ko, vo) in ((1, (qa1, ka1, va1)), (2, (qa2, ka2, va2))):
        d = ATTN_PAIRS[g][1]
        store_strided(qo, proj(OFF_QA + g * W, W), rot_q, d)
        store_strided(ko, proj(OFF_KA + g * W, W), rot_1, d)
        store_strided(vo, proj(OFF_VA + g * W, W), ident, d)
    store_natural(qr, proj(OFF_QR, RET_QK_W), rot_1)
    store_natural(kr, proj(OFF_KR, RET_QK_W), rot_k)
    for h in range(RET_V_W // W):
        vr[:, h * W:(h + 1) * W] = proj(OFF_VR + h * W, W).astype(vr.dtype)
        gr[:, h * W:(h + 1) * W] = proj(OFF_GR + h * W, W).astype(gr.dtype)
    for h in range(2 * D_MODEL // W):
        z = proj(OFF_GL + h * W, W) + bg_ref[:, h * W:(h + 1) * W]
        gt[:, h * W:(h + 1) * W] = jax.nn.sigmoid(z).astype(gt.dtype)


def _in_projection(x2, norm_w, w_bf, b_gate, cos_t, sin_t, B, S):
    T = B * S
    tm = TM_INPROJ
    nt = S // tm
    W = ATTN_GROUP_W
    row = lambda i: (i, 0)
    const = lambda i: (0, 0)
    nat = lambda width: pl.BlockSpec((tm, width), row)

    P = W // LANES

    def strided_spec(d):
        return pl.BlockSpec((1, P, d, tm // d, LANES), lambda i: (i // nt, 0, 0, i % nt, 0))

    def strided_shape(d):
        return jax.ShapeDtypeStruct((B, P, d, S // d, LANES), BF16)

    pair_spec = pl.BlockSpec((1, P, tm, LANES), lambda i: (i // nt, 0, i % nt, 0))
    pair_shape = jax.ShapeDtypeStruct((B, P, S, LANES), BF16)
    nat_shape = lambda width: jax.ShapeDtypeStruct((T, width), BF16)
    d1, d2 = ATTN_PAIRS[1][1], ATTN_PAIRS[2][1]
    out_shape = ([pair_shape] * 3 + [strided_shape(d1)] * 3 + [strided_shape(d2)] * 3
                 + [nat_shape(RET_QK_W)] * 2 + [nat_shape(RET_V_W)] * 2 + [nat_shape(2 * D_MODEL)])
    out_specs = ([pair_spec] * 3 + [strided_spec(d1)] * 3 + [strided_spec(d2)] * 3
                 + [nat(RET_QK_W)] * 2 + [nat(RET_V_W)] * 2 + [nat(2 * D_MODEL)])
    in_specs = [
        pl.BlockSpec((tm, D_MODEL), row),
        pl.BlockSpec((1, D_MODEL), const),
        pl.BlockSpec((D_MODEL, IN_W), const, pipeline_mode=pl.Buffered(1)),
        pl.BlockSpec((1, 2 * D_MODEL), const),
        pl.BlockSpec((tm, LANES), lambda i: (i % nt, 0)),
        pl.BlockSpec((tm, LANES), lambda i: (i % nt, 0)),
    ]
    return pl.pallas_call(
        _inproj_body,
        grid=(T // tm,),
        in_specs=in_specs,
        out_specs=out_specs,
        out_shape=out_shape,
        scratch_shapes=[pltpu.VMEM((W // LANES, tm, LANES), F32)],
        compiler_params=_cparams(("parallel",)),
        name="in_projection",
    )(x2, norm_w, w_bf, b_gate, cos_t, sin_t)


def _attn_body(q0_ref, k0_ref, v0_ref, q1_ref, k1_ref, v1_ref, q2_ref, k2_ref, v2_ref, o_ref,
               va_ref, vb_ref, part_ref, bias_ref, *, S):
    QB, H = ATTN_QB, ATTN_HALF
    lane = lax.broadcasted_iota(jnp.int32, (QB, LANES), 1)
    head0 = lane < HEAD_DIM
    ones = jnp.ones((QB, LANES), BF16)

    qi = lax.broadcasted_iota(jnp.int32, (QB, QB + 2 * H), 0)
    ki = lax.broadcasted_iota(jnp.int32, (QB, QB + 2 * H), 1)
    for n in range(3):
        bias_ref[n] = jnp.where(jnp.abs(ki - qi - n * H) <= H, 0.0, NEG_BIG).astype(F32)

    def scores(q_rows, k_rows, bias, h):
        qm = jnp.where(head0 if h == 0 else jnp.logical_not(head0), q_rows, jnp.zeros_like(q_rows))
        return lax.dot_general(qm, k_rows, (((1,), (1,)), ((), ())), preferred_element_type=F32) + bias

    def weights(s):
        m = jnp.max(s, axis=-1, keepdims=True)
        return m, jnp.exp(s - m).astype(BF16)

    def heads_to_lanes(m0, a, m1, b):
        num = jnp.where(head0, a, b)
        den = pltpu.roll(jnp.where(head0, b, a), HEAD_DIM, 1)
        mx = jnp.where(head0, m0, m1)
        return num, den, mx

    def run_group(q_ref, k_ref, v_ref, d, prepare, store):
        L = S // d
        KW = min(L, QB + 2 * H)
        nb = L // QB

        def fill(t, carry):
            r = t // nb
            rows = pl.ds(pl.multiple_of((t % nb) * QB, QB), QB)
            v = v_ref[0, 0, r, rows, :]
            dst = pl.ds(pl.multiple_of(t * QB, QB), QB)
            va_ref[dst, :] = jnp.where(head0, v, ones)
            vb_ref[dst, :] = jnp.where(head0, ones, v)
            return carry

        lax.fori_loop(0, S // QB, fill, 0)

        def trip(i, carry):
            blocks = []
            for u in range(ATTN_UNROLL):
                t = i * ATTN_UNROLL + u
                r = t // nb
                q0 = pl.multiple_of((t % nb) * QB, QB)
                ws = pl.multiple_of(jnp.clip(q0 - H, 0, L - KW), H)
                bias = bias_ref[(q0 - ws) // H][:, :KW]
                q_rows = q_ref[0, 0, r, pl.ds(q0, QB), :]
                k_rows = k_ref[0, 0, r, pl.ds(ws, KW), :]
                vrows = pl.ds(pl.multiple_of(r * L + ws, H), KW)
                blocks.append((r, q0, vrows, [scores(q_rows, k_rows, bias, h) for h in range(2)]))
            blocks = [(r, q0, vrows, [weights(s) for s in ss]) for r, q0, vrows, ss in blocks]
            done = []
            for r, q0, vrows, ((m0, p0), (m1, p1)) in blocks:
                a = jnp.dot(p0, va_ref[vrows, :], preferred_element_type=F32)
                b = jnp.dot(p1, vb_ref[vrows, :], preferred_element_type=F32)
                done.append((r, q0, m0, a, m1, b))
            done = [(r, q0, prepare(q0, *heads_to_lanes(m0, a, m1, b))) for r, q0, m0, a, m1, b in done]
            for r, q0, vals in done:
                store(r, q0, vals)
            return carry

        lax.fori_loop(0, S // QB // ATTN_UNROLL, trip, 0)

    def store_partial(g):
        d = ATTN_PAIRS[g][1]

        def store(r, q0, vals):
            rows = pl.ds(r + q0 * d, QB, stride=d)
            for n, val in enumerate(vals):
                part_ref[3 * (g - 1) + n, rows, :] = val
        return store

    keep = lambda q0, num, den, mx: (num, den, mx)
    run_group(q1_ref, k1_ref, v1_ref, ATTN_PAIRS[1][1], keep, store_partial(1))
    run_group(q2_ref, k2_ref, v2_ref, ATTN_PAIRS[2][1], keep, store_partial(2))

    def merge(q0, num, den, mx):
        rows = pl.ds(q0, QB)
        nums = [num, part_ref[0, rows, :], part_ref[3, rows, :]]
        dens = [den, part_ref[1, rows, :], part_ref[4, rows, :]]
        mxs = [mx, part_ref[2, rows, :], part_ref[5, rows, :]]
        top = jnp.maximum(jnp.maximum(mxs[0], mxs[1]), mxs[2])
        ws = [jnp.exp(m - top) for m in mxs]
        n = ws[0] * nums[0] + ws[1] * nums[1] + ws[2] * nums[2]
        dn = ws[0] * dens[0] + ws[1] * dens[1] + ws[2] * dens[2]
        return (n / dn).astype(o_ref.dtype)

    def store_out(r, q0, y):
        o_ref[0, 0, pl.ds(q0, QB), :] = y

    run_group(q0_ref, k0_ref, v0_ref, ATTN_PAIRS[0][1], merge, store_out)


def _attention(qkv, B, S):
    P = ATTN_GROUP_W // LANES
    in_specs = []
    for _, d in ATTN_PAIRS:
        in_specs += [pl.BlockSpec((1, 1, d, S // d, LANES), lambda b, p: (b, p, 0, 0, 0))] * 3
    return pl.pallas_call(
        functools.partial(_attn_body, S=S),
        grid=(B, P),
        in_specs=in_specs,
        out_specs=pl.BlockSpec((1, 1, S, LANES), lambda b, p: (b, p, 0, 0)),
        out_shape=jax.ShapeDtypeStruct((B, P, S, LANES), BF16),
        scratch_shapes=[pltpu.VMEM((S, LANES), BF16), pltpu.VMEM((S, LANES), BF16),
                        pltpu.VMEM((6, S, LANES), F32),
                        pltpu.VMEM((3, ATTN_QB, ATTN_QB + 2 * ATTN_HALF), F32)],
        compiler_params=_cparams(("parallel", "parallel")),
        name="attention",
    )(*qkv)


def _log_sigmoid(z):
    return jnp.minimum(z, 0.0) - jnp.log(1.0 + jnp.exp(-jnp.abs(z)))


def _ret_body(dec_ref, q_ref, k_ref, v_ref, g_ref, gnw_ref, o_ref, kt_ref, sf_ref, sb_ref, *, S):
    C = RET_CHUNK
    nc = S // C
    lg = _log_sigmoid(dec_ref[0])
    a_row = lax.broadcasted_iota(jnp.int32, (C, LANES), 0).astype(F32)
    lane = lax.broadcasted_iota(jnp.int32, (C, LANES), 1)
    rel = (lax.broadcasted_iota(jnp.int32, (C, C), 0) - lax.broadcasted_iota(jnp.int32, (C, C), 1)).astype(F32)

    heads = []
    for h in range(2):
        lgf = lg[h:h + 1, :]
        lgb = lg[2 + h:3 + h, :]
        in_head = (lane < RET_QK_DIM) if h == 0 else (lane >= RET_QK_DIM)
        heads.append(dict(
            in_head=in_head,
            xi_f=jnp.where(in_head, jnp.exp((a_row + 1.0) * lgf), 0.0),
            xi_b=jnp.where(in_head, jnp.exp((C - a_row) * lgb), 0.0),
            zeta_f=jnp.exp((C - 1.0 - a_row) * lgf),
            zeta_b=jnp.exp(a_row * lgb),
            dloc=jnp.where(rel > 0, jnp.exp(rel * lgf), jnp.where(rel < 0, jnp.exp(-rel * lgb), 2.0)),
            cd_f=jnp.exp(C * lgf),
            cd_b=jnp.exp(C * lgb),
        ))

    def chunk_rows(n):
        return pl.ds(pl.multiple_of(n * C, C), C)

    def v_head(rows, h):
        return v_ref[0, rows, h * RET_V_DIM:(h + 1) * RET_V_DIM]

    def products(i, carry):
        items = []
        for u in range(RET_UNROLL):
            n = i * RET_UNROLL + u
            rows = chunk_rows(n)
            kt = k_ref[0, rows, :].astype(F32).T.astype(BF16)
            kt_ref[n] = kt
            for h, hd in enumerate(heads):
                vh = v_head(rows, h).astype(F32)
                items.append((n, h, kt, (vh * hd["zeta_f"]).astype(BF16), (vh * hd["zeta_b"]).astype(BF16)))
        outs = [(n, h, jnp.dot(kt, vf, preferred_element_type=F32), jnp.dot(kt, vb, preferred_element_type=F32))
                for n, h, kt, vf, vb in items]
        for n, h, f, b in outs:
            sf_ref[n, h] = f
            sb_ref[n, h] = b
        return carry

    lax.fori_loop(0, nc // RET_UNROLL, products, 0)

    def scan(ref, key, order):
        def step(i, state):
            n = order(i)
            new = []
            for h, hd in enumerate(heads):
                kv = ref[n, h]
                ref[n, h] = state[h]
                new.append(hd[key] * state[h] + kv)
            return tuple(new)
        zero = jnp.zeros(ref.shape[2:], F32)
        lax.fori_loop(0, nc, step, (zero, zero))

    scan(sf_ref, "cd_f", lambda i: i)
    scan(sb_ref, "cd_b", lambda i: nc - 1 - i)

    def outputs(i, carry):
        items = []
        for u in range(RET_UNROLL):
            n = i * RET_UNROLL + u
            rows = chunk_rows(n)
            qp = q_ref[0, rows, :]
            qf = qp.astype(F32)
            kt = kt_ref[n]
            for h, hd in enumerate(heads):
                qm = jnp.where(hd["in_head"], qp, jnp.zeros_like(qp))
                qx = jnp.concatenate([(qf * hd["xi_f"]).astype(BF16), (qf * hd["xi_b"]).astype(BF16)], axis=1)
                st = jnp.concatenate([sf_ref[n, h].astype(BF16), sb_ref[n, h].astype(BF16)], axis=0)
                items.append((rows, h, hd, jnp.dot(qm, kt, preferred_element_type=F32),
                              jnp.dot(qx, st, preferred_element_type=F32)))
        items = [(rows, h, hd, (s * hd["dloc"]).astype(BF16), cross) for rows, h, hd, s, cross in items]
        items = [(rows, h, cross + jnp.dot(p, v_head(rows, h), preferred_element_type=F32))
                 for rows, h, hd, p, cross in items]
        for rows, h, ret in items:
            mu = jnp.mean(ret, axis=-1, keepdims=True)
            xc = ret - mu
            var = jnp.mean(xc * xc, axis=-1, keepdims=True)
            cols = slice(h * RET_V_DIM, (h + 1) * RET_V_DIM)
            gate = g_ref[0, rows, cols].astype(F32)
            y = xc * lax.rsqrt(var + NORM_EPS) * gnw_ref[:, cols] * (gate * jax.nn.sigmoid(gate))
            o_ref[0, rows, cols] = y.astype(o_ref.dtype)
        return carry

    lax.fori_loop(0, nc // RET_UNROLL, outputs, 0)


def _retention(dec, qr, kr, vr, gr, gn_w, B, S):
    nc = S // RET_CHUNK
    npairs = RET_HEADS // 2
    qk_spec = pl.BlockSpec((1, S, 2 * RET_QK_DIM), lambda b, p: (b, 0, p))
    v_spec = pl.BlockSpec((1, S, 2 * RET_V_DIM), lambda b, p: (b, 0, p))
    return pl.pallas_call(
        functools.partial(_ret_body, S=S),
        grid=(B, npairs),
        in_specs=[pl.BlockSpec((1, 4, LANES), lambda b, p: (p, 0, 0)),
                  qk_spec, qk_spec, v_spec, v_spec,
                  pl.BlockSpec((1, 2 * RET_V_DIM), lambda b, p: (0, p))],
        out_specs=v_spec,
        out_shape=jax.ShapeDtypeStruct((B, S, RET_V_W), BF16),
        scratch_shapes=[pltpu.VMEM((nc, 2 * RET_QK_DIM, RET_CHUNK), BF16),
                        pltpu.VMEM((nc, 2, 2 * RET_QK_DIM, RET_V_DIM), F32),
                        pltpu.VMEM((nc, 2, 2 * RET_QK_DIM, RET_V_DIM), F32)],
        compiler_params=_cparams(("parallel", "parallel")),
        name="retention",
    )(dec, qr.reshape(B, S, RET_QK_W), kr.reshape(B, S, RET_QK_W),
      vr.reshape(B, S, RET_V_W), gr.reshape(B, S, RET_V_W), gn_w)


ROUTE_EID, ROUTE_RANK, ROUTE_GATE = 0, 2, 4
ROUTE_FIELDS = 8
ROUTER_EXPERT_LANE0 = MOE_GROUPS


def _merge_body(yatt_ref, yret_ref, gt_ref, x_ref, wa_ref, wb_ref, wo_ref,
                nw_ref, wrh_ref, wrl_ref, br_ref,
                h_ref, hn_ref, route_ref, route_t_ref, cnt_ref):
    tm = x_ref.shape[0]
    i = pl.program_id(0)

    y_att = jnp.concatenate([yatt_ref[0, c] for c in range(yatt_ref.shape[1])], axis=1)
    a = jnp.dot(y_att, wa_ref[...], preferred_element_type=F32)
    b = jnp.dot(yret_ref[...], wb_ref[...], preferred_element_type=F32)
    merged = gt_ref[:, :D_MODEL].astype(F32) * a + gt_ref[:, D_MODEL:].astype(F32) * b
    mix = jnp.dot(merged.astype(BF16), wo_ref[...], preferred_element_type=F32)
    h = x_ref[...] + mix
    h_ref[...] = h
    ms = jnp.mean(h * h, axis=-1, keepdims=True)
    hn = h * lax.rsqrt(ms + NORM_EPS) * nw_ref[...]
    for j in range(ROW_TILES):
        hn_ref[pl.ds(j, tm, stride=ROW_TILES), :] = hn[:, j * LANES:(j + 1) * LANES]

    hi = hn.astype(BF16)
    lo = (hn - hi.astype(F32)).astype(BF16)
    logits = (jnp.dot(hi, wrh_ref[...], preferred_element_type=F32)
              + jnp.dot(hi, wrl_ref[...], preferred_element_type=F32)
              + jnp.dot(lo, wrh_ref[...], preferred_element_type=F32)) + br_ref[...]
    lane = lax.broadcasted_iota(jnp.int32, (tm, LANES), 1)
    far = jnp.int32(4 * LANES)

    def first_argmax(vals, vmax):
        return jnp.min(jnp.where(vals == vmax, lane, far), axis=-1, keepdims=True)

    is_group = lane < MOE_GROUPS
    gl = jnp.where(is_group, logits, NEG_BIG)
    gmax = jnp.max(gl, axis=-1, keepdims=True)
    g_w = 1.0 / jnp.sum(jnp.where(is_group, jnp.exp(gl - gmax), 0.0), axis=-1, keepdims=True)
    g_idx = first_argmax(gl, gmax)
    e_lane = lane - ROUTER_EXPERT_LANE0
    in_group = (e_lane >= 0) & (e_lane < MOE_N_EXPERTS) & (jnp.right_shift(e_lane, 3) == g_idx)
    el = jnp.where(in_group, logits, NEG_BIG)
    m1 = jnp.max(el, axis=-1, keepdims=True)
    i1 = first_argmax(el, m1)
    el2 = jnp.where(lane == i1, NEG_BIG, el)
    m2 = jnp.max(el2, axis=-1, keepdims=True)
    i2 = first_argmax(el2, m2)
    ex = jnp.exp(m2 - m1)
    gate1 = g_w / (1.0 + ex)
    gate2 = g_w * ex / (1.0 + ex)

    @pl.when(i == 0)
    def _():
        cnt_ref[...] = jnp.zeros(cnt_ref.shape, F32)

    hot1 = lane == i1
    hot2 = lane == i2
    onehot = jnp.where(hot1 | hot2, 1.0, 0.0)
    r_idx = lax.broadcasted_iota(jnp.int32, (tm, tm), 0)
    c_idx = lax.broadcasted_iota(jnp.int32, (tm, tm), 1)
    lower = jnp.where(c_idx < r_idx, 1.0, 0.0).astype(BF16)
    before = jnp.dot(lower, onehot.astype(BF16), preferred_element_type=F32) + cnt_ref[...]
    rank1 = jnp.sum(jnp.where(hot1, before, 0.0), axis=-1, keepdims=True)
    rank2 = jnp.sum(jnp.where(hot2, before, 0.0), axis=-1, keepdims=True)
    cnt_ref[...] = cnt_ref[...] + jnp.sum(onehot, axis=0, keepdims=True)

    rec = jnp.zeros((tm, LANES), F32)
    for pos, val in ((ROUTE_EID, (i1 - ROUTER_EXPERT_LANE0).astype(F32)),
                     (ROUTE_EID + 1, (i2 - ROUTER_EXPERT_LANE0).astype(F32)),
                     (ROUTE_RANK, rank1), (ROUTE_RANK + 1, rank2),
                     (ROUTE_GATE, gate1), (ROUTE_GATE + 1, gate2)):
        rec = jnp.where(lane == pos, val, rec)
    route_ref[...] = rec
    route_t_ref[...] = rec.T[:route_t_ref.shape[0], :]


def _merge_route(y_att, y_ret, gates, x2, wa, wb, wo, nw, wr_hi, wr_lo, b_r, T, S):
    tm = TM_MERGE
    nt = S // tm
    row = lambda i: (i, 0)
    const = lambda i: (0, 0)
    full = lambda arr: pl.BlockSpec(arr.shape, const)
    in_specs = ([pl.BlockSpec((1, y_att.shape[1], tm, LANES), lambda i: (i // nt, 0, i % nt, 0)),
                 pl.BlockSpec((tm, RET_V_W), row), pl.BlockSpec((tm, 2 * D_MODEL), row),
                 pl.BlockSpec((tm, D_MODEL), row),
                 full(wa), full(wb), full(wo), full(nw), full(wr_hi), full(wr_lo), full(b_r)])
    return pl.pallas_call(
        _merge_body,
        grid=(T // tm,),
        in_specs=in_specs,
        out_specs=[pl.BlockSpec((tm, D_MODEL), row),
                   pl.BlockSpec((tm * ROW_TILES, LANES), row),
                   pl.BlockSpec((tm, LANES), row), pl.BlockSpec((ROUTE_FIELDS, tm), lambda i: (0, i)),
                   pl.BlockSpec((1, LANES), const)],
        out_shape=[jax.ShapeDtypeStruct((T, D_MODEL), F32), jax.ShapeDtypeStruct((T * ROW_TILES, LANES), F32),
                   jax.ShapeDtypeStruct((T, LANES), F32), jax.ShapeDtypeStruct((ROUTE_FIELDS, T), F32),
                   jax.ShapeDtypeStruct((1, LANES), F32)],
        compiler_params=_cparams(("arbitrary",)),
        name="merge_route",
    )(y_att, y_ret, gates, x2, wa, wb, wo, nw, wr_hi, wr_lo, b_r)


ISSUE_UNROLL = 8


def _tile_rows(n):
    return pl.ds(pl.multiple_of(n * ROW_TILES, ROW_TILES), ROW_TILES)


def _dispatch_body(slot_ref, hn_ref, xs_ref, sem, *, T):
    i = pl.program_id(0)
    ch = hn_ref.shape[0] // ROW_TILES

    def row_copy(j, slot):
        return pltpu.make_async_copy(hn_ref.at[_tile_rows(j)], xs_ref.at[_tile_rows(slot)], sem)

    def issue(j, carry):
        t = i * ch + j
        row_copy(j, slot_ref[t]).start(priority=0)
        row_copy(j, slot_ref[T + t]).start(priority=1)
        return carry

    lax.fori_loop(0, ch, issue, 0, unroll=ISSUE_UNROLL)
    for _ in range(2):
        pltpu.make_async_copy(hn_ref, xs_ref.at[pl.ds(0, ch * ROW_TILES)], sem).wait()


def _dispatch(slots, hn, n_slots, T):
    ch = DISPATCH_CHUNK
    grid_spec = pltpu.PrefetchScalarGridSpec(
        num_scalar_prefetch=1,
        grid=(T // ch,),
        in_specs=[pl.BlockSpec((ch * ROW_TILES, LANES), lambda i, s: (i, 0))],
        out_specs=pl.BlockSpec(memory_space=pl.ANY),
        scratch_shapes=[pltpu.SemaphoreType.DMA(())],
    )
    return pl.pallas_call(
        functools.partial(_dispatch_body, T=T),
        grid_spec=grid_spec,
        out_shape=jax.ShapeDtypeStruct((n_slots * ROW_TILES, LANES), F32),
        compiler_params=_cparams(("arbitrary",)),
        name="moe_dispatch",
    )(slots, hn)


def _expert_body(blk_ref, eid_ref, valid_ref, fresh_ref, x_ref, w1_ref, w3_ref, w2_ref, y_ref,
                 w1b, w3b, w2b):
    i = pl.program_id(0)
    valid = valid_ref[i]

    @pl.when(valid > 0)
    def _():
        @pl.when(fresh_ref[i] == 1)
        def _():
            w1b[...] = w1_ref[0].astype(BF16)
            w3b[...] = w3_ref[0].astype(BF16)
            w2b[...] = w2_ref[0].astype(BF16)

        bm = x_ref.shape[0] // ROW_TILES
        live =lax.broadcasted_iota(jnp.int32, (bm, LANES), 0) < valid
        x = jnp.concatenate(
            [jnp.where(live, x_ref[pl.ds(j, bm, stride=ROW_TILES), :], 0.0).astype(BF16)
             for j in range(ROW_TILES)], axis=1)
        a = jnp.dot(x, w1b[...], preferred_element_type=F32)
        b = jnp.dot(x, w3b[...], preferred_element_type=F32)
        hid = (a * jax.nn.sigmoid(a) * b).astype(BF16)
        y = jnp.dot(hid, w2b[...], preferred_element_type=F32)
        for j in range(ROW_TILES):
            y_ref[pl.ds(j, bm, stride=ROW_TILES), :] = y[:, j * LANES:(j + 1) * LANES]


def _experts(blk, blk_eid, blk_valid, blk_fresh, x_slots, w1, w3, w2, n_blocks):
    bm = MOE_BM
    grid_spec = pltpu.PrefetchScalarGridSpec(
        num_scalar_prefetch=4,
        grid=(n_blocks,),
        in_specs=[
            pl.BlockSpec((bm * ROW_TILES, LANES), lambda i, blk, eid, val, fr: (blk[i], 0)),
            pl.BlockSpec((1, D_MODEL, MOE_HIDDEN), lambda i, blk, eid, val, fr: (eid[i], 0, 0)),
            pl.BlockSpec((1, D_MODEL, MOE_HIDDEN), lambda i, blk, eid, val, fr: (eid[i], 0, 0)),
            pl.BlockSpec((1, MOE_HIDDEN, D_MODEL), lambda i, blk, eid, val, fr: (eid[i], 0, 0)),
        ],
        out_specs=pl.BlockSpec((bm * ROW_TILES, LANES), lambda i, blk, eid, val, fr: (blk[i], 0)),
        scratch_shapes=[pltpu.VMEM((D_MODEL, MOE_HIDDEN), BF16), pltpu.VMEM((D_MODEL, MOE_HIDDEN), BF16),
                        pltpu.VMEM((MOE_HIDDEN, D_MODEL), BF16)],
    )
    return pl.pallas_call(
        _expert_body,
        grid_spec=grid_spec,
        out_shape=jax.ShapeDtypeStruct(x_slots.shape, F32),
        compiler_params=_cparams(("arbitrary",)),
        name="moe_experts",
    )(blk, blk_eid, blk_valid, blk_fresh, x_slots, w1, w3, w2)


def _combine_body(slot_ref, ys_ref, h_ref, route_ref, nw_ref, o_ref, ybuf, sem, *, T):
    i = pl.program_id(0)
    n = pl.num_programs(0)
    tm = h_ref.shape[0]

    def row_copy(slot, buf, k, j):
        return pltpu.make_async_copy(ys_ref.at[_tile_rows(slot)], ybuf.at[buf, k, _tile_rows(j)], sem.at[buf])

    def issue(tile, buf):
        def one(j, carry):
            t = tile * tm + j
            row_copy(slot_ref[t], buf, 0, j).start(priority=0)
            row_copy(slot_ref[T + t], buf, 1, j).start(priority=1)
            return carry
        lax.fori_loop(0, tm, one, 0, unroll=ISSUE_UNROLL)

    @pl.when(i == 0)
    def _():
        issue(0, 0)

    @pl.when(i + 1 < n)
    def _():
        issue(i + 1, (i + 1) % 2)

    buf = i % 2
    for k in range(2):
        pltpu.make_async_copy(ys_ref.at[pl.ds(0, tm * ROW_TILES)], ybuf.at[buf, k], sem.at[buf]).wait()
    route = route_ref[...]
    g1 = route[:, ROUTE_GATE:ROUTE_GATE + 1]
    g2 = route[:, ROUTE_GATE + 1:ROUTE_GATE + 2]
    hs = []
    ss = jnp.zeros((tm, 1), F32)
    for j in range(ROW_TILES):
        tile_row = pl.ds(j, tm, stride=ROW_TILES)
        hj = h_ref[:, j * LANES:(j + 1) * LANES] + (ybuf[buf, 0, tile_row, :] * g1 + ybuf[buf, 1, tile_row, :] * g2)
        hs.append(hj)
        ss = ss + jnp.sum(hj * hj, axis=-1, keepdims=True)
    inv = lax.rsqrt(ss * (1.0 / D_MODEL) + NORM_EPS)
    for j, hj in enumerate(hs):
        cols = slice(j * LANES, (j + 1) * LANES)
        o_ref[:, cols] = hj * inv * nw_ref[:, cols]


def _combine(slots, y_slots, h, route, nw, T):
    tm = TM_COMBINE
    row = lambda i, s: (i, 0)
    grid_spec = pltpu.PrefetchScalarGridSpec(
        num_scalar_prefetch=1,
        grid=(T // tm,),
        in_specs=[pl.BlockSpec(memory_space=pl.ANY),
                  pl.BlockSpec((tm, D_MODEL), row),
                  pl.BlockSpec((tm, LANES), row),
                  pl.BlockSpec((1, D_MODEL), lambda i, s: (0, 0))],
        out_specs=pl.BlockSpec((tm, D_MODEL), row),
        scratch_shapes=[pltpu.VMEM((2, 2, tm * ROW_TILES, LANES), F32), pltpu.SemaphoreType.DMA((2,))],
    )
    return pl.pallas_call(
        functools.partial(_combine_body, T=T),
        grid_spec=grid_spec,
        out_shape=jax.ShapeDtypeStruct((T, D_MODEL), F32),
        compiler_params=_cparams(("arbitrary",)),
        name="moe_combine",
    )(slots, y_slots, h, route, nw)


def _rotary_tables(S):
    inv_freq = (1.0 / (np.float32(ROPE_THETA) ** (np.arange(0, HEAD_DIM, 2, dtype=np.float32) / HEAD_DIM))
                ).astype(np.float32)
    ang = np.arange(S, dtype=np.float32)[:, None] * inv_freq[None, :]
    cos, sin = np.cos(ang), np.sin(ang)
    reps = LANES // HEAD_DIM
    cos_t = np.tile(np.concatenate([cos, cos], axis=1), (1, reps)).astype(np.float32)
    sin_t = np.tile(np.concatenate([-sin, sin], axis=1), (1, reps)).astype(np.float32)
    return jnp.asarray(cos_t), jnp.asarray(sin_t)


def _layer(h_in, norm_mix_w, w_in, b_branch_gate, ret_decay_fwd, ret_decay_bwd, ret_gn_w, w_attn_branch,
           w_ret_branch, w_out, norm_moe_w, moe_w_group, moe_b_group, moe_w_expert, moe_b_expert,
           moe_w1, moe_w3, moe_w2, next_norm_w, B, S, cos_t, sin_t):
    T = B * S
    (qa0, ka0, va0, qa1, ka1, va1, qa2, ka2, va2, qr, kr, vr, gr, gates) = _in_projection(
        h_in, norm_mix_w[None, :], w_in.astype(BF16), b_branch_gate[None, :], cos_t, sin_t, B, S)

    unit = lambda a: a[:, :, None]
    y_att = _attention((unit(qa0), unit(ka0), unit(va0), qa1, ka1, va1, qa2, ka2, va2), B, S)

    dec = jnp.stack([ret_decay_fwd.reshape(RET_HEADS // 2, 2), ret_decay_bwd.reshape(RET_HEADS // 2, 2)], axis=1)
    dec = jnp.broadcast_to(dec.reshape(RET_HEADS // 2, 4, 1), (RET_HEADS // 2, 4, LANES)).astype(F32)
    y_ret = _retention(dec, qr, kr, vr, gr, ret_gn_w[None, :], B, S).reshape(T, RET_V_W)

    pad = LANES - MOE_GROUPS - MOE_N_EXPERTS
    w_r = jnp.concatenate([moe_w_group, moe_w_expert, jnp.zeros((D_MODEL, pad), F32)], axis=1)
    w_r_hi = w_r.astype(BF16)
    w_r_lo = (w_r - w_r_hi.astype(F32)).astype(BF16)
    b_r = jnp.concatenate([moe_b_group, moe_b_expert, jnp.zeros((pad,), F32)])[None, :]

    h_mid, hn, route, route_t, cnt = _merge_route(
        y_att, y_ret, gates, h_in, w_attn_branch.astype(BF16), w_ret_branch.astype(BF16),
        w_out.astype(BF16), norm_moe_w[None, :], w_r_hi, w_r_lo, b_r, T, S)

    bm = MOE_BM
    counts = cnt[0, ROUTER_EXPERT_LANE0:ROUTER_EXPERT_LANE0 + MOE_N_EXPERTS].astype(jnp.int32)
    nblk = (counts + bm - 1) // bm
    blk_end = jnp.cumsum(nblk)
    pstart = (blk_end - nblk) * bm
    n_blocks = (2 * T) // bm + MOE_N_EXPERTS
    n_active = blk_end[-1]
    bidx = jnp.minimum(jnp.arange(n_blocks, dtype=jnp.int32), n_active - 1)
    blk_eid = jnp.sum(bidx[:, None] >= blk_end[None, :], axis=1).astype(jnp.int32)
    blk_valid = jnp.clip(counts[blk_eid] - (bidx * bm - pstart[blk_eid]), 0, bm)
    blk_valid = jnp.where(jnp.arange(n_blocks) < n_active, blk_valid, 0).astype(jnp.int32)
    blk_fresh = jnp.concatenate([jnp.ones((1,), jnp.int32), (blk_eid[1:] != blk_eid[:-1]).astype(jnp.int32)])
    eid = route_t[ROUTE_EID:ROUTE_EID + 2].astype(jnp.int32)
    rank = route_t[ROUTE_RANK:ROUTE_RANK + 2].astype(jnp.int32)
    start = jnp.sum(jnp.where(eid[..., None] == jnp.arange(MOE_N_EXPERTS, dtype=jnp.int32),
                              pstart.astype(jnp.int32), 0), axis=-1)
    slots = (start + rank).reshape(2 * T)

    x_slots = _dispatch(slots, hn, n_blocks * bm, T)
    y_slots = _experts(bidx, blk_eid, blk_valid, blk_fresh, x_slots, moe_w1, moe_w3, moe_w2, n_blocks)
    return _combine(slots, y_slots, h_mid, route, next_norm_w[None, :], T)


def kernel(x, norm_mix_w, w_in, b_branch_gate, ret_decay_fwd, ret_decay_bwd, ret_gn_w, w_attn_branch,
           w_ret_branch, w_out, norm_moe_w, moe_w_group, moe_b_group, moe_w_expert, moe_b_expert, moe_w1,
           moe_w3, moe_w2, norm_final_w):
    B, S, D = x.shape
    depth = norm_mix_w.shape[0]
    assert depth == 1, "the final norm is fused into the layer's combine stage"
    assert D == D_MODEL and S % TM_INPROJ == 0 and (B * S) < (1 << 24)
    cos_t, sin_t = _rotary_tables(S)
    out = _layer(x.reshape(B * S, D), norm_mix_w[0], w_in[0], b_branch_gate[0], ret_decay_fwd[0],
                 ret_decay_bwd[0], ret_gn_w[0], w_attn_branch[0], w_ret_branch[0], w_out[0], norm_moe_w[0],
                 moe_w_group[0], moe_b_group[0], moe_w_expert[0], moe_b_expert[0], moe_w1[0], moe_w3[0],
                 moe_w2[0], norm_final_w, B, S, cos_t, sin_t)
    return out.reshape(B, S, D)
```

```python
import functools

import numpy as np
import jax
import jax.numpy as jnp
from jax import lax
from jax.experimental import pallas as pl
from jax.experimental.pallas import tpu as pltpu

F32 = jnp.float32
BF16 = jnp.bfloat16

D_MODEL = 1024
HEAD_DIM = 64
ATTN_PAIRS = ((128, 1), (512, 4), (2048, 16))
ATTN_HEADS_PER_GROUP = 8
ATTN_GROUP_W = ATTN_HEADS_PER_GROUP * HEAD_DIM
ATTN_HALF = 64
ROPE_THETA = 10000.0
RET_HEADS = 8
RET_QK_DIM = 64
RET_V_DIM = 128
RET_CHUNK = 128
RET_QK_W = RET_HEADS * RET_QK_DIM
RET_V_W = RET_HEADS * RET_V_DIM
MOE_GROUPS = 8
MOE_EXPERTS_PER_GROUP = 8
MOE_N_EXPERTS = MOE_GROUPS * MOE_EXPERTS_PER_GROUP
MOE_HIDDEN = 512
NORM_EPS = 1e-6

LANES = 128
ROW_TILES = D_MODEL // LANES
NEG_BIG = -1e30

TM_INPROJ = 512
TM_MERGE = 512
MERGE_SPLIT = 2
MERGE_COL_CHUNKS = 4
TM_COMBINE = 256
MOE_BM = 256
DISPATCH_CHUNK = 512
ATTN_QB = 128
ATTN_UNROLL = 8
RET_UNROLL = 4

VMEM_LIMIT = 56 * 1024 * 1024

_A = 3 * ATTN_GROUP_W
OFF_QA, OFF_KA, OFF_VA = 0, _A, 2 * _A
OFF_QR = 3 * _A
OFF_KR = OFF_QR + RET_QK_W
OFF_VR = OFF_KR + RET_QK_W
OFF_GR = OFF_VR + RET_V_W
OFF_GL = OFF_GR + RET_V_W
IN_W = OFF_GL + 2 * D_MODEL


def _cparams(sem, vmem=VMEM_LIMIT):
    return pltpu.CompilerParams(dimension_semantics=sem, vmem_limit_bytes=vmem)


def _inproj_body(x_ref, nw_ref, w_ref, bg_ref, cos_ref, sin_ref,
                 qa0, ka0, va0, qa1, ka1, va1, qa2, ka2, va2, qr, kr, vr, gr, gt,
                 stage_ref):
    tm = x_ref.shape[0]
    x = x_ref[...]
    ms = jnp.mean(x * x, axis=-1, keepdims=True)
    xn = (x * lax.rsqrt(ms + NORM_EPS) * nw_ref[...]).astype(BF16)
    cos = cos_ref[...]
    sin = sin_ref[...]
    lane = lax.broadcasted_iota(jnp.int32, (tm, LANES), 1)
    first_half = (lane & (HEAD_DIM - 1)) < (HEAD_DIM // 2)

    def proj(c0, width):
        return jnp.dot(xn, w_ref[:, c0:c0 + width], preferred_element_type=F32)

    def rotary(a, scale):
        partner = jnp.where(first_half, pltpu.roll(a, LANES - HEAD_DIM // 2, 1),
                            pltpu.roll(a, HEAD_DIM // 2, 1))
        r = a * cos + partner * sin
        return r * scale if scale != 1.0 else r

    def chunks(acc):
        return [acc[:, c * LANES:(c + 1) * LANES] for c in range(acc.shape[1] // LANES)]

    def store_natural(out_ref, acc, fn):
        for c, a in enumerate(chunks(acc)):
            out_ref[:, c * LANES:(c + 1) * LANES] = fn(a).astype(out_ref.dtype)

    def store_pairs(out_ref, acc, fn):
        for c, a in enumerate(chunks(acc)):
            out_ref[0, c] = fn(a).astype(out_ref.dtype)

    def store_strided(out_ref, acc, fn, d):
        for c, a in enumerate(chunks(acc)):
            stage_ref[c] = fn(a)
        for c in range(acc.shape[1] // LANES):
            for r in range(d):
                out_ref[0, c, r] = stage_ref[c, pl.ds(r, tm // d, stride=d), :].astype(out_ref.dtype)

    ident = lambda a: a
    rot_q = lambda a: rotary(a, HEAD_DIM ** -0.5)
    rot_1 = lambda a: rotary(a, 1.0)
    rot_k = lambda a: rotary(a, RET_QK_DIM ** -0.5)

    W = ATTN_GROUP_W
    store_pairs(qa0, proj(OFF_QA, W), rot_q)
    store_pairs(ka0, proj(OFF_KA, W), rot_1)
    store_pairs(va0, proj(OFF_VA, W), ident)
    for g, (qo, ko, vo) in ((1, (qa1, ka1, va1)), (2, (qa2, ka2, va2))):
        d = ATTN_PAIRS[g][1]
        store_strided(qo, proj(OFF_QA + g * W, W), rot_q, d)
        store_strided(ko, proj(OFF_KA + g * W, W), rot_1, d)
        store_strided(vo, proj(OFF_VA + g * W, W), ident, d)
    store_natural(qr, proj(OFF_QR, RET_QK_W), rot_1)
    store_natural(kr, proj(OFF_KR, RET_QK_W), rot_k)
    for h in range(RET_V_W // W):
        vr[:, h * W:(h + 1) * W] = proj(OFF_VR + h * W, W).astype(vr.dtype)
        gr[:, h * W:(h + 1) * W] = proj(OFF_GR + h * W, W).astype(gr.dtype)
    for h in range(2 * D_MODEL // W):
        z = proj(OFF_GL + h * W, W) + bg_ref[:, h * W:(h + 1) * W]
        gt[:, h * W:(h + 1) * W] = jax.nn.sigmoid(z).astype(gt.dtype)


def _in_projection(x2, norm_w, w_bf, b_gate, cos_t, sin_t, B, S):
    T = B * S
    tm = TM_INPROJ
    nt = S // tm
    W = ATTN_GROUP_W
    row = lambda i: (i, 0)
    const = lambda i: (0, 0)
    nat = lambda width: pl.BlockSpec((tm, width), row)

    P = W // LANES

    def strided_spec(d):
        return pl.BlockSpec((1, P, d, tm // d, LANES), lambda i: (i // nt, 0, 0, i % nt, 0))

    def strided_shape(d):
        return jax.ShapeDtypeStruct((B, P, d, S // d, LANES), BF16)

    pair_spec = pl.BlockSpec((1, P, tm, LANES), lambda i: (i // nt, 0, i % nt, 0))
    pair_shape = jax.ShapeDtypeStruct((B, P, S, LANES), BF16)
    nat_shape = lambda width: jax.ShapeDtypeStruct((T, width), BF16)
    d1, d2 = ATTN_PAIRS[1][1], ATTN_PAIRS[2][1]
    out_shape = ([pair_shape] * 3 + [strided_shape(d1)] * 3 + [strided_shape(d2)] * 3
                 + [nat_shape(RET_QK_W)] * 2 + [nat_shape(RET_V_W)] * 2 + [nat_shape(2 * D_MODEL)])
    out_specs = ([pair_spec] * 3 + [strided_spec(d1)] * 3 + [strided_spec(d2)] * 3
                 + [nat(RET_QK_W)] * 2 + [nat(RET_V_W)] * 2 + [nat(2 * D_MODEL)])
    in_specs = [
        pl.BlockSpec((tm, D_MODEL), row),
        pl.BlockSpec((1, D_MODEL), const),
        pl.BlockSpec((D_MODEL, IN_W), const, pipeline_mode=pl.Buffered(1)),
        pl.BlockSpec((1, 2 * D_MODEL), const),
        pl.BlockSpec((tm, LANES), lambda i: (i % nt, 0)),
        pl.BlockSpec((tm, LANES), lambda i: (i % nt, 0)),
    ]
    return pl.pallas_call(
        _inproj_body,
        grid=(T // tm,),
        in_specs=in_specs,
        out_specs=out_specs,
        out_shape=out_shape,
        scratch_shapes=[pltpu.VMEM((W // LANES, tm, LANES), F32)],
        compiler_params=_cparams(("parallel",)),
        name="in_projection",
    )(x2, norm_w, w_bf, b_gate, cos_t, sin_t)


def _attn_body(q0_ref, k0_ref, v0_ref, q1_ref, k1_ref, v1_ref, q2_ref, k2_ref, v2_ref, o_ref,
               va_ref, vb_ref, part_ref, bias_ref, *, S):
    QB, H = ATTN_QB, ATTN_HALF
    lane = lax.broadcasted_iota(jnp.int32, (QB, LANES), 1)
    head0 = lane < HEAD_DIM
    ones = jnp.ones((QB, LANES), BF16)

    qi = lax.broadcasted_iota(jnp.int32, (QB, QB + 2 * H), 0)
    ki = lax.broadcasted_iota(jnp.int32, (QB, QB + 2 * H), 1)
    for n in range(3):
        bias_ref[n] = jnp.where(jnp.abs(ki - qi - n * H) <= H, 0.0, NEG_BIG).astype(F32)

    def scores(q_rows, k_rows, bias, h):
        qm = jnp.where(head0 if h == 0 else jnp.logical_not(head0), q_rows, jnp.zeros_like(q_rows))
        return lax.dot_general(qm, k_rows, (((1,), (1,)), ((), ())), preferred_element_type=F32) + bias

    def weights(s):
        m = jnp.max(s, axis=-1, keepdims=True)
        return m, jnp.exp(s - m).astype(BF16)

    def heads_to_lanes(m0, a, m1, b):
        num = jnp.where(head0, a, b)
        den = pltpu.roll(jnp.where(head0, b, a), HEAD_DIM, 1)
        mx = jnp.where(head0, m0, m1)
        return num, den, mx

    def run_group(q_ref, k_ref, v_ref, d, prepare, store):
        L = S // d
        KW = min(L, QB + 2 * H)
        nb = L // QB

        def fill(t, carry):
            r = t // nb
            rows = pl.ds(pl.multiple_of((t % nb) * QB, QB), QB)
            v = v_ref[0, 0, r, rows, :]
            dst = pl.ds(pl.multiple_of(t * QB, QB), QB)
            va_ref[dst, :] = jnp.where(head0, v, ones)
            vb_ref[dst, :] = jnp.where(head0, ones, v)
            return carry

        lax.fori_loop(0, S // QB, fill, 0, unroll=4)

        def trip(i, carry):
            blocks = []
            for u in range(ATTN_UNROLL):
                t = i * ATTN_UNROLL + u
                r = t // nb
                q0 = pl.multiple_of((t % nb) * QB, QB)
                ws = pl.multiple_of(jnp.clip(q0 - H, 0, L - KW), H)
                bias = bias_ref[(q0 - ws) // H][:, :KW]
                q_rows = q_ref[0, 0, r, pl.ds(q0, QB), :]
                k_rows = k_ref[0, 0, r, pl.ds(ws, KW), :]
                vrows = pl.ds(pl.multiple_of(r * L + ws, H), KW)
                blocks.append((r, q0, vrows, [scores(q_rows, k_rows, bias, h) for h in range(2)]))
            blocks = [(r, q0, vrows, [weights(s) for s in ss]) for r, q0, vrows, ss in blocks]
            done = []
            for r, q0, vrows, ((m0, p0), (m1, p1)) in blocks:
                a = jnp.dot(p0, va_ref[vrows, :], preferred_element_type=F32)
                b = jnp.dot(p1, vb_ref[vrows, :], preferred_element_type=F32)
                done.append((r, q0, m0, a, m1, b))
            done = [(r, q0, prepare(q0, *heads_to_lanes(m0, a, m1, b))) for r, q0, m0, a, m1, b in done]
            for r, q0, vals in done:
                store(r, q0, vals)
            return carry

        lax.fori_loop(0, S // QB // ATTN_UNROLL, trip, 0)

    def store_partial(g):
        d = ATTN_PAIRS[g][1]

        def store(r, q0, vals):
            rows = pl.ds(r + q0 * d, QB, stride=d)
            for n, val in enumerate(vals):
                part_ref[3 * (g - 1) + n, rows, :] = val
        return store

    keep = lambda q0, num, den, mx: (num, den, mx)
    run_group(q1_ref, k1_ref, v1_ref, ATTN_PAIRS[1][1], keep, store_partial(1))
    run_group(q2_ref, k2_ref, v2_ref, ATTN_PAIRS[2][1], keep, store_partial(2))

    def merge(q0, num, den, mx):
        rows = pl.ds(q0, QB)
        nums = [num, part_ref[0, rows, :], part_ref[3, rows, :]]
        dens = [den, part_ref[1, rows, :], part_ref[4, rows, :]]
        mxs = [mx, part_ref[2, rows, :], part_ref[5, rows, :]]
        top = jnp.maximum(jnp.maximum(mxs[0], mxs[1]), mxs[2])
        ws = [jnp.exp(m - top) for m in mxs]
        n = ws[0] * nums[0] + ws[1] * nums[1] + ws[2] * nums[2]
        dn = ws[0] * dens[0] + ws[1] * dens[1] + ws[2] * dens[2]
        return (n / dn).astype(o_ref.dtype)

    def store_out(r, q0, y):
        o_ref[0, 0, pl.ds(q0, QB), :] = y

    run_group(q0_ref, k0_ref, v0_ref, ATTN_PAIRS[0][1], merge, store_out)


def _attention(qkv, B, S):
    P = ATTN_GROUP_W // LANES
    in_specs = []
    for _, d in ATTN_PAIRS:
        in_specs += [pl.BlockSpec((1, 1, d, S // d, LANES), lambda b, p: (b, p, 0, 0, 0))] * 3
    return pl.pallas_call(
        functools.partial(_attn_body, S=S),
        grid=(B, P),
        in_specs=in_specs,
        out_specs=pl.BlockSpec((1, 1, S, LANES), lambda b, p: (b, p, 0, 0)),
        out_shape=jax.ShapeDtypeStruct((B, P, S, LANES), BF16),
        scratch_shapes=[pltpu.VMEM((S, LANES), BF16), pltpu.VMEM((S, LANES), BF16),
                        pltpu.VMEM((6, S, LANES), F32),
                        pltpu.VMEM((3, ATTN_QB, ATTN_QB + 2 * ATTN_HALF), F32)],
        compiler_params=_cparams(("parallel", "parallel")),
        name="attention",
    )(*qkv)


def _log_sigmoid(z):
    return jnp.minimum(z, 0.0) - jnp.log(1.0 + jnp.exp(-jnp.abs(z)))


def _ret_body(dec_ref, q_ref, k_ref, v_ref, g_ref, gnw_ref, o_ref, kt_ref, sf_ref, sb_ref, *, S):
    C = RET_CHUNK
    nc = S // C
    lg = _log_sigmoid(dec_ref[0])
    a_row = lax.broadcasted_iota(jnp.int32, (C, LANES), 0).astype(F32)
    lane = lax.broadcasted_iota(jnp.int32, (C, LANES), 1)
    rel = (lax.broadcasted_iota(jnp.int32, (C, C), 0) - lax.broadcasted_iota(jnp.int32, (C, C), 1)).astype(F32)

    heads = []
    for h in range(2):
        lgf = lg[h:h + 1, :]
        lgb = lg[2 + h:3 + h, :]
        in_head = (lane < RET_QK_DIM) if h == 0 else (lane >= RET_QK_DIM)
        heads.append(dict(
            in_head=in_head,
            xi_f=jnp.where(in_head, jnp.exp((a_row + 1.0) * lgf), 0.0),
            xi_b=jnp.where(in_head, jnp.exp((C - a_row) * lgb), 0.0),
            zeta_f=jnp.exp((C - 1.0 - a_row) * lgf),
            zeta_b=jnp.exp(a_row * lgb),
            dloc=jnp.where(rel > 0, jnp.exp(rel * lgf), jnp.where(rel < 0, jnp.exp(-rel * lgb), 2.0)),
            cd_f=jnp.exp(C * lgf),
            cd_b=jnp.exp(C * lgb),
        ))

    def chunk_rows(n):
        return pl.ds(pl.multiple_of(n * C, C), C)

    def v_head(rows, h):
        return v_ref[0, rows, h * RET_V_DIM:(h + 1) * RET_V_DIM]

    def products(i, carry):
        items = []
        for u in range(RET_UNROLL):
            n = i * RET_UNROLL + u
            rows = chunk_rows(n)
            kt = k_ref[0, rows, :].astype(F32).T.astype(BF16)
            kt_ref[n] = kt
            for h, hd in enumerate(heads):
                vh = v_head(rows, h).astype(F32)
                items.append((n, h, kt, (vh * hd["zeta_f"]).astype(BF16), (vh * hd["zeta_b"]).astype(BF16)))
        outs = [(n, h, jnp.dot(kt, vf, preferred_element_type=F32), jnp.dot(kt, vb, preferred_element_type=F32))
                for n, h, kt, vf, vb in items]
        for n, h, f, b in outs:
            sf_ref[n, h] = f
            sb_ref[n, h] = b
        return carry

    lax.fori_loop(0, nc // RET_UNROLL, products, 0)

    def scan(ref, key, order):
        def step(i, state):
            n = order(i)
            new = []
            for h, hd in enumerate(heads):
                kv = ref[n, h]
                ref[n, h] = state[h]
                new.append(hd[key] * state[h] + kv)
            return tuple(new)
        zero = jnp.zeros(ref.shape[2:], F32)
        lax.fori_loop(0, nc, step, (zero, zero))

    scan(sf_ref, "cd_f", lambda i: i)
    scan(sb_ref, "cd_b", lambda i: nc - 1 - i)

    def outputs(i, carry):
        items = []
        for u in range(RET_UNROLL):
            n = i * RET_UNROLL + u
            rows = chunk_rows(n)
            qp = q_ref[0, rows, :]
            qf = qp.astype(F32)
            kt = kt_ref[n]
            for h, hd in enumerate(heads):
                qm = jnp.where(hd["in_head"], qp, jnp.zeros_like(qp))
                qx = jnp.concatenate([(qf * hd["xi_f"]).astype(BF16), (qf * hd["xi_b"]).astype(BF16)], axis=1)
                st = jnp.concatenate([sf_ref[n, h].astype(BF16), sb_ref[n, h].astype(BF16)], axis=0)
                items.append((rows, h, hd, jnp.dot(qm, kt, preferred_element_type=F32),
                              jnp.dot(qx, st, preferred_element_type=F32)))
        items = [(rows, h, hd, (s * hd["dloc"]).astype(BF16), cross) for rows, h, hd, s, cross in items]
        items = [(rows, h, cross + jnp.dot(p, v_head(rows, h), preferred_element_type=F32))
                 for rows, h, hd, p, cross in items]
        for rows, h, ret in items:
            mu = jnp.mean(ret, axis=-1, keepdims=True)
            xc = ret - mu
            var = jnp.mean(xc * xc, axis=-1, keepdims=True)
            cols = slice(h * RET_V_DIM, (h + 1) * RET_V_DIM)
            gate = g_ref[0, rows, cols].astype(F32)
            y = xc * lax.rsqrt(var + NORM_EPS) * gnw_ref[:, cols] * (gate * jax.nn.sigmoid(gate))
            o_ref[0, rows, cols] = y.astype(o_ref.dtype)
        return carry

    lax.fori_loop(0, nc // RET_UNROLL, outputs, 0)


def _retention(dec, qr, kr, vr, gr, gn_w, B, S):
    nc = S // RET_CHUNK
    npairs = RET_HEADS // 2
    qk_spec = pl.BlockSpec((1, S, 2 * RET_QK_DIM), lambda b, p: (b, 0, p))
    v_spec = pl.BlockSpec((1, S, 2 * RET_V_DIM), lambda b, p: (b, 0, p))
    return pl.pallas_call(
        functools.partial(_ret_body, S=S),
        grid=(B, npairs),
        in_specs=[pl.BlockSpec((1, 4, LANES), lambda b, p: (p, 0, 0)),
                  qk_spec, qk_spec, v_spec, v_spec,
                  pl.BlockSpec((1, 2 * RET_V_DIM), lambda b, p: (0, p))],
        out_specs=v_spec,
        out_shape=jax.ShapeDtypeStruct((B, S, RET_V_W), BF16),
        scratch_shapes=[pltpu.VMEM((nc, 2 * RET_QK_DIM, RET_CHUNK), BF16),
                        pltpu.VMEM((nc, 2, 2 * RET_QK_DIM, RET_V_DIM), F32),
                        pltpu.VMEM((nc, 2, 2 * RET_QK_DIM, RET_V_DIM), F32)],
        compiler_params=_cparams(("parallel", "parallel")),
        name="retention",
    )(dec, qr.reshape(B, S, RET_QK_W), kr.reshape(B, S, RET_QK_W),
      vr.reshape(B, S, RET_V_W), gr.reshape(B, S, RET_V_W), gn_w)


ROUTE_EID, ROUTE_RANK, ROUTE_GATE = 0, 2, 4
ROUTE_FIELDS = 8
ROUTER_EXPERT_LANE0 = MOE_GROUPS


def _merge_body(yatt_ref, yret_ref, gt_ref, x_ref, wa_ref, wb_ref, wo_ref,
                nw_ref, wrh_ref, wrl_ref, br_ref,
                h_ref, hn_ref, route_ref, route_t_ref, cnt_ref):
    tm = x_ref.shape[0]
    hm = tm // MERGE_SPLIT
    i = pl.program_id(0)
    cw = D_MODEL // MERGE_COL_CHUNKS

    def branch_products(rows):
        y_att = jnp.concatenate([yatt_ref[0, c, rows, :] for c in range(yatt_ref.shape[1])], axis=1)
        y_ret = yret_ref[rows, :]
        chunks = []
        for c in range(MERGE_COL_CHUNKS):
            cols = slice(c * cw, (c + 1) * cw)
            a = jnp.dot(y_att, wa_ref[:, cols], preferred_element_type=F32)
            b = jnp.dot(y_ret, wb_ref[:, cols], preferred_element_type=F32)
            g_att = gt_ref[rows, c * cw:(c + 1) * cw].astype(F32)
            g_ret = gt_ref[rows, D_MODEL + c * cw:D_MODEL + (c + 1) * cw].astype(F32)
            chunks.append((g_att * a + g_ret * b).astype(BF16))
        return jnp.concatenate(chunks, axis=1)

    def residual_norm(hf, rows, merged):
        mix = jnp.dot(merged, wo_ref[...], preferred_element_type=F32)
        h = x_ref[rows, :] + mix
        h_ref[rows, :] = h
        ms = jnp.mean(h * h, axis=-1, keepdims=True)
        hn = h * lax.rsqrt(ms + NORM_EPS) * nw_ref[...]
        for j in range(ROW_TILES):
            hn_ref[pl.ds(hf * hm * ROW_TILES + j, hm, stride=ROW_TILES), :] = hn[:, j * LANES:(j + 1) * LANES]
        hi = hn.astype(BF16)
        lo = (hn - hi.astype(F32)).astype(BF16)
        return hi, lo

    def router_logits(hi, lo):
        return (jnp.dot(hi, wrh_ref[...], preferred_element_type=F32)
                + jnp.dot(hi, wrl_ref[...], preferred_element_type=F32)
                + jnp.dot(lo, wrh_ref[...], preferred_element_type=F32)) + br_ref[...]

    lane = lax.broadcasted_iota(jnp.int32, (hm, LANES), 1)
    far = jnp.int32(4 * LANES)

    def first_argmax(vals, vmax):
        return jnp.min(jnp.where(vals == vmax, lane, far), axis=-1, keepdims=True)

    def route(logits):
        is_group = lane < MOE_GROUPS
        gl = jnp.where(is_group, logits, NEG_BIG)
        gmax = jnp.max(gl, axis=-1, keepdims=True)
        g_w = 1.0 / jnp.sum(jnp.where(is_group, jnp.exp(gl - gmax), 0.0), axis=-1, keepdims=True)
        g_idx = first_argmax(gl, gmax)
        e_lane = lane - ROUTER_EXPERT_LANE0
        in_group = (e_lane >= 0) & (e_lane < MOE_N_EXPERTS) & (jnp.right_shift(e_lane, 3) == g_idx)
        el = jnp.where(in_group, logits, NEG_BIG)
        m1 = jnp.max(el, axis=-1, keepdims=True)
        i1 = first_argmax(el, m1)
        el2 = jnp.where(lane == i1, NEG_BIG, el)
        m2 = jnp.max(el2, axis=-1, keepdims=True)
        i2 = first_argmax(el2, m2)
        ex = jnp.exp(m2 - m1)
        return i1, i2, g_w / (1.0 + ex), g_w * ex / (1.0 + ex)

    rows = [slice(hf * hm, (hf + 1) * hm) for hf in range(MERGE_SPLIT)]
    merged = [branch_products(r) for r in rows]
    split = [residual_norm(hf, r, m) for hf, (r, m) in enumerate(zip(rows, merged))]
    routed = [route(router_logits(hi, lo)) for hi, lo in split]

    @pl.when(i == 0)
    def _():
        cnt_ref[...] = jnp.zeros(cnt_ref.shape, F32)

    r_idx = lax.broadcasted_iota(jnp.int32, (hm, hm), 0)
    c_idx = lax.broadcasted_iota(jnp.int32, (hm, hm), 1)
    lower = jnp.where(c_idx < r_idx, 1.0, 0.0).astype(BF16)
    running = cnt_ref[...]
    for hf, (i1, i2, gate1, gate2) in enumerate(routed):
        hot1 = lane == i1
        hot2 = lane == i2
        onehot = jnp.where(hot1 | hot2, 1.0, 0.0)
        before = jnp.dot(lower, onehot.astype(BF16), preferred_element_type=F32) + running
        rank1 = jnp.sum(jnp.where(hot1, before, 0.0), axis=-1, keepdims=True)
        rank2 = jnp.sum(jnp.where(hot2, before, 0.0), axis=-1, keepdims=True)
        running = running + jnp.sum(onehot, axis=0, keepdims=True)
        rec = jnp.zeros((hm, LANES), F32)
        for pos, val in ((ROUTE_EID, (i1 - ROUTER_EXPERT_LANE0).astype(F32)),
                         (ROUTE_EID + 1, (i2 - ROUTER_EXPERT_LANE0).astype(F32)),
                         (ROUTE_RANK, rank1), (ROUTE_RANK + 1, rank2),
                         (ROUTE_GATE, gate1), (ROUTE_GATE + 1, gate2)):
            rec = jnp.where(lane == pos, val, rec)
        route_ref[rows[hf], :] = rec
        route_t_ref[:, hf * hm:(hf + 1) * hm] = rec.T[:route_t_ref.shape[0], :]
    cnt_ref[...] = running


def _merge_route(y_att, y_ret, gates, x2, wa, wb, wo, nw, wr_hi, wr_lo, b_r, T, S):
    tm = TM_MERGE
    nt = S // tm
    row = lambda i: (i, 0)
    const = lambda i: (0, 0)
    full = lambda arr: pl.BlockSpec(arr.shape, const)
    in_specs = ([pl.BlockSpec((1, y_att.shape[1], tm, LANES), lambda i: (i // nt, 0, i % nt, 0)),
                 pl.BlockSpec((tm, RET_V_W), row), pl.BlockSpec((tm, 2 * D_MODEL), row),
                 pl.BlockSpec((tm, D_MODEL), row),
                 full(wa), full(wb), full(wo), full(nw), full(wr_hi), full(wr_lo), full(b_r)])
    return pl.pallas_call(
        _merge_body,
        grid=(T // tm,),
        in_specs=in_specs,
        out_specs=[pl.BlockSpec((tm, D_MODEL), row),
                   pl.BlockSpec((tm * ROW_TILES, LANES), row),
                   pl.BlockSpec((tm, LANES), row), pl.BlockSpec((ROUTE_FIELDS, tm), lambda i: (0, i)),
                   pl.BlockSpec((1, LANES), const)],
        out_shape=[jax.ShapeDtypeStruct((T, D_MODEL), F32), jax.ShapeDtypeStruct((T * ROW_TILES, LANES), F32),
                   jax.ShapeDtypeStruct((T, LANES), F32), jax.ShapeDtypeStruct((ROUTE_FIELDS, T), F32),
                   jax.ShapeDtypeStruct((1, LANES), F32)],
        compiler_params=_cparams(("arbitrary",)),
        name="merge_route",
    )(y_att, y_ret, gates, x2, wa, wb, wo, nw, wr_hi, wr_lo, b_r)


ISSUE_UNROLL = 8


def _tile_rows(n):
    return pl.ds(pl.multiple_of(n * ROW_TILES, ROW_TILES), ROW_TILES)


def _dispatch_body(slot_ref, hn_ref, xs_ref, sem, *, T):
    i = pl.program_id(0)
    ch = hn_ref.shape[0] // ROW_TILES

    def row_copy(j, slot):
        return pltpu.make_async_copy(hn_ref.at[_tile_rows(j)], xs_ref.at[_tile_rows(slot)], sem)

    def issue(j, carry):
        t = i * ch + j
        row_copy(j, slot_ref[t]).start(priority=0)
        row_copy(j, slot_ref[T + t]).start(priority=1)
        return carry

    lax.fori_loop(0, ch, issue, 0, unroll=ISSUE_UNROLL)
    for _ in range(2):
        pltpu.make_async_copy(hn_ref, xs_ref.at[pl.ds(0, ch * ROW_TILES)], sem).wait()


def _dispatch(slots, hn, n_slots, T):
    ch = DISPATCH_CHUNK
    grid_spec = pltpu.PrefetchScalarGridSpec(
        num_scalar_prefetch=1,
        grid=(T // ch,),
        in_specs=[pl.BlockSpec((ch * ROW_TILES, LANES), lambda i, s: (i, 0))],
        out_specs=pl.BlockSpec(memory_space=pl.ANY),
        scratch_shapes=[pltpu.SemaphoreType.DMA(())],
    )
    return pl.pallas_call(
        functools.partial(_dispatch_body, T=T),
        grid_spec=grid_spec,
        out_shape=jax.ShapeDtypeStruct((n_slots * ROW_TILES, LANES), F32),
        compiler_params=_cparams(("arbitrary",)),
        name="moe_dispatch",
    )(slots, hn)


def _expert_body(blk_ref, eid_ref, valid_ref, fresh_ref, x_ref, w1_ref, w3_ref, w2_ref, y_ref,
                 w1b, w3b, w2b):
    i = pl.program_id(0)
    valid = valid_ref[i]

    @pl.when(valid > 0)
    def _():
        @pl.when(fresh_ref[i] == 1)
        def _():
            w1b[...] = w1_ref[0].astype(BF16)
            w3b[...] = w3_ref[0].astype(BF16)
            w2b[...] = w2_ref[0].astype(BF16)

        bm = x_ref.shape[0] // ROW_TILES
        live =lax.broadcasted_iota(jnp.int32, (bm, LANES), 0) < valid
        x = jnp.concatenate(
            [jnp.where(live, x_ref[pl.ds(j, bm, stride=ROW_TILES), :], 0.0).astype(BF16)
             for j in range(ROW_TILES)], axis=1)
        a = jnp.dot(x, w1b[...], preferred_element_type=F32)
        b = jnp.dot(x, w3b[...], preferred_element_type=F32)
        hid = (a * jax.nn.sigmoid(a) * b).astype(BF16)
        y = jnp.dot(hid, w2b[...], preferred_element_type=F32)
        for j in range(ROW_TILES):
            y_ref[pl.ds(j, bm, stride=ROW_TILES), :] = y[:, j * LANES:(j + 1) * LANES]


def _experts(blk, blk_eid, blk_valid, blk_fresh, x_slots, w1, w3, w2, n_blocks):
    bm = MOE_BM
    grid_spec = pltpu.PrefetchScalarGridSpec(
        num_scalar_prefetch=4,
        grid=(n_blocks,),
        in_specs=[
            pl.BlockSpec((bm * ROW_TILES, LANES), lambda i, blk, eid, val, fr: (blk[i], 0)),
            pl.BlockSpec((1, D_MODEL, MOE_HIDDEN), lambda i, blk, eid, val, fr: (eid[i], 0, 0)),
            pl.BlockSpec((1, D_MODEL, MOE_HIDDEN), lambda i, blk, eid, val, fr: (eid[i], 0, 0)),
            pl.BlockSpec((1, MOE_HIDDEN, D_MODEL), lambda i, blk, eid, val, fr: (eid[i], 0, 0)),
        ],
        out_specs=pl.BlockSpec((bm * ROW_TILES, LANES), lambda i, blk, eid, val, fr: (blk[i], 0)),
        scratch_shapes=[pltpu.VMEM((D_MODEL, MOE_HIDDEN), BF16), pltpu.VMEM((D_MODEL, MOE_HIDDEN), BF16),
                        pltpu.VMEM((MOE_HIDDEN, D_MODEL), BF16)],
    )
    return pl.pallas_call(
        _expert_body,
        grid_spec=grid_spec,
        out_shape=jax.ShapeDtypeStruct(x_slots.shape, F32),
        compiler_params=_cparams(("arbitrary",)),
        name="moe_experts",
    )(blk, blk_eid, blk_valid, blk_fresh, x_slots, w1, w3, w2)


def _combine_body(slot_ref, ys_ref, h_ref, route_ref, nw_ref, o_ref, ybuf, sem, *, T):
    i = pl.program_id(0)
    n = pl.num_programs(0)
    tm = h_ref.shape[0]

    def row_copy(slot, buf, k, j):
        return pltpu.make_async_copy(ys_ref.at[_tile_rows(slot)], ybuf.at[buf, k, _tile_rows(j)], sem.at[buf])

    def issue(tile, buf):
        def one(j, carry):
            t = tile * tm + j
            row_copy(slot_ref[t], buf, 0, j).start(priority=0)
            row_copy(slot_ref[T + t], buf, 1, j).start(priority=1)
            return carry
        lax.fori_loop(0, tm, one, 0, unroll=ISSUE_UNROLL)

    @pl.when(i == 0)
    def _():
        issue(0, 0)

    @pl.when(i + 1 < n)
    def _():
        issue(i + 1, (i + 1) % 2)

    buf = i % 2
    for k in range(2):
        pltpu.make_async_copy(ys_ref.at[pl.ds(0, tm * ROW_TILES)], ybuf.at[buf, k], sem.at[buf]).wait()
    route = route_ref[...]
    g1 = route[:, ROUTE_GATE:ROUTE_GATE + 1]
    g2 = route[:, ROUTE_GATE + 1:ROUTE_GATE + 2]
    hs = []
    ss = jnp.zeros((tm, 1), F32)
    for j in range(ROW_TILES):
        tile_row = pl.ds(j, tm, stride=ROW_TILES)
        hj = h_ref[:, j * LANES:(j + 1) * LANES] + (ybuf[buf, 0, tile_row, :] * g1 + ybuf[buf, 1, tile_row, :] * g2)
        hs.append(hj)
        ss = ss + jnp.sum(hj * hj, axis=-1, keepdims=True)
    inv = lax.rsqrt(ss * (1.0 / D_MODEL) + NORM_EPS)
    for j, hj in enumerate(hs):
        cols = slice(j * LANES, (j + 1) * LANES)
        o_ref[:, cols] = hj * inv * nw_ref[:, cols]


def _combine(slots, y_slots, h, route, nw, T):
    tm = TM_COMBINE
    row = lambda i, s: (i, 0)
    grid_spec = pltpu.PrefetchScalarGridSpec(
        num_scalar_prefetch=1,
        grid=(T // tm,),
        in_specs=[pl.BlockSpec(memory_space=pl.ANY),
                  pl.BlockSpec((tm, D_MODEL), row),
                  pl.BlockSpec((tm, LANES), row),
                  pl.BlockSpec((1, D_MODEL), lambda i, s: (0, 0))],
        out_specs=pl.BlockSpec((tm, D_MODEL), row),
        scratch_shapes=[pltpu.VMEM((2, 2, tm * ROW_TILES, LANES), F32), pltpu.SemaphoreType.DMA((2,))],
    )
    return pl.pallas_call(
        functools.partial(_combine_body, T=T),
        grid_spec=grid_spec,
        out_shape=jax.ShapeDtypeStruct((T, D_MODEL), F32),
        compiler_params=_cparams(("arbitrary",)),
        name="moe_combine",
    )(slots, y_slots, h, route, nw)


def _rotary_tables(S):
    inv_freq = (1.0 / (np.float32(ROPE_THETA) ** (np.arange(0, HEAD_DIM, 2, dtype=np.float32) / HEAD_DIM))
                ).astype(np.float32)
    ang = np.arange(S, dtype=np.float32)[:, None] * inv_freq[None, :]
    cos, sin = np.cos(ang), np.sin(ang)
    reps = LANES // HEAD_DIM
    cos_t = np.tile(np.concatenate([cos, cos], axis=1), (1, reps)).astype(np.float32)
    sin_t = np.tile(np.concatenate([-sin, sin], axis=1), (1, reps)).astype(np.float32)
    return jnp.asarray(cos_t), jnp.asarray(sin_t)


def _layer(h_in, norm_mix_w, w_in, b_branch_gate, ret_decay_fwd, ret_decay_bwd, ret_gn_w, w_attn_branch,
           w_ret_branch, w_out, norm_moe_w, moe_w_group, moe_b_group, moe_w_expert, moe_b_expert,
           moe_w1, moe_w3, moe_w2, next_norm_w, B, S, cos_t, sin_t):
    T = B * S
    (qa0, ka0, va0, qa1, ka1, va1, qa2, ka2, va2, qr, kr, vr, gr, gates) = _in_projection(
        h_in, norm_mix_w[None, :], w_in.astype(BF16), b_branch_gate[None, :], cos_t, sin_t, B, S)

    unit = lambda a: a[:, :, None]
    y_att = _attention((unit(qa0), unit(ka0), unit(va0), qa1, ka1, va1, qa2, ka2, va2), B, S)

    dec = jnp.stack([ret_decay_fwd.reshape(RET_HEADS // 2, 2), ret_decay_bwd.reshape(RET_HEADS // 2, 2)], axis=1)
    dec = jnp.broadcast_to(dec.reshape(RET_HEADS // 2, 4, 1), (RET_HEADS // 2, 4, LANES)).astype(F32)
    y_ret = _retention(dec, qr, kr, vr, gr, ret_gn_w[None, :], B, S).reshape(T, RET_V_W)

    pad = LANES - MOE_GROUPS - MOE_N_EXPERTS
    w_r = jnp.concatenate([moe_w_group, moe_w_expert, jnp.zeros((D_MODEL, pad), F32)], axis=1)
    w_r_hi = w_r.astype(BF16)
    w_r_lo = (w_r - w_r_hi.astype(F32)).astype(BF16)
    b_r = jnp.concatenate([moe_b_group, moe_b_expert, jnp.zeros((pad,), F32)])[None, :]

    h_mid, hn, route, route_t, cnt = _merge_route(
        y_att, y_ret, gates, h_in, w_attn_branch.astype(BF16), w_ret_branch.astype(BF16),
        w_out.astype(BF16), norm_moe_w[None, :], w_r_hi, w_r_lo, b_r, T, S)

    bm = MOE_BM
    counts = cnt[0, ROUTER_EXPERT_LANE0:ROUTER_EXPERT_LANE0 + MOE_N_EXPERTS].astype(jnp.int32)
    nblk = (counts + bm - 1) // bm
    blk_end = jnp.cumsum(nblk)
    pstart = (blk_end - nblk) * bm
    n_blocks = (2 * T) // bm + MOE_N_EXPERTS
    n_active = blk_end[-1]
    bidx = jnp.minimum(jnp.arange(n_blocks, dtype=jnp.int32), n_active - 1)
    blk_eid = jnp.sum(bidx[:, None] >= blk_end[None, :], axis=1).astype(jnp.int32)
    mine = blk_eid[:, None] == jnp.arange(MOE_N_EXPERTS, dtype=jnp.int32)[None, :]
    seg_end = jnp.sum(jnp.where(mine, (pstart + counts)[None, :], 0), axis=1)
    blk_valid = jnp.clip(seg_end - bidx * bm, 0, bm)
    blk_valid = jnp.where(jnp.arange(n_blocks) < n_active, blk_valid, 0).astype(jnp.int32)
    blk_fresh = jnp.concatenate([jnp.ones((1,), jnp.int32), (blk_eid[1:] != blk_eid[:-1]).astype(jnp.int32)])
    eid = route_t[ROUTE_EID:ROUTE_EID + 2].astype(jnp.int32)
    rank = route_t[ROUTE_RANK:ROUTE_RANK + 2].astype(jnp.int32)
    start = jnp.sum(jnp.where(eid[..., None] == jnp.arange(MOE_N_EXPERTS, dtype=jnp.int32),
                              pstart.astype(jnp.int32), 0), axis=-1)
    slots = (start + rank).reshape(2 * T)

    x_slots = _dispatch(slots, hn, n_blocks * bm, T)
    y_slots = _experts(bidx, blk_eid, blk_valid, blk_fresh, x_slots, moe_w1, moe_w3, moe_w2, n_blocks)
    return _combine(slots, y_slots, h_mid, route, next_norm_w[None, :], T)


def kernel(x, norm_mix_w, w_in, b_branch_gate, ret_decay_fwd, ret_decay_bwd, ret_gn_w, w_attn_branch,
           w_ret_branch, w_out, norm_moe_w, moe_w_group, moe_b_group, moe_w_expert, moe_b_expert, moe_w1,
           moe_w3, moe_w2, norm_final_w):
    B, S, D = x.shape
    depth = norm_mix_w.shape[0]
    assert depth == 1, "the final norm is fused into the layer's combine stage"
    assert D == D_MODEL and S % TM_INPROJ == 0 and (B * S) < (1 << 24)
    cos_t, sin_t = _rotary_tables(S)
    out = _layer(x.reshape(B * S, D), norm_mix_w[0], w_in[0], b_branch_gate[0], ret_decay_fwd[0],
                 ret_decay_bwd[0], ret_gn_w[0], w_attn_branch[0], w_ret_branch[0], w_out[0], norm_moe_w[0],
                 moe_w_group[0], moe_b_group[0], moe_w_expert[0], moe_b_expert[0], moe_w1[0], moe_w3[0],
                 moe_w2[0], norm_final_w, B, S, cos_t, sin_t)
    return out.reshape(B, S, D)
```

```python
import functools

import numpy as np
import jax
import jax.numpy as jnp
from jax import lax
from jax.experimental import pallas as pl
from jax.experimental.pallas import tpu as pltpu

F32 = jnp.float32
BF16 = jnp.bfloat16

D_MODEL = 1024
HEAD_DIM = 64
ATTN_PAIRS = ((128, 1), (512, 4), (2048, 16))
ATTN_HEADS_PER_GROUP = 8
ATTN_GROUP_W = ATTN_HEADS_PER_GROUP * HEAD_DIM
ATTN_HALF = 64
ROPE_THETA = 10000.0
RET_HEADS = 8
RET_QK_DIM = 64
RET_V_DIM = 128
RET_CHUNK = 128
RET_QK_W = RET_HEADS * RET_QK_DIM
RET_V_W = RET_HEADS * RET_V_DIM
MOE_GROUPS = 8
MOE_EXPERTS_PER_GROUP = 8
MOE_N_EXPERTS = MOE_GROUPS * MOE_EXPERTS_PER_GROUP
MOE_HIDDEN = 512
NORM_EPS = 1e-6

LANES = 128
ROW_TILES = D_MODEL // LANES
NEG_BIG = -1e30
LOG2_E = 1.4426950408889634

TM_INPROJ = 512
TM_MERGE = 512
MERGE_SPLIT = 2
MERGE_COL_CHUNKS = 4
TM_COMBINE = 256
MOE_BM = 256
DISPATCH_CHUNK = 512
ATTN_QB = 128
ATTN_UNROLL = 8
RET_UNROLL = 8

VMEM_LIMIT = 56 * 1024 * 1024

_A = 3 * ATTN_GROUP_W
OFF_QA, OFF_KA, OFF_VA = 0, _A, 2 * _A
OFF_QR = 3 * _A
OFF_KR = OFF_QR + RET_QK_W
OFF_VR = OFF_KR + RET_QK_W
OFF_GR = OFF_VR + RET_V_W
OFF_GL = OFF_GR + RET_V_W
IN_W = OFF_GL + 2 * D_MODEL


def _cparams(sem, vmem=VMEM_LIMIT):
    return pltpu.CompilerParams(dimension_semantics=sem, vmem_limit_bytes=vmem)


def _inproj_body(x_ref, nw_ref, w_ref, bg_ref, cos_ref, sin_ref,
                 qa0, ka0, va0, qa1, ka1, va1, qa2, ka2, va2, qr, kr, vr, gr, gt,
                 stage_ref):
    tm = x_ref.shape[0]
    x = x_ref[...]
    ms = jnp.mean(x * x, axis=-1, keepdims=True)
    xn = (x * lax.rsqrt(ms + NORM_EPS) * nw_ref[...]).astype(BF16)
    cos = cos_ref[...]
    sin = sin_ref[...]
    lane = lax.broadcasted_iota(jnp.int32, (tm, LANES), 1)
    first_half = (lane & (HEAD_DIM - 1)) < (HEAD_DIM // 2)

    def proj(c0, width):
        return jnp.dot(xn, w_ref[:, c0:c0 + width], preferred_element_type=F32)

    def rotary(a, scale):
        partner = jnp.where(first_half, pltpu.roll(a, LANES - HEAD_DIM // 2, 1),
                            pltpu.roll(a, HEAD_DIM // 2, 1))
        r = a * cos + partner * sin
        return r * scale if scale != 1.0 else r

    def chunks(acc):
        return [acc[:, c * LANES:(c + 1) * LANES] for c in range(acc.shape[1] // LANES)]

    def store_natural(out_ref, acc, fn):
        for c, a in enumerate(chunks(acc)):
            out_ref[:, c * LANES:(c + 1) * LANES] = fn(a).astype(out_ref.dtype)

    def store_pairs(out_ref, acc, fn):
        for c, a in enumerate(chunks(acc)):
            out_ref[0, c] = fn(a).astype(out_ref.dtype)

    def store_strided(out_ref, acc, fn, d):
        for c, a in enumerate(chunks(acc)):
            stage_ref[c] = fn(a)
        for c in range(acc.shape[1] // LANES):
            for r in range(d):
                out_ref[0, c, r] = stage_ref[c, pl.ds(r, tm // d, stride=d), :].astype(out_ref.dtype)

    ident = lambda a: a
    rot_q = lambda a: rotary(a, HEAD_DIM ** -0.5 * LOG2_E)
    rot_1 = lambda a: rotary(a, 1.0)
    rot_k = lambda a: rotary(a, RET_QK_DIM ** -0.5)

    W = ATTN_GROUP_W
    store_pairs(qa0, proj(OFF_QA, W), rot_q)
    store_pairs(ka0, proj(OFF_KA, W), rot_1)
    store_pairs(va0, proj(OFF_VA, W), ident)
    for g, (qo, ko, vo) in ((1, (qa1, ka1, va1)), (2, (qa2, ka2, va2))):
        d = ATTN_PAIRS[g][1]
        store_strided(qo, proj(OFF_QA + g * W, W), rot_q, d)
        store_strided(ko, proj(OFF_KA + g * W, W), rot_1, d)
        store_strided(vo, proj(OFF_VA + g * W, W), ident, d)
    store_natural(qr, proj(OFF_QR, RET_QK_W), rot_1)
    store_natural(kr, proj(OFF_KR, RET_QK_W), rot_k)
    for h in range(RET_V_W // W):
        vr[:, h * W:(h + 1) * W] = proj(OFF_VR + h * W, W).astype(vr.dtype)
        gr[:, h * W:(h + 1) * W] = proj(OFF_GR + h * W, W).astype(gr.dtype)
    for h in range(2 * D_MODEL // W):
        z = proj(OFF_GL + h * W, W) + bg_ref[:, h * W:(h + 1) * W]
        gt[:, h * W:(h + 1) * W] = jax.nn.sigmoid(z).astype(gt.dtype)


def _in_projection(x2, norm_w, w_bf, b_gate, cos_t, sin_t, B, S):
    T = B * S
    tm = TM_INPROJ
    nt = S // tm
    W = ATTN_GROUP_W
    row = lambda i: (i, 0)
    const = lambda i: (0, 0)
    nat = lambda width: pl.BlockSpec((tm, width), row)

    P = W // LANES

    def strided_spec(d):
        return pl.BlockSpec((1, P, d, tm // d, LANES), lambda i: (i // nt, 0, 0, i % nt, 0))

    def strided_shape(d):
        return jax.ShapeDtypeStruct((B, P, d, S // d, LANES), BF16)

    pair_spec = pl.BlockSpec((1, P, tm, LANES), lambda i: (i // nt, 0, i % nt, 0))
    pair_shape = jax.ShapeDtypeStruct((B, P, S, LANES), BF16)
    nat_shape = lambda width: jax.ShapeDtypeStruct((T, width), BF16)
    d1, d2 = ATTN_PAIRS[1][1], ATTN_PAIRS[2][1]
    out_shape = ([pair_shape] * 3 + [strided_shape(d1)] * 3 + [strided_shape(d2)] * 3
                 + [nat_shape(RET_QK_W)] * 2 + [nat_shape(RET_V_W)] * 2 + [nat_shape(2 * D_MODEL)])
    out_specs = ([pair_spec] * 3 + [strided_spec(d1)] * 3 + [strided_spec(d2)] * 3
                 + [nat(RET_QK_W)] * 2 + [nat(RET_V_W)] * 2 + [nat(2 * D_MODEL)])
    in_specs = [
        pl.BlockSpec((tm, D_MODEL), row),
        pl.BlockSpec((1, D_MODEL), const),
        pl.BlockSpec((D_MODEL, IN_W), const, pipeline_mode=pl.Buffered(1)),
        pl.BlockSpec((1, 2 * D_MODEL), const),
        pl.BlockSpec((tm, LANES), lambda i: (i % nt, 0)),
        pl.BlockSpec((tm, LANES), lambda i: (i % nt, 0)),
    ]
    return pl.pallas_call(
        _inproj_body,
        grid=(T // tm,),
        in_specs=in_specs,
        out_specs=out_specs,
        out_shape=out_shape,
        scratch_shapes=[pltpu.VMEM((W // LANES, tm, LANES), F32)],
        compiler_params=_cparams(("parallel",)),
        name="in_projection",
    )(x2, norm_w, w_bf, b_gate, cos_t, sin_t)


def _attn_body(q0_ref, k0_ref, v0_ref, q1_ref, k1_ref, v1_ref, q2_ref, k2_ref, v2_ref, o_ref,
               va_ref, vb_ref, part_ref, bias_ref, *, S):
    QB, H = ATTN_QB, ATTN_HALF
    lane = lax.broadcasted_iota(jnp.int32, (QB, LANES), 1)
    head0 = lane < HEAD_DIM
    ones = jnp.ones((QB, LANES), BF16)

    qi = lax.broadcasted_iota(jnp.int32, (QB, QB + 2 * H), 0)
    ki = lax.broadcasted_iota(jnp.int32, (QB, QB + 2 * H), 1)
    for n in range(3):
        bias_ref[n] = jnp.where(jnp.abs(ki - qi - n * H) <= H, 0.0, NEG_BIG).astype(F32)

    def scores(q_rows, k_rows, bias, h):
        qm = jnp.where(head0 if h == 0 else jnp.logical_not(head0), q_rows, jnp.zeros_like(q_rows))
        return lax.dot_general(qm, k_rows, (((1,), (1,)), ((), ())), preferred_element_type=F32) + bias

    def weights(s):
        m = jnp.max(s, axis=-1, keepdims=True)
        return m, jnp.exp2(s - m).astype(BF16)

    def heads_to_lanes(m0, a, m1, b):
        num = jnp.where(head0, a, b)
        den = pltpu.roll(jnp.where(head0, b, a), HEAD_DIM, 1)
        mx = jnp.where(head0, m0, m1)
        return num, den, mx

    def run_group(q_ref, k_ref, v_ref, d, prepare, store):
        L = S // d
        KW = min(L, QB + 2 * H)
        nb = L // QB

        def fill(t, carry):
            r = t // nb
            rows = pl.ds(pl.multiple_of((t % nb) * QB, QB), QB)
            v = v_ref[0, 0, r, rows, :]
            dst = pl.ds(pl.multiple_of(t * QB, QB), QB)
            va_ref[dst, :] = jnp.where(head0, v, ones)
            vb_ref[dst, :] = jnp.where(head0, ones, v)
            return carry

        lax.fori_loop(0, S // QB, fill, 0, unroll=4)

        def trip(i, carry):
            blocks = []
            for u in range(ATTN_UNROLL):
                t = i * ATTN_UNROLL + u
                r = t // nb
                q0 = pl.multiple_of((t % nb) * QB, QB)
                ws = pl.multiple_of(jnp.clip(q0 - H, 0, L - KW), H)
                bias = bias_ref[(q0 - ws) // H][:, :KW]
                q_rows = q_ref[0, 0, r, pl.ds(q0, QB), :]
                k_rows = k_ref[0, 0, r, pl.ds(ws, KW), :]
                vrows = pl.ds(pl.multiple_of(r * L + ws, H), KW)
                blocks.append((r, q0, vrows, [scores(q_rows, k_rows, bias, h) for h in range(2)]))
            blocks = [(r, q0, vrows, [weights(s) for s in ss]) for r, q0, vrows, ss in blocks]
            done = []
            for r, q0, vrows, ((m0, p0), (m1, p1)) in blocks:
                a = jnp.dot(p0, va_ref[vrows, :], preferred_element_type=F32)
                b = jnp.dot(p1, vb_ref[vrows, :], preferred_element_type=F32)
                done.append((r, q0, m0, a, m1, b))
            done = [(r, q0, prepare(q0, *heads_to_lanes(m0, a, m1, b))) for r, q0, m0, a, m1, b in done]
            for r, q0, vals in done:
                store(r, q0, vals)
            return carry

        lax.fori_loop(0, S // QB // ATTN_UNROLL, trip, 0)

    def store_partial(g):
        d = ATTN_PAIRS[g][1]

        def store(r, q0, vals):
            rows = pl.ds(r + q0 * d, QB, stride=d)
            for n, val in enumerate(vals):
                part_ref[3 * (g - 1) + n, rows, :] = val
        return store

    keep = lambda q0, num, den, mx: (num, den, mx)
    run_group(q1_ref, k1_ref, v1_ref, ATTN_PAIRS[1][1], keep, store_partial(1))
    run_group(q2_ref, k2_ref, v2_ref, ATTN_PAIRS[2][1], keep, store_partial(2))

    def merge(q0, num, den, mx):
        rows = pl.ds(q0, QB)
        nums = [num, part_ref[0, rows, :], part_ref[3, rows, :]]
        dens = [den, part_ref[1, rows, :], part_ref[4, rows, :]]
        mxs = [mx, part_ref[2, rows, :], part_ref[5, rows, :]]
        top = jnp.maximum(jnp.maximum(mxs[0], mxs[1]), mxs[2])
        ws = [jnp.exp2(m - top) for m in mxs]
        n = ws[0] * nums[0] + ws[1] * nums[1] + ws[2] * nums[2]
        dn = ws[0] * dens[0] + ws[1] * dens[1] + ws[2] * dens[2]
        return (n / dn).astype(o_ref.dtype)

    def store_out(r, q0, y):
        o_ref[0, 0, pl.ds(q0, QB), :] = y

    run_group(q0_ref, k0_ref, v0_ref, ATTN_PAIRS[0][1], merge, store_out)


def _attention(qkv, B, S):
    P = ATTN_GROUP_W // LANES
    in_specs = []
    for _, d in ATTN_PAIRS:
        in_specs += [pl.BlockSpec((1, 1, d, S // d, LANES), lambda b, p: (b, p, 0, 0, 0))] * 3
    return pl.pallas_call(
        functools.partial(_attn_body, S=S),
        grid=(B, P),
        in_specs=in_specs,
        out_specs=pl.BlockSpec((1, 1, S, LANES), lambda b, p: (b, p, 0, 0)),
        out_shape=jax.ShapeDtypeStruct((B, P, S, LANES), BF16),
        scratch_shapes=[pltpu.VMEM((S, LANES), BF16), pltpu.VMEM((S, LANES), BF16),
                        pltpu.VMEM((6, S, LANES), F32),
                        pltpu.VMEM((3, ATTN_QB, ATTN_QB + 2 * ATTN_HALF), F32)],
        compiler_params=_cparams(("parallel", "parallel")),
        name="attention",
    )(*qkv)


def _log_sigmoid(z):
    return jnp.minimum(z, 0.0) - jnp.log(1.0 + jnp.exp(-jnp.abs(z)))


def _ret_body(dec_ref, q_ref, k_ref, v_ref, g_ref, gnw_ref, o_ref, kt_ref, sf_ref, sb_ref, *, S):
    C = RET_CHUNK
    nc = S // C
    lg = _log_sigmoid(dec_ref[0])
    a_row = lax.broadcasted_iota(jnp.int32, (C, LANES), 0).astype(F32)
    lane = lax.broadcasted_iota(jnp.int32, (C, LANES), 1)
    rel = (lax.broadcasted_iota(jnp.int32, (C, C), 0) - lax.broadcasted_iota(jnp.int32, (C, C), 1)).astype(F32)

    heads = []
    for h in range(2):
        lgf = lg[h:h + 1, :]
        lgb = lg[2 + h:3 + h, :]
        in_head = (lane < RET_QK_DIM) if h == 0 else (lane >= RET_QK_DIM)
        heads.append(dict(
            in_head=in_head,
            xi_f=jnp.where(in_head, jnp.exp((a_row + 1.0) * lgf), 0.0),
            xi_b=jnp.where(in_head, jnp.exp((C - a_row) * lgb), 0.0),
            zeta_f=jnp.exp((C - 1.0 - a_row) * lgf),
            zeta_b=jnp.exp(a_row * lgb),
            dloc=jnp.where(rel > 0, jnp.exp(rel * lgf), jnp.where(rel < 0, jnp.exp(-rel * lgb), 2.0)),
            cd_f=jnp.exp(C * lgf),
            cd_b=jnp.exp(C * lgb),
        ))

    def chunk_rows(n):
        return pl.ds(pl.multiple_of(n * C, C), C)

    def v_head(rows, h):
        return v_ref[0, rows, h * RET_V_DIM:(h + 1) * RET_V_DIM]

    def products(i, carry):
        items = []
        for u in range(RET_UNROLL):
            n = i * RET_UNROLL + u
            rows = chunk_rows(n)
            kt = k_ref[0, rows, :].astype(F32).T.astype(BF16)
            kt_ref[n] = kt
            for h, hd in enumerate(heads):
                vh = v_head(rows, h).astype(F32)
                items.append((n, h, kt, (vh * hd["zeta_f"]).astype(BF16), (vh * hd["zeta_b"]).astype(BF16)))
        outs = [(n, h, jnp.dot(kt, vf, preferred_element_type=F32), jnp.dot(kt, vb, preferred_element_type=F32))
                for n, h, kt, vf, vb in items]
        for n, h, f, b in outs:
            sf_ref[n, h] = f
            sb_ref[n, h] = b
        return carry

    lax.fori_loop(0, nc // RET_UNROLL, products, 0)

    def scan(ref, key, order):
        def step(i, state):
            n = order(i)
            new = []
            for h, hd in enumerate(heads):
                kv = ref[n, h]
                ref[n, h] = state[h]
                new.append(hd[key] * state[h] + kv)
            return tuple(new)
        zero = jnp.zeros(ref.shape[2:], F32)
        lax.fori_loop(0, nc, step, (zero, zero))

    scan(sf_ref, "cd_f", lambda i: i)
    scan(sb_ref, "cd_b", lambda i: nc - 1 - i)

    def outputs(i, carry):
        items = []
        for u in range(RET_UNROLL):
            n = i * RET_UNROLL + u
            rows = chunk_rows(n)
            qp = q_ref[0, rows, :]
            qf = qp.astype(F32)
            kt = kt_ref[n]
            for h, hd in enumerate(heads):
                qm = jnp.where(hd["in_head"], qp, jnp.zeros_like(qp))
                qx = jnp.concatenate([(qf * hd["xi_f"]).astype(BF16), (qf * hd["xi_b"]).astype(BF16)], axis=1)
                st = jnp.concatenate([sf_ref[n, h].astype(BF16), sb_ref[n, h].astype(BF16)], axis=0)
                items.append((rows, h, hd, jnp.dot(qm, kt, preferred_element_type=F32),
                              jnp.dot(qx, st, preferred_element_type=F32)))
        items = [(rows, h, hd, (s * hd["dloc"]).astype(BF16), cross) for rows, h, hd, s, cross in items]
        items = [(rows, h, cross + jnp.dot(p, v_head(rows, h), preferred_element_type=F32))
                 for rows, h, hd, p, cross in items]
        for rows, h, ret in items:
            mu = jnp.mean(ret, axis=-1, keepdims=True)
            xc = ret - mu
            var = jnp.mean(xc * xc, axis=-1, keepdims=True)
            cols = slice(h * RET_V_DIM, (h + 1) * RET_V_DIM)
            gate = g_ref[0, rows, cols].astype(F32)
            y = xc * lax.rsqrt(var + NORM_EPS) * gnw_ref[:, cols] * (gate * jax.nn.sigmoid(gate))
            o_ref[0, rows, cols] = y.astype(o_ref.dtype)
        return carry

    lax.fori_loop(0, nc // RET_UNROLL, outputs, 0)


def _retention(dec, qr, kr, vr, gr, gn_w, B, S):
    nc = S // RET_CHUNK
    npairs = RET_HEADS // 2
    qk_spec = pl.BlockSpec((1, S, 2 * RET_QK_DIM), lambda b, p: (b, 0, p))
    v_spec = pl.BlockSpec((1, S, 2 * RET_V_DIM), lambda b, p: (b, 0, p))
    return pl.pallas_call(
        functools.partial(_ret_body, S=S),
        grid=(B, npairs),
        in_specs=[pl.BlockSpec((1, 4, LANES), lambda b, p: (p, 0, 0)),
                  qk_spec, qk_spec, v_spec, v_spec,
                  pl.BlockSpec((1, 2 * RET_V_DIM), lambda b, p: (0, p))],
        out_specs=v_spec,
        out_shape=jax.ShapeDtypeStruct((B, S, RET_V_W), BF16),
        scratch_shapes=[pltpu.VMEM((nc, 2 * RET_QK_DIM, RET_CHUNK), BF16),
                        pltpu.VMEM((nc, 2, 2 * RET_QK_DIM, RET_V_DIM), F32),
                        pltpu.VMEM((nc, 2, 2 * RET_QK_DIM, RET_V_DIM), F32)],
        compiler_params=_cparams(("parallel", "parallel")),
        name="retention",
    )(dec, qr.reshape(B, S, RET_QK_W), kr.reshape(B, S, RET_QK_W),
      vr.reshape(B, S, RET_V_W), gr.reshape(B, S, RET_V_W), gn_w)


ROUTE_EID, ROUTE_RANK, ROUTE_GATE = 0, 2, 4
ROUTE_FIELDS = 8
ROUTER_EXPERT_LANE0 = MOE_GROUPS


def _merge_body(yatt_ref, yret_ref, gt_ref, x_ref, wa_ref, wb_ref, wo_ref,
                nw_ref, wrh_ref, wrl_ref, br_ref,
                h_ref, hn_ref, route_ref, route_t_ref, cnt_ref):
    tm = x_ref.shape[0]
    hm = tm // MERGE_SPLIT
    i = pl.program_id(0)
    cw = D_MODEL // MERGE_COL_CHUNKS

    def branch_products(rows):
        y_att = jnp.concatenate([yatt_ref[0, c, rows, :] for c in range(yatt_ref.shape[1])], axis=1)
        y_ret = yret_ref[rows, :]
        chunks = []
        for c in range(MERGE_COL_CHUNKS):
            cols = slice(c * cw, (c + 1) * cw)
            a = jnp.dot(y_att, wa_ref[:, cols], preferred_element_type=F32)
            b = jnp.dot(y_ret, wb_ref[:, cols], preferred_element_type=F32)
            g_att = gt_ref[rows, c * cw:(c + 1) * cw].astype(F32)
            g_ret = gt_ref[rows, D_MODEL + c * cw:D_MODEL + (c + 1) * cw].astype(F32)
            chunks.append((g_att * a + g_ret * b).astype(BF16))
        return jnp.concatenate(chunks, axis=1)

    def residual_norm(hf, rows, merged):
        mix = jnp.dot(merged, wo_ref[...], preferred_element_type=F32)
        h = x_ref[rows, :] + mix
        h_ref[rows, :] = h
        ms = jnp.mean(h * h, axis=-1, keepdims=True)
        hn = h * lax.rsqrt(ms + NORM_EPS) * nw_ref[...]
        for j in range(ROW_TILES):
            hn_ref[pl.ds(hf * hm * ROW_TILES + j, hm, stride=ROW_TILES), :] = hn[:, j * LANES:(j + 1) * LANES]
        hi = hn.astype(BF16)
        lo = (hn - hi.astype(F32)).astype(BF16)
        return hi, lo

    def router_logits(hi, lo):
        return (jnp.dot(hi, wrh_ref[...], preferred_element_type=F32)
                + jnp.dot(hi, wrl_ref[...], preferred_element_type=F32)
                + jnp.dot(lo, wrh_ref[...], preferred_element_type=F32)) + br_ref[...]

    lane = lax.broadcasted_iota(jnp.int32, (hm, LANES), 1)
    far = jnp.int32(4 * LANES)

    def first_argmax(vals, vmax):
        return jnp.min(jnp.where(vals == vmax, lane, far), axis=-1, keepdims=True)

    def route(logits):
        is_group = lane < MOE_GROUPS
        gl = jnp.where(is_group, logits, NEG_BIG)
        gmax = jnp.max(gl, axis=-1, keepdims=True)
        g_w = 1.0 / jnp.sum(jnp.where(is_group, jnp.exp(gl - gmax), 0.0), axis=-1, keepdims=True)
        g_idx = first_argmax(gl, gmax)
        e_lane = lane - ROUTER_EXPERT_LANE0
        in_group = (e_lane >= 0) & (e_lane < MOE_N_EXPERTS) & (jnp.right_shift(e_lane, 3) == g_idx)
        el = jnp.where(in_group, logits, NEG_BIG)
        m1 = jnp.max(el, axis=-1, keepdims=True)
        i1 = first_argmax(el, m1)
        el2 = jnp.where(lane == i1, NEG_BIG, el)
        m2 = jnp.max(el2, axis=-1, keepdims=True)
        i2 = first_argmax(el2, m2)
        ex = jnp.exp(m2 - m1)
        return i1, i2, g_w / (1.0 + ex), g_w * ex / (1.0 + ex)

    rows = [slice(hf * hm, (hf + 1) * hm) for hf in range(MERGE_SPLIT)]
    merged = [branch_products(r) for r in rows]
    split = [residual_norm(hf, r, m) for hf, (r, m) in enumerate(zip(rows, merged))]
    routed = [route(router_logits(hi, lo)) for hi, lo in split]

    @pl.when(i == 0)
    def _():
        cnt_ref[...] = jnp.zeros(cnt_ref.shape, F32)

    r_idx = lax.broadcasted_iota(jnp.int32, (hm, hm), 0)
    c_idx = lax.broadcasted_iota(jnp.int32, (hm, hm), 1)
    lower = jnp.where(c_idx < r_idx, 1.0, 0.0).astype(BF16)
    running = cnt_ref[...]
    for hf, (i1, i2, gate1, gate2) in enumerate(routed):
        hot1 = lane == i1
        hot2 = lane == i2
        onehot = jnp.where(hot1 | hot2, 1.0, 0.0)
        before = jnp.dot(lower, onehot.astype(BF16), preferred_element_type=F32) + running
        rank1 = jnp.sum(jnp.where(hot1, before, 0.0), axis=-1, keepdims=True)
        rank2 = jnp.sum(jnp.where(hot2, before, 0.0), axis=-1, keepdims=True)
        running = running + jnp.sum(onehot, axis=0, keepdims=True)
        rec = jnp.zeros((hm, LANES), F32)
        for pos, val in ((ROUTE_EID, (i1 - ROUTER_EXPERT_LANE0).astype(F32)),
                         (ROUTE_EID + 1, (i2 - ROUTER_EXPERT_LANE0).astype(F32)),
                         (ROUTE_RANK, rank1), (ROUTE_RANK + 1, rank2),
                         (ROUTE_GATE, gate1), (ROUTE_GATE + 1, gate2)):
            rec = jnp.where(lane == pos, val, rec)
        route_ref[rows[hf], :] = rec
        route_t_ref[:, hf * hm:(hf + 1) * hm] = rec.T[:route_t_ref.shape[0], :]
    cnt_ref[...] = running


def _merge_route(y_att, y_ret, gates, x2, wa, wb, wo, nw, wr_hi, wr_lo, b_r, T, S):
    tm = TM_MERGE
    nt = S // tm
    row = lambda i: (i, 0)
    const = lambda i: (0, 0)
    full = lambda arr: pl.BlockSpec(arr.shape, const)
    in_specs = ([pl.BlockSpec((1, y_att.shape[1], tm, LANES), lambda i: (i // nt, 0, i % nt, 0)),
                 pl.BlockSpec((tm, RET_V_W), row), pl.BlockSpec((tm, 2 * D_MODEL), row),
                 pl.BlockSpec((tm, D_MODEL), row),
                 full(wa), full(wb), full(wo), full(nw), full(wr_hi), full(wr_lo), full(b_r)])
    return pl.pallas_call(
        _merge_body,
        grid=(T // tm,),
        in_specs=in_specs,
        out_specs=[pl.BlockSpec((tm, D_MODEL), row),
                   pl.BlockSpec((tm * ROW_TILES, LANES), row),
                   pl.BlockSpec((tm, LANES), row), pl.BlockSpec((ROUTE_FIELDS, tm), lambda i: (0, i)),
                   pl.BlockSpec((1, LANES), const)],
        out_shape=[jax.ShapeDtypeStruct((T, D_MODEL), F32), jax.ShapeDtypeStruct((T * ROW_TILES, LANES), F32),
                   jax.ShapeDtypeStruct((T, LANES), F32), jax.ShapeDtypeStruct((ROUTE_FIELDS, T), F32),
                   jax.ShapeDtypeStruct((1, LANES), F32)],
        compiler_params=_cparams(("arbitrary",)),
        name="merge_route",
    )(y_att, y_ret, gates, x2, wa, wb, wo, nw, wr_hi, wr_lo, b_r)


ISSUE_UNROLL = 8


def _tile_rows(n):
    return pl.ds(pl.multiple_of(n * ROW_TILES, ROW_TILES), ROW_TILES)


def _dispatch_body(slot_ref, hn_ref, xs_ref, sem, *, T):
    i = pl.program_id(0)
    ch = hn_ref.shape[0] // ROW_TILES

    def row_copy(j, slot):
        return pltpu.make_async_copy(hn_ref.at[_tile_rows(j)], xs_ref.at[_tile_rows(slot)], sem)

    def issue(j, carry):
        t = i * ch + j
        row_copy(j, slot_ref[t]).start(priority=0)
        row_copy(j, slot_ref[T + t]).start(priority=1)
        return carry

    lax.fori_loop(0, ch, issue, 0, unroll=ISSUE_UNROLL)
    for _ in range(2):
        pltpu.make_async_copy(hn_ref, xs_ref.at[pl.ds(0, ch * ROW_TILES)], sem).wait()


def _dispatch(slots, hn, n_slots, T):
    ch = DISPATCH_CHUNK
    grid_spec = pltpu.PrefetchScalarGridSpec(
        num_scalar_prefetch=1,
        grid=(T // ch,),
        in_specs=[pl.BlockSpec((ch * ROW_TILES, LANES), lambda i, s: (i, 0))],
        out_specs=pl.BlockSpec(memory_space=pl.ANY),
        scratch_shapes=[pltpu.SemaphoreType.DMA(())],
    )
    return pl.pallas_call(
        functools.partial(_dispatch_body, T=T),
        grid_spec=grid_spec,
        out_shape=jax.ShapeDtypeStruct((n_slots * ROW_TILES, LANES), F32),
        compiler_params=_cparams(("arbitrary",)),
        name="moe_dispatch",
    )(slots, hn)


def _expert_body(blk_ref, eid_ref, valid_ref, fresh_ref, x_ref, w1_ref, w3_ref, w2_ref, y_ref,
                 w1b, w3b, w2b):
    i = pl.program_id(0)
    valid = valid_ref[i]

    @pl.when(valid > 0)
    def _():
        @pl.when(fresh_ref[i] == 1)
        def _():
            w1b[...] = w1_ref[0].astype(BF16)
            w3b[...] = w3_ref[0].astype(BF16)
            w2b[...] = w2_ref[0].astype(BF16)

        bm = x_ref.shape[0] // ROW_TILES
        live =lax.broadcasted_iota(jnp.int32, (bm, LANES), 0) < valid
        x = jnp.concatenate(
            [jnp.where(live, x_ref[pl.ds(j, bm, stride=ROW_TILES), :], 0.0).astype(BF16)
             for j in range(ROW_TILES)], axis=1)
        a = jnp.dot(x, w1b[...], preferred_element_type=F32)
        b = jnp.dot(x, w3b[...], preferred_element_type=F32)
        hid = (a * jax.nn.sigmoid(a) * b).astype(BF16)
        y = jnp.dot(hid, w2b[...], preferred_element_type=F32)
        for j in range(ROW_TILES):
            y_ref[pl.ds(j, bm, stride=ROW_TILES), :] = y[:, j * LANES:(j + 1) * LANES]


def _experts(blk, blk_eid, blk_valid, blk_fresh, x_slots, w1, w3, w2, n_blocks):
    bm = MOE_BM
    grid_spec = pltpu.PrefetchScalarGridSpec(
        num_scalar_prefetch=4,
        grid=(n_blocks,),
        in_specs=[
            pl.BlockSpec((bm * ROW_TILES, LANES), lambda i, blk, eid, val, fr: (blk[i], 0)),
            pl.BlockSpec((1, D_MODEL, MOE_HIDDEN), lambda i, blk, eid, val, fr: (eid[i], 0, 0)),
            pl.BlockSpec((1, D_MODEL, MOE_HIDDEN), lambda i, blk, eid, val, fr: (eid[i], 0, 0)),
            pl.BlockSpec((1, MOE_HIDDEN, D_MODEL), lambda i, blk, eid, val, fr: (eid[i], 0, 0)),
        ],
        out_specs=pl.BlockSpec((bm * ROW_TILES, LANES), lambda i, blk, eid, val, fr: (blk[i], 0)),
        scratch_shapes=[pltpu.VMEM((D_MODEL, MOE_HIDDEN), BF16), pltpu.VMEM((D_MODEL, MOE_HIDDEN), BF16),
                        pltpu.VMEM((MOE_HIDDEN, D_MODEL), BF16)],
    )
    return pl.pallas_call(
        _expert_body,
        grid_spec=grid_spec,
        out_shape=jax.ShapeDtypeStruct(x_slots.shape, F32),
        compiler_params=_cparams(("arbitrary",)),
        name="moe_experts",
    )(blk, blk_eid, blk_valid, blk_fresh, x_slots, w1, w3, w2)


def _combine_body(slot_ref, ys_ref, h_ref, route_ref, nw_ref, o_ref, ybuf, sem, *, T):
    i = pl.program_id(0)
    n = pl.num_programs(0)
    tm = h_ref.shape[0]

    def row_copy(slot, buf, k, j):
        return pltpu.make_async_copy(ys_ref.at[_tile_rows(slot)], ybuf.at[buf, k, _tile_rows(j)], sem.at[buf])

    def issue(tile, buf):
        def one(j, carry):
            t = tile * tm + j
            row_copy(slot_ref[t], buf, 0, j).start(priority=0)
            row_copy(slot_ref[T + t], buf, 1, j).start(priority=1)
            return carry
        lax.fori_loop(0, tm, one, 0, unroll=ISSUE_UNROLL)

    @pl.when(i == 0)
    def _():
        issue(0, 0)

    @pl.when(i + 1 < n)
    def _():
        issue(i + 1, (i + 1) % 2)

    buf = i % 2
    for k in range(2):
        pltpu.make_async_copy(ys_ref.at[pl.ds(0, tm * ROW_TILES)], ybuf.at[buf, k], sem.at[buf]).wait()
    route = route_ref[...]
    g1 = route[:, ROUTE_GATE:ROUTE_GATE + 1]
    g2 = route[:, ROUTE_GATE + 1:ROUTE_GATE + 2]
    hs = []
    ss = jnp.zeros((tm, 1), F32)
    for j in range(ROW_TILES):
        tile_row = pl.ds(j, tm, stride=ROW_TILES)
        hj = h_ref[:, j * LANES:(j + 1) * LANES] + (ybuf[buf, 0, tile_row, :] * g1 + ybuf[buf, 1, tile_row, :] * g2)
        hs.append(hj)
        ss = ss + jnp.sum(hj * hj, axis=-1, keepdims=True)
    inv = lax.rsqrt(ss * (1.0 / D_MODEL) + NORM_EPS)
    for j, hj in enumerate(hs):
        cols = slice(j * LANES, (j + 1) * LANES)
        o_ref[:, cols] = hj * inv * nw_ref[:, cols]


def _combine(slots, y_slots, h, route, nw, T):
    tm = TM_COMBINE
    row = lambda i, s: (i, 0)
    grid_spec = pltpu.PrefetchScalarGridSpec(
        num_scalar_prefetch=1,
        grid=(T // tm,),
        in_specs=[pl.BlockSpec(memory_space=pl.ANY),
                  pl.BlockSpec((tm, D_MODEL), row),
                  pl.BlockSpec((tm, LANES), row),
                  pl.BlockSpec((1, D_MODEL), lambda i, s: (0, 0))],
        out_specs=pl.BlockSpec((tm, D_MODEL), row),
        scratch_shapes=[pltpu.VMEM((2, 2, tm * ROW_TILES, LANES), F32), pltpu.SemaphoreType.DMA((2,))],
    )
    return pl.pallas_call(
        functools.partial(_combine_body, T=T),
        grid_spec=grid_spec,
        out_shape=jax.ShapeDtypeStruct((T, D_MODEL), F32),
        compiler_params=_cparams(("arbitrary",)),
        name="moe_combine",
    )(slots, y_slots, h, route, nw)


def _rotary_tables(S):
    inv_freq = (1.0 / (np.float32(ROPE_THETA) ** (np.arange(0, HEAD_DIM, 2, dtype=np.float32) / HEAD_DIM))
                ).astype(np.float32)
    ang = np.arange(S, dtype=np.float32)[:, None] * inv_freq[None, :]
    cos, sin = np.cos(ang), np.sin(ang)
    reps = LANES // HEAD_DIM
    cos_t = np.tile(np.concatenate([cos, cos], axis=1), (1, reps)).astype(np.float32)
    sin_t = np.tile(np.concatenate([-sin, sin], axis=1), (1, reps)).astype(np.float32)
    return jnp.asarray(cos_t), jnp.asarray(sin_t)


def _layer(h_in, norm_mix_w, w_in, b_branch_gate, ret_decay_fwd, ret_decay_bwd, ret_gn_w, w_attn_branch,
           w_ret_branch, w_out, norm_moe_w, moe_w_group, moe_b_group, moe_w_expert, moe_b_expert,
           moe_w1, moe_w3, moe_w2, next_norm_w, B, S, cos_t, sin_t):
    T = B * S
    (qa0, ka0, va0, qa1, ka1, va1, qa2, ka2, va2, qr, kr, vr, gr, gates) = _in_projection(
        h_in, norm_mix_w[None, :], w_in.astype(BF16), b_branch_gate[None, :], cos_t, sin_t, B, S)

    unit = lambda a: a[:, :, None]
    y_att = _attention((unit(qa0), unit(ka0), unit(va0), qa1, ka1, va1, qa2, ka2, va2), B, S)

    dec = jnp.stack([ret_decay_fwd.reshape(RET_HEADS // 2, 2), ret_decay_bwd.reshape(RET_HEADS // 2, 2)], axis=1)
    dec = jnp.broadcast_to(dec.reshape(RET_HEADS // 2, 4, 1), (RET_HEADS // 2, 4, LANES)).astype(F32)
    y_ret = _retention(dec, qr, kr, vr, gr, ret_gn_w[None, :], B, S).reshape(T, RET_V_W)

    pad = LANES - MOE_GROUPS - MOE_N_EXPERTS
    w_r = jnp.concatenate([moe_w_group, moe_w_expert, jnp.zeros((D_MODEL, pad), F32)], axis=1)
    w_r_hi = w_r.astype(BF16)
    w_r_lo = (w_r - w_r_hi.astype(F32)).astype(BF16)
    b_r = jnp.concatenate([moe_b_group, moe_b_expert, jnp.zeros((pad,), F32)])[None, :]

    h_mid, hn, route, route_t, cnt = _merge_route(
        y_att, y_ret, gates, h_in, w_attn_branch.astype(BF16), w_ret_branch.astype(BF16),
        w_out.astype(BF16), norm_moe_w[None, :], w_r_hi, w_r_lo, b_r, T, S)

    bm = MOE_BM
    counts = cnt[0, ROUTER_EXPERT_LANE0:ROUTER_EXPERT_LANE0 + MOE_N_EXPERTS].astype(jnp.int32)
    nblk = (counts + bm - 1) // bm
    blk_end = jnp.cumsum(nblk)
    pstart = (blk_end - nblk) * bm
    n_blocks = (2 * T) // bm + MOE_N_EXPERTS
    n_active = blk_end[-1]
    bidx = jnp.minimum(jnp.arange(n_blocks, dtype=jnp.int32), n_active - 1)
    blk_eid = jnp.sum(bidx[:, None] >= blk_end[None, :], axis=1).astype(jnp.int32)
    mine = blk_eid[:, None] == jnp.arange(MOE_N_EXPERTS, dtype=jnp.int32)[None, :]
    seg_end = jnp.sum(jnp.where(mine, (pstart + counts)[None, :], 0), axis=1)
    blk_valid = jnp.clip(seg_end - bidx * bm, 0, bm)
    blk_valid = jnp.where(jnp.arange(n_blocks) < n_active, blk_valid, 0).astype(jnp.int32)
    blk_fresh = jnp.concatenate([jnp.ones((1,), jnp.int32), (blk_eid[1:] != blk_eid[:-1]).astype(jnp.int32)])
    eid = route_t[ROUTE_EID:ROUTE_EID + 2].astype(jnp.int32)
    rank = route_t[ROUTE_RANK:ROUTE_RANK + 2].astype(jnp.int32)
    start = jnp.sum(jnp.where(eid[..., None] == jnp.arange(MOE_N_EXPERTS, dtype=jnp.int32),
                              pstart.astype(jnp.int32), 0), axis=-1)
    slots = (start + rank).reshape(2 * T)

    x_slots = _dispatch(slots, hn, n_blocks * bm, T)
    y_slots = _experts(bidx, blk_eid, blk_valid, blk_fresh, x_slots, moe_w1, moe_w3, moe_w2, n_blocks)
    return _combine(slots, y_slots, h_mid, route, next_norm_w[None, :], T)


def kernel(x, norm_mix_w, w_in, b_branch_gate, ret_decay_fwd, ret_decay_bwd, ret_gn_w, w_attn_branch,
           w_ret_branch, w_out, norm_moe_w, moe_w_group, moe_b_group, moe_w_expert, moe_b_expert, moe_w1,
           moe_w3, moe_w2, norm_final_w):
    B, S, D = x.shape
    depth = norm_mix_w.shape[0]
    assert depth == 1, "the final norm is fused into the layer's combine stage"
    assert D == D_MODEL and S % TM_INPROJ == 0 and (B * S) < (1 << 24)
    cos_t, sin_t = _rotary_tables(S)
    out = _layer(x.reshape(B * S, D), norm_mix_w[0], w_in[0], b_branch_gate[0], ret_decay_fwd[0],
                 ret_decay_bwd[0], ret_gn_w[0], w_attn_branch[0], w_ret_branch[0], w_out[0], norm_moe_w[0],
                 moe_w_group[0], moe_b_group[0], moe_w_expert[0], moe_b_expert[0], moe_w1[0], moe_w3[0],
                 moe_w2[0], norm_final_w, B, S, cos_t, sin_t)
    return out.reshape(B, S, D)
```

```python
import functools

import numpy as np
import jax
import jax.numpy as jnp
from jax import lax
from jax.experimental import pallas as pl
from jax.experimental.pallas import tpu as pltpu

F32 = jnp.float32
BF16 = jnp.bfloat16

D_MODEL = 1024
HEAD_DIM = 64
ATTN_PAIRS = ((128, 1), (512, 4), (2048, 16))
ATTN_HEADS_PER_GROUP = 8
ATTN_GROUP_W = ATTN_HEADS_PER_GROUP * HEAD_DIM
ATTN_HALF = 64
ROPE_THETA = 10000.0
RET_HEADS = 8
RET_QK_DIM = 64
RET_V_DIM = 128
RET_CHUNK = 128
RET_QK_W = RET_HEADS * RET_QK_DIM
RET_V_W = RET_HEADS * RET_V_DIM
MOE_GROUPS = 8
MOE_EXPERTS_PER_GROUP = 8
MOE_N_EXPERTS = MOE_GROUPS * MOE_EXPERTS_PER_GROUP
MOE_HIDDEN = 512
NORM_EPS = 1e-6

LANES = 128
ROW_TILES = D_MODEL // LANES
NEG_BIG = -1e30
LOG2_E = 1.4426950408889634

TM_INPROJ = 512
TM_MERGE = 512
MERGE_SPLIT = 2
MERGE_COL_CHUNKS = 4
TM_COMBINE = 256
MOE_BM = 256
ATTN_QB = 128
ATTN_UNROLL = 8
RET_UNROLL = 8

VMEM_LIMIT = 56 * 1024 * 1024

_A = 3 * ATTN_GROUP_W
OFF_QA, OFF_KA, OFF_VA = 0, _A, 2 * _A
OFF_QR = 3 * _A
OFF_KR = OFF_QR + RET_QK_W
OFF_VR = OFF_KR + RET_QK_W
OFF_GR = OFF_VR + RET_V_W
OFF_GL = OFF_GR + RET_V_W
IN_W = OFF_GL + 2 * D_MODEL


def _cparams(sem, vmem=VMEM_LIMIT):
    return pltpu.CompilerParams(dimension_semantics=sem, vmem_limit_bytes=vmem)


def _inproj_body(x_ref, nw_ref, w_ref, bg_ref, cos_ref, sin_ref,
                 qa0, ka0, va0, qa1, ka1, va1, qa2, ka2, va2, qr, kr, vr, gr, gt,
                 stage_ref):
    tm = x_ref.shape[0]
    x = x_ref[...]
    ms = jnp.mean(x * x, axis=-1, keepdims=True)
    xn = (x * lax.rsqrt(ms + NORM_EPS) * nw_ref[...]).astype(BF16)
    cos = cos_ref[...]
    sin = sin_ref[...]
    lane = lax.broadcasted_iota(jnp.int32, (tm, LANES), 1)
    first_half = (lane & (HEAD_DIM - 1)) < (HEAD_DIM // 2)

    def proj(c0, width):
        return jnp.dot(xn, w_ref[:, c0:c0 + width], preferred_element_type=F32)

    def rotary(a, scale):
        partner = jnp.where(first_half, pltpu.roll(a, LANES - HEAD_DIM // 2, 1),
                            pltpu.roll(a, HEAD_DIM // 2, 1))
        r = a * cos + partner * sin
        return r * scale if scale != 1.0 else r

    def chunks(acc):
        return [acc[:, c * LANES:(c + 1) * LANES] for c in range(acc.shape[1] // LANES)]

    def store_natural(out_ref, acc, fn):
        for c, a in enumerate(chunks(acc)):
            out_ref[:, c * LANES:(c + 1) * LANES] = fn(a).astype(out_ref.dtype)

    def store_pairs(out_ref, acc, fn):
        for c, a in enumerate(chunks(acc)):
            out_ref[0, c] = fn(a).astype(out_ref.dtype)

    def store_strided(out_ref, acc, fn, d):
        for c, a in enumerate(chunks(acc)):
            stage_ref[c] = fn(a)
        for c in range(acc.shape[1] // LANES):
            for r in range(d):
                out_ref[0, c, r] = stage_ref[c, pl.ds(r, tm // d, stride=d), :].astype(out_ref.dtype)

    ident = lambda a: a
    rot_q = lambda a: rotary(a, HEAD_DIM ** -0.5 * LOG2_E)
    rot_1 = lambda a: rotary(a, 1.0)
    rot_k = lambda a: rotary(a, RET_QK_DIM ** -0.5)

    W = ATTN_GROUP_W
    store_pairs(qa0, proj(OFF_QA, W), rot_q)
    store_pairs(ka0, proj(OFF_KA, W), rot_1)
    store_pairs(va0, proj(OFF_VA, W), ident)
    for g, (qo, ko, vo) in ((1, (qa1, ka1, va1)), (2, (qa2, ka2, va2))):
        d = ATTN_PAIRS[g][1]
        store_strided(qo, proj(OFF_QA + g * W, W), rot_q, d)
        store_strided(ko, proj(OFF_KA + g * W, W), rot_1, d)
        store_strided(vo, proj(OFF_VA + g * W, W), ident, d)
    store_natural(qr, proj(OFF_QR, RET_QK_W), rot_1)
    store_natural(kr, proj(OFF_KR, RET_QK_W), rot_k)
    for h in range(RET_V_W // W):
        vr[:, h * W:(h + 1) * W] = proj(OFF_VR + h * W, W).astype(vr.dtype)
        gr[:, h * W:(h + 1) * W] = proj(OFF_GR + h * W, W).astype(gr.dtype)
    for h in range(2 * D_MODEL // W):
        z = proj(OFF_GL + h * W, W) + bg_ref[:, h * W:(h + 1) * W]
        gt[:, h * W:(h + 1) * W] = jax.nn.sigmoid(z).astype(gt.dtype)


def _in_projection(x2, norm_w, w_bf, b_gate, cos_t, sin_t, B, S):
    T = B * S
    tm = TM_INPROJ
    nt = S // tm
    W = ATTN_GROUP_W
    row = lambda i: (i, 0)
    const = lambda i: (0, 0)
    nat = lambda width: pl.BlockSpec((tm, width), row)

    P = W // LANES

    def strided_spec(d):
        return pl.BlockSpec((1, P, d, tm // d, LANES), lambda i: (i // nt, 0, 0, i % nt, 0))

    def strided_shape(d):
        return jax.ShapeDtypeStruct((B, P, d, S // d, LANES), BF16)

    pair_spec = pl.BlockSpec((1, P, tm, LANES), lambda i: (i // nt, 0, i % nt, 0))
    pair_shape = jax.ShapeDtypeStruct((B, P, S, LANES), BF16)
    nat_shape = lambda width: jax.ShapeDtypeStruct((T, width), BF16)
    d1, d2 = ATTN_PAIRS[1][1], ATTN_PAIRS[2][1]
    out_shape = ([pair_shape] * 3 + [strided_shape(d1)] * 3 + [strided_shape(d2)] * 3
                 + [nat_shape(RET_QK_W)] * 2 + [nat_shape(RET_V_W)] * 2 + [nat_shape(2 * D_MODEL)])
    out_specs = ([pair_spec] * 3 + [strided_spec(d1)] * 3 + [strided_spec(d2)] * 3
                 + [nat(RET_QK_W)] * 2 + [nat(RET_V_W)] * 2 + [nat(2 * D_MODEL)])
    in_specs = [
        pl.BlockSpec((tm, D_MODEL), row),
        pl.BlockSpec((1, D_MODEL), const),
        pl.BlockSpec((D_MODEL, IN_W), const, pipeline_mode=pl.Buffered(1)),
        pl.BlockSpec((1, 2 * D_MODEL), const),
        pl.BlockSpec((tm, LANES), lambda i: (i % nt, 0)),
        pl.BlockSpec((tm, LANES), lambda i: (i % nt, 0)),
    ]
    return pl.pallas_call(
        _inproj_body,
        grid=(T // tm,),
        in_specs=in_specs,
        out_specs=out_specs,
        out_shape=out_shape,
        scratch_shapes=[pltpu.VMEM((W // LANES, tm, LANES), F32)],
        compiler_params=_cparams(("parallel",)),
        name="in_projection",
    )(x2, norm_w, w_bf, b_gate, cos_t, sin_t)


def _attn_body(q0_ref, k0_ref, v0_ref, q1_ref, k1_ref, v1_ref, q2_ref, k2_ref, v2_ref, o_ref,
               va_ref, vb_ref, part_ref, bias_ref, *, S):
    QB, H = ATTN_QB, ATTN_HALF
    lane = lax.broadcasted_iota(jnp.int32, (QB, LANES), 1)
    head0 = lane < HEAD_DIM
    ones = jnp.ones((QB, LANES), BF16)

    qi = lax.broadcasted_iota(jnp.int32, (QB, QB + 2 * H), 0)
    ki = lax.broadcasted_iota(jnp.int32, (QB, QB + 2 * H), 1)
    for n in range(3):
        bias_ref[n] = jnp.where(jnp.abs(ki - qi - n * H) <= H, 0.0, NEG_BIG).astype(F32)

    def scores(q_rows, k_rows, bias, h):
        qm = jnp.where(head0 if h == 0 else jnp.logical_not(head0), q_rows, jnp.zeros_like(q_rows))
        return lax.dot_general(qm, k_rows, (((1,), (1,)), ((), ())), preferred_element_type=F32) + bias

    def weights(s):
        m = jnp.max(s, axis=-1, keepdims=True)
        return m, jnp.exp2(s - m).astype(BF16)

    def heads_to_lanes(m0, a, m1, b):
        num = jnp.where(head0, a, b)
        den = pltpu.roll(jnp.where(head0, b, a), HEAD_DIM, 1)
        mx = jnp.where(head0, m0, m1)
        return num, den, mx

    def run_group(q_ref, k_ref, v_ref, d, prepare, store):
        L = S // d
        KW = min(L, QB + 2 * H)
        nb = L // QB

        def fill(t, carry):
            r = t // nb
            rows = pl.ds(pl.multiple_of((t % nb) * QB, QB), QB)
            v = v_ref[0, 0, r, rows, :]
            dst = pl.ds(pl.multiple_of(t * QB, QB), QB)
            va_ref[dst, :] = jnp.where(head0, v, ones)
            vb_ref[dst, :] = jnp.where(head0, ones, v)
            return carry

        lax.fori_loop(0, S // QB, fill, 0, unroll=4)

        def trip(i, carry):
            blocks = []
            for u in range(ATTN_UNROLL):
                t = i * ATTN_UNROLL + u
                r = t // nb
                q0 = pl.multiple_of((t % nb) * QB, QB)
                ws = pl.multiple_of(jnp.clip(q0 - H, 0, L - KW), H)
                bias = bias_ref[(q0 - ws) // H][:, :KW]
                q_rows = q_ref[0, 0, r, pl.ds(q0, QB), :]
                k_rows = k_ref[0, 0, r, pl.ds(ws, KW), :]
                vrows = pl.ds(pl.multiple_of(r * L + ws, H), KW)
                blocks.append((r, q0, vrows, [scores(q_rows, k_rows, bias, h) for h in range(2)]))
            blocks = [(r, q0, vrows, [weights(s) for s in ss]) for r, q0, vrows, ss in blocks]
            done = []
            for r, q0, vrows, ((m0, p0), (m1, p1)) in blocks:
                a = jnp.dot(p0, va_ref[vrows, :], preferred_element_type=F32)
                b = jnp.dot(p1, vb_ref[vrows, :], preferred_element_type=F32)
                done.append((r, q0, m0, a, m1, b))
            done = [(r, q0, prepare(q0, *heads_to_lanes(m0, a, m1, b))) for r, q0, m0, a, m1, b in done]
            for r, q0, vals in done:
                store(r, q0, vals)
            return carry

        lax.fori_loop(0, S // QB // ATTN_UNROLL, trip, 0)

    def store_partial(g):
        d = ATTN_PAIRS[g][1]

        def store(r, q0, vals):
            rows = pl.ds(r + q0 * d, QB, stride=d)
            for n, val in enumerate(vals):
                part_ref[3 * (g - 1) + n, rows, :] = val
        return store

    keep = lambda q0, num, den, mx: (num, den, mx)
    run_group(q1_ref, k1_ref, v1_ref, ATTN_PAIRS[1][1], keep, store_partial(1))
    run_group(q2_ref, k2_ref, v2_ref, ATTN_PAIRS[2][1], keep, store_partial(2))

    def merge(q0, num, den, mx):
        rows = pl.ds(q0, QB)
        nums = [num, part_ref[0, rows, :], part_ref[3, rows, :]]
        dens = [den, part_ref[1, rows, :], part_ref[4, rows, :]]
        mxs = [mx, part_ref[2, rows, :], part_ref[5, rows, :]]
        top = jnp.maximum(jnp.maximum(mxs[0], mxs[1]), mxs[2])
        ws = [jnp.exp2(m - top) for m in mxs]
        n = ws[0] * nums[0] + ws[1] * nums[1] + ws[2] * nums[2]
        dn = ws[0] * dens[0] + ws[1] * dens[1] + ws[2] * dens[2]
        return (n / dn).astype(o_ref.dtype)

    def store_out(r, q0, y):
        o_ref[0, 0, pl.ds(q0, QB), :] = y

    run_group(q0_ref, k0_ref, v0_ref, ATTN_PAIRS[0][1], merge, store_out)


def _attention(qkv, B, S):
    P = ATTN_GROUP_W // LANES
    in_specs = []
    for _, d in ATTN_PAIRS:
        in_specs += [pl.BlockSpec((1, 1, d, S // d, LANES), lambda b, p: (b, p, 0, 0, 0))] * 3
    return pl.pallas_call(
        functools.partial(_attn_body, S=S),
        grid=(B, P),
        in_specs=in_specs,
        out_specs=pl.BlockSpec((1, 1, S, LANES), lambda b, p: (b, p, 0, 0)),
        out_shape=jax.ShapeDtypeStruct((B, P, S, LANES), BF16),
        scratch_shapes=[pltpu.VMEM((S, LANES), BF16), pltpu.VMEM((S, LANES), BF16),
                        pltpu.VMEM((6, S, LANES), F32),
                        pltpu.VMEM((3, ATTN_QB, ATTN_QB + 2 * ATTN_HALF), F32)],
        compiler_params=_cparams(("parallel", "parallel")),
        name="attention",
    )(*qkv)


def _log_sigmoid(z):
    return jnp.minimum(z, 0.0) - jnp.log(1.0 + jnp.exp(-jnp.abs(z)))


def _ret_body(dec_ref, q_ref, k_ref, v_ref, g_ref, gnw_ref, o_ref, kt_ref, sf_ref, sb_ref, *, S):
    C = RET_CHUNK
    nc = S // C
    lg = _log_sigmoid(dec_ref[0])
    a_row = lax.broadcasted_iota(jnp.int32, (C, LANES), 0).astype(F32)
    lane = lax.broadcasted_iota(jnp.int32, (C, LANES), 1)
    rel = (lax.broadcasted_iota(jnp.int32, (C, C), 0) - lax.broadcasted_iota(jnp.int32, (C, C), 1)).astype(F32)

    heads = []
    for h in range(2):
        lgf = lg[h:h + 1, :]
        lgb = lg[2 + h:3 + h, :]
        in_head = (lane < RET_QK_DIM) if h == 0 else (lane >= RET_QK_DIM)
        heads.append(dict(
            in_head=in_head,
            xi_f=jnp.where(in_head, jnp.exp((a_row + 1.0) * lgf), 0.0),
            xi_b=jnp.where(in_head, jnp.exp((C - a_row) * lgb), 0.0),
            zeta_f=jnp.exp((C - 1.0 - a_row) * lgf),
            zeta_b=jnp.exp(a_row * lgb),
            dloc=jnp.where(rel > 0, jnp.exp(rel * lgf), jnp.where(rel < 0, jnp.exp(-rel * lgb), 2.0)),
            cd_f=jnp.exp(C * lgf),
            cd_b=jnp.exp(C * lgb),
        ))

    def chunk_rows(n):
        return pl.ds(pl.multiple_of(n * C, C), C)

    def v_head(rows, h):
        return v_ref[0, rows, h * RET_V_DIM:(h + 1) * RET_V_DIM]

    def products(i, carry):
        items = []
        for u in range(RET_UNROLL):
            n = i * RET_UNROLL + u
            rows = chunk_rows(n)
            kt = k_ref[0, rows, :].astype(F32).T.astype(BF16)
            kt_ref[n] = kt
            for h, hd in enumerate(heads):
                vh = v_head(rows, h).astype(F32)
                items.append((n, h, kt, (vh * hd["zeta_f"]).astype(BF16), (vh * hd["zeta_b"]).astype(BF16)))
        outs = [(n, h, jnp.dot(kt, vf, preferred_element_type=F32), jnp.dot(kt, vb, preferred_element_type=F32))
                for n, h, kt, vf, vb in items]
        for n, h, f, b in outs:
            sf_ref[n, h] = f
            sb_ref[n, h] = b
        return carry

    lax.fori_loop(0, nc // RET_UNROLL, products, 0)

    def scan(ref, key, order):
        def step(i, state):
            n = order(i)
            new = []
            for h, hd in enumerate(heads):
                kv = ref[n, h]
                ref[n, h] = state[h]
                new.append(hd[key] * state[h] + kv)
            return tuple(new)
        zero = jnp.zeros(ref.shape[2:], F32)
        lax.fori_loop(0, nc, step, (zero, zero))

    scan(sf_ref, "cd_f", lambda i: i)
    scan(sb_ref, "cd_b", lambda i: nc - 1 - i)

    def outputs(i, carry):
        items = []
        for u in range(RET_UNROLL):
            n = i * RET_UNROLL + u
            rows = chunk_rows(n)
            qp = q_ref[0, rows, :]
            qf = qp.astype(F32)
            kt = kt_ref[n]
            for h, hd in enumerate(heads):
                qm = jnp.where(hd["in_head"], qp, jnp.zeros_like(qp))
                qx = jnp.concatenate([(qf * hd["xi_f"]).astype(BF16), (qf * hd["xi_b"]).astype(BF16)], axis=1)
                st = jnp.concatenate([sf_ref[n, h].astype(BF16), sb_ref[n, h].astype(BF16)], axis=0)
                items.append((rows, h, hd, jnp.dot(qm, kt, preferred_element_type=F32),
                              jnp.dot(qx, st, preferred_element_type=F32)))
        items = [(rows, h, hd, (s * hd["dloc"]).astype(BF16), cross) for rows, h, hd, s, cross in items]
        items = [(rows, h, cross + jnp.dot(p, v_head(rows, h), preferred_element_type=F32))
                 for rows, h, hd, p, cross in items]
        for rows, h, ret in items:
            mu = jnp.mean(ret, axis=-1, keepdims=True)
            xc = ret - mu
            var = jnp.mean(xc * xc, axis=-1, keepdims=True)
            cols = slice(h * RET_V_DIM, (h + 1) * RET_V_DIM)
            gate = g_ref[0, rows, cols].astype(F32)
            y = xc * lax.rsqrt(var + NORM_EPS) * gnw_ref[:, cols] * (gate * jax.nn.sigmoid(gate))
            o_ref[0, rows, cols] = y.astype(o_ref.dtype)
        return carry

    lax.fori_loop(0, nc // RET_UNROLL, outputs, 0)


def _retention(dec, qr, kr, vr, gr, gn_w, B, S):
    nc = S // RET_CHUNK
    npairs = RET_HEADS // 2
    qk_spec = pl.BlockSpec((1, S, 2 * RET_QK_DIM), lambda b, p: (b, 0, p))
    v_spec = pl.BlockSpec((1, S, 2 * RET_V_DIM), lambda b, p: (b, 0, p))
    return pl.pallas_call(
        functools.partial(_ret_body, S=S),
        grid=(B, npairs),
        in_specs=[pl.BlockSpec((1, 4, LANES), lambda b, p: (p, 0, 0)),
                  qk_spec, qk_spec, v_spec, v_spec,
                  pl.BlockSpec((1, 2 * RET_V_DIM), lambda b, p: (0, p))],
        out_specs=v_spec,
        out_shape=jax.ShapeDtypeStruct((B, S, RET_V_W), BF16),
        scratch_shapes=[pltpu.VMEM((nc, 2 * RET_QK_DIM, RET_CHUNK), BF16),
                        pltpu.VMEM((nc, 2, 2 * RET_QK_DIM, RET_V_DIM), F32),
                        pltpu.VMEM((nc, 2, 2 * RET_QK_DIM, RET_V_DIM), F32)],
        compiler_params=_cparams(("parallel", "parallel")),
        name="retention",
    )(dec, qr.reshape(B, S, RET_QK_W), kr.reshape(B, S, RET_QK_W),
      vr.reshape(B, S, RET_V_W), gr.reshape(B, S, RET_V_W), gn_w)


ROUTE_EID, ROUTE_RANK, ROUTE_GATE = 0, 2, 4
ROUTE_FIELDS = 8
ROUTER_EXPERT_LANE0 = MOE_GROUPS


def _merge_body(yatt_ref, yret_ref, gt_ref, x_ref, wa_ref, wb_ref, wo_ref,
                nw_ref, wrh_ref, wrl_ref, br_ref,
                h_ref, hn_ref, route_ref, route_t_ref, cnt_ref):
    tm = x_ref.shape[0]
    hm = tm // MERGE_SPLIT
    i = pl.program_id(0)
    cw = D_MODEL // MERGE_COL_CHUNKS

    def branch_products(rows):
        y_att = jnp.concatenate([yatt_ref[0, c, rows, :] for c in range(yatt_ref.shape[1])], axis=1)
        y_ret = yret_ref[rows, :]
        chunks = []
        for c in range(MERGE_COL_CHUNKS):
            cols = slice(c * cw, (c + 1) * cw)
            a = jnp.dot(y_att, wa_ref[:, cols], preferred_element_type=F32)
            b = jnp.dot(y_ret, wb_ref[:, cols], preferred_element_type=F32)
            g_att = gt_ref[rows, c * cw:(c + 1) * cw].astype(F32)
            g_ret = gt_ref[rows, D_MODEL + c * cw:D_MODEL + (c + 1) * cw].astype(F32)
            chunks.append((g_att * a + g_ret * b).astype(BF16))
        return jnp.concatenate(chunks, axis=1)

    def residual_norm(hf, rows, merged):
        mix = jnp.dot(merged, wo_ref[...], preferred_element_type=F32)
        h = x_ref[rows, :] + mix
        h_ref[rows, :] = h
        ms = jnp.mean(h * h, axis=-1, keepdims=True)
        hn = h * lax.rsqrt(ms + NORM_EPS) * nw_ref[...]
        for j in range(ROW_TILES):
            hn_ref[pl.ds(hf * hm * ROW_TILES + j, hm, stride=ROW_TILES), :] = hn[:, j * LANES:(j + 1) * LANES]
        hi = hn.astype(BF16)
        lo = (hn - hi.astype(F32)).astype(BF16)
        return hi, lo

    def router_logits(hi, lo):
        return (jnp.dot(hi, wrh_ref[...], preferred_element_type=F32)
                + jnp.dot(hi, wrl_ref[...], preferred_element_type=F32)
                + jnp.dot(lo, wrh_ref[...], preferred_element_type=F32)) + br_ref[...]

    lane = lax.broadcasted_iota(jnp.int32, (hm, LANES), 1)
    far = jnp.int32(4 * LANES)

    def first_argmax(vals, vmax):
        return jnp.min(jnp.where(vals == vmax, lane, far), axis=-1, keepdims=True)

    def route(logits):
        is_group = lane < MOE_GROUPS
        gl = jnp.where(is_group, logits, NEG_BIG)
        gmax = jnp.max(gl, axis=-1, keepdims=True)
        g_w = 1.0 / jnp.sum(jnp.where(is_group, jnp.exp(gl - gmax), 0.0), axis=-1, keepdims=True)
        g_idx = first_argmax(gl, gmax)
        e_lane = lane - ROUTER_EXPERT_LANE0
        in_group = (e_lane >= 0) & (e_lane < MOE_N_EXPERTS) & (jnp.right_shift(e_lane, 3) == g_idx)
        el = jnp.where(in_group, logits, NEG_BIG)
        m1 = jnp.max(el, axis=-1, keepdims=True)
        i1 = first_argmax(el, m1)
        el2 = jnp.where(lane == i1, NEG_BIG, el)
        m2 = jnp.max(el2, axis=-1, keepdims=True)
        i2 = first_argmax(el2, m2)
        ex = jnp.exp(m2 - m1)
        return i1, i2, g_w / (1.0 + ex), g_w * ex / (1.0 + ex)

    rows = [slice(hf * hm, (hf + 1) * hm) for hf in range(MERGE_SPLIT)]
    merged = [branch_products(r) for r in rows]
    split = [residual_norm(hf, r, m) for hf, (r, m) in enumerate(zip(rows, merged))]
    routed = [route(router_logits(hi, lo)) for hi, lo in split]

    @pl.when(i == 0)
    def _():
        cnt_ref[...] = jnp.zeros(cnt_ref.shape, F32)

    r_idx = lax.broadcasted_iota(jnp.int32, (hm, hm), 0)
    c_idx = lax.broadcasted_iota(jnp.int32, (hm, hm), 1)
    lower = jnp.where(c_idx < r_idx, 1.0, 0.0).astype(BF16)
    running = cnt_ref[...]
    for hf, (i1, i2, gate1, gate2) in enumerate(routed):
        hot1 = lane == i1
        hot2 = lane == i2
        onehot = jnp.where(hot1 | hot2, 1.0, 0.0)
        before = jnp.dot(lower, onehot.astype(BF16), preferred_element_type=F32) + running
        rank1 = jnp.sum(jnp.where(hot1, before, 0.0), axis=-1, keepdims=True)
        rank2 = jnp.sum(jnp.where(hot2, before, 0.0), axis=-1, keepdims=True)
        running = running + jnp.sum(onehot, axis=0, keepdims=True)
        rec = jnp.zeros((hm, LANES), F32)
        for pos, val in ((ROUTE_EID, (i1 - ROUTER_EXPERT_LANE0).astype(F32)),
                         (ROUTE_EID + 1, (i2 - ROUTER_EXPERT_LANE0).astype(F32)),
                         (ROUTE_RANK, rank1), (ROUTE_RANK + 1, rank2),
                         (ROUTE_GATE, gate1), (ROUTE_GATE + 1, gate2)):
            rec = jnp.where(lane == pos, val, rec)
        route_ref[rows[hf], :] = rec
        route_t_ref[:, hf * hm:(hf + 1) * hm] = rec.T[:route_t_ref.shape[0], :]
    cnt_ref[...] = running


def _merge_route(y_att, y_ret, gates, x2, wa, wb, wo, nw, wr_hi, wr_lo, b_r, T, S):
    tm = TM_MERGE
    nt = S // tm
    row = lambda i: (i, 0)
    const = lambda i: (0, 0)
    full = lambda arr: pl.BlockSpec(arr.shape, const)
    in_specs = ([pl.BlockSpec((1, y_att.shape[1], tm, LANES), lambda i: (i // nt, 0, i % nt, 0)),
                 pl.BlockSpec((tm, RET_V_W), row), pl.BlockSpec((tm, 2 * D_MODEL), row),
                 pl.BlockSpec((tm, D_MODEL), row),
                 full(wa), full(wb), full(wo), full(nw), full(wr_hi), full(wr_lo), full(b_r)])
    return pl.pallas_call(
        _merge_body,
        grid=(T // tm,),
        in_specs=in_specs,
        out_specs=[pl.BlockSpec((tm, D_MODEL), row),
                   pl.BlockSpec((tm * ROW_TILES, LANES), row),
                   pl.BlockSpec((tm, LANES), row), pl.BlockSpec((ROUTE_FIELDS, tm), lambda i: (0, i)),
                   pl.BlockSpec((1, LANES), const)],
        out_shape=[jax.ShapeDtypeStruct((T, D_MODEL), F32), jax.ShapeDtypeStruct((T * ROW_TILES, LANES), F32),
                   jax.ShapeDtypeStruct((T, LANES), F32), jax.ShapeDtypeStruct((ROUTE_FIELDS, T), F32),
                   jax.ShapeDtypeStruct((1, LANES), F32)],
        compiler_params=_cparams(("arbitrary",)),
        name="merge_route",
    )(y_att, y_ret, gates, x2, wa, wb, wo, nw, wr_hi, wr_lo, b_r)


ISSUE_UNROLL = 8
DUMP_TILES = 1024


def _slot_table_body(slot_ref, pad_lo_ref, pad_hi_ref, row_ref, *, n_assign):
    def place(a, carry):
        row_ref[slot_ref[a]] = a * ROW_TILES
        return carry

    lax.fori_loop(0, n_assign, place, 0, unroll=ISSUE_UNROLL)

    def pad_range(e, carry):
        def mark(s, c):
            row_ref[s] = (n_assign + (s & (DUMP_TILES - 1))) * ROW_TILES
            return c
        lax.fori_loop(pad_lo_ref[e], pad_hi_ref[e], mark, 0)
        return carry

    lax.fori_loop(0, pad_lo_ref.shape[0], pad_range, 0)


def _slot_table(slots, pad_lo, pad_hi, n_slots):
    grid_spec = pltpu.PrefetchScalarGridSpec(
        num_scalar_prefetch=3,
        grid=(1,),
        in_specs=[],
        out_specs=pl.BlockSpec(memory_space=pltpu.SMEM),
    )
    return pl.pallas_call(
        functools.partial(_slot_table_body, n_assign=slots.shape[0]),
        grid_spec=grid_spec,
        out_shape=jax.ShapeDtypeStruct((n_slots,), jnp.int32),
        compiler_params=_cparams(("arbitrary",)),
        name="moe_slot_table",
    )(slots, pad_lo, pad_hi)


def _tile(row):
    return pl.ds(pl.multiple_of(row, ROW_TILES), ROW_TILES)


def _expert_body(row_ref, eid_ref, valid_ref, fresh_ref, nact_ref,
                 hn_ref, w1_ref, w3_ref, w2_ref, ya_ref,
                 xbuf, ybuf, w1b, w3b, w2b, gsem, ssem, *, T):
    i = pl.program_id(0)
    n_act = nact_ref[0]
    bm = MOE_BM
    token_rows = T * ROW_TILES

    def gather_issue(blk, buf):
        def pair(jj, carry):
            for par in range(2):
                j = 2 * jj + par
                src = row_ref[blk * bm + j] & (token_rows - 1)
                pltpu.make_async_copy(hn_ref.at[_tile(src)], xbuf.at[buf, _tile(j * ROW_TILES)],
                                      gsem.at[buf]).start(priority=par)
            return carry
        lax.fori_loop(0, bm // 2, pair, 0, unroll=ISSUE_UNROLL // 2)

    def scatter_issue(blk, buf):
        def pair(jj, carry):
            for par in range(2):
                j = 2 * jj + par
                pltpu.make_async_copy(ybuf.at[buf, _tile(j * ROW_TILES)], ya_ref.at[_tile(row_ref[blk * bm + j])],
                                      ssem.at[buf]).start(priority=par)
            return carry
        lax.fori_loop(0, bm // 2, pair, 0, unroll=ISSUE_UNROLL // 2)

    def gather_wait(buf):
        pltpu.make_async_copy(hn_ref.at[pl.ds(0, bm * ROW_TILES)], xbuf.at[buf], gsem.at[buf]).wait()

    def scatter_wait(buf):
        pltpu.make_async_copy(ybuf.at[buf], ya_ref.at[pl.ds(0, bm * ROW_TILES)], ssem.at[buf]).wait()

    @pl.when(i < n_act)
    def _():
        buf = i % 2

        @pl.when(i == 0)
        def _():
            gather_issue(0, 0)

        gather_wait(buf)

        @pl.when(i + 1 < n_act)
        def _():
            gather_issue(i + 1, 1 - buf)

        @pl.when(i >= 2)
        def _():
            scatter_wait(buf)

        @pl.when(fresh_ref[i] == 1)
        def _():
            w1b[...] = w1_ref[0].astype(BF16)
            w3b[...] = w3_ref[0].astype(BF16)
            w2b[...] = w2_ref[0].astype(BF16)

        live = lax.broadcasted_iota(jnp.int32, (bm, LANES), 0) < valid_ref[i]
        x = jnp.concatenate(
            [jnp.where(live, xbuf[buf, pl.ds(j, bm, stride=ROW_TILES), :], 0.0).astype(BF16)
             for j in range(ROW_TILES)], axis=1)
        a = jnp.dot(x, w1b[...], preferred_element_type=F32)
        b = jnp.dot(x, w3b[...], preferred_element_type=F32)
        hid = (a * jax.nn.sigmoid(a) * b).astype(BF16)
        y = jnp.dot(hid, w2b[...], preferred_element_type=F32)
        for j in range(ROW_TILES):
            ybuf[buf, pl.ds(j, bm, stride=ROW_TILES), :] = y[:, j * LANES:(j + 1) * LANES]

        scatter_issue(i, buf)

        @pl.when(i == n_act - 1)
        def _():
            @pl.when(i >= 1)
            def _():
                scatter_wait(1 - buf)
            scatter_wait(buf)


def _experts(asg, blk_eid, blk_valid, blk_fresh, n_active, hn, w1, w3, w2, n_blocks, T):
    bm = MOE_BM
    w_spec = lambda shape: pl.BlockSpec(shape, lambda i, asg, eid, val, fr, na: (eid[i], 0, 0))
    grid_spec = pltpu.PrefetchScalarGridSpec(
        num_scalar_prefetch=5,
        grid=(n_blocks,),
        in_specs=[
            pl.BlockSpec(memory_space=pl.ANY),
            w_spec((1, D_MODEL, MOE_HIDDEN)), w_spec((1, D_MODEL, MOE_HIDDEN)), w_spec((1, MOE_HIDDEN, D_MODEL)),
        ],
        out_specs=pl.BlockSpec(memory_space=pl.ANY),
        scratch_shapes=[pltpu.VMEM((2, bm * ROW_TILES, LANES), F32), pltpu.VMEM((2, bm * ROW_TILES, LANES), F32),
                        pltpu.VMEM((D_MODEL, MOE_HIDDEN), BF16), pltpu.VMEM((D_MODEL, MOE_HIDDEN), BF16),
                        pltpu.VMEM((MOE_HIDDEN, D_MODEL), BF16),
                        pltpu.SemaphoreType.DMA((2,)), pltpu.SemaphoreType.DMA((2,))],
    )
    return pl.pallas_call(
        functools.partial(_expert_body, T=T),
        grid_spec=grid_spec,
        out_shape=jax.ShapeDtypeStruct(((2 * T + DUMP_TILES) * ROW_TILES, LANES), F32),
        compiler_params=_cparams(("arbitrary",)),
        name="moe_experts",
    )(asg, blk_eid, blk_valid, blk_fresh, n_active, hn, w1, w3, w2)


def _combine_body(y0_ref, y1_ref, h_ref, route_ref, nw_ref, o_ref):
    tm = h_ref.shape[0]
    route = route_ref[...]
    g1 = route[:, ROUTE_GATE:ROUTE_GATE + 1]
    g2 = route[:, ROUTE_GATE + 1:ROUTE_GATE + 2]
    hs = []
    ss = jnp.zeros((tm, 1), F32)
    for j in range(ROW_TILES):
        tile_row = pl.ds(j, tm, stride=ROW_TILES)
        hj = h_ref[:, j * LANES:(j + 1) * LANES] + (y0_ref[tile_row, :] * g1 + y1_ref[tile_row, :] * g2)
        hs.append(hj)
        ss = ss + jnp.sum(hj * hj, axis=-1, keepdims=True)
    inv = lax.rsqrt(ss * (1.0 / D_MODEL) + NORM_EPS)
    for j, hj in enumerate(hs):
        cols = slice(j * LANES, (j + 1) * LANES)
        o_ref[:, cols] = hj * inv * nw_ref[:, cols]


def _combine(y_assign, h, route, nw, T):
    tm = TM_COMBINE
    row = lambda i: (i, 0)
    nt = T // tm
    return pl.pallas_call(
        _combine_body,
        grid=(nt,),
        in_specs=[pl.BlockSpec((tm * ROW_TILES, LANES), row),
                  pl.BlockSpec((tm * ROW_TILES, LANES), lambda i: (i + nt, 0)),
                  pl.BlockSpec((tm, D_MODEL), row),
                  pl.BlockSpec((tm, LANES), row),
                  pl.BlockSpec((1, D_MODEL), lambda i: (0, 0))],
        out_specs=pl.BlockSpec((tm, D_MODEL), row),
        out_shape=jax.ShapeDtypeStruct((T, D_MODEL), F32),
        compiler_params=_cparams(("parallel",)),
        name="moe_combine",
    )(y_assign, y_assign, h, route, nw)


def _rotary_tables(S):
    inv_freq = (1.0 / (np.float32(ROPE_THETA) ** (np.arange(0, HEAD_DIM, 2, dtype=np.float32) / HEAD_DIM))
                ).astype(np.float32)
    ang = np.arange(S, dtype=np.float32)[:, None] * inv_freq[None, :]
    cos, sin = np.cos(ang), np.sin(ang)
    reps = LANES // HEAD_DIM
    cos_t = np.tile(np.concatenate([cos, cos], axis=1), (1, reps)).astype(np.float32)
    sin_t = np.tile(np.concatenate([-sin, sin], axis=1), (1, reps)).astype(np.float32)
    return jnp.asarray(cos_t), jnp.asarray(sin_t)


def _layer(h_in, norm_mix_w, w_in, b_branch_gate, ret_decay_fwd, ret_decay_bwd, ret_gn_w, w_attn_branch,
           w_ret_branch, w_out, norm_moe_w, moe_w_group, moe_b_group, moe_w_expert, moe_b_expert,
           moe_w1, moe_w3, moe_w2, next_norm_w, B, S, cos_t, sin_t):
    T = B * S
    (qa0, ka0, va0, qa1, ka1, va1, qa2, ka2, va2, qr, kr, vr, gr, gates) = _in_projection(
        h_in, norm_mix_w[None, :], w_in.astype(BF16), b_branch_gate[None, :], cos_t, sin_t, B, S)

    unit = lambda a: a[:, :, None]
    y_att = _attention((unit(qa0), unit(ka0), unit(va0), qa1, ka1, va1, qa2, ka2, va2), B, S)

    dec = jnp.stack([ret_decay_fwd.reshape(RET_HEADS // 2, 2), ret_decay_bwd.reshape(RET_HEADS // 2, 2)], axis=1)
    dec = jnp.broadcast_to(dec.reshape(RET_HEADS // 2, 4, 1), (RET_HEADS // 2, 4, LANES)).astype(F32)
    y_ret = _retention(dec, qr, kr, vr, gr, ret_gn_w[None, :], B, S).reshape(T, RET_V_W)

    pad = LANES - MOE_GROUPS - MOE_N_EXPERTS
    w_r = jnp.concatenate([moe_w_group, moe_w_expert, jnp.zeros((D_MODEL, pad), F32)], axis=1)
    w_r_hi = w_r.astype(BF16)
    w_r_lo = (w_r - w_r_hi.astype(F32)).astype(BF16)
    b_r = jnp.concatenate([moe_b_group, moe_b_expert, jnp.zeros((pad,), F32)])[None, :]

    h_mid, hn, route, route_t, cnt = _merge_route(
        y_att, y_ret, gates, h_in, w_attn_branch.astype(BF16), w_ret_branch.astype(BF16),
        w_out.astype(BF16), norm_moe_w[None, :], w_r_hi, w_r_lo, b_r, T, S)

    bm = MOE_BM
    counts = cnt[0, ROUTER_EXPERT_LANE0:ROUTER_EXPERT_LANE0 + MOE_N_EXPERTS].astype(jnp.int32)
    nblk = (counts + bm - 1) // bm
    blk_end = jnp.cumsum(nblk)
    pstart = (blk_end - nblk) * bm
    n_blocks = (2 * T) // bm + MOE_N_EXPERTS
    n_active = blk_end[-1]
    bidx = jnp.minimum(jnp.arange(n_blocks, dtype=jnp.int32), n_active - 1)
    blk_eid = jnp.sum(bidx[:, None] >= blk_end[None, :], axis=1).astype(jnp.int32)
    mine = blk_eid[:, None] == jnp.arange(MOE_N_EXPERTS, dtype=jnp.int32)[None, :]
    seg_end = jnp.sum(jnp.where(mine, (pstart + counts)[None, :], 0), axis=1)
    blk_valid = jnp.clip(seg_end - bidx * bm, 0, bm)
    blk_valid = jnp.where(jnp.arange(n_blocks) < n_active, blk_valid, 0).astype(jnp.int32)
    blk_fresh = jnp.concatenate([jnp.ones((1,), jnp.int32), (blk_eid[1:] != blk_eid[:-1]).astype(jnp.int32)])
    eid = route_t[ROUTE_EID:ROUTE_EID + 2].astype(jnp.int32)
    rank = route_t[ROUTE_RANK:ROUTE_RANK + 2].astype(jnp.int32)
    start = jnp.sum(jnp.where(eid[..., None] == jnp.arange(MOE_N_EXPERTS, dtype=jnp.int32),
                              pstart.astype(jnp.int32), 0), axis=-1)
    slots = (start + rank).reshape(2 * T)

    pad_lo = jnp.concatenate([pstart + counts, (n_active * bm)[None]]).astype(jnp.int32)
    pad_hi = jnp.concatenate([pstart + nblk * bm, jnp.full((1,), n_blocks * bm, jnp.int32)]).astype(jnp.int32)
    asg = _slot_table(slots, pad_lo, pad_hi, n_blocks * bm)
    y_assign = _experts(asg, blk_eid, blk_valid, blk_fresh, n_active[None].astype(jnp.int32), hn,
                        moe_w1, moe_w3, moe_w2, n_blocks, T)
    return _combine(y_assign, h_mid, route, next_norm_w[None, :], T)


def kernel(x, norm_mix_w, w_in, b_branch_gate, ret_decay_fwd, ret_decay_bwd, ret_gn_w, w_attn_branch,
           w_ret_branch, w_out, norm_moe_w, moe_w_group, moe_b_group, moe_w_expert, moe_b_expert, moe_w1,
           moe_w3, moe_w2, norm_final_w):
    B, S, D = x.shape
    depth = norm_mix_w.shape[0]
    assert depth == 1, "the final norm is fused into the layer's combine stage"
    assert D == D_MODEL and S % TM_INPROJ == 0 and (B * S) < (1 << 24)
    assert (B * S) & (B * S - 1) == 0, "the expert stage maps output rows to token rows by masking"
    cos_t, sin_t = _rotary_tables(S)
    out = _layer(x.reshape(B * S, D), norm_mix_w[0], w_in[0], b_branch_gate[0], ret_decay_fwd[0],
                 ret_decay_bwd[0], ret_gn_w[0], w_attn_branch[0], w_ret_branch[0], w_out[0], norm_moe_w[0],
                 moe_w_group[0], moe_b_group[0], moe_w_expert[0], moe_b_expert[0], moe_w1[0], moe_w3[0],
                 moe_w2[0], norm_final_w, B, S, cos_t, sin_t)
    return out.reshape(B, S, D)
```

```python
import functools

import numpy as np
import jax
import jax.numpy as jnp
from jax import lax
from jax.experimental import pallas as pl
from jax.experimental.pallas import tpu as pltpu

F32 = jnp.float32
BF16 = jnp.bfloat16

D_MODEL = 1024
HEAD_DIM = 64
ATTN_PAIRS = ((128, 1), (512, 4), (2048, 16))
ATTN_HEADS_PER_GROUP = 8
ATTN_GROUP_W = ATTN_HEADS_PER_GROUP * HEAD_DIM
ATTN_HALF = 64
ROPE_THETA = 10000.0
RET_HEADS = 8
RET_QK_DIM = 64
RET_V_DIM = 128
RET_CHUNK = 128
RET_QK_W = RET_HEADS * RET_QK_DIM
RET_V_W = RET_HEADS * RET_V_DIM
MOE_GROUPS = 8
MOE_EXPERTS_PER_GROUP = 8
MOE_N_EXPERTS = MOE_GROUPS * MOE_EXPERTS_PER_GROUP
MOE_HIDDEN = 512
NORM_EPS = 1e-6

LANES = 128
ROW_TILES = D_MODEL // LANES
NEG_BIG = -1e30
LOG2_E = 1.4426950408889634

TM_INPROJ = 512
TM_MERGE = 512
MERGE_SPLIT = 2
MERGE_COL_CHUNKS = 4
TM_COMBINE = 256
MOE_BM = 512
DISPATCH_CHUNK = 2048
ATTN_QB = 128
ATTN_UNROLL = 8
RET_UNROLL = 8

VMEM_LIMIT = 56 * 1024 * 1024

_A = 3 * ATTN_GROUP_W
OFF_QA, OFF_KA, OFF_VA = 0, _A, 2 * _A
OFF_QR = 3 * _A
OFF_KR = OFF_QR + RET_QK_W
OFF_VR = OFF_KR + RET_QK_W
OFF_GR = OFF_VR + RET_V_W
OFF_GL = OFF_GR + RET_V_W
IN_W = OFF_GL + 2 * D_MODEL


def _cparams(sem, vmem=VMEM_LIMIT):
    return pltpu.CompilerParams(dimension_semantics=sem, vmem_limit_bytes=vmem)


def _inproj_body(x_ref, nw_ref, w_ref, bg_ref, cos_ref, sin_ref,
                 qa0, ka0, va0, qa1, ka1, va1, qa2, ka2, va2, qr, kr, vr, gr, gt,
                 stage_ref):
    tm = x_ref.shape[0]
    x = x_ref[...]
    ms = jnp.mean(x * x, axis=-1, keepdims=True)
    xn = (x * lax.rsqrt(ms + NORM_EPS) * nw_ref[...]).astype(BF16)
    cos = cos_ref[...]
    sin = sin_ref[...]
    lane = lax.broadcasted_iota(jnp.int32, (tm, LANES), 1)
    first_half = (lane & (HEAD_DIM - 1)) < (HEAD_DIM // 2)

    def proj(c0, width):
        return jnp.dot(xn, w_ref[:, c0:c0 + width], preferred_element_type=F32)

    def rotary(a, scale):
        partner = jnp.where(first_half, pltpu.roll(a, LANES - HEAD_DIM // 2, 1),
                            pltpu.roll(a, HEAD_DIM // 2, 1))
        r = a * cos + partner * sin
        return r * scale if scale != 1.0 else r

    def chunks(acc):
        return [acc[:, c * LANES:(c + 1) * LANES] for c in range(acc.shape[1] // LANES)]

    def store_natural(out_ref, acc, fn):
        for c, a in enumerate(chunks(acc)):
            out_ref[:, c * LANES:(c + 1) * LANES] = fn(a).astype(out_ref.dtype)

    def store_pairs(out_ref, acc, fn):
        for c, a in enumerate(chunks(acc)):
            out_ref[0, c] = fn(a).astype(out_ref.dtype)

    def store_strided(out_ref, acc, fn, d):
        for c, a in enumerate(chunks(acc)):
            stage_ref[c] = fn(a)
        for c in range(acc.shape[1] // LANES):
            for r in range(d):
                out_ref[0, c, r] = stage_ref[c, pl.ds(r, tm // d, stride=d), :].astype(out_ref.dtype)

    ident = lambda a: a
    rot_q = lambda a: rotary(a, HEAD_DIM ** -0.5 * LOG2_E)
    rot_1 = lambda a: rotary(a, 1.0)
    rot_k = lambda a: rotary(a, RET_QK_DIM ** -0.5)

    W = ATTN_GROUP_W
    store_pairs(qa0, proj(OFF_QA, W), rot_q)
    store_pairs(ka0, proj(OFF_KA, W), rot_1)
    store_pairs(va0, proj(OFF_VA, W), ident)
    for g, (qo, ko, vo) in ((1, (qa1, ka1, va1)), (2, (qa2, ka2, va2))):
        d = ATTN_PAIRS[g][1]
        store_strided(qo, proj(OFF_QA + g * W, W), rot_q, d)
        store_strided(ko, proj(OFF_KA + g * W, W), rot_1, d)
        store_strided(vo, proj(OFF_VA + g * W, W), ident, d)
    store_natural(qr, proj(OFF_QR, RET_QK_W), rot_1)
    store_natural(kr, proj(OFF_KR, RET_QK_W), rot_k)
    for h in range(RET_V_W // W):
        vr[:, h * W:(h + 1) * W] = proj(OFF_VR + h * W, W).astype(vr.dtype)
        gr[:, h * W:(h + 1) * W] = proj(OFF_GR + h * W, W).astype(gr.dtype)
    for h in range(2 * D_MODEL // W):
        z = proj(OFF_GL + h * W, W) + bg_ref[:, h * W:(h + 1) * W]
        gt[:, h * W:(h + 1) * W] = jax.nn.sigmoid(z).astype(gt.dtype)


def _in_projection(x2, norm_w, w_bf, b_gate, cos_t, sin_t, B, S):
    T = B * S
    tm = TM_INPROJ
    nt = S // tm
    W = ATTN_GROUP_W
    row = lambda i: (i, 0)
    const = lambda i: (0, 0)
    nat = lambda width: pl.BlockSpec((tm, width), row)

    P = W // LANES

    def strided_spec(d):
        return pl.BlockSpec((1, P, d, tm // d, LANES), lambda i: (i // nt, 0, 0, i % nt, 0))

    def strided_shape(d):
        return jax.ShapeDtypeStruct((B, P, d, S // d, LANES), BF16)

    pair_spec = pl.BlockSpec((1, P, tm, LANES), lambda i: (i // nt, 0, i % nt, 0))
    pair_shape = jax.ShapeDtypeStruct((B, P, S, LANES), BF16)
    nat_shape = lambda width: jax.ShapeDtypeStruct((T, width), BF16)
    d1, d2 = ATTN_PAIRS[1][1], ATTN_PAIRS[2][1]
    out_shape = ([pair_shape] * 3 + [strided_shape(d1)] * 3 + [strided_shape(d2)] * 3
                 + [nat_shape(RET_QK_W)] * 2 + [nat_shape(RET_V_W)] * 2 + [nat_shape(2 * D_MODEL)])
    out_specs = ([pair_spec] * 3 + [strided_spec(d1)] * 3 + [strided_spec(d2)] * 3
                 + [nat(RET_QK_W)] * 2 + [nat(RET_V_W)] * 2 + [nat(2 * D_MODEL)])
    in_specs = [
        pl.BlockSpec((tm, D_MODEL), row),
        pl.BlockSpec((1, D_MODEL), const),
        pl.BlockSpec((D_MODEL, IN_W), const, pipeline_mode=pl.Buffered(1)),
        pl.BlockSpec((1, 2 * D_MODEL), const),
        pl.BlockSpec((tm, LANES), lambda i: (i % nt, 0)),
        pl.BlockSpec((tm, LANES), lambda i: (i % nt, 0)),
    ]
    return pl.pallas_call(
        _inproj_body,
        grid=(T // tm,),
        in_specs=in_specs,
        out_specs=out_specs,
        out_shape=out_shape,
        scratch_shapes=[pltpu.VMEM((W // LANES, tm, LANES), F32)],
        compiler_params=_cparams(("parallel",)),
        name="in_projection",
    )(x2, norm_w, w_bf, b_gate, cos_t, sin_t)


def _attn_body(q0_ref, k0_ref, v0_ref, q1_ref, k1_ref, v1_ref, q2_ref, k2_ref, v2_ref, o_ref,
               va_ref, vb_ref, part_ref, bias_ref, *, S):
    QB, H = ATTN_QB, ATTN_HALF
    lane = lax.broadcasted_iota(jnp.int32, (QB, LANES), 1)
    head0 = lane < HEAD_DIM
    ones = jnp.ones((QB, LANES), BF16)

    qi = lax.broadcasted_iota(jnp.int32, (QB, QB + 2 * H), 0)
    ki = lax.broadcasted_iota(jnp.int32, (QB, QB + 2 * H), 1)
    for n in range(3):
        bias_ref[n] = jnp.where(jnp.abs(ki - qi - n * H) <= H, 0.0, NEG_BIG).astype(F32)

    def scores(q_rows, k_rows, bias, h):
        qm = jnp.where(head0 if h == 0 else jnp.logical_not(head0), q_rows, jnp.zeros_like(q_rows))
        return lax.dot_general(qm, k_rows, (((1,), (1,)), ((), ())), preferred_element_type=F32) + bias

    def weights(s):
        m = jnp.max(s, axis=-1, keepdims=True)
        return m, jnp.exp2(s - m).astype(BF16)

    def heads_to_lanes(m0, a, m1, b):
        num = jnp.where(head0, a, b)
        den = pltpu.roll(jnp.where(head0, b, a), HEAD_DIM, 1)
        mx = jnp.where(head0, m0, m1)
        return num, den, mx

    def run_group(q_ref, k_ref, v_ref, d, prepare, store):
        L = S // d
        KW = min(L, QB + 2 * H)
        nb = L // QB

        def fill(t, carry):
            r = t // nb
            rows = pl.ds(pl.multiple_of((t % nb) * QB, QB), QB)
            v = v_ref[0, 0, r, rows, :]
            dst = pl.ds(pl.multiple_of(t * QB, QB), QB)
            va_ref[dst, :] = jnp.where(head0, v, ones)
            vb_ref[dst, :] = jnp.where(head0, ones, v)
            return carry

        lax.fori_loop(0, S // QB, fill, 0, unroll=4)

        def trip(i, carry):
            blocks = []
            for u in range(ATTN_UNROLL):
                t = i * ATTN_UNROLL + u
                r = t // nb
                q0 = pl.multiple_of((t % nb) * QB, QB)
                ws = pl.multiple_of(jnp.clip(q0 - H, 0, L - KW), H)
                bias = bias_ref[(q0 - ws) // H][:, :KW]
                q_rows = q_ref[0, 0, r, pl.ds(q0, QB), :]
                k_rows = k_ref[0, 0, r, pl.ds(ws, KW), :]
                vrows = pl.ds(pl.multiple_of(r * L + ws, H), KW)
                blocks.append((r, q0, vrows, [scores(q_rows, k_rows, bias, h) for h in range(2)]))
            blocks = [(r, q0, vrows, [weights(s) for s in ss]) for r, q0, vrows, ss in blocks]
            done = []
            for r, q0, vrows, ((m0, p0), (m1, p1)) in blocks:
                a = jnp.dot(p0, va_ref[vrows, :], preferred_element_type=F32)
                b = jnp.dot(p1, vb_ref[vrows, :], preferred_element_type=F32)
                done.append((r, q0, m0, a, m1, b))
            done = [(r, q0, prepare(q0, *heads_to_lanes(m0, a, m1, b))) for r, q0, m0, a, m1, b in done]
            for r, q0, vals in done:
                store(r, q0, vals)
            return carry

        lax.fori_loop(0, S // QB // ATTN_UNROLL, trip, 0)

    def store_partial(g):
        d = ATTN_PAIRS[g][1]

        def store(r, q0, vals):
            rows = pl.ds(r + q0 * d, QB, stride=d)
            for n, val in enumerate(vals):
                part_ref[3 * (g - 1) + n, rows, :] = val
        return store

    keep = lambda q0, num, den, mx: (num, den, mx)
    run_group(q1_ref, k1_ref, v1_ref, ATTN_PAIRS[1][1], keep, store_partial(1))
    run_group(q2_ref, k2_ref, v2_ref, ATTN_PAIRS[2][1], keep, store_partial(2))

    def merge(q0, num, den, mx):
        rows = pl.ds(q0, QB)
        nums = [num, part_ref[0, rows, :], part_ref[3, rows, :]]
        dens = [den, part_ref[1, rows, :], part_ref[4, rows, :]]
        mxs = [mx, part_ref[2, rows, :], part_ref[5, rows, :]]
        top = jnp.maximum(jnp.maximum(mxs[0], mxs[1]), mxs[2])
        ws = [jnp.exp2(m - top) for m in mxs]
        n = ws[0] * nums[0] + ws[1] * nums[1] + ws[2] * nums[2]
        dn = ws[0] * dens[0] + ws[1] * dens[1] + ws[2] * dens[2]
        return (n / dn).astype(o_ref.dtype)

    def store_out(r, q0, y):
        o_ref[0, 0, pl.ds(q0, QB), :] = y

    run_group(q0_ref, k0_ref, v0_ref, ATTN_PAIRS[0][1], merge, store_out)


def _attention(qkv, B, S):
    P = ATTN_GROUP_W // LANES
    in_specs = []
    for _, d in ATTN_PAIRS:
        in_specs += [pl.BlockSpec((1, 1, d, S // d, LANES), lambda b, p: (b, p, 0, 0, 0))] * 3
    return pl.pallas_call(
        functools.partial(_attn_body, S=S),
        grid=(B, P),
        in_specs=in_specs,
        out_specs=pl.BlockSpec((1, 1, S, LANES), lambda b, p: (b, p, 0, 0)),
        out_shape=jax.ShapeDtypeStruct((B, P, S, LANES), BF16),
        scratch_shapes=[pltpu.VMEM((S, LANES), BF16), pltpu.VMEM((S, LANES), BF16),
                        pltpu.VMEM((6, S, LANES), F32),
                        pltpu.VMEM((3, ATTN_QB, ATTN_QB + 2 * ATTN_HALF), F32)],
        compiler_params=_cparams(("parallel", "parallel")),
        name="attention",
    )(*qkv)


def _log_sigmoid(z):
    return jnp.minimum(z, 0.0) - jnp.log(1.0 + jnp.exp(-jnp.abs(z)))


def _ret_body(dec_ref, q_ref, k_ref, v_ref, g_ref, gnw_ref, o_ref, kt_ref, sf_ref, sb_ref, *, S):
    C = RET_CHUNK
    nc = S // C
    lg = _log_sigmoid(dec_ref[0])
    a_row = lax.broadcasted_iota(jnp.int32, (C, LANES), 0).astype(F32)
    lane = lax.broadcasted_iota(jnp.int32, (C, LANES), 1)
    rel = (lax.broadcasted_iota(jnp.int32, (C, C), 0) - lax.broadcasted_iota(jnp.int32, (C, C), 1)).astype(F32)

    heads = []
    for h in range(2):
        lgf = lg[h:h + 1, :]
        lgb = lg[2 + h:3 + h, :]
        in_head = (lane < RET_QK_DIM) if h == 0 else (lane >= RET_QK_DIM)
        heads.append(dict(
            in_head=in_head,
            xi_f=jnp.where(in_head, jnp.exp((a_row + 1.0) * lgf), 0.0),
            xi_b=jnp.where(in_head, jnp.exp((C - a_row) * lgb), 0.0),
            zeta_f=jnp.exp((C - 1.0 - a_row) * lgf),
            zeta_b=jnp.exp(a_row * lgb),
            dloc=jnp.where(rel > 0, jnp.exp(rel * lgf), jnp.where(rel < 0, jnp.exp(-rel * lgb), 2.0)),
            cd_f=jnp.exp(C * lgf),
            cd_b=jnp.exp(C * lgb),
        ))

    def chunk_rows(n):
        return pl.ds(pl.multiple_of(n * C, C), C)

    def v_head(rows, h):
        return v_ref[0, rows, h * RET_V_DIM:(h + 1) * RET_V_DIM]

    def products(i, carry):
        items = []
        for u in range(RET_UNROLL):
            n = i * RET_UNROLL + u
            rows = chunk_rows(n)
            kt = k_ref[0, rows, :].astype(F32).T.astype(BF16)
            kt_ref[n] = kt
            for h, hd in enumerate(heads):
                vh = v_head(rows, h).astype(F32)
                items.append((n, h, kt, (vh * hd["zeta_f"]).astype(BF16), (vh * hd["zeta_b"]).astype(BF16)))
        outs = [(n, h, jnp.dot(kt, vf, preferred_element_type=F32), jnp.dot(kt, vb, preferred_element_type=F32))
                for n, h, kt, vf, vb in items]
        for n, h, f, b in outs:
            sf_ref[n, h] = f
            sb_ref[n, h] = b
        return carry

    lax.fori_loop(0, nc // RET_UNROLL, products, 0)

    def scan(ref, key, order):
        def step(i, state):
            n = order(i)
            new = []
            for h, hd in enumerate(heads):
                kv = ref[n, h]
                ref[n, h] = state[h]
                new.append(hd[key] * state[h] + kv)
            return tuple(new)
        zero = jnp.zeros(ref.shape[2:], F32)
        lax.fori_loop(0, nc, step, (zero, zero))

    scan(sf_ref, "cd_f", lambda i: i)
    scan(sb_ref, "cd_b", lambda i: nc - 1 - i)

    def outputs(i, carry):
        items = []
        for u in range(RET_UNROLL):
            n = i * RET_UNROLL + u
            rows = chunk_rows(n)
            qp = q_ref[0, rows, :]
            qf = qp.astype(F32)
            kt = kt_ref[n]
            for h, hd in enumerate(heads):
                qm = jnp.where(hd["in_head"], qp, jnp.zeros_like(qp))
                qx = jnp.concatenate([(qf * hd["xi_f"]).astype(BF16), (qf * hd["xi_b"]).astype(BF16)], axis=1)
                st = jnp.concatenate([sf_ref[n, h].astype(BF16), sb_ref[n, h].astype(BF16)], axis=0)
                items.append((rows, h, hd, jnp.dot(qm, kt, preferred_element_type=F32),
                              jnp.dot(qx, st, preferred_element_type=F32)))
        items = [(rows, h, hd, (s * hd["dloc"]).astype(BF16), cross) for rows, h, hd, s, cross in items]
        items = [(rows, h, cross + jnp.dot(p, v_head(rows, h), preferred_element_type=F32))
                 for rows, h, hd, p, cross in items]
        for rows, h, ret in items:
            mu = jnp.mean(ret, axis=-1, keepdims=True)
            xc = ret - mu
            var = jnp.mean(xc * xc, axis=-1, keepdims=True)
            cols = slice(h * RET_V_DIM, (h + 1) * RET_V_DIM)
            gate = g_ref[0, rows, cols].astype(F32)
            y = xc * lax.rsqrt(var + NORM_EPS) * gnw_ref[:, cols] * (gate * jax.nn.sigmoid(gate))
            o_ref[0, rows, cols] = y.astype(o_ref.dtype)
        return carry

    lax.fori_loop(0, nc // RET_UNROLL, outputs, 0)


def _retention(dec, qr, kr, vr, gr, gn_w, B, S):
    nc = S // RET_CHUNK
    npairs = RET_HEADS // 2
    qk_spec = pl.BlockSpec((1, S, 2 * RET_QK_DIM), lambda b, p: (b, 0, p))
    v_spec = pl.BlockSpec((1, S, 2 * RET_V_DIM), lambda b, p: (b, 0, p))
    return pl.pallas_call(
        functools.partial(_ret_body, S=S),
        grid=(B, npairs),
        in_specs=[pl.BlockSpec((1, 4, LANES), lambda b, p: (p, 0, 0)),
                  qk_spec, qk_spec, v_spec, v_spec,
                  pl.BlockSpec((1, 2 * RET_V_DIM), lambda b, p: (0, p))],
        out_specs=v_spec,
        out_shape=jax.ShapeDtypeStruct((B, S, RET_V_W), BF16),
        scratch_shapes=[pltpu.VMEM((nc, 2 * RET_QK_DIM, RET_CHUNK), BF16),
                        pltpu.VMEM((nc, 2, 2 * RET_QK_DIM, RET_V_DIM), F32),
                        pltpu.VMEM((nc, 2, 2 * RET_QK_DIM, RET_V_DIM), F32)],
        compiler_params=_cparams(("parallel", "parallel")),
        name="retention",
    )(dec, qr.reshape(B, S, RET_QK_W), kr.reshape(B, S, RET_QK_W),
      vr.reshape(B, S, RET_V_W), gr.reshape(B, S, RET_V_W), gn_w)


ROUTE_EID, ROUTE_RANK, ROUTE_GATE = 0, 2, 4
ROUTE_FIELDS = 8
ROUTER_EXPERT_LANE0 = MOE_GROUPS


def _merge_body(yatt_ref, yret_ref, gt_ref, x_ref, wa_ref, wb_ref, wo_ref,
                nw_ref, wrh_ref, wrl_ref, br_ref,
                h_ref, hn_ref, route_ref, route_t_ref, cnt_ref):
    tm = x_ref.shape[0]
    hm = tm // MERGE_SPLIT
    i = pl.program_id(0)
    cw = D_MODEL // MERGE_COL_CHUNKS

    def branch_products(rows):
        y_att = jnp.concatenate([yatt_ref[0, c, rows, :] for c in range(yatt_ref.shape[1])], axis=1)
        y_ret = yret_ref[rows, :]
        chunks = []
        for c in range(MERGE_COL_CHUNKS):
            cols = slice(c * cw, (c + 1) * cw)
            a = jnp.dot(y_att, wa_ref[:, cols], preferred_element_type=F32)
            b = jnp.dot(y_ret, wb_ref[:, cols], preferred_element_type=F32)
            g_att = gt_ref[rows, c * cw:(c + 1) * cw].astype(F32)
            g_ret = gt_ref[rows, D_MODEL + c * cw:D_MODEL + (c + 1) * cw].astype(F32)
            chunks.append((g_att * a + g_ret * b).astype(BF16))
        return jnp.concatenate(chunks, axis=1)

    def residual_norm(hf, rows, merged):
        mix = jnp.dot(merged, wo_ref[...], preferred_element_type=F32)
        h = x_ref[rows, :] + mix
        h_ref[rows, :] = h
        ms = jnp.mean(h * h, axis=-1, keepdims=True)
        hn = h * lax.rsqrt(ms + NORM_EPS) * nw_ref[...]
        for j in range(ROW_TILES):
            hn_ref[pl.ds(hf * hm * ROW_TILES + j, hm, stride=ROW_TILES), :] = hn[:, j * LANES:(j + 1) * LANES]
        hi = hn.astype(BF16)
        lo = (hn - hi.astype(F32)).astype(BF16)
        return hi, lo

    def router_logits(hi, lo):
        return (jnp.dot(hi, wrh_ref[...], preferred_element_type=F32)
                + jnp.dot(hi, wrl_ref[...], preferred_element_type=F32)
                + jnp.dot(lo, wrh_ref[...], preferred_element_type=F32)) + br_ref[...]

    lane = lax.broadcasted_iota(jnp.int32, (hm, LANES), 1)
    far = jnp.int32(4 * LANES)

    def first_argmax(vals, vmax):
        return jnp.min(jnp.where(vals == vmax, lane, far), axis=-1, keepdims=True)

    def route(logits):
        is_group = lane < MOE_GROUPS
        gl = jnp.where(is_group, logits, NEG_BIG)
        gmax = jnp.max(gl, axis=-1, keepdims=True)
        g_w = 1.0 / jnp.sum(jnp.where(is_group, jnp.exp(gl - gmax), 0.0), axis=-1, keepdims=True)
        g_idx = first_argmax(gl, gmax)
        e_lane = lane - ROUTER_EXPERT_LANE0
        in_group = (e_lane >= 0) & (e_lane < MOE_N_EXPERTS) & (jnp.right_shift(e_lane, 3) == g_idx)
        el = jnp.where(in_group, logits, NEG_BIG)
        m1 = jnp.max(el, axis=-1, keepdims=True)
        i1 = first_argmax(el, m1)
        el2 = jnp.where(lane == i1, NEG_BIG, el)
        m2 = jnp.max(el2, axis=-1, keepdims=True)
        i2 = first_argmax(el2, m2)
        ex = jnp.exp(m2 - m1)
        return i1, i2, g_w / (1.0 + ex), g_w * ex / (1.0 + ex)

    rows = [slice(hf * hm, (hf + 1) * hm) for hf in range(MERGE_SPLIT)]
    merged = [branch_products(r) for r in rows]
    split = [residual_norm(hf, r, m) for hf, (r, m) in enumerate(zip(rows, merged))]
    routed = [route(router_logits(hi, lo)) for hi, lo in split]

    @pl.when(i == 0)
    def _():
        cnt_ref[...] = jnp.zeros(cnt_ref.shape, F32)

    r_idx = lax.broadcasted_iota(jnp.int32, (hm, hm), 0)
    c_idx = lax.broadcasted_iota(jnp.int32, (hm, hm), 1)
    lower = jnp.where(c_idx < r_idx, 1.0, 0.0).astype(BF16)
    running = cnt_ref[...]
    for hf, (i1, i2, gate1, gate2) in enumerate(routed):
        hot1 = lane == i1
        hot2 = lane == i2
        onehot = jnp.where(hot1 | hot2, 1.0, 0.0)
        before = jnp.dot(lower, onehot.astype(BF16), preferred_element_type=F32) + running
        rank1 = jnp.sum(jnp.where(hot1, before, 0.0), axis=-1, keepdims=True)
        rank2 = jnp.sum(jnp.where(hot2, before, 0.0), axis=-1, keepdims=True)
        running = running + jnp.sum(onehot, axis=0, keepdims=True)
        rec = jnp.zeros((hm, LANES), F32)
        for pos, val in ((ROUTE_EID, (i1 - ROUTER_EXPERT_LANE0).astype(F32)),
                         (ROUTE_EID + 1, (i2 - ROUTER_EXPERT_LANE0).astype(F32)),
                         (ROUTE_RANK, rank1), (ROUTE_RANK + 1, rank2),
                         (ROUTE_GATE, gate1), (ROUTE_GATE + 1, gate2)):
            rec = jnp.where(lane == pos, val, rec)
        route_ref[rows[hf], :] = rec
        route_t_ref[:, hf * hm:(hf + 1) * hm] = rec.T[:route_t_ref.shape[0], :]
    cnt_ref[...] = running


def _merge_route(y_att, y_ret, gates, x2, wa, wb, wo, nw, wr_hi, wr_lo, b_r, T, S):
    tm = TM_MERGE
    nt = S // tm
    row = lambda i: (i, 0)
    const = lambda i: (0, 0)
    full = lambda arr: pl.BlockSpec(arr.shape, const)
    in_specs = ([pl.BlockSpec((1, y_att.shape[1], tm, LANES), lambda i: (i // nt, 0, i % nt, 0)),
                 pl.BlockSpec((tm, RET_V_W), row), pl.BlockSpec((tm, 2 * D_MODEL), row),
                 pl.BlockSpec((tm, D_MODEL), row),
                 full(wa), full(wb), full(wo), full(nw), full(wr_hi), full(wr_lo), full(b_r)])
    return pl.pallas_call(
        _merge_body,
        grid=(T // tm,),
        in_specs=in_specs,
        out_specs=[pl.BlockSpec((tm, D_MODEL), row),
                   pl.BlockSpec((tm * ROW_TILES, LANES), row),
                   pl.BlockSpec((tm, LANES), row), pl.BlockSpec((ROUTE_FIELDS, tm), lambda i: (0, i)),
                   pl.BlockSpec((1, LANES), const)],
        out_shape=[jax.ShapeDtypeStruct((T, D_MODEL), F32), jax.ShapeDtypeStruct((T * ROW_TILES, LANES), F32),
                   jax.ShapeDtypeStruct((T, LANES), F32), jax.ShapeDtypeStruct((ROUTE_FIELDS, T), F32),
                   jax.ShapeDtypeStruct((1, LANES), F32)],
        compiler_params=_cparams(("arbitrary",)),
        name="merge_route",
    )(y_att, y_ret, gates, x2, wa, wb, wo, nw, wr_hi, wr_lo, b_r)


ISSUE_UNROLL = 8


def _tile_rows(n):
    return pl.ds(pl.multiple_of(n * ROW_TILES, ROW_TILES), ROW_TILES)


def _dispatch_body(slot_ref, hn_ref, xs_ref, sem, *, T):
    i = pl.program_id(0)
    ch = hn_ref.shape[0] // ROW_TILES

    def row_copy(j, slot):
        return pltpu.make_async_copy(hn_ref.at[_tile_rows(j)], xs_ref.at[_tile_rows(slot)], sem)

    def issue(j, carry):
        t = i * ch + j
        row_copy(j, slot_ref[t]).start(priority=0)
        row_copy(j, slot_ref[T + t]).start(priority=1)
        return carry

    lax.fori_loop(0, ch, issue, 0, unroll=ISSUE_UNROLL)
    for _ in range(2):
        pltpu.make_async_copy(hn_ref, xs_ref.at[pl.ds(0, ch * ROW_TILES)], sem).wait()


def _dispatch(slots, hn, n_slots, T):
    ch = DISPATCH_CHUNK
    grid_spec = pltpu.PrefetchScalarGridSpec(
        num_scalar_prefetch=1,
        grid=(T // ch,),
        in_specs=[pl.BlockSpec((ch * ROW_TILES, LANES), lambda i, s: (i, 0))],
        out_specs=pl.BlockSpec(memory_space=pl.ANY),
        scratch_shapes=[pltpu.SemaphoreType.DMA(())],
    )
    return pl.pallas_call(
        functools.partial(_dispatch_body, T=T),
        grid_spec=grid_spec,
        out_shape=jax.ShapeDtypeStruct((n_slots * ROW_TILES, LANES), F32),
        compiler_params=_cparams(("arbitrary",)),
        name="moe_dispatch",
    )(slots, hn)


def _expert_body(blk_ref, eid_ref, valid_ref, fresh_ref, x_ref, w1_ref, w3_ref, w2_ref, y_ref,
                 w1b, w3b, w2b):
    i = pl.program_id(0)
    valid = valid_ref[i]

    @pl.when(valid > 0)
    def _():
        @pl.when(fresh_ref[i] == 1)
        def _():
            w1b[...] = w1_ref[0].astype(BF16)
            w3b[...] = w3_ref[0].astype(BF16)
            w2b[...] = w2_ref[0].astype(BF16)

        bm = x_ref.shape[0] // ROW_TILES
        live =lax.broadcasted_iota(jnp.int32, (bm, LANES), 0) < valid
        x = jnp.concatenate(
            [jnp.where(live, x_ref[pl.ds(j, bm, stride=ROW_TILES), :], 0.0).astype(BF16)
             for j in range(ROW_TILES)], axis=1)
        a = jnp.dot(x, w1b[...], preferred_element_type=F32)
        b = jnp.dot(x, w3b[...], preferred_element_type=F32)
        hid = (a * jax.nn.sigmoid(a) * b).astype(BF16)
        y = jnp.dot(hid, w2b[...], preferred_element_type=F32)
        for j in range(ROW_TILES):
            y_ref[pl.ds(j, bm, stride=ROW_TILES), :] = y[:, j * LANES:(j + 1) * LANES]


def _experts(blk, blk_eid, blk_valid, blk_fresh, x_slots, w1, w3, w2, n_blocks):
    bm = MOE_BM
    grid_spec = pltpu.PrefetchScalarGridSpec(
        num_scalar_prefetch=4,
        grid=(n_blocks,),
        in_specs=[
            pl.BlockSpec((bm * ROW_TILES, LANES), lambda i, blk, eid, val, fr: (blk[i], 0)),
            pl.BlockSpec((1, D_MODEL, MOE_HIDDEN), lambda i, blk, eid, val, fr: (eid[i], 0, 0)),
            pl.BlockSpec((1, D_MODEL, MOE_HIDDEN), lambda i, blk, eid, val, fr: (eid[i], 0, 0)),
            pl.BlockSpec((1, MOE_HIDDEN, D_MODEL), lambda i, blk, eid, val, fr: (eid[i], 0, 0)),
        ],
        out_specs=pl.BlockSpec((bm * ROW_TILES, LANES), lambda i, blk, eid, val, fr: (blk[i], 0)),
        scratch_shapes=[pltpu.VMEM((D_MODEL, MOE_HIDDEN), BF16), pltpu.VMEM((D_MODEL, MOE_HIDDEN), BF16),
                        pltpu.VMEM((MOE_HIDDEN, D_MODEL), BF16)],
    )
    return pl.pallas_call(
        _expert_body,
        grid_spec=grid_spec,
        out_shape=jax.ShapeDtypeStruct(x_slots.shape, F32),
        compiler_params=_cparams(("arbitrary",)),
        name="moe_experts",
    )(blk, blk_eid, blk_valid, blk_fresh, x_slots, w1, w3, w2)


def _combine_body(slot_ref, ys_ref, h_ref, route_ref, nw_ref, o_ref, ybuf, sem, *, T):
    i = pl.program_id(0)
    n = pl.num_programs(0)
    tm = h_ref.shape[0]

    def row_copy(slot, buf, k, j):
        return pltpu.make_async_copy(ys_ref.at[_tile_rows(slot)], ybuf.at[buf, k, _tile_rows(j)], sem.at[buf])

    def issue(tile, buf):
        def one(j, carry):
            t = tile * tm + j
            row_copy(slot_ref[t], buf, 0, j).start(priority=0)
            row_copy(slot_ref[T + t], buf, 1, j).start(priority=1)
            return carry
        lax.fori_loop(0, tm, one, 0, unroll=ISSUE_UNROLL)

    @pl.when(i == 0)
    def _():
        issue(0, 0)

    @pl.when(i + 1 < n)
    def _():
        issue(i + 1, (i + 1) % 2)

    buf = i % 2
    for k in range(2):
        pltpu.make_async_copy(ys_ref.at[pl.ds(0, tm * ROW_TILES)], ybuf.at[buf, k], sem.at[buf]).wait()
    route = route_ref[...]
    g1 = route[:, ROUTE_GATE:ROUTE_GATE + 1]
    g2 = route[:, ROUTE_GATE + 1:ROUTE_GATE + 2]
    hs = []
    ss = jnp.zeros((tm, 1), F32)
    for j in range(ROW_TILES):
        tile_row = pl.ds(j, tm, stride=ROW_TILES)
        hj = h_ref[:, j * LANES:(j + 1) * LANES] + (ybuf[buf, 0, tile_row, :] * g1 + ybuf[buf, 1, tile_row, :] * g2)
        hs.append(hj)
        ss = ss + jnp.sum(hj * hj, axis=-1, keepdims=True)
    inv = lax.rsqrt(ss * (1.0 / D_MODEL) + NORM_EPS)
    for j, hj in enumerate(hs):
        cols = slice(j * LANES, (j + 1) * LANES)
        o_ref[:, cols] = hj * inv * nw_ref[:, cols]


def _combine(slots, y_slots, h, route, nw, T):
    tm = TM_COMBINE
    row = lambda i, s: (i, 0)
    grid_spec = pltpu.PrefetchScalarGridSpec(
        num_scalar_prefetch=1,
        grid=(T // tm,),
        in_specs=[pl.BlockSpec(memory_space=pl.ANY),
                  pl.BlockSpec((tm, D_MODEL), row),
                  pl.BlockSpec((tm, LANES), row),
                  pl.BlockSpec((1, D_MODEL), lambda i, s: (0, 0))],
        out_specs=pl.BlockSpec((tm, D_MODEL), row),
        scratch_shapes=[pltpu.VMEM((2, 2, tm * ROW_TILES, LANES), F32), pltpu.SemaphoreType.DMA((2,))],
    )
    return pl.pallas_call(
        functools.partial(_combine_body, T=T),
        grid_spec=grid_spec,
        out_shape=jax.ShapeDtypeStruct((T, D_MODEL), F32),
        compiler_params=_cparams(("arbitrary",)),
        name="moe_combine",
    )(slots, y_slots, h, route, nw)


def _rotary_tables(S):
    inv_freq = (1.0 / (np.float32(ROPE_THETA) ** (np.arange(0, HEAD_DIM, 2, dtype=np.float32) / HEAD_DIM))
                ).astype(np.float32)
    ang = np.arange(S, dtype=np.float32)[:, None] * inv_freq[None, :]
    cos, sin = np.cos(ang), np.sin(ang)
    reps = LANES // HEAD_DIM
    cos_t = np.tile(np.concatenate([cos, cos], axis=1), (1, reps)).astype(np.float32)
    sin_t = np.tile(np.concatenate([-sin, sin], axis=1), (1, reps)).astype(np.float32)
    return jnp.asarray(cos_t), jnp.asarray(sin_t)


def _layer(h_in, norm_mix_w, w_in, b_branch_gate, ret_decay_fwd, ret_decay_bwd, ret_gn_w, w_attn_branch,
           w_ret_branch, w_out, norm_moe_w, moe_w_group, moe_b_group, moe_w_expert, moe_b_expert,
           moe_w1, moe_w3, moe_w2, next_norm_w, B, S, cos_t, sin_t):
    T = B * S
    (qa0, ka0, va0, qa1, ka1, va1, qa2, ka2, va2, qr, kr, vr, gr, gates) = _in_projection(
        h_in, norm_mix_w[None, :], w_in.astype(BF16), b_branch_gate[None, :], cos_t, sin_t, B, S)

    unit = lambda a: a[:, :, None]
    y_att = _attention((unit(qa0), unit(ka0), unit(va0), qa1, ka1, va1, qa2, ka2, va2), B, S)

    dec = jnp.stack([ret_decay_fwd.reshape(RET_HEADS // 2, 2), ret_decay_bwd.reshape(RET_HEADS // 2, 2)], axis=1)
    dec = jnp.broadcast_to(dec.reshape(RET_HEADS // 2, 4, 1), (RET_HEADS // 2, 4, LANES)).astype(F32)
    y_ret = _retention(dec, qr, kr, vr, gr, ret_gn_w[None, :], B, S).reshape(T, RET_V_W)

    pad = LANES - MOE_GROUPS - MOE_N_EXPERTS
    w_r = jnp.concatenate([moe_w_group, moe_w_expert, jnp.zeros((D_MODEL, pad), F32)], axis=1)
    w_r_hi = w_r.astype(BF16)
    w_r_lo = (w_r - w_r_hi.astype(F32)).astype(BF16)
    b_r = jnp.concatenate([moe_b_group, moe_b_expert, jnp.zeros((pad,), F32)])[None, :]

    h_mid, hn, route, route_t, cnt = _merge_route(
        y_att, y_ret, gates, h_in, w_attn_branch.astype(BF16), w_ret_branch.astype(BF16),
        w_out.astype(BF16), norm_moe_w[None, :], w_r_hi, w_r_lo, b_r, T, S)

    bm = MOE_BM
    counts = cnt[0, ROUTER_EXPERT_LANE0:ROUTER_EXPERT_LANE0 + MOE_N_EXPERTS].astype(jnp.int32)
    nblk = (counts + bm - 1) // bm
    blk_end = jnp.cumsum(nblk)
    pstart = (blk_end - nblk) * bm
    n_blocks = (2 * T) // bm + MOE_N_EXPERTS
    n_active = blk_end[-1]
    bidx = jnp.minimum(jnp.arange(n_blocks, dtype=jnp.int32), n_active - 1)
    blk_eid = jnp.sum(bidx[:, None] >= blk_end[None, :], axis=1).astype(jnp.int32)
    mine = blk_eid[:, None] == jnp.arange(MOE_N_EXPERTS, dtype=jnp.int32)[None, :]
    seg_end = jnp.sum(jnp.where(mine, (pstart + counts)[None, :], 0), axis=1)
    blk_valid = jnp.clip(seg_end - bidx * bm, 0, bm)
    blk_valid = jnp.where(jnp.arange(n_blocks) < n_active, blk_valid, 0).astype(jnp.int32)
    blk_fresh = jnp.concatenate([jnp.ones((1,), jnp.int32), (blk_eid[1:] != blk_eid[:-1]).astype(jnp.int32)])
    eid = route_t[ROUTE_EID:ROUTE_EID + 2].astype(jnp.int32)
    rank = route_t[ROUTE_RANK:ROUTE_RANK + 2].astype(jnp.int32)
    start = jnp.sum(jnp.where(eid[..., None] == jnp.arange(MOE_N_EXPERTS, dtype=jnp.int32),
                              pstart.astype(jnp.int32), 0), axis=-1)
    slots = (start + rank).reshape(2 * T)

    x_slots = _dispatch(slots, hn, n_blocks * bm, T)
    y_slots = _experts(bidx, blk_eid, blk_valid, blk_fresh, x_slots, moe_w1, moe_w3, moe_w2, n_blocks)
    return _combine(slots, y_slots, h_mid, route, next_norm_w[None, :], T)


def kernel(x, norm_mix_w, w_in, b_branch_gate, ret_decay_fwd, ret_decay_bwd, ret_gn_w, w_attn_branch,
           w_ret_branch, w_out, norm_moe_w, moe_w_group, moe_b_group, moe_w_expert, moe_b_expert, moe_w1,
           moe_w3, moe_w2, norm_final_w):
    B, S, D = x.shape
    depth = norm_mix_w.shape[0]
    assert depth == 1, "the final norm is fused into the layer's combine stage"
    assert D == D_MODEL and S % TM_INPROJ == 0 and (B * S) < (1 << 24)
    cos_t, sin_t = _rotary_tables(S)
    out = _layer(x.reshape(B * S, D), norm_mix_w[0], w_in[0], b_branch_gate[0], ret_decay_fwd[0],
                 ret_decay_bwd[0], ret_gn_w[0], w_attn_branch[0], w_ret_branch[0], w_out[0], norm_moe_w[0],
                 moe_w_group[0], moe_b_group[0], moe_w_expert[0], moe_b_expert[0], moe_w1[0], moe_w3[0],
                 moe_w2[0], norm_final_w, B, S, cos_t, sin_t)
    return out.reshape(B, S, D)
```

```python
import functools

import numpy as np
import jax
import jax.numpy as jnp
from jax import lax
from jax.experimental import pallas as pl
from jax.experimental.pallas import tpu as pltpu

F32 = jnp.float32
BF16 = jnp.bfloat16

D_MODEL = 1024
HEAD_DIM = 64
ATTN_PAIRS = ((128, 1), (512, 4), (2048, 16))
ATTN_HEADS_PER_GROUP = 8
ATTN_GROUP_W = ATTN_HEADS_PER_GROUP * HEAD_DIM
ATTN_HALF = 64
ROPE_THETA = 10000.0
RET_HEADS = 8
RET_QK_DIM = 64
RET_V_DIM = 128
RET_CHUNK = 128
RET_QK_W = RET_HEADS * RET_QK_DIM
RET_V_W = RET_HEADS * RET_V_DIM
MOE_GROUPS = 8
MOE_EXPERTS_PER_GROUP = 8
MOE_N_EXPERTS = MOE_GROUPS * MOE_EXPERTS_PER_GROUP
MOE_HIDDEN = 512
NORM_EPS = 1e-6

LANES = 128
ROW_TILES = D_MODEL // LANES
NEG_BIG = -1e30
LOG2_E = 1.4426950408889634

TM_INPROJ = 512
TM_MERGE = 512
MERGE_SPLIT = 2
MERGE_COL_CHUNKS = 4
TM_COMBINE = 256
MOE_BM = 512
DISPATCH_CHUNK = 2048
ATTN_QB = 128
ATTN_UNROLL = 8
RET_UNROLL = 8

VMEM_LIMIT = 56 * 1024 * 1024

_A = 3 * ATTN_GROUP_W
OFF_QA, OFF_KA, OFF_VA = 0, _A, 2 * _A
OFF_QR = 3 * _A
OFF_KR = OFF_QR + RET_QK_W
OFF_VR = OFF_KR + RET_QK_W
OFF_GR = OFF_VR + RET_V_W
OFF_GL = OFF_GR + RET_V_W
IN_W = OFF_GL + 2 * D_MODEL


def _cparams(sem, vmem=VMEM_LIMIT):
    return pltpu.CompilerParams(dimension_semantics=sem, vmem_limit_bytes=vmem)


def _inproj_body(x_ref, nw_ref, w_ref, bg_ref, cos_ref, sin_ref,
                 qa0, ka0, va0, qa1, ka1, va1, qa2, ka2, va2, qr, kr, vr, gr, gt,
                 stage_ref):
    tm = x_ref.shape[0]
    x = x_ref[...]
    ms = jnp.mean(x * x, axis=-1, keepdims=True)
    xn = (x * lax.rsqrt(ms + NORM_EPS) * nw_ref[...]).astype(BF16)
    cos = cos_ref[...]
    sin = sin_ref[...]
    lane = lax.broadcasted_iota(jnp.int32, (tm, LANES), 1)
    first_half = (lane & (HEAD_DIM - 1)) < (HEAD_DIM // 2)

    def proj(c0, width):
        return jnp.dot(xn, w_ref[:, c0:c0 + width], preferred_element_type=F32)

    def rotary(a, scale):
        partner = jnp.where(first_half, pltpu.roll(a, LANES - HEAD_DIM // 2, 1),
                            pltpu.roll(a, HEAD_DIM // 2, 1))
        r = a * cos + partner * sin
        return r * scale if scale != 1.0 else r

    def chunks(acc):
        return [acc[:, c * LANES:(c + 1) * LANES] for c in range(acc.shape[1] // LANES)]

    def store_natural(out_ref, acc, fn):
        for c, a in enumerate(chunks(acc)):
            out_ref[:, c * LANES:(c + 1) * LANES] = fn(a).astype(out_ref.dtype)

    def store_pairs(out_ref, acc, fn):
        for c, a in enumerate(chunks(acc)):
            out_ref[0, c] = fn(a).astype(out_ref.dtype)

    def store_strided(out_ref, acc, fn, d):
        for c, a in enumerate(chunks(acc)):
            stage_ref[c] = fn(a)
        for c in range(acc.shape[1] // LANES):
            for r in range(d):
                out_ref[0, c, r] = stage_ref[c, pl.ds(r, tm // d, stride=d), :].astype(out_ref.dtype)

    ident = lambda a: a
    rot_q = lambda a: rotary(a, HEAD_DIM ** -0.5 * LOG2_E)
    rot_1 = lambda a: rotary(a, 1.0)
    rot_k = lambda a: rotary(a, RET_QK_DIM ** -0.5)

    W = ATTN_GROUP_W
    store_pairs(qa0, proj(OFF_QA, W), rot_q)
    store_pairs(ka0, proj(OFF_KA, W), rot_1)
    store_pairs(va0, proj(OFF_VA, W), ident)
    for g, (qo, ko, vo) in ((1, (qa1, ka1, va1)), (2, (qa2, ka2, va2))):
        d = ATTN_PAIRS[g][1]
        store_strided(qo, proj(OFF_QA + g * W, W), rot_q, d)
        store_strided(ko, proj(OFF_KA + g * W, W), rot_1, d)
        store_strided(vo, proj(OFF_VA + g * W, W), ident, d)
    store_natural(qr, proj(OFF_QR, RET_QK_W), rot_1)
    store_natural(kr, proj(OFF_KR, RET_QK_W), rot_k)
    for h in range(RET_V_W // W):
        vr[:, h * W:(h + 1) * W] = proj(OFF_VR + h * W, W).astype(vr.dtype)
        gr[:, h * W:(h + 1) * W] = proj(OFF_GR + h * W, W).astype(gr.dtype)
    for h in range(2 * D_MODEL // W):
        z = proj(OFF_GL + h * W, W) + bg_ref[:, h * W:(h + 1) * W]
        gt[:, h * W:(h + 1) * W] = jax.nn.sigmoid(z).astype(gt.dtype)


def _in_projection(x2, norm_w, w_bf, b_gate, cos_t, sin_t, B, S):
    T = B * S
    tm = TM_INPROJ
    nt = S // tm
    W = ATTN_GROUP_W
    row = lambda i: (i, 0)
    const = lambda i: (0, 0)
    nat = lambda width: pl.BlockSpec((tm, width), row)

    P = W // LANES

    def strided_spec(d):
        return pl.BlockSpec((1, P, d, tm // d, LANES), lambda i: (i // nt, 0, 0, i % nt, 0))

    def strided_shape(d):
        return jax.ShapeDtypeStruct((B, P, d, S // d, LANES), BF16)

    pair_spec = pl.BlockSpec((1, P, tm, LANES), lambda i: (i // nt, 0, i % nt, 0))
    pair_shape = jax.ShapeDtypeStruct((B, P, S, LANES), BF16)
    nat_shape = lambda width: jax.ShapeDtypeStruct((T, width), BF16)
    d1, d2 = ATTN_PAIRS[1][1], ATTN_PAIRS[2][1]
    out_shape = ([pair_shape] * 3 + [strided_shape(d1)] * 3 + [strided_shape(d2)] * 3
                 + [nat_shape(RET_QK_W)] * 2 + [nat_shape(RET_V_W)] * 2 + [nat_shape(2 * D_MODEL)])
    out_specs = ([pair_spec] * 3 + [strided_spec(d1)] * 3 + [strided_spec(d2)] * 3
                 + [nat(RET_QK_W)] * 2 + [nat(RET_V_W)] * 2 + [nat(2 * D_MODEL)])
    in_specs = [
        pl.BlockSpec((tm, D_MODEL), row),
        pl.BlockSpec((1, D_MODEL), const),
        pl.BlockSpec((D_MODEL, IN_W), const, pipeline_mode=pl.Buffered(1)),
        pl.BlockSpec((1, 2 * D_MODEL), const),
        pl.BlockSpec((tm, LANES), lambda i: (i % nt, 0)),
        pl.BlockSpec((tm, LANES), lambda i: (i % nt, 0)),
    ]
    return pl.pallas_call(
        _inproj_body,
        grid=(T // tm,),
        in_specs=in_specs,
        out_specs=out_specs,
        out_shape=out_shape,
        scratch_shapes=[pltpu.VMEM((W // LANES, tm, LANES), F32)],
        compiler_params=_cparams(("parallel",)),
        name="in_projection",
    )(x2, norm_w, w_bf, b_gate, cos_t, sin_t)


def _attn_body(q0_ref, k0_ref, v0_ref, q1_ref, k1_ref, v1_ref, q2_ref, k2_ref, v2_ref, o_ref,
               va_ref, vb_ref, part_ref, bias_ref, *, S):
    QB, H = ATTN_QB, ATTN_HALF
    lane = lax.broadcasted_iota(jnp.int32, (QB, LANES), 1)
    head0 = lane < HEAD_DIM
    ones = jnp.ones((QB, LANES), BF16)

    qi = lax.broadcasted_iota(jnp.int32, (QB, QB + 2 * H), 0)
    ki = lax.broadcasted_iota(jnp.int32, (QB, QB + 2 * H), 1)
    for n in range(3):
        bias_ref[n] = jnp.where(jnp.abs(ki - qi - n * H) <= H, 0.0, NEG_BIG).astype(F32)

    def scores(q_rows, k_rows, bias, h):
        qm = jnp.where(head0 if h == 0 else jnp.logical_not(head0), q_rows, jnp.zeros_like(q_rows))
        return lax.dot_general(qm, k_rows, (((1,), (1,)), ((), ())), preferred_element_type=F32) + bias

    def weights(s):
        m = jnp.max(s, axis=-1, keepdims=True)
        return m, jnp.exp2(s - m).astype(BF16)

    def heads_to_lanes(m0, a, m1, b):
        num = jnp.where(head0, a, b)
        den = pltpu.roll(jnp.where(head0, b, a), HEAD_DIM, 1)
        mx = jnp.where(head0, m0, m1)
        return num, den, mx

    def run_group(q_ref, k_ref, v_ref, d, prepare, store):
        L = S // d
        KW = min(L, QB + 2 * H)
        nb = L // QB

        def fill(t, carry):
            r = t // nb
            rows = pl.ds(pl.multiple_of((t % nb) * QB, QB), QB)
            v = v_ref[0, 0, r, rows, :]
            dst = pl.ds(pl.multiple_of(t * QB, QB), QB)
            va_ref[dst, :] = jnp.where(head0, v, ones)
            vb_ref[dst, :] = jnp.where(head0, ones, v)
            return carry

        lax.fori_loop(0, S // QB, fill, 0, unroll=4)

        def trip(i, carry):
            blocks = []
            for u in range(ATTN_UNROLL):
                t = i * ATTN_UNROLL + u
                r = t // nb
                q0 = pl.multiple_of((t % nb) * QB, QB)
                ws = pl.multiple_of(jnp.clip(q0 - H, 0, L - KW), H)
                bias = bias_ref[(q0 - ws) // H][:, :KW]
                q_rows = q_ref[0, 0, r, pl.ds(q0, QB), :]
                k_rows = k_ref[0, 0, r, pl.ds(ws, KW), :]
                vrows = pl.ds(pl.multiple_of(r * L + ws, H), KW)
                blocks.append((r, q0, vrows, [scores(q_rows, k_rows, bias, h) for h in range(2)]))
            blocks = [(r, q0, vrows, [weights(s) for s in ss]) for r, q0, vrows, ss in blocks]
            done = []
            for r, q0, vrows, ((m0, p0), (m1, p1)) in blocks:
                a = jnp.dot(p0, va_ref[vrows, :], preferred_element_type=F32)
                b = jnp.dot(p1, vb_ref[vrows, :], preferred_element_type=F32)
                done.append((r, q0, m0, a, m1, b))
            done = [(r, q0, prepare(q0, *heads_to_lanes(m0, a, m1, b))) for r, q0, m0, a, m1, b in done]
            for r, q0, vals in done:
                store(r, q0, vals)
            return carry

        lax.fori_loop(0, S // QB // ATTN_UNROLL, trip, 0)

    def store_partial(g):
        d = ATTN_PAIRS[g][1]

        def store(r, q0, vals):
            rows = pl.ds(r + q0 * d, QB, stride=d)
            for n, val in enumerate(vals):
                part_ref[3 * (g - 1) + n, rows, :] = val
        return store

    keep = lambda q0, num, den, mx: (num, den, mx)
    run_group(q1_ref, k1_ref, v1_ref, ATTN_PAIRS[1][1], keep, store_partial(1))
    run_group(q2_ref, k2_ref, v2_ref, ATTN_PAIRS[2][1], keep, store_partial(2))

    def merge(q0, num, den, mx):
        rows = pl.ds(q0, QB)
        nums = [num, part_ref[0, rows, :], part_ref[3, rows, :]]
        dens = [den, part_ref[1, rows, :], part_ref[4, rows, :]]
        mxs = [mx, part_ref[2, rows, :], part_ref[5, rows, :]]
        top = jnp.maximum(jnp.maximum(mxs[0], mxs[1]), mxs[2])
        ws = [jnp.exp2(m - top) for m in mxs]
        n = ws[0] * nums[0] + ws[1] * nums[1] + ws[2] * nums[2]
        dn = ws[0] * dens[0] + ws[1] * dens[1] + ws[2] * dens[2]
        return (n / dn).astype(o_ref.dtype)

    def store_out(r, q0, y):
        o_ref[0, 0, pl.ds(q0, QB), :] = y

    run_group(q0_ref, k0_ref, v0_ref, ATTN_PAIRS[0][1], merge, store_out)


def _attention(qkv, B, S):
    P = ATTN_GROUP_W // LANES
    in_specs = []
    for _, d in ATTN_PAIRS:
        in_specs += [pl.BlockSpec((1, 1, d, S // d, LANES), lambda b, p: (b, p, 0, 0, 0))] * 3
    return pl.pallas_call(
        functools.partial(_attn_body, S=S),
        grid=(B, P),
        in_specs=in_specs,
        out_specs=pl.BlockSpec((1, 1, S, LANES), lambda b, p: (b, p, 0, 0)),
        out_shape=jax.ShapeDtypeStruct((B, P, S, LANES), BF16),
        scratch_shapes=[pltpu.VMEM((S, LANES), BF16), pltpu.VMEM((S, LANES), BF16),
                        pltpu.VMEM((6, S, LANES), F32),
                        pltpu.VMEM((3, ATTN_QB, ATTN_QB + 2 * ATTN_HALF), F32)],
        compiler_params=_cparams(("parallel", "parallel")),
        name="attention",
    )(*qkv)


def _log_sigmoid(z):
    return jnp.minimum(z, 0.0) - jnp.log(1.0 + jnp.exp(-jnp.abs(z)))


def _ret_body(dec_ref, q_ref, k_ref, v_ref, g_ref, gnw_ref, o_ref, kt_ref, sf_ref, sb_ref, *, S):
    C = RET_CHUNK
    nc = S // C
    lg = _log_sigmoid(dec_ref[0])
    a_row = lax.broadcasted_iota(jnp.int32, (C, LANES), 0).astype(F32)
    lane = lax.broadcasted_iota(jnp.int32, (C, LANES), 1)
    rel = (lax.broadcasted_iota(jnp.int32, (C, C), 0) - lax.broadcasted_iota(jnp.int32, (C, C), 1)).astype(F32)

    heads = []
    for h in range(2):
        lgf = lg[h:h + 1, :]
        lgb = lg[2 + h:3 + h, :]
        in_head = (lane < RET_QK_DIM) if h == 0 else (lane >= RET_QK_DIM)
        heads.append(dict(
            in_head=in_head,
            xi_f=jnp.where(in_head, jnp.exp((a_row + 1.0) * lgf), 0.0),
            xi_b=jnp.where(in_head, jnp.exp((C - a_row) * lgb), 0.0),
            zeta_f=jnp.exp((C - 1.0 - a_row) * lgf),
            zeta_b=jnp.exp(a_row * lgb),
            dloc=jnp.where(rel > 0, jnp.exp(rel * lgf), jnp.where(rel < 0, jnp.exp(-rel * lgb), 2.0)),
            cd_f=jnp.exp(C * lgf),
            cd_b=jnp.exp(C * lgb),
        ))

    def chunk_rows(n):
        return pl.ds(pl.multiple_of(n * C, C), C)

    def v_head(rows, h):
        return v_ref[0, rows, h * RET_V_DIM:(h + 1) * RET_V_DIM]

    def products(i, carry):
        items = []
        for u in range(RET_UNROLL):
            n = i * RET_UNROLL + u
            rows = chunk_rows(n)
            kt = k_ref[0, rows, :].astype(F32).T.astype(BF16)
            kt_ref[n] = kt
            for h, hd in enumerate(heads):
                vh = v_head(rows, h).astype(F32)
                items.append((n, h, kt, (vh * hd["zeta_f"]).astype(BF16), (vh * hd["zeta_b"]).astype(BF16)))
        outs = [(n, h, jnp.dot(kt, vf, preferred_element_type=F32), jnp.dot(kt, vb, preferred_element_type=F32))
                for n, h, kt, vf, vb in items]
        for n, h, f, b in outs:
            sf_ref[n, h] = f
            sb_ref[n, h] = b
        return carry

    lax.fori_loop(0, nc // RET_UNROLL, products, 0)

    def scan(ref, key, order):
        def step(i, state):
            n = order(i)
            new = []
            for h, hd in enumerate(heads):
                kv = ref[n, h]
                ref[n, h] = state[h]
                new.append(hd[key] * state[h] + kv)
            return tuple(new)
        zero = jnp.zeros(ref.shape[2:], F32)
        lax.fori_loop(0, nc, step, (zero, zero))

    scan(sf_ref, "cd_f", lambda i: i)
    scan(sb_ref, "cd_b", lambda i: nc - 1 - i)

    def outputs(i, carry):
        items = []
        for u in range(RET_UNROLL):
            n = i * RET_UNROLL + u
            rows = chunk_rows(n)
            qp = q_ref[0, rows, :]
            qf = qp.astype(F32)
            kt = kt_ref[n]
            for h, hd in enumerate(heads):
                qm = jnp.where(hd["in_head"], qp, jnp.zeros_like(qp))
                qx = jnp.concatenate([(qf * hd["xi_f"]).astype(BF16), (qf * hd["xi_b"]).astype(BF16)], axis=1)
                st = jnp.concatenate([sf_ref[n, h].astype(BF16), sb_ref[n, h].astype(BF16)], axis=0)
                items.append((rows, h, hd, jnp.dot(qm, kt, preferred_element_type=F32),
                              jnp.dot(qx, st, preferred_element_type=F32)))
        items = [(rows, h, hd, (s * hd["dloc"]).astype(BF16), cross) for rows, h, hd, s, cross in items]
        items = [(rows, h, cross + jnp.dot(p, v_head(rows, h), preferred_element_type=F32))
                 for rows, h, hd, p, cross in items]
        for rows, h, ret in items:
            mu = jnp.mean(ret, axis=-1, keepdims=True)
            xc = ret - mu
            var = jnp.mean(xc * xc, axis=-1, keepdims=True)
            cols = slice(h * RET_V_DIM, (h + 1) * RET_V_DIM)
            gate = g_ref[0, rows, cols].astype(F32)
            y = xc * lax.rsqrt(var + NORM_EPS) * gnw_ref[:, cols] * (gate * jax.nn.sigmoid(gate))
            o_ref[0, rows, cols] = y.astype(o_ref.dtype)
        return carry

    lax.fori_loop(0, nc // RET_UNROLL, outputs, 0)


def _retention(dec, qr, kr, vr, gr, gn_w, B, S):
    nc = S // RET_CHUNK
    npairs = RET_HEADS // 2
    qk_spec = pl.BlockSpec((1, S, 2 * RET_QK_DIM), lambda b, p: (b, 0, p))
    v_spec = pl.BlockSpec((1, S, 2 * RET_V_DIM), lambda b, p: (b, 0, p))
    return pl.pallas_call(
        functools.partial(_ret_body, S=S),
        grid=(B, npairs),
        in_specs=[pl.BlockSpec((1, 4, LANES), lambda b, p: (p, 0, 0)),
                  qk_spec, qk_spec, v_spec, v_spec,
                  pl.BlockSpec((1, 2 * RET_V_DIM), lambda b, p: (0, p))],
        out_specs=v_spec,
        out_shape=jax.ShapeDtypeStruct((B, S, RET_V_W), BF16),
        scratch_shapes=[pltpu.VMEM((nc, 2 * RET_QK_DIM, RET_CHUNK), BF16),
                        pltpu.VMEM((nc, 2, 2 * RET_QK_DIM, RET_V_DIM), F32),
                        pltpu.VMEM((nc, 2, 2 * RET_QK_DIM, RET_V_DIM), F32)],
        compiler_params=_cparams(("parallel", "parallel")),
        name="retention",
    )(dec, qr.reshape(B, S, RET_QK_W), kr.reshape(B, S, RET_QK_W),
      vr.reshape(B, S, RET_V_W), gr.reshape(B, S, RET_V_W), gn_w)


ROUTE_EID, ROUTE_RANK, ROUTE_GATE = 0, 2, 4
ROUTE_FIELDS = 8
ROUTER_EXPERT_LANE0 = MOE_GROUPS


def _merge_body(yatt_ref, yret_ref, gt_ref, x_ref, wa_ref, wb_ref, wo_ref,
                nw_ref, wrh_ref, wrl_ref, br_ref,
                h_ref, hn_ref, route_ref, route_t_ref, cnt_ref):
    tm = x_ref.shape[0]
    hm = tm // MERGE_SPLIT
    i = pl.program_id(0)
    cw = D_MODEL // MERGE_COL_CHUNKS

    def branch_products(rows):
        y_att = jnp.concatenate([yatt_ref[0, c, rows, :] for c in range(yatt_ref.shape[1])], axis=1)
        y_ret = yret_ref[rows, :]
        chunks = []
        for c in range(MERGE_COL_CHUNKS):
            cols = slice(c * cw, (c + 1) * cw)
            a = jnp.dot(y_att, wa_ref[:, cols], preferred_element_type=F32)
            b = jnp.dot(y_ret, wb_ref[:, cols], preferred_element_type=F32)
            g_att = gt_ref[rows, c * cw:(c + 1) * cw].astype(F32)
            g_ret = gt_ref[rows, D_MODEL + c * cw:D_MODEL + (c + 1) * cw].astype(F32)
            chunks.append((g_att * a + g_ret * b).astype(BF16))
        return jnp.concatenate(chunks, axis=1)

    def residual_norm(hf, rows, merged):
        mix = jnp.dot(merged, wo_ref[...], preferred_element_type=F32)
        h = x_ref[rows, :] + mix
        h_ref[rows, :] = h
        ms = jnp.mean(h * h, axis=-1, keepdims=True)
        hn = h * lax.rsqrt(ms + NORM_EPS) * nw_ref[...]
        for j in range(ROW_TILES):
            hn_ref[pl.ds(hf * hm * ROW_TILES + j, hm, stride=ROW_TILES), :] = hn[:, j * LANES:(j + 1) * LANES]
        hi = hn.astype(BF16)
        lo = (hn - hi.astype(F32)).astype(BF16)
        return hi, lo

    def router_logits(hi, lo):
        return (jnp.dot(hi, wrh_ref[...], preferred_element_type=F32)
                + jnp.dot(hi, wrl_ref[...], preferred_element_type=F32)
                + jnp.dot(lo, wrh_ref[...], preferred_element_type=F32)) + br_ref[...]

    lane = lax.broadcasted_iota(jnp.int32, (hm, LANES), 1)
    far = jnp.int32(4 * LANES)

    def first_argmax(vals, vmax):
        return jnp.min(jnp.where(vals == vmax, lane, far), axis=-1, keepdims=True)

    def route(logits):
        is_group = lane < MOE_GROUPS
        gl = jnp.where(is_group, logits, NEG_BIG)
        gmax = jnp.max(gl, axis=-1, keepdims=True)
        g_w = 1.0 / jnp.sum(jnp.where(is_group, jnp.exp(gl - gmax), 0.0), axis=-1, keepdims=True)
        g_idx = first_argmax(gl, gmax)
        e_lane = lane - ROUTER_EXPERT_LANE0
        in_group = (e_lane >= 0) & (e_lane < MOE_N_EXPERTS) & (jnp.right_shift(e_lane, 3) == g_idx)
        el = jnp.where(in_group, logits, NEG_BIG)
        m1 = jnp.max(el, axis=-1, keepdims=True)
        i1 = first_argmax(el, m1)
        el2 = jnp.where(lane == i1, NEG_BIG, el)
        m2 = jnp.max(el2, axis=-1, keepdims=True)
        i2 = first_argmax(el2, m2)
        ex = jnp.exp(m2 - m1)
        return i1, i2, g_w / (1.0 + ex), g_w * ex / (1.0 + ex)

    rows = [slice(hf * hm, (hf + 1) * hm) for hf in range(MERGE_SPLIT)]
    merged = [branch_products(r) for r in rows]
    split = [residual_norm(hf, r, m) for hf, (r, m) in enumerate(zip(rows, merged))]
    routed = [route(router_logits(hi, lo)) for hi, lo in split]

    @pl.when(i == 0)
    def _():
        cnt_ref[...] = jnp.zeros(cnt_ref.shape, F32)

    r_idx = lax.broadcasted_iota(jnp.int32, (hm, hm), 0)
    c_idx = lax.broadcasted_iota(jnp.int32, (hm, hm), 1)
    lower = jnp.where(c_idx < r_idx, 1.0, 0.0).astype(BF16)
    running = cnt_ref[...]
    for hf, (i1, i2, gate1, gate2) in enumerate(routed):
        hot1 = lane == i1
        hot2 = lane == i2
        onehot = jnp.where(hot1 | hot2, 1.0, 0.0)
        before = jnp.dot(lower, onehot.astype(BF16), preferred_element_type=F32) + running
        rank1 = jnp.sum(jnp.where(hot1, before, 0.0), axis=-1, keepdims=True)
        rank2 = jnp.sum(jnp.where(hot2, before, 0.0), axis=-1, keepdims=True)
        running = running + jnp.sum(onehot, axis=0, keepdims=True)
        rec = jnp.zeros((hm, LANES), F32)
        for pos, val in ((ROUTE_EID, (i1 - ROUTER_EXPERT_LANE0).astype(F32)),
                         (ROUTE_EID + 1, (i2 - ROUTER_EXPERT_LANE0).astype(F32)),
                         (ROUTE_RANK, rank1), (ROUTE_RANK + 1, rank2),
                         (ROUTE_GATE, gate1), (ROUTE_GATE + 1, gate2)):
            rec = jnp.where(lane == pos, val, rec)
        route_ref[rows[hf], :] = rec
        route_t_ref[:, hf * hm:(hf + 1) * hm] = rec.T[:route_t_ref.shape[0], :]
    cnt_ref[...] = running


def _merge_route(y_att, y_ret, gates, x2, wa, wb, wo, nw, wr_hi, wr_lo, b_r, T, S):
    tm = TM_MERGE
    nt = S // tm
    row = lambda i: (i, 0)
    const = lambda i: (0, 0)
    full = lambda arr: pl.BlockSpec(arr.shape, const)
    in_specs = ([pl.BlockSpec((1, y_att.shape[1], tm, LANES), lambda i: (i // nt, 0, i % nt, 0)),
                 pl.BlockSpec((tm, RET_V_W), row), pl.BlockSpec((tm, 2 * D_MODEL), row),
                 pl.BlockSpec((tm, D_MODEL), row),
                 full(wa), full(wb), full(wo), full(nw), full(wr_hi), full(wr_lo), full(b_r)])
    return pl.pallas_call(
        _merge_body,
        grid=(T // tm,),
        in_specs=in_specs,
        out_specs=[pl.BlockSpec((tm, D_MODEL), row),
                   pl.BlockSpec((tm * ROW_TILES, LANES), row),
                   pl.BlockSpec((tm, LANES), row), pl.BlockSpec((ROUTE_FIELDS, tm), lambda i: (0, i)),
                   pl.BlockSpec((1, LANES), const)],
        out_shape=[jax.ShapeDtypeStruct((T, D_MODEL), F32), jax.ShapeDtypeStruct((T * ROW_TILES, LANES), F32),
                   jax.ShapeDtypeStruct((T, LANES), F32), jax.ShapeDtypeStruct((ROUTE_FIELDS, T), F32),
                   jax.ShapeDtypeStruct((1, LANES), F32)],
        compiler_params=_cparams(("arbitrary",)),
        name="merge_route",
    )(y_att, y_ret, gates, x2, wa, wb, wo, nw, wr_hi, wr_lo, b_r)


ISSUE_UNROLL = 8


def _tile_rows(n):
    return pl.ds(pl.multiple_of(n * ROW_TILES, ROW_TILES), ROW_TILES)


def _dispatch_body(slot_ref, hn_ref, xs_ref, sem, *, T):
    i = pl.program_id(0)
    ch = hn_ref.shape[0] // ROW_TILES

    def row_copy(j, slot):
        return pltpu.make_async_copy(hn_ref.at[_tile_rows(j)], xs_ref.at[_tile_rows(slot)], sem)

    def issue(j, carry):
        t = i * ch + j
        row_copy(j, slot_ref[t]).start(priority=0)
        row_copy(j, slot_ref[T + t]).start(priority=1)
        return carry

    lax.fori_loop(0, ch, issue, 0, unroll=ISSUE_UNROLL)
    for _ in range(2):
        pltpu.make_async_copy(hn_ref, xs_ref.at[pl.ds(0, ch * ROW_TILES)], sem).wait()


def _dispatch(slots, hn, n_slots, T):
    ch = DISPATCH_CHUNK
    grid_spec = pltpu.PrefetchScalarGridSpec(
        num_scalar_prefetch=1,
        grid=(T // ch,),
        in_specs=[pl.BlockSpec((ch * ROW_TILES, LANES), lambda i, s: (i, 0))],
        out_specs=pl.BlockSpec(memory_space=pl.ANY),
        scratch_shapes=[pltpu.SemaphoreType.DMA(())],
    )
    return pl.pallas_call(
        functools.partial(_dispatch_body, T=T),
        grid_spec=grid_spec,
        out_shape=jax.ShapeDtypeStruct((n_slots * ROW_TILES, LANES), F32),
        compiler_params=_cparams(("arbitrary",)),
        name="moe_dispatch",
    )(slots, hn)


def _expert_body(blk_ref, eid_ref, valid_ref, fresh_ref, next_ref, x_ref, w1_ref, w3_ref, w2_ref, y_ref,
                 w1b, w3b, w2b, w1s, w3s, w2s, wsem):
    i = pl.program_id(0)
    valid = valid_ref[i]

    def weight_copies(e):
        return (pltpu.make_async_copy(w1_ref.at[e], w1s, wsem.at[0]),
                pltpu.make_async_copy(w3_ref.at[e], w3s, wsem.at[1]),
                pltpu.make_async_copy(w2_ref.at[e], w2s, wsem.at[2]))

    @pl.when(i == 0)
    def _():
        for cp in weight_copies(eid_ref[0]):
            cp.start()

    @pl.when(valid > 0)
    def _():
        @pl.when(fresh_ref[i] == 1)
        def _():
            for cp in weight_copies(eid_ref[i]):
                cp.wait()
            w1b[...] = w1s[...].astype(BF16)
            w3b[...] = w3s[...].astype(BF16)
            w2b[...] = w2s[...].astype(BF16)

            @pl.when(next_ref[i] >= 0)
            def _():
                for cp in weight_copies(next_ref[i]):
                    cp.start()

        bm = x_ref.shape[0] // ROW_TILES
        live =lax.broadcasted_iota(jnp.int32, (bm, LANES), 0) < valid
        x = jnp.concatenate(
            [jnp.where(live, x_ref[pl.ds(j, bm, stride=ROW_TILES), :], 0.0).astype(BF16)
             for j in range(ROW_TILES)], axis=1)
        a = jnp.dot(x, w1b[...], preferred_element_type=F32)
        b = jnp.dot(x, w3b[...], preferred_element_type=F32)
        hid = (a * jax.nn.sigmoid(a) * b).astype(BF16)
        y = jnp.dot(hid, w2b[...], preferred_element_type=F32)
        for j in range(ROW_TILES):
            y_ref[pl.ds(j, bm, stride=ROW_TILES), :] = y[:, j * LANES:(j + 1) * LANES]


def _experts(blk, blk_eid, blk_valid, blk_fresh, blk_next, x_slots, w1, w3, w2, n_blocks):
    bm = MOE_BM
    slot_block = lambda i, blk, eid, val, fr, nx: (blk[i], 0)
    grid_spec = pltpu.PrefetchScalarGridSpec(
        num_scalar_prefetch=5,
        grid=(n_blocks,),
        in_specs=[pl.BlockSpec((bm * ROW_TILES, LANES), slot_block),
                  pl.BlockSpec(memory_space=pl.ANY), pl.BlockSpec(memory_space=pl.ANY),
                  pl.BlockSpec(memory_space=pl.ANY)],
        out_specs=pl.BlockSpec((bm * ROW_TILES, LANES), slot_block),
        scratch_shapes=[pltpu.VMEM((D_MODEL, MOE_HIDDEN), BF16), pltpu.VMEM((D_MODEL, MOE_HIDDEN), BF16),
                        pltpu.VMEM((MOE_HIDDEN, D_MODEL), BF16),
                        pltpu.VMEM((D_MODEL, MOE_HIDDEN), F32), pltpu.VMEM((D_MODEL, MOE_HIDDEN), F32),
                        pltpu.VMEM((MOE_HIDDEN, D_MODEL), F32), pltpu.SemaphoreType.DMA((3,))],
    )
    return pl.pallas_call(
        _expert_body,
        grid_spec=grid_spec,
        out_shape=jax.ShapeDtypeStruct(x_slots.shape, F32),
        compiler_params=_cparams(("arbitrary",)),
        name="moe_experts",
    )(blk, blk_eid, blk_valid, blk_fresh, blk_next, x_slots, w1, w3, w2)


def _combine_body(slot_ref, ys_ref, h_ref, route_ref, nw_ref, o_ref, ybuf, sem, *, T):
    i = pl.program_id(0)
    n = pl.num_programs(0)
    tm = h_ref.shape[0]

    def row_copy(slot, buf, k, j):
        return pltpu.make_async_copy(ys_ref.at[_tile_rows(slot)], ybuf.at[buf, k, _tile_rows(j)], sem.at[buf])

    def issue(tile, buf):
        def one(j, carry):
            t = tile * tm + j
            row_copy(slot_ref[t], buf, 0, j).start(priority=0)
            row_copy(slot_ref[T + t], buf, 1, j).start(priority=1)
            return carry
        lax.fori_loop(0, tm, one, 0, unroll=ISSUE_UNROLL)

    @pl.when(i == 0)
    def _():
        issue(0, 0)

    @pl.when(i + 1 < n)
    def _():
        issue(i + 1, (i + 1) % 2)

    buf = i % 2
    for k in range(2):
        pltpu.make_async_copy(ys_ref.at[pl.ds(0, tm * ROW_TILES)], ybuf.at[buf, k], sem.at[buf]).wait()
    route = route_ref[...]
    g1 = route[:, ROUTE_GATE:ROUTE_GATE + 1]
    g2 = route[:, ROUTE_GATE + 1:ROUTE_GATE + 2]
    hs = []
    ss = jnp.zeros((tm, 1), F32)
    for j in range(ROW_TILES):
        tile_row = pl.ds(j, tm, stride=ROW_TILES)
        hj = h_ref[:, j * LANES:(j + 1) * LANES] + (ybuf[buf, 0, tile_row, :] * g1 + ybuf[buf, 1, tile_row, :] * g2)
        hs.append(hj)
        ss = ss + jnp.sum(hj * hj, axis=-1, keepdims=True)
    inv = lax.rsqrt(ss * (1.0 / D_MODEL) + NORM_EPS)
    for j, hj in enumerate(hs):
        cols = slice(j * LANES, (j + 1) * LANES)
        o_ref[:, cols] = hj * inv * nw_ref[:, cols]


def _combine(slots, y_slots, h, route, nw, T):
    tm = TM_COMBINE
    row = lambda i, s: (i, 0)
    grid_spec = pltpu.PrefetchScalarGridSpec(
        num_scalar_prefetch=1,
        grid=(T // tm,),
        in_specs=[pl.BlockSpec(memory_space=pl.ANY),
                  pl.BlockSpec((tm, D_MODEL), row),
                  pl.BlockSpec((tm, LANES), row),
                  pl.BlockSpec((1, D_MODEL), lambda i, s: (0, 0))],
        out_specs=pl.BlockSpec((tm, D_MODEL), row),
        scratch_shapes=[pltpu.VMEM((2, 2, tm * ROW_TILES, LANES), F32), pltpu.SemaphoreType.DMA((2,))],
    )
    return pl.pallas_call(
        functools.partial(_combine_body, T=T),
        grid_spec=grid_spec,
        out_shape=jax.ShapeDtypeStruct((T, D_MODEL), F32),
        compiler_params=_cparams(("arbitrary",)),
        name="moe_combine",
    )(slots, y_slots, h, route, nw)


def _rotary_tables(S):
    inv_freq = (1.0 / (np.float32(ROPE_THETA) ** (np.arange(0, HEAD_DIM, 2, dtype=np.float32) / HEAD_DIM))
                ).astype(np.float32)
    ang = np.arange(S, dtype=np.float32)[:, None] * inv_freq[None, :]
    cos, sin = np.cos(ang), np.sin(ang)
    reps = LANES // HEAD_DIM
    cos_t = np.tile(np.concatenate([cos, cos], axis=1), (1, reps)).astype(np.float32)
    sin_t = np.tile(np.concatenate([-sin, sin], axis=1), (1, reps)).astype(np.float32)
    return jnp.asarray(cos_t), jnp.asarray(sin_t)


def _layer(h_in, norm_mix_w, w_in, b_branch_gate, ret_decay_fwd, ret_decay_bwd, ret_gn_w, w_attn_branch,
           w_ret_branch, w_out, norm_moe_w, moe_w_group, moe_b_group, moe_w_expert, moe_b_expert,
           moe_w1, moe_w3, moe_w2, next_norm_w, B, S, cos_t, sin_t):
    T = B * S
    (qa0, ka0, va0, qa1, ka1, va1, qa2, ka2, va2, qr, kr, vr, gr, gates) = _in_projection(
        h_in, norm_mix_w[None, :], w_in.astype(BF16), b_branch_gate[None, :], cos_t, sin_t, B, S)

    unit = lambda a: a[:, :, None]
    y_att = _attention((unit(qa0), unit(ka0), unit(va0), qa1, ka1, va1, qa2, ka2, va2), B, S)

    dec = jnp.stack([ret_decay_fwd.reshape(RET_HEADS // 2, 2), ret_decay_bwd.reshape(RET_HEADS // 2, 2)], axis=1)
    dec = jnp.broadcast_to(dec.reshape(RET_HEADS // 2, 4, 1), (RET_HEADS // 2, 4, LANES)).astype(F32)
    y_ret = _retention(dec, qr, kr, vr, gr, ret_gn_w[None, :], B, S).reshape(T, RET_V_W)

    pad = LANES - MOE_GROUPS - MOE_N_EXPERTS
    w_r = jnp.concatenate([moe_w_group, moe_w_expert, jnp.zeros((D_MODEL, pad), F32)], axis=1)
    w_r_hi = w_r.astype(BF16)
    w_r_lo = (w_r - w_r_hi.astype(F32)).astype(BF16)
    b_r = jnp.concatenate([moe_b_group, moe_b_expert, jnp.zeros((pad,), F32)])[None, :]

    h_mid, hn, route, route_t, cnt = _merge_route(
        y_att, y_ret, gates, h_in, w_attn_branch.astype(BF16), w_ret_branch.astype(BF16),
        w_out.astype(BF16), norm_moe_w[None, :], w_r_hi, w_r_lo, b_r, T, S)

    bm = MOE_BM
    counts = cnt[0, ROUTER_EXPERT_LANE0:ROUTER_EXPERT_LANE0 + MOE_N_EXPERTS].astype(jnp.int32)
    nblk = (counts + bm - 1) // bm
    blk_end = jnp.cumsum(nblk)
    pstart = (blk_end - nblk) * bm
    n_blocks = (2 * T) // bm + MOE_N_EXPERTS
    n_active = blk_end[-1]
    bidx = jnp.minimum(jnp.arange(n_blocks, dtype=jnp.int32), n_active - 1)
    blk_eid = jnp.sum(bidx[:, None] >= blk_end[None, :], axis=1).astype(jnp.int32)
    mine = blk_eid[:, None] == jnp.arange(MOE_N_EXPERTS, dtype=jnp.int32)[None, :]
    seg_end = jnp.sum(jnp.where(mine, (pstart + counts)[None, :], 0), axis=1)
    blk_valid = jnp.clip(seg_end - bidx * bm, 0, bm)
    blk_valid = jnp.where(jnp.arange(n_blocks) < n_active, blk_valid, 0).astype(jnp.int32)
    blk_fresh = jnp.concatenate([jnp.ones((1,), jnp.int32), (blk_eid[1:] != blk_eid[:-1]).astype(jnp.int32)])
    ar = jnp.arange(MOE_N_EXPERTS, dtype=jnp.int32)
    later = jnp.min(jnp.where((nblk > 0)[None, :] & (ar[None, :] > ar[:, None]), ar[None, :], MOE_N_EXPERTS), axis=1)
    later = jnp.where(later < MOE_N_EXPERTS, later, -1)
    blk_next = (jnp.sum(jnp.where(mine, later[None, :] + 1, 0), axis=1) - 1).astype(jnp.int32)
    eid = route_t[ROUTE_EID:ROUTE_EID + 2].astype(jnp.int32)
    rank = route_t[ROUTE_RANK:ROUTE_RANK + 2].astype(jnp.int32)
    start = jnp.sum(jnp.where(eid[..., None] == jnp.arange(MOE_N_EXPERTS, dtype=jnp.int32),
                              pstart.astype(jnp.int32), 0), axis=-1)
    slots = (start + rank).reshape(2 * T)

    x_slots = _dispatch(slots, hn, n_blocks * bm, T)
    y_slots = _experts(bidx, blk_eid, blk_valid, blk_fresh, blk_next, x_slots, moe_w1, moe_w3, moe_w2, n_blocks)
    return _combine(slots, y_slots, h_mid, route, next_norm_w[None, :], T)


def kernel(x, norm_mix_w, w_in, b_branch_gate, ret_decay_fwd, ret_decay_bwd, ret_gn_w, w_attn_branch,
           w_ret_branch, w_out, norm_moe_w, moe_w_group, moe_b_group, moe_w_expert, moe_b_expert, moe_w1,
           moe_w3, moe_w2, norm_final_w):
    B, S, D = x.shape
    depth = norm_mix_w.shape[0]
    assert depth == 1, "the final norm is fused into the layer's combine stage"
    assert D == D_MODEL and S % TM_INPROJ == 0 and (B * S) < (1 << 24)
    cos_t, sin_t = _rotary_tables(S)
    out = _layer(x.reshape(B * S, D), norm_mix_w[0], w_in[0], b_branch_gate[0], ret_decay_fwd[0],
                 ret_decay_bwd[0], ret_gn_w[0], w_attn_branch[0], w_ret_branch[0], w_out[0], norm_moe_w[0],
                 moe_w_group[0], moe_b_group[0], moe_w_expert[0], moe_b_expert[0], moe_w1[0], moe_w3[0],
                 moe_w2[0], norm_final_w, B, S, cos_t, sin_t)
    return out.reshape(B, S, D)
```

```python
import functools

import numpy as np
import jax
import jax.numpy as jnp
from jax import lax
from jax.experimental import pallas as pl
from jax.experimental.pallas import tpu as pltpu

F32 = jnp.float32
BF16 = jnp.bfloat16

D_MODEL = 1024
HEAD_DIM = 64
ATTN_PAIRS = ((128, 1), (512, 4), (2048, 16))
ATTN_HEADS_PER_GROUP = 8
ATTN_GROUP_W = ATTN_HEADS_PER_GROUP * HEAD_DIM
ATTN_HALF = 64
ROPE_THETA = 10000.0
RET_HEADS = 8
RET_QK_DIM = 64
RET_V_DIM = 128
RET_CHUNK = 128
RET_QK_W = RET_HEADS * RET_QK_DIM
RET_V_W = RET_HEADS * RET_V_DIM
MOE_GROUPS = 8
MOE_EXPERTS_PER_GROUP = 8
MOE_N_EXPERTS = MOE_GROUPS * MOE_EXPERTS_PER_GROUP
MOE_HIDDEN = 512
NORM_EPS = 1e-6

LANES = 128
ROW_TILES = D_MODEL // LANES
NEG_BIG = -1e30
LOG2_E = 1.4426950408889634

TM_INPROJ = 512
TM_MERGE = 1024
MERGE_SPLIT = 4
MERGE_COL_CHUNKS = 4
TM_COMBINE = 256
MOE_BM = 512
DISPATCH_CHUNK = 2048
ATTN_QB = 128
ATTN_UNROLL = 8
RET_UNROLL = 8

VMEM_LIMIT = 56 * 1024 * 1024

_A = 3 * ATTN_GROUP_W
OFF_QA, OFF_KA, OFF_VA = 0, _A, 2 * _A
OFF_QR = 3 * _A
OFF_KR = OFF_QR + RET_QK_W
OFF_VR = OFF_KR + RET_QK_W
OFF_GR = OFF_VR + RET_V_W
OFF_GL = OFF_GR + RET_V_W
IN_W = OFF_GL + 2 * D_MODEL


def _cparams(sem, vmem=VMEM_LIMIT):
    return pltpu.CompilerParams(dimension_semantics=sem, vmem_limit_bytes=vmem)


def _inproj_body(x_ref, nw_ref, w_ref, bg_ref, cos_ref, sin_ref,
                 qa0, ka0, va0, qa1, ka1, va1, qa2, ka2, va2, qr, kr, vr, gr, gt,
                 stage_ref):
    tm = x_ref.shape[0]
    x = x_ref[...]
    ms = jnp.mean(x * x, axis=-1, keepdims=True)
    xn = (x * lax.rsqrt(ms + NORM_EPS) * nw_ref[...]).astype(BF16)
    cos = cos_ref[...]
    sin = sin_ref[...]
    lane = lax.broadcasted_iota(jnp.int32, (tm, LANES), 1)
    first_half = (lane & (HEAD_DIM - 1)) < (HEAD_DIM // 2)

    def proj(c0, width):
        return jnp.dot(xn, w_ref[:, c0:c0 + width], preferred_element_type=F32)

    def rotary(a, scale):
        partner = jnp.where(first_half, pltpu.roll(a, LANES - HEAD_DIM // 2, 1),
                            pltpu.roll(a, HEAD_DIM // 2, 1))
        r = a * cos + partner * sin
        return r * scale if scale != 1.0 else r

    def chunks(acc):
        return [acc[:, c * LANES:(c + 1) * LANES] for c in range(acc.shape[1] // LANES)]

    def store_natural(out_ref, acc, fn):
        for c, a in enumerate(chunks(acc)):
            out_ref[:, c * LANES:(c + 1) * LANES] = fn(a).astype(out_ref.dtype)

    def store_pairs(out_ref, acc, fn):
        for c, a in enumerate(chunks(acc)):
            out_ref[0, c] = fn(a).astype(out_ref.dtype)

    def store_strided(out_ref, acc, fn, d):
        for c, a in enumerate(chunks(acc)):
            stage_ref[c] = fn(a)
        for c in range(acc.shape[1] // LANES):
            for r in range(d):
                out_ref[0, c, r] = stage_ref[c, pl.ds(r, tm // d, stride=d), :].astype(out_ref.dtype)

    ident = lambda a: a
    rot_q = lambda a: rotary(a, HEAD_DIM ** -0.5 * LOG2_E)
    rot_1 = lambda a: rotary(a, 1.0)
    rot_k = lambda a: rotary(a, RET_QK_DIM ** -0.5)

    W = ATTN_GROUP_W
    store_pairs(qa0, proj(OFF_QA, W), rot_q)
    store_pairs(ka0, proj(OFF_KA, W), rot_1)
    store_pairs(va0, proj(OFF_VA, W), ident)
    for g, (qo, ko, vo) in ((1, (qa1, ka1, va1)), (2, (qa2, ka2, va2))):
        d = ATTN_PAIRS[g][1]
        store_strided(qo, proj(OFF_QA + g * W, W), rot_q, d)
        store_strided(ko, proj(OFF_KA + g * W, W), rot_1, d)
        store_strided(vo, proj(OFF_VA + g * W, W), ident, d)
    store_natural(qr, proj(OFF_QR, RET_QK_W), rot_1)
    store_natural(kr, proj(OFF_KR, RET_QK_W), rot_k)
    for h in range(RET_V_W // W):
        vr[:, h * W:(h + 1) * W] = proj(OFF_VR + h * W, W).astype(vr.dtype)
        gr[:, h * W:(h + 1) * W] = proj(OFF_GR + h * W, W).astype(gr.dtype)
    for h in range(2 * D_MODEL // W):
        z = proj(OFF_GL + h * W, W) + bg_ref[:, h * W:(h + 1) * W]
        gt[:, h * W:(h + 1) * W] = jax.nn.sigmoid(z).astype(gt.dtype)


def _in_projection(x2, norm_w, w_bf, b_gate, cos_t, sin_t, B, S):
    T = B * S
    tm = TM_INPROJ
    nt = S // tm
    W = ATTN_GROUP_W
    row = lambda i: (i, 0)
    const = lambda i: (0, 0)
    nat = lambda width: pl.BlockSpec((tm, width), row)

    P = W // LANES

    def strided_spec(d):
        return pl.BlockSpec((1, P, d, tm // d, LANES), lambda i: (i // nt, 0, 0, i % nt, 0))

    def strided_shape(d):
        return jax.ShapeDtypeStruct((B, P, d, S // d, LANES), BF16)

    pair_spec = pl.BlockSpec((1, P, tm, LANES), lambda i: (i // nt, 0, i % nt, 0))
    pair_shape = jax.ShapeDtypeStruct((B, P, S, LANES), BF16)
    nat_shape = lambda width: jax.ShapeDtypeStruct((T, width), BF16)
    d1, d2 = ATTN_PAIRS[1][1], ATTN_PAIRS[2][1]
    out_shape = ([pair_shape] * 3 + [strided_shape(d1)] * 3 + [strided_shape(d2)] * 3
                 + [nat_shape(RET_QK_W)] * 2 + [nat_shape(RET_V_W)] * 2 + [nat_shape(2 * D_MODEL)])
    out_specs = ([pair_spec] * 3 + [strided_spec(d1)] * 3 + [strided_spec(d2)] * 3
                 + [nat(RET_QK_W)] * 2 + [nat(RET_V_W)] * 2 + [nat(2 * D_MODEL)])
    in_specs = [
        pl.BlockSpec((tm, D_MODEL), row),
        pl.BlockSpec((1, D_MODEL), const),
        pl.BlockSpec((D_MODEL, IN_W), const, pipeline_mode=pl.Buffered(1)),
        pl.BlockSpec((1, 2 * D_MODEL), const),
        pl.BlockSpec((tm, LANES), lambda i: (i % nt, 0)),
        pl.BlockSpec((tm, LANES), lambda i: (i % nt, 0)),
    ]
    return pl.pallas_call(
        _inproj_body,
        grid=(T // tm,),
        in_specs=in_specs,
        out_specs=out_specs,
        out_shape=out_shape,
        scratch_shapes=[pltpu.VMEM((W // LANES, tm, LANES), F32)],
        compiler_params=_cparams(("parallel",)),
        name="in_projection",
    )(x2, norm_w, w_bf, b_gate, cos_t, sin_t)


def _attn_body(q0_ref, k0_ref, v0_ref, q1_ref, k1_ref, v1_ref, q2_ref, k2_ref, v2_ref, o_ref,
               va_ref, vb_ref, part_ref, bias_ref, *, S):
    QB, H = ATTN_QB, ATTN_HALF
    lane = lax.broadcasted_iota(jnp.int32, (QB, LANES), 1)
    head0 = lane < HEAD_DIM
    ones = jnp.ones((QB, LANES), BF16)

    qi = lax.broadcasted_iota(jnp.int32, (QB, QB + 2 * H), 0)
    ki = lax.broadcasted_iota(jnp.int32, (QB, QB + 2 * H), 1)
    for n in range(3):
        bias_ref[n] = jnp.where(jnp.abs(ki - qi - n * H) <= H, 0.0, NEG_BIG).astype(F32)

    def scores(q_rows, k_rows, bias, h):
        qm = jnp.where(head0 if h == 0 else jnp.logical_not(head0), q_rows, jnp.zeros_like(q_rows))
        return lax.dot_general(qm, k_rows, (((1,), (1,)), ((), ())), preferred_element_type=F32) + bias

    def weights(s):
        m = jnp.max(s, axis=-1, keepdims=True)
        return m, jnp.exp2(s - m).astype(BF16)

    def heads_to_lanes(m0, a, m1, b):
        num = jnp.where(head0, a, b)
        den = pltpu.roll(jnp.where(head0, b, a), HEAD_DIM, 1)
        mx = jnp.where(head0, m0, m1)
        return num, den, mx

    def run_group(q_ref, k_ref, v_ref, d, prepare, store):
        L = S // d
        KW = min(L, QB + 2 * H)
        nb = L // QB

        def fill(t, carry):
            r = t // nb
            rows = pl.ds(pl.multiple_of((t % nb) * QB, QB), QB)
            v = v_ref[0, 0, r, rows, :]
            dst = pl.ds(pl.multiple_of(t * QB, QB), QB)
            va_ref[dst, :] = jnp.where(head0, v, ones)
            vb_ref[dst, :] = jnp.where(head0, ones, v)
            return carry

        lax.fori_loop(0, S // QB, fill, 0, unroll=4)

        def trip(i, carry):
            blocks = []
            for u in range(ATTN_UNROLL):
                t = i * ATTN_UNROLL + u
                r = t // nb
                q0 = pl.multiple_of((t % nb) * QB, QB)
                ws = pl.multiple_of(jnp.clip(q0 - H, 0, L - KW), H)
                bias = bias_ref[(q0 - ws) // H][:, :KW]
                q_rows = q_ref[0, 0, r, pl.ds(q0, QB), :]
                k_rows = k_ref[0, 0, r, pl.ds(ws, KW), :]
                vrows = pl.ds(pl.multiple_of(r * L + ws, H), KW)
                blocks.append((r, q0, vrows, [scores(q_rows, k_rows, bias, h) for h in range(2)]))
            blocks = [(r, q0, vrows, [weights(s) for s in ss]) for r, q0, vrows, ss in blocks]
            done = []
            for r, q0, vrows, ((m0, p0), (m1, p1)) in blocks:
                a = jnp.dot(p0, va_ref[vrows, :], preferred_element_type=F32)
                b = jnp.dot(p1, vb_ref[vrows, :], preferred_element_type=F32)
                done.append((r, q0, m0, a, m1, b))
            done = [(r, q0, prepare(q0, *heads_to_lanes(m0, a, m1, b))) for r, q0, m0, a, m1, b in done]
            for r, q0, vals in done:
                store(r, q0, vals)
            return carry

        lax.fori_loop(0, S // QB // ATTN_UNROLL, trip, 0)

    def store_partial(g):
        d = ATTN_PAIRS[g][1]

        def store(r, q0, vals):
            rows = pl.ds(r + q0 * d, QB, stride=d)
            for n, val in enumerate(vals):
                part_ref[3 * (g - 1) + n, rows, :] = val
        return store

    keep = lambda q0, num, den, mx: (num, den, mx)
    run_group(q1_ref, k1_ref, v1_ref, ATTN_PAIRS[1][1], keep, store_partial(1))
    run_group(q2_ref, k2_ref, v2_ref, ATTN_PAIRS[2][1], keep, store_partial(2))

    def merge(q0, num, den, mx):
        rows = pl.ds(q0, QB)
        nums = [num, part_ref[0, rows, :], part_ref[3, rows, :]]
        dens = [den, part_ref[1, rows, :], part_ref[4, rows, :]]
        mxs = [mx, part_ref[2, rows, :], part_ref[5, rows, :]]
        top = jnp.maximum(jnp.maximum(mxs[0], mxs[1]), mxs[2])
        ws = [jnp.exp2(m - top) for m in mxs]
        n = ws[0] * nums[0] + ws[1] * nums[1] + ws[2] * nums[2]
        dn = ws[0] * dens[0] + ws[1] * dens[1] + ws[2] * dens[2]
        return (n / dn).astype(o_ref.dtype)

    def store_out(r, q0, y):
        o_ref[0, 0, pl.ds(q0, QB), :] = y

    run_group(q0_ref, k0_ref, v0_ref, ATTN_PAIRS[0][1], merge, store_out)


def _attention(qkv, B, S):
    P = ATTN_GROUP_W // LANES
    in_specs = []
    for _, d in ATTN_PAIRS:
        in_specs += [pl.BlockSpec((1, 1, d, S // d, LANES), lambda b, p: (b, p, 0, 0, 0))] * 3
    return pl.pallas_call(
        functools.partial(_attn_body, S=S),
        grid=(B, P),
        in_specs=in_specs,
        out_specs=pl.BlockSpec((1, 1, S, LANES), lambda b, p: (b, p, 0, 0)),
        out_shape=jax.ShapeDtypeStruct((B, P, S, LANES), BF16),
        scratch_shapes=[pltpu.VMEM((S, LANES), BF16), pltpu.VMEM((S, LANES), BF16),
                        pltpu.VMEM((6, S, LANES), F32),
                        pltpu.VMEM((3, ATTN_QB, ATTN_QB + 2 * ATTN_HALF), F32)],
        compiler_params=_cparams(("parallel", "parallel")),
        name="attention",
    )(*qkv)


def _log_sigmoid(z):
    return jnp.minimum(z, 0.0) - jnp.log(1.0 + jnp.exp(-jnp.abs(z)))


def _ret_body(dec_ref, q_ref, k_ref, v_ref, g_ref, gnw_ref, o_ref, kt_ref, sf_ref, sb_ref, *, S):
    C = RET_CHUNK
    nc = S // C
    lg = _log_sigmoid(dec_ref[0])
    a_row = lax.broadcasted_iota(jnp.int32, (C, LANES), 0).astype(F32)
    lane = lax.broadcasted_iota(jnp.int32, (C, LANES), 1)
    rel = (lax.broadcasted_iota(jnp.int32, (C, C), 0) - lax.broadcasted_iota(jnp.int32, (C, C), 1)).astype(F32)

    heads = []
    for h in range(2):
        lgf = lg[h:h + 1, :]
        lgb = lg[2 + h:3 + h, :]
        in_head = (lane < RET_QK_DIM) if h == 0 else (lane >= RET_QK_DIM)
        heads.append(dict(
            in_head=in_head,
            xi_f=jnp.where(in_head, jnp.exp((a_row + 1.0) * lgf), 0.0),
            xi_b=jnp.where(in_head, jnp.exp((C - a_row) * lgb), 0.0),
            zeta_f=jnp.exp((C - 1.0 - a_row) * lgf),
            zeta_b=jnp.exp(a_row * lgb),
            dloc=jnp.where(rel > 0, jnp.exp(rel * lgf), jnp.where(rel < 0, jnp.exp(-rel * lgb), 2.0)),
            cd_f=jnp.exp(C * lgf),
            cd_b=jnp.exp(C * lgb),
        ))

    def chunk_rows(n):
        return pl.ds(pl.multiple_of(n * C, C), C)

    def v_head(rows, h):
        return v_ref[0, rows, h * RET_V_DIM:(h + 1) * RET_V_DIM]

    def products(i, carry):
        items = []
        for u in range(RET_UNROLL):
            n = i * RET_UNROLL + u
            rows = chunk_rows(n)
            kt = k_ref[0, rows, :].astype(F32).T.astype(BF16)
            kt_ref[n] = kt
            for h, hd in enumerate(heads):
                vh = v_head(rows, h).astype(F32)
                items.append((n, h, kt, (vh * hd["zeta_f"]).astype(BF16), (vh * hd["zeta_b"]).astype(BF16)))
        outs = [(n, h, jnp.dot(kt, vf, preferred_element_type=F32), jnp.dot(kt, vb, preferred_element_type=F32))
                for n, h, kt, vf, vb in items]
        for n, h, f, b in outs:
            sf_ref[n, h] = f
            sb_ref[n, h] = b
        return carry

    lax.fori_loop(0, nc // RET_UNROLL, products, 0)

    def scan(ref, key, order):
        def step(i, state):
            n = order(i)
            new = []
            for h, hd in enumerate(heads):
                kv = ref[n, h]
                ref[n, h] = state[h]
                new.append(hd[key] * state[h] + kv)
            return tuple(new)
        zero = jnp.zeros(ref.shape[2:], F32)
        lax.fori_loop(0, nc, step, (zero, zero))

    scan(sf_ref, "cd_f", lambda i: i)
    scan(sb_ref, "cd_b", lambda i: nc - 1 - i)

    def outputs(i, carry):
        items = []
        for u in range(RET_UNROLL):
            n = i * RET_UNROLL + u
            rows = chunk_rows(n)
            qp = q_ref[0, rows, :]
            qf = qp.astype(F32)
            kt = kt_ref[n]
            for h, hd in enumerate(heads):
                qm = jnp.where(hd["in_head"], qp, jnp.zeros_like(qp))
                qx = jnp.concatenate([(qf * hd["xi_f"]).astype(BF16), (qf * hd["xi_b"]).astype(BF16)], axis=1)
                st = jnp.concatenate([sf_ref[n, h].astype(BF16), sb_ref[n, h].astype(BF16)], axis=0)
                items.append((rows, h, hd, jnp.dot(qm, kt, preferred_element_type=F32),
                              jnp.dot(qx, st, preferred_element_type=F32)))
        items = [(rows, h, hd, (s * hd["dloc"]).astype(BF16), cross) for rows, h, hd, s, cross in items]
        items = [(rows, h, cross + jnp.dot(p, v_head(rows, h), preferred_element_type=F32))
                 for rows, h, hd, p, cross in items]
        for rows, h, ret in items:
            mu = jnp.mean(ret, axis=-1, keepdims=True)
            xc = ret - mu
            var = jnp.mean(xc * xc, axis=-1, keepdims=True)
            cols = slice(h * RET_V_DIM, (h + 1) * RET_V_DIM)
            gate = g_ref[0, rows, cols].astype(F32)
            y = xc * lax.rsqrt(var + NORM_EPS) * gnw_ref[:, cols] * (gate * jax.nn.sigmoid(gate))
            o_ref[0, rows, cols] = y.astype(o_ref.dtype)
        return carry

    lax.fori_loop(0, nc // RET_UNROLL, outputs, 0)


def _retention(dec, qr, kr, vr, gr, gn_w, B, S):
    nc = S // RET_CHUNK
    npairs = RET_HEADS // 2
    qk_spec = pl.BlockSpec((1, S, 2 * RET_QK_DIM), lambda b, p: (b, 0, p))
    v_spec = pl.BlockSpec((1, S, 2 * RET_V_DIM), lambda b, p: (b, 0, p))
    return pl.pallas_call(
        functools.partial(_ret_body, S=S),
        grid=(B, npairs),
        in_specs=[pl.BlockSpec((1, 4, LANES), lambda b, p: (p, 0, 0)),
                  qk_spec, qk_spec, v_spec, v_spec,
                  pl.BlockSpec((1, 2 * RET_V_DIM), lambda b, p: (0, p))],
        out_specs=v_spec,
        out_shape=jax.ShapeDtypeStruct((B, S, RET_V_W), BF16),
        scratch_shapes=[pltpu.VMEM((nc, 2 * RET_QK_DIM, RET_CHUNK), BF16),
                        pltpu.VMEM((nc, 2, 2 * RET_QK_DIM, RET_V_DIM), F32),
                        pltpu.VMEM((nc, 2, 2 * RET_QK_DIM, RET_V_DIM), F32)],
        compiler_params=_cparams(("parallel", "parallel")),
        name="retention",
    )(dec, qr.reshape(B, S, RET_QK_W), kr.reshape(B, S, RET_QK_W),
      vr.reshape(B, S, RET_V_W), gr.reshape(B, S, RET_V_W), gn_w)


ROUTE_EID, ROUTE_RANK, ROUTE_GATE = 0, 2, 4
ROUTE_FIELDS = 8
ROUTER_EXPERT_LANE0 = MOE_GROUPS


def _merge_body(yatt_ref, yret_ref, gt_ref, x_ref, wa_ref, wb_ref, wo_ref,
                nw_ref, wrh_ref, wrl_ref, br_ref,
                h_ref, hn_ref, route_ref, route_t_ref, cnt_ref):
    tm = x_ref.shape[0]
    hm = tm // MERGE_SPLIT
    i = pl.program_id(0)
    cw = D_MODEL // MERGE_COL_CHUNKS

    def branch_products(rows):
        y_att = jnp.concatenate([yatt_ref[0, c, rows, :] for c in range(yatt_ref.shape[1])], axis=1)
        y_ret = yret_ref[rows, :]
        chunks = []
        for c in range(MERGE_COL_CHUNKS):
            cols = slice(c * cw, (c + 1) * cw)
            a = jnp.dot(y_att, wa_ref[:, cols], preferred_element_type=F32)
            b = jnp.dot(y_ret, wb_ref[:, cols], preferred_element_type=F32)
            g_att = gt_ref[rows, c * cw:(c + 1) * cw].astype(F32)
            g_ret = gt_ref[rows, D_MODEL + c * cw:D_MODEL + (c + 1) * cw].astype(F32)
            chunks.append((g_att * a + g_ret * b).astype(BF16))
        return jnp.concatenate(chunks, axis=1)

    def residual_norm(hf, rows, merged):
        mix = jnp.dot(merged, wo_ref[...], preferred_element_type=F32)
        h = x_ref[rows, :] + mix
        h_ref[rows, :] = h
        ms = jnp.mean(h * h, axis=-1, keepdims=True)
        hn = h * lax.rsqrt(ms + NORM_EPS) * nw_ref[...]
        for j in range(ROW_TILES):
            hn_ref[pl.ds(hf * hm * ROW_TILES + j, hm, stride=ROW_TILES), :] = hn[:, j * LANES:(j + 1) * LANES]
        hi = hn.astype(BF16)
        lo = (hn - hi.astype(F32)).astype(BF16)
        return hi, lo

    def router_logits(hi, lo):
        return (jnp.dot(hi, wrh_ref[...], preferred_element_type=F32)
                + jnp.dot(hi, wrl_ref[...], preferred_element_type=F32)
                + jnp.dot(lo, wrh_ref[...], preferred_element_type=F32)) + br_ref[...]

    lane = lax.broadcasted_iota(jnp.int32, (hm, LANES), 1)
    far = jnp.int32(4 * LANES)

    def first_argmax(vals, vmax):
        return jnp.min(jnp.where(vals == vmax, lane, far), axis=-1, keepdims=True)

    def route(logits):
        is_group = lane < MOE_GROUPS
        gl = jnp.where(is_group, logits, NEG_BIG)
        gmax = jnp.max(gl, axis=-1, keepdims=True)
        g_w = 1.0 / jnp.sum(jnp.where(is_group, jnp.exp(gl - gmax), 0.0), axis=-1, keepdims=True)
        g_idx = first_argmax(gl, gmax)
        e_lane = lane - ROUTER_EXPERT_LANE0
        in_group = (e_lane >= 0) & (e_lane < MOE_N_EXPERTS) & (jnp.right_shift(e_lane, 3) == g_idx)
        el = jnp.where(in_group, logits, NEG_BIG)
        m1 = jnp.max(el, axis=-1, keepdims=True)
        i1 = first_argmax(el, m1)
        el2 = jnp.where(lane == i1, NEG_BIG, el)
        m2 = jnp.max(el2, axis=-1, keepdims=True)
        i2 = first_argmax(el2, m2)
        ex = jnp.exp(m2 - m1)
        return i1, i2, g_w / (1.0 + ex), g_w * ex / (1.0 + ex)

    rows = [slice(hf * hm, (hf + 1) * hm) for hf in range(MERGE_SPLIT)]
    merged = [branch_products(r) for r in rows]
    split = [residual_norm(hf, r, m) for hf, (r, m) in enumerate(zip(rows, merged))]
    routed = [route(router_logits(hi, lo)) for hi, lo in split]

    @pl.when(i == 0)
    def _():
        cnt_ref[...] = jnp.zeros(cnt_ref.shape, F32)

    r_idx = lax.broadcasted_iota(jnp.int32, (hm, hm), 0)
    c_idx = lax.broadcasted_iota(jnp.int32, (hm, hm), 1)
    lower = jnp.where(c_idx < r_idx, 1.0, 0.0).astype(BF16)
    running = cnt_ref[...]
    for hf, (i1, i2, gate1, gate2) in enumerate(routed):
        hot1 = lane == i1
        hot2 = lane == i2
        onehot = jnp.where(hot1 | hot2, 1.0, 0.0)
        before = jnp.dot(lower, onehot.astype(BF16), preferred_element_type=F32) + running
        rank1 = jnp.sum(jnp.where(hot1, before, 0.0), axis=-1, keepdims=True)
        rank2 = jnp.sum(jnp.where(hot2, before, 0.0), axis=-1, keepdims=True)
        running = running + jnp.sum(onehot, axis=0, keepdims=True)
        rec = jnp.zeros((hm, LANES), F32)
        for pos, val in ((ROUTE_EID, (i1 - ROUTER_EXPERT_LANE0).astype(F32)),
                         (ROUTE_EID + 1, (i2 - ROUTER_EXPERT_LANE0).astype(F32)),
                         (ROUTE_RANK, rank1), (ROUTE_RANK + 1, rank2),
                         (ROUTE_GATE, gate1), (ROUTE_GATE + 1, gate2)):
            rec = jnp.where(lane == pos, val, rec)
        route_ref[rows[hf], :] = rec
        route_t_ref[:, hf * hm:(hf + 1) * hm] = rec.T[:route_t_ref.shape[0], :]
    cnt_ref[...] = running


def _merge_route(y_att, y_ret, gates, x2, wa, wb, wo, nw, wr_hi, wr_lo, b_r, T, S):
    tm = TM_MERGE
    nt = S // tm
    row = lambda i: (i, 0)
    const = lambda i: (0, 0)
    full = lambda arr: pl.BlockSpec(arr.shape, const)
    in_specs = ([pl.BlockSpec((1, y_att.shape[1], tm, LANES), lambda i: (i // nt, 0, i % nt, 0)),
                 pl.BlockSpec((tm, RET_V_W), row), pl.BlockSpec((tm, 2 * D_MODEL), row),
                 pl.BlockSpec((tm, D_MODEL), row),
                 full(wa), full(wb), full(wo), full(nw), full(wr_hi), full(wr_lo), full(b_r)])
    return pl.pallas_call(
        _merge_body,
        grid=(T // tm,),
        in_specs=in_specs,
        out_specs=[pl.BlockSpec((tm, D_MODEL), row),
                   pl.BlockSpec((tm * ROW_TILES, LANES), row),
                   pl.BlockSpec((tm, LANES), row), pl.BlockSpec((ROUTE_FIELDS, tm), lambda i: (0, i)),
                   pl.BlockSpec((1, LANES), const)],
        out_shape=[jax.ShapeDtypeStruct((T, D_MODEL), F32), jax.ShapeDtypeStruct((T * ROW_TILES, LANES), F32),
                   jax.ShapeDtypeStruct((T, LANES), F32), jax.ShapeDtypeStruct((ROUTE_FIELDS, T), F32),
                   jax.ShapeDtypeStruct((1, LANES), F32)],
        compiler_params=_cparams(("arbitrary",)),
        name="merge_route",
    )(y_att, y_ret, gates, x2, wa, wb, wo, nw, wr_hi, wr_lo, b_r)


ISSUE_UNROLL = 8


def _tile_rows(n):
    return pl.ds(pl.multiple_of(n * ROW_TILES, ROW_TILES), ROW_TILES)


def _dispatch_body(slot_ref, hn_ref, xs_ref, sem, *, T):
    i = pl.program_id(0)
    ch = hn_ref.shape[0] // ROW_TILES

    def row_copy(j, slot):
        return pltpu.make_async_copy(hn_ref.at[_tile_rows(j)], xs_ref.at[_tile_rows(slot)], sem)

    def issue(j, carry):
        t = i * ch + j
        row_copy(j, slot_ref[t]).start(priority=0)
        row_copy(j, slot_ref[T + t]).start(priority=1)
        return carry

    lax.fori_loop(0, ch, issue, 0, unroll=ISSUE_UNROLL)
    for _ in range(2):
        pltpu.make_async_copy(hn_ref, xs_ref.at[pl.ds(0, ch * ROW_TILES)], sem).wait()


def _dispatch(slots, hn, n_slots, T):
    ch = DISPATCH_CHUNK
    grid_spec = pltpu.PrefetchScalarGridSpec(
        num_scalar_prefetch=1,
        grid=(T // ch,),
        in_specs=[pl.BlockSpec((ch * ROW_TILES, LANES), lambda i, s: (i, 0))],
        out_specs=pl.BlockSpec(memory_space=pl.ANY),
        scratch_shapes=[pltpu.SemaphoreType.DMA(())],
    )
    return pl.pallas_call(
        functools.partial(_dispatch_body, T=T),
        grid_spec=grid_spec,
        out_shape=jax.ShapeDtypeStruct((n_slots * ROW_TILES, LANES), F32),
        compiler_params=_cparams(("arbitrary",)),
        name="moe_dispatch",
    )(slots, hn)


def _expert_body(blk_ref, eid_ref, valid_ref, fresh_ref, next_ref, x_ref, w1_ref, w3_ref, w2_ref, y_ref,
                 w1b, w3b, w2b, w1s, w3s, w2s, wsem):
    i = pl.program_id(0)
    valid = valid_ref[i]

    def weight_copies(e):
        return (pltpu.make_async_copy(w1_ref.at[e], w1s, wsem.at[0]),
                pltpu.make_async_copy(w3_ref.at[e], w3s, wsem.at[1]),
                pltpu.make_async_copy(w2_ref.at[e], w2s, wsem.at[2]))

    @pl.when(i == 0)
    def _():
        for cp in weight_copies(eid_ref[0]):
            cp.start()

    @pl.when(valid > 0)
    def _():
        @pl.when(fresh_ref[i] == 1)
        def _():
            for cp in weight_copies(eid_ref[i]):
                cp.wait()
            w1b[...] = w1s[...].astype(BF16)
            w3b[...] = w3s[...].astype(BF16)
            w2b[...] = w2s[...].astype(BF16)

            @pl.when(next_ref[i] >= 0)
            def _():
                for cp in weight_copies(next_ref[i]):
                    cp.start()

        bm = x_ref.shape[0] // ROW_TILES
        live =lax.broadcasted_iota(jnp.int32, (bm, LANES), 0) < valid
        x = jnp.concatenate(
            [jnp.where(live, x_ref[pl.ds(j, bm, stride=ROW_TILES), :], 0.0).astype(BF16)
             for j in range(ROW_TILES)], axis=1)
        a = jnp.dot(x, w1b[...], preferred_element_type=F32)
        b = jnp.dot(x, w3b[...], preferred_element_type=F32)
        hid = (a * jax.nn.sigmoid(a) * b).astype(BF16)
        y = jnp.dot(hid, w2b[...], preferred_element_type=F32)
        for j in range(ROW_TILES):
            y_ref[pl.ds(j, bm, stride=ROW_TILES), :] = y[:, j * LANES:(j + 1) * LANES]


def _experts(blk, blk_eid, blk_valid, blk_fresh, blk_next, x_slots, w1, w3, w2, n_blocks):
    bm = MOE_BM
    slot_block = lambda i, blk, eid, val, fr, nx: (blk[i], 0)
    grid_spec = pltpu.PrefetchScalarGridSpec(
        num_scalar_prefetch=5,
        grid=(n_blocks,),
        in_specs=[pl.BlockSpec((bm * ROW_TILES, LANES), slot_block),
                  pl.BlockSpec(memory_space=pl.ANY), pl.BlockSpec(memory_space=pl.ANY),
                  pl.BlockSpec(memory_space=pl.ANY)],
        out_specs=pl.BlockSpec((bm * ROW_TILES, LANES), slot_block),
        scratch_shapes=[pltpu.VMEM((D_MODEL, MOE_HIDDEN), BF16), pltpu.VMEM((D_MODEL, MOE_HIDDEN), BF16),
                        pltpu.VMEM((MOE_HIDDEN, D_MODEL), BF16),
                        pltpu.VMEM((D_MODEL, MOE_HIDDEN), F32), pltpu.VMEM((D_MODEL, MOE_HIDDEN), F32),
                        pltpu.VMEM((MOE_HIDDEN, D_MODEL), F32), pltpu.SemaphoreType.DMA((3,))],
    )
    return pl.pallas_call(
        _expert_body,
        grid_spec=grid_spec,
        out_shape=jax.ShapeDtypeStruct(x_slots.shape, F32),
        compiler_params=_cparams(("arbitrary",)),
        name="moe_experts",
    )(blk, blk_eid, blk_valid, blk_fresh, blk_next, x_slots, w1, w3, w2)


def _combine_body(slot_ref, ys_ref, h_ref, route_ref, nw_ref, o_ref, ybuf, sem, *, T):
    i = pl.program_id(0)
    n = pl.num_programs(0)
    tm = h_ref.shape[0]

    def row_copy(slot, buf, k, j):
        return pltpu.make_async_copy(ys_ref.at[_tile_rows(slot)], ybuf.at[buf, k, _tile_rows(j)], sem.at[buf])

    def issue(tile, buf):
        def one(j, carry):
            t = tile * tm + j
            row_copy(slot_ref[t], buf, 0, j).start(priority=0)
            row_copy(slot_ref[T + t], buf, 1, j).start(priority=1)
            return carry
        lax.fori_loop(0, tm, one, 0, unroll=ISSUE_UNROLL)

    @pl.when(i == 0)
    def _():
        issue(0, 0)

    @pl.when(i + 1 < n)
    def _():
        issue(i + 1, (i + 1) % 2)

    buf = i % 2
    for k in range(2):
        pltpu.make_async_copy(ys_ref.at[pl.ds(0, tm * ROW_TILES)], ybuf.at[buf, k], sem.at[buf]).wait()
    route = route_ref[...]
    g1 = route[:, ROUTE_GATE:ROUTE_GATE + 1]
    g2 = route[:, ROUTE_GATE + 1:ROUTE_GATE + 2]
    hs = []
    ss = jnp.zeros((tm, 1), F32)
    for j in range(ROW_TILES):
        tile_row = pl.ds(j, tm, stride=ROW_TILES)
        hj = h_ref[:, j * LANES:(j + 1) * LANES] + (ybuf[buf, 0, tile_row, :] * g1 + ybuf[buf, 1, tile_row, :] * g2)
        hs.append(hj)
        ss = ss + jnp.sum(hj * hj, axis=-1, keepdims=True)
    inv = lax.rsqrt(ss * (1.0 / D_MODEL) + NORM_EPS)
    for j, hj in enumerate(hs):
        cols = slice(j * LANES, (j + 1) * LANES)
        o_ref[:, cols] = hj * inv * nw_ref[:, cols]


def _combine(slots, y_slots, h, route, nw, T):
    tm = TM_COMBINE
    row = lambda i, s: (i, 0)
    grid_spec = pltpu.PrefetchScalarGridSpec(
        num_scalar_prefetch=1,
        grid=(T // tm,),
        in_specs=[pl.BlockSpec(memory_space=pl.ANY),
                  pl.BlockSpec((tm, D_MODEL), row),
                  pl.BlockSpec((tm, LANES), row),
                  pl.BlockSpec((1, D_MODEL), lambda i, s: (0, 0))],
        out_specs=pl.BlockSpec((tm, D_MODEL), row),
        scratch_shapes=[pltpu.VMEM((2, 2, tm * ROW_TILES, LANES), F32), pltpu.SemaphoreType.DMA((2,))],
    )
    return pl.pallas_call(
        functools.partial(_combine_body, T=T),
        grid_spec=grid_spec,
        out_shape=jax.ShapeDtypeStruct((T, D_MODEL), F32),
        compiler_params=_cparams(("arbitrary",)),
        name="moe_combine",
    )(slots, y_slots, h, route, nw)


def _rotary_tables(S):
    inv_freq = (1.0 / (np.float32(ROPE_THETA) ** (np.arange(0, HEAD_DIM, 2, dtype=np.float32) / HEAD_DIM))
                ).astype(np.float32)
    ang = np.arange(S, dtype=np.float32)[:, None] * inv_freq[None, :]
    cos, sin = np.cos(ang), np.sin(ang)
    reps = LANES // HEAD_DIM
    cos_t = np.tile(np.concatenate([cos, cos], axis=1), (1, reps)).astype(np.float32)
    sin_t = np.tile(np.concatenate([-sin, sin], axis=1), (1, reps)).astype(np.float32)
    return jnp.asarray(cos_t), jnp.asarray(sin_t)


def _layer(h_in, norm_mix_w, w_in, b_branch_gate, ret_decay_fwd, ret_decay_bwd, ret_gn_w, w_attn_branch,
           w_ret_branch, w_out, norm_moe_w, moe_w_group, moe_b_group, moe_w_expert, moe_b_expert,
           moe_w1, moe_w3, moe_w2, next_norm_w, B, S, cos_t, sin_t):
    T = B * S
    (qa0, ka0, va0, qa1, ka1, va1, qa2, ka2, va2, qr, kr, vr, gr, gates) = _in_projection(
        h_in, norm_mix_w[None, :], w_in.astype(BF16), b_branch_gate[None, :], cos_t, sin_t, B, S)

    unit = lambda a: a[:, :, None]
    y_att = _attention((unit(qa0), unit(ka0), unit(va0), qa1, ka1, va1, qa2, ka2, va2), B, S)

    dec = jnp.stack([ret_decay_fwd.reshape(RET_HEADS // 2, 2), ret_decay_bwd.reshape(RET_HEADS // 2, 2)], axis=1)
    dec = jnp.broadcast_to(dec.reshape(RET_HEADS // 2, 4, 1), (RET_HEADS // 2, 4, LANES)).astype(F32)
    y_ret = _retention(dec, qr, kr, vr, gr, ret_gn_w[None, :], B, S).reshape(T, RET_V_W)

    pad = LANES - MOE_GROUPS - MOE_N_EXPERTS
    w_r = jnp.concatenate([moe_w_group, moe_w_expert, jnp.zeros((D_MODEL, pad), F32)], axis=1)
    w_r_hi = w_r.astype(BF16)
    w_r_lo = (w_r - w_r_hi.astype(F32)).astype(BF16)
    b_r = jnp.concatenate([moe_b_group, moe_b_expert, jnp.zeros((pad,), F32)])[None, :]

    h_mid, hn, route, route_t, cnt = _merge_route(
        y_att, y_ret, gates, h_in, w_attn_branch.astype(BF16), w_ret_branch.astype(BF16),
        w_out.astype(BF16), norm_moe_w[None, :], w_r_hi, w_r_lo, b_r, T, S)

    bm = MOE_BM
    counts = cnt[0, ROUTER_EXPERT_LANE0:ROUTER_EXPERT_LANE0 + MOE_N_EXPERTS].astype(jnp.int32)
    nblk = (counts + bm - 1) // bm
    blk_end = jnp.cumsum(nblk)
    pstart = (blk_end - nblk) * bm
    n_blocks = (2 * T) // bm + MOE_N_EXPERTS
    n_active = blk_end[-1]
    bidx = jnp.minimum(jnp.arange(n_blocks, dtype=jnp.int32), n_active - 1)
    blk_eid = jnp.sum(bidx[:, None] >= blk_end[None, :], axis=1).astype(jnp.int32)
    mine = blk_eid[:, None] == jnp.arange(MOE_N_EXPERTS, dtype=jnp.int32)[None, :]
    seg_end = jnp.sum(jnp.where(mine, (pstart + counts)[None, :], 0), axis=1)
    blk_valid = jnp.clip(seg_end - bidx * bm, 0, bm)
    blk_valid = jnp.where(jnp.arange(n_blocks) < n_active, blk_valid, 0).astype(jnp.int32)
    blk_fresh = jnp.concatenate([jnp.ones((1,), jnp.int32), (blk_eid[1:] != blk_eid[:-1]).astype(jnp.int32)])
    ar = jnp.arange(MOE_N_EXPERTS, dtype=jnp.int32)
    later = jnp.min(jnp.where((nblk > 0)[None, :] & (ar[None, :] > ar[:, None]), ar[None, :], MOE_N_EXPERTS), axis=1)
    later = jnp.where(later < MOE_N_EXPERTS, later, -1)
    blk_next = (jnp.sum(jnp.where(mine, later[None, :] + 1, 0), axis=1) - 1).astype(jnp.int32)
    eid = route_t[ROUTE_EID:ROUTE_EID + 2].astype(jnp.int32)
    rank = route_t[ROUTE_RANK:ROUTE_RANK + 2].astype(jnp.int32)
    start = jnp.sum(jnp.where(eid[..., None] == jnp.arange(MOE_N_EXPERTS, dtype=jnp.int32),
                              pstart.astype(jnp.int32), 0), axis=-1)
    slots = (start + rank).reshape(2 * T)

    x_slots = _dispatch(slots, hn, n_blocks * bm, T)
    y_slots = _experts(bidx, blk_eid, blk_valid, blk_fresh, blk_next, x_slots, moe_w1, moe_w3, moe_w2, n_blocks)
    return _combine(slots, y_slots, h_mid, route, next_norm_w[None, :], T)


def kernel(x, norm_mix_w, w_in, b_branch_gate, ret_decay_fwd, ret_decay_bwd, ret_gn_w, w_attn_branch,
           w_ret_branch, w_out, norm_moe_w, moe_w_group, moe_b_group, moe_w_expert, moe_b_expert, moe_w1,
           moe_w3, moe_w2, norm_final_w):
    B, S, D = x.shape
    depth = norm_mix_w.shape[0]
    assert depth == 1, "the final norm is fused into the layer's combine stage"
    assert D == D_MODEL and S % TM_INPROJ == 0 and (B * S) < (1 << 24)
    cos_t, sin_t = _rotary_tables(S)
    out = _layer(x.reshape(B * S, D), norm_mix_w[0], w_in[0], b_branch_gate[0], ret_decay_fwd[0],
                 ret_decay_bwd[0], ret_gn_w[0], w_attn_branch[0], w_ret_branch[0], w_out[0], norm_moe_w[0],
                 moe_w_group[0], moe_b_group[0], moe_w_expert[0], moe_b_expert[0], moe_w1[0], moe_w3[0],
                 moe_w2[0], norm_final_w, B, S, cos_t, sin_t)
    return out.reshape(B, S, D)
```

```python
import functools

import numpy as np
import jax
import jax.numpy as jnp
from jax import lax
from jax.experimental import pallas as pl
from jax.experimental.pallas import tpu as pltpu

F32 = jnp.float32
BF16 = jnp.bfloat16

D_MODEL = 1024
HEAD_DIM = 64
ATTN_PAIRS = ((128, 1), (512, 4), (2048, 16))
ATTN_HEADS_PER_GROUP = 8
ATTN_GROUP_W = ATTN_HEADS_PER_GROUP * HEAD_DIM
ATTN_HALF = 64
ROPE_THETA = 10000.0
RET_HEADS = 8
RET_QK_DIM = 64
RET_V_DIM = 128
RET_CHUNK = 128
RET_QK_W = RET_HEADS * RET_QK_DIM
RET_V_W = RET_HEADS * RET_V_DIM
MOE_GROUPS = 8
MOE_EXPERTS_PER_GROUP = 8
MOE_N_EXPERTS = MOE_GROUPS * MOE_EXPERTS_PER_GROUP
MOE_HIDDEN = 512
NORM_EPS = 1e-6

LANES = 128
ROW_TILES = D_MODEL // LANES
NEG_BIG = -1e30
LOG2_E = 1.4426950408889634

TM_INPROJ = 512
TM_MERGE = 1024
MERGE_SPLIT = 4
MERGE_COL_CHUNKS = 4
TM_COMBINE = 512
MOE_BM = 512
DISPATCH_CHUNK = 4096
ATTN_QB = 128
ATTN_UNROLL = 8
RET_UNROLL = 8

VMEM_LIMIT = 56 * 1024 * 1024

_A = 3 * ATTN_GROUP_W
OFF_QA, OFF_KA, OFF_VA = 0, _A, 2 * _A
OFF_QR = 3 * _A
OFF_KR = OFF_QR + RET_QK_W
OFF_VR = OFF_KR + RET_QK_W
OFF_GR = OFF_VR + RET_V_W
OFF_GL = OFF_GR + RET_V_W
IN_W = OFF_GL + 2 * D_MODEL


def _cparams(sem, vmem=VMEM_LIMIT):
    return pltpu.CompilerParams(dimension_semantics=sem, vmem_limit_bytes=vmem)


def _inproj_body(x_ref, nw_ref, w_ref, bg_ref, cos_ref, sin_ref,
                 qa0, ka0, va0, qa1, ka1, va1, qa2, ka2, va2, qr, kr, vr, gr, gt,
                 stage_ref):
    tm = x_ref.shape[0]
    x = x_ref[...]
    ms = jnp.mean(x * x, axis=-1, keepdims=True)
    xn = (x * lax.rsqrt(ms + NORM_EPS) * nw_ref[...]).astype(BF16)
    cos = cos_ref[...]
    sin = sin_ref[...]
    lane = lax.broadcasted_iota(jnp.int32, (tm, LANES), 1)
    first_half = (lane & (HEAD_DIM - 1)) < (HEAD_DIM // 2)

    def proj(c0, width):
        return jnp.dot(xn, w_ref[:, c0:c0 + width], preferred_element_type=F32)

    def rotary(a, scale):
        partner = jnp.where(first_half, pltpu.roll(a, LANES - HEAD_DIM // 2, 1),
                            pltpu.roll(a, HEAD_DIM // 2, 1))
        r = a * cos + partner * sin
        return r * scale if scale != 1.0 else r

    def chunks(acc):
        return [acc[:, c * LANES:(c + 1) * LANES] for c in range(acc.shape[1] // LANES)]

    def store_natural(out_ref, acc, fn):
        for c, a in enumerate(chunks(acc)):
            out_ref[:, c * LANES:(c + 1) * LANES] = fn(a).astype(out_ref.dtype)

    def store_pairs(out_ref, acc, fn):
        for c, a in enumerate(chunks(acc)):
            out_ref[0, c] = fn(a).astype(out_ref.dtype)

    def store_strided(out_ref, acc, fn, d):
        for c, a in enumerate(chunks(acc)):
            stage_ref[c] = fn(a)
        for c in range(acc.shape[1] // LANES):
            for r in range(d):
                out_ref[0, c, r] = stage_ref[c, pl.ds(r, tm // d, stride=d), :].astype(out_ref.dtype)

    ident = lambda a: a
    rot_q = lambda a: rotary(a, HEAD_DIM ** -0.5 * LOG2_E)
    rot_1 = lambda a: rotary(a, 1.0)
    rot_k = lambda a: rotary(a, RET_QK_DIM ** -0.5)

    W = ATTN_GROUP_W
    store_pairs(qa0, proj(OFF_QA, W), rot_q)
    store_pairs(ka0, proj(OFF_KA, W), rot_1)
    store_pairs(va0, proj(OFF_VA, W), ident)
    for g, (qo, ko, vo) in ((1, (qa1, ka1, va1)), (2, (qa2, ka2, va2))):
        d = ATTN_PAIRS[g][1]
        store_strided(qo, proj(OFF_QA + g * W, W), rot_q, d)
        store_strided(ko, proj(OFF_KA + g * W, W), rot_1, d)
        store_strided(vo, proj(OFF_VA + g * W, W), ident, d)
    store_natural(qr, proj(OFF_QR, RET_QK_W), rot_1)
    store_natural(kr, proj(OFF_KR, RET_QK_W), rot_k)
    for h in range(RET_V_W // W):
        vr[:, h * W:(h + 1) * W] = proj(OFF_VR + h * W, W).astype(vr.dtype)
        gr[:, h * W:(h + 1) * W] = proj(OFF_GR + h * W, W).astype(gr.dtype)
    for h in range(2 * D_MODEL // W):
        z = proj(OFF_GL + h * W, W) + bg_ref[:, h * W:(h + 1) * W]
        gt[:, h * W:(h + 1) * W] = jax.nn.sigmoid(z).astype(gt.dtype)


def _in_projection(x2, norm_w, w_bf, b_gate, cos_t, sin_t, B, S):
    T = B * S
    tm = TM_INPROJ
    nt = S // tm
    W = ATTN_GROUP_W
    row = lambda i: (i, 0)
    const = lambda i: (0, 0)
    nat = lambda width: pl.BlockSpec((tm, width), row)

    P = W // LANES

    def strided_spec(d):
        return pl.BlockSpec((1, P, d, tm // d, LANES), lambda i: (i // nt, 0, 0, i % nt, 0))

    def strided_shape(d):
        return jax.ShapeDtypeStruct((B, P, d, S // d, LANES), BF16)

    pair_spec = pl.BlockSpec((1, P, tm, LANES), lambda i: (i // nt, 0, i % nt, 0))
    pair_shape = jax.ShapeDtypeStruct((B, P, S, LANES), BF16)
    nat_shape = lambda width: jax.ShapeDtypeStruct((T, width), BF16)
    d1, d2 = ATTN_PAIRS[1][1], ATTN_PAIRS[2][1]
    out_shape = ([pair_shape] * 3 + [strided_shape(d1)] * 3 + [strided_shape(d2)] * 3
                 + [nat_shape(RET_QK_W)] * 2 + [nat_shape(RET_V_W)] * 2 + [nat_shape(2 * D_MODEL)])
    out_specs = ([pair_spec] * 3 + [strided_spec(d1)] * 3 + [strided_spec(d2)] * 3
                 + [nat(RET_QK_W)] * 2 + [nat(RET_V_W)] * 2 + [nat(2 * D_MODEL)])
    in_specs = [
        pl.BlockSpec((tm, D_MODEL), row),
        pl.BlockSpec((1, D_MODEL), const),
        pl.BlockSpec((D_MODEL, IN_W), const, pipeline_mode=pl.Buffered(1)),
        pl.BlockSpec((1, 2 * D_MODEL), const),
        pl.BlockSpec((tm, LANES), lambda i: (i % nt, 0)),
        pl.BlockSpec((tm, LANES), lambda i: (i % nt, 0)),
    ]
    return pl.pallas_call(
        _inproj_body,
        grid=(T // tm,),
        in_specs=in_specs,
        out_specs=out_specs,
        out_shape=out_shape,
        scratch_shapes=[pltpu.VMEM((W // LANES, tm, LANES), F32)],
        compiler_params=_cparams(("parallel",)),
        name="in_projection",
    )(x2, norm_w, w_bf, b_gate, cos_t, sin_t)


def _attn_body(q0_ref, k0_ref, v0_ref, q1_ref, k1_ref, v1_ref, q2_ref, k2_ref, v2_ref, o_ref,
               va_ref, vb_ref, part_ref, bias_ref, *, S):
    QB, H = ATTN_QB, ATTN_HALF
    lane = lax.broadcasted_iota(jnp.int32, (QB, LANES), 1)
    head0 = lane < HEAD_DIM
    ones = jnp.ones((QB, LANES), BF16)

    qi = lax.broadcasted_iota(jnp.int32, (QB, QB + 2 * H), 0)
    ki = lax.broadcasted_iota(jnp.int32, (QB, QB + 2 * H), 1)
    for n in range(3):
        bias_ref[n] = jnp.where(jnp.abs(ki - qi - n * H) <= H, 0.0, NEG_BIG).astype(F32)

    def scores(q_rows, k_rows, bias, h):
        qm = jnp.where(head0 if h == 0 else jnp.logical_not(head0), q_rows, jnp.zeros_like(q_rows))
        return lax.dot_general(qm, k_rows, (((1,), (1,)), ((), ())), preferred_element_type=F32) + bias

    def weights(s):
        m = jnp.max(s, axis=-1, keepdims=True)
        return m, jnp.exp2(s - m).astype(BF16)

    def heads_to_lanes(m0, a, m1, b):
        num = jnp.where(head0, a, b)
        den = pltpu.roll(jnp.where(head0, b, a), HEAD_DIM, 1)
        mx = jnp.where(head0, m0, m1)
        return num, den, mx

    def run_group(q_ref, k_ref, v_ref, d, prepare, store):
        L = S // d
        KW = min(L, QB + 2 * H)
        nb = L // QB

        def fill(t, carry):
            r = t // nb
            rows = pl.ds(pl.multiple_of((t % nb) * QB, QB), QB)
            v = v_ref[0, 0, r, rows, :]
            dst = pl.ds(pl.multiple_of(t * QB, QB), QB)
            va_ref[dst, :] = jnp.where(head0, v, ones)
            vb_ref[dst, :] = jnp.where(head0, ones, v)
            return carry

        lax.fori_loop(0, S // QB, fill, 0, unroll=4)

        def trip(i, carry):
            blocks = []
            for u in range(ATTN_UNROLL):
                t = i * ATTN_UNROLL + u
                r = t // nb
                q0 = pl.multiple_of((t % nb) * QB, QB)
                ws = pl.multiple_of(jnp.clip(q0 - H, 0, L - KW), H)
                bias = bias_ref[(q0 - ws) // H][:, :KW]
                q_rows = q_ref[0, 0, r, pl.ds(q0, QB), :]
                k_rows = k_ref[0, 0, r, pl.ds(ws, KW), :]
                vrows = pl.ds(pl.multiple_of(r * L + ws, H), KW)
                blocks.append((r, q0, vrows, [scores(q_rows, k_rows, bias, h) for h in range(2)]))
            blocks = [(r, q0, vrows, [weights(s) for s in ss]) for r, q0, vrows, ss in blocks]
            done = []
            for r, q0, vrows, ((m0, p0), (m1, p1)) in blocks:
                a = jnp.dot(p0, va_ref[vrows, :], preferred_element_type=F32)
                b = jnp.dot(p1, vb_ref[vrows, :], preferred_element_type=F32)
                done.append((r, q0, m0, a, m1, b))
            done = [(r, q0, prepare(q0, *heads_to_lanes(m0, a, m1, b))) for r, q0, m0, a, m1, b in done]
            for r, q0, vals in done:
                store(r, q0, vals)
            return carry

        lax.fori_loop(0, S // QB // ATTN_UNROLL, trip, 0)

    def store_partial(g):
        d = ATTN_PAIRS[g][1]

        def store(r, q0, vals):
            rows = pl.ds(r + q0 * d, QB, stride=d)
            for n, val in enumerate(vals):
                part_ref[3 * (g - 1) + n, rows, :] = val
        return store

    keep = lambda q0, num, den, mx: (num, den, mx)
    run_group(q1_ref, k1_ref, v1_ref, ATTN_PAIRS[1][1], keep, store_partial(1))
    run_group(q2_ref, k2_ref, v2_ref, ATTN_PAIRS[2][1], keep, store_partial(2))

    def merge(q0, num, den, mx):
        rows = pl.ds(q0, QB)
        nums = [num, part_ref[0, rows, :], part_ref[3, rows, :]]
        dens = [den, part_ref[1, rows, :], part_ref[4, rows, :]]
        mxs = [mx, part_ref[2, rows, :], part_ref[5, rows, :]]
        top = jnp.maximum(jnp.maximum(mxs[0], mxs[1]), mxs[2])
        ws = [jnp.exp2(m - top) for m in mxs]
        n = ws[0] * nums[0] + ws[1] * nums[1] + ws[2] * nums[2]
        dn = ws[0] * dens[0] + ws[1] * dens[1] + ws[2] * dens[2]
        return (n / dn).astype(o_ref.dtype)

    def store_out(r, q0, y):
        o_ref[0, 0, pl.ds(q0, QB), :] = y

    run_group(q0_ref, k0_ref, v0_ref, ATTN_PAIRS[0][1], merge, store_out)


def _attention(qkv, B, S):
    P = ATTN_GROUP_W // LANES
    in_specs = []
    for _, d in ATTN_PAIRS:
        in_specs += [pl.BlockSpec((1, 1, d, S // d, LANES), lambda b, p: (b, p, 0, 0, 0))] * 3
    return pl.pallas_call(
        functools.partial(_attn_body, S=S),
        grid=(B, P),
        in_specs=in_specs,
        out_specs=pl.BlockSpec((1, 1, S, LANES), lambda b, p: (b, p, 0, 0)),
        out_shape=jax.ShapeDtypeStruct((B, P, S, LANES), BF16),
        scratch_shapes=[pltpu.VMEM((S, LANES), BF16), pltpu.VMEM((S, LANES), BF16),
                        pltpu.VMEM((6, S, LANES), F32),
                        pltpu.VMEM((3, ATTN_QB, ATTN_QB + 2 * ATTN_HALF), F32)],
        compiler_params=_cparams(("parallel", "parallel")),
        name="attention",
    )(*qkv)


def _log_sigmoid(z):
    return jnp.minimum(z, 0.0) - jnp.log(1.0 + jnp.exp(-jnp.abs(z)))


def _ret_body(dec_ref, q_ref, k_ref, v_ref, g_ref, gnw_ref, o_ref, kt_ref, sf_ref, sb_ref, st_ref, *, S):
    C = RET_CHUNK
    nc = S // C
    lg = _log_sigmoid(dec_ref[0])
    a_row = lax.broadcasted_iota(jnp.int32, (C, LANES), 0).astype(F32)
    lane = lax.broadcasted_iota(jnp.int32, (C, LANES), 1)
    rel = (lax.broadcasted_iota(jnp.int32, (C, C), 0) - lax.broadcasted_iota(jnp.int32, (C, C), 1)).astype(F32)

    heads = []
    for h in range(2):
        lgf = lg[h:h + 1, :]
        lgb = lg[2 + h:3 + h, :]
        in_head = (lane < RET_QK_DIM) if h == 0 else (lane >= RET_QK_DIM)
        heads.append(dict(
            in_head=in_head,
            xi_f=jnp.where(in_head, jnp.exp((a_row + 1.0) * lgf), 0.0),
            xi_b=jnp.where(in_head, jnp.exp((C - a_row) * lgb), 0.0),
            zeta_f=jnp.exp((C - 1.0 - a_row) * lgf),
            zeta_b=jnp.exp(a_row * lgb),
            dloc=jnp.where(rel > 0, jnp.exp(rel * lgf), jnp.where(rel < 0, jnp.exp(-rel * lgb), 2.0)),
            cd_f=jnp.exp(C * lgf),
            cd_b=jnp.exp(C * lgb),
        ))

    def chunk_rows(n):
        return pl.ds(pl.multiple_of(n * C, C), C)

    def v_head(rows, h):
        return v_ref[0, rows, h * RET_V_DIM:(h + 1) * RET_V_DIM]

    def products(i, carry):
        items = []
        for u in range(RET_UNROLL):
            n = i * RET_UNROLL + u
            rows = chunk_rows(n)
            kt = k_ref[0, rows, :].astype(F32).T.astype(BF16)
            kt_ref[n] = kt
            for h, hd in enumerate(heads):
                vh = v_head(rows, h).astype(F32)
                items.append((n, h, kt, (vh * hd["zeta_f"]).astype(BF16), (vh * hd["zeta_b"]).astype(BF16)))
        outs = [(n, h, jnp.dot(kt, vf, preferred_element_type=F32), jnp.dot(kt, vb, preferred_element_type=F32))
                for n, h, kt, vf, vb in items]
        for n, h, f, b in outs:
            sf_ref[n, h] = f
            sb_ref[n, h] = b
        return carry

    lax.fori_loop(0, nc // RET_UNROLL, products, 0)

    def scan(ref, key, order, half):
        rows = slice(half * 2 * RET_QK_DIM, (half + 1) * 2 * RET_QK_DIM)

        def step(i, state):
            n = order(i)
            new = []
            for h, hd in enumerate(heads):
                st_ref[n, h, rows, :] = state[h].astype(BF16)
                new.append(hd[key] * state[h] + ref[n, h])
            return tuple(new)
        zero = jnp.zeros(ref.shape[2:], F32)
        lax.fori_loop(0, nc, step, (zero, zero))

    scan(sf_ref, "cd_f", lambda i: i, 0)
    scan(sb_ref, "cd_b", lambda i: nc - 1 - i, 1)

    def outputs(i, carry):
        items = []
        for u in range(RET_UNROLL):
            n = i * RET_UNROLL + u
            rows = chunk_rows(n)
            qp = q_ref[0, rows, :]
            qf = qp.astype(F32)
            kt = kt_ref[n]
            for h, hd in enumerate(heads):
                qm = jnp.where(hd["in_head"], qp, jnp.zeros_like(qp))
                qx = jnp.concatenate([(qf * hd["xi_f"]).astype(BF16), (qf * hd["xi_b"]).astype(BF16)], axis=1)
                items.append((rows, h, hd, jnp.dot(qm, kt, preferred_element_type=F32),
                              jnp.dot(qx, st_ref[n, h], preferred_element_type=F32)))
        items = [(rows, h, hd, (s * hd["dloc"]).astype(BF16), cross) for rows, h, hd, s, cross in items]
        items = [(rows, h, cross + jnp.dot(p, v_head(rows, h), preferred_element_type=F32))
                 for rows, h, hd, p, cross in items]
        for rows, h, ret in items:
            mu = jnp.mean(ret, axis=-1, keepdims=True)
            xc = ret - mu
            var = jnp.mean(xc * xc, axis=-1, keepdims=True)
            cols = slice(h * RET_V_DIM, (h + 1) * RET_V_DIM)
            gate = g_ref[0, rows, cols].astype(F32)
            y = xc * lax.rsqrt(var + NORM_EPS) * gnw_ref[:, cols] * (gate * jax.nn.sigmoid(gate))
            o_ref[0, rows, cols] = y.astype(o_ref.dtype)
        return carry

    lax.fori_loop(0, nc // RET_UNROLL, outputs, 0)


def _retention(dec, qr, kr, vr, gr, gn_w, B, S):
    nc = S // RET_CHUNK
    npairs = RET_HEADS // 2
    qk_spec = pl.BlockSpec((1, S, 2 * RET_QK_DIM), lambda b, p: (b, 0, p))
    v_spec = pl.BlockSpec((1, S, 2 * RET_V_DIM), lambda b, p: (b, 0, p))
    return pl.pallas_call(
        functools.partial(_ret_body, S=S),
        grid=(B, npairs),
        in_specs=[pl.BlockSpec((1, 4, LANES), lambda b, p: (p, 0, 0)),
                  qk_spec, qk_spec, v_spec, v_spec,
                  pl.BlockSpec((1, 2 * RET_V_DIM), lambda b, p: (0, p))],
        out_specs=v_spec,
        out_shape=jax.ShapeDtypeStruct((B, S, RET_V_W), BF16),
        scratch_shapes=[pltpu.VMEM((nc, 2 * RET_QK_DIM, RET_CHUNK), BF16),
                        pltpu.VMEM((nc, 2, 2 * RET_QK_DIM, RET_V_DIM), F32),
                        pltpu.VMEM((nc, 2, 2 * RET_QK_DIM, RET_V_DIM), F32),
                        pltpu.VMEM((nc, 2, 4 * RET_QK_DIM, RET_V_DIM), BF16)],
        compiler_params=_cparams(("parallel", "parallel")),
        name="retention",
    )(dec, qr.reshape(B, S, RET_QK_W), kr.reshape(B, S, RET_QK_W),
      vr.reshape(B, S, RET_V_W), gr.reshape(B, S, RET_V_W), gn_w)


ROUTE_EID, ROUTE_RANK, ROUTE_GATE = 0, 2, 4
ROUTE_FIELDS = 8
ROUTER_EXPERT_LANE0 = MOE_GROUPS


def _merge_body(yatt_ref, yret_ref, gt_ref, x_ref, wa_ref, wb_ref, wo_ref,
                nw_ref, wrh_ref, wrl_ref, br_ref,
                h_ref, hn_ref, route_ref, route_t_ref, cnt_ref):
    tm = x_ref.shape[0]
    hm = tm // MERGE_SPLIT
    i = pl.program_id(0)
    cw = D_MODEL // MERGE_COL_CHUNKS

    def branch_products(rows):
        y_att = jnp.concatenate([yatt_ref[0, c, rows, :] for c in range(yatt_ref.shape[1])], axis=1)
        y_ret = yret_ref[rows, :]
        chunks = []
        for c in range(MERGE_COL_CHUNKS):
            cols = slice(c * cw, (c + 1) * cw)
            a = jnp.dot(y_att, wa_ref[:, cols], preferred_element_type=F32)
            b = jnp.dot(y_ret, wb_ref[:, cols], preferred_element_type=F32)
            g_att = gt_ref[rows, c * cw:(c + 1) * cw].astype(F32)
            g_ret = gt_ref[rows, D_MODEL + c * cw:D_MODEL + (c + 1) * cw].astype(F32)
            chunks.append((g_att * a + g_ret * b).astype(BF16))
        return jnp.concatenate(chunks, axis=1)

    def residual_norm(hf, rows, merged):
        mix = jnp.dot(merged, wo_ref[...], preferred_element_type=F32)
        h = x_ref[rows, :] + mix
        h_ref[rows, :] = h
        ms = jnp.mean(h * h, axis=-1, keepdims=True)
        hn = h * lax.rsqrt(ms + NORM_EPS) * nw_ref[...]
        for j in range(ROW_TILES):
            hn_ref[pl.ds(hf * hm * ROW_TILES + j, hm, stride=ROW_TILES), :] = hn[:, j * LANES:(j + 1) * LANES]
        hi = hn.astype(BF16)
        lo = (hn - hi.astype(F32)).astype(BF16)
        return hi, lo

    def router_logits(hi, lo):
        return (jnp.dot(hi, wrh_ref[...], preferred_element_type=F32)
                + jnp.dot(hi, wrl_ref[...], preferred_element_type=F32)
                + jnp.dot(lo, wrh_ref[...], preferred_element_type=F32)) + br_ref[...]

    lane = lax.broadcasted_iota(jnp.int32, (hm, LANES), 1)
    far = jnp.int32(4 * LANES)

    def first_argmax(vals, vmax):
        return jnp.min(jnp.where(vals == vmax, lane, far), axis=-1, keepdims=True)

    def route(logits):
        is_group = lane < MOE_GROUPS
        gl = jnp.where(is_group, logits, NEG_BIG)
        gmax = jnp.max(gl, axis=-1, keepdims=True)
        g_w = 1.0 / jnp.sum(jnp.where(is_group, jnp.exp(gl - gmax), 0.0), axis=-1, keepdims=True)
        g_idx = first_argmax(gl, gmax)
        e_lane = lane - ROUTER_EXPERT_LANE0
        in_group = (e_lane >= 0) & (e_lane < MOE_N_EXPERTS) & (jnp.right_shift(e_lane, 3) == g_idx)
        el = jnp.where(in_group, logits, NEG_BIG)
        m1 = jnp.max(el, axis=-1, keepdims=True)
        i1 = first_argmax(el, m1)
        el2 = jnp.where(lane == i1, NEG_BIG, el)
        m2 = jnp.max(el2, axis=-1, keepdims=True)
        i2 = first_argmax(el2, m2)
        ex = jnp.exp(m2 - m1)
        return i1, i2, g_w / (1.0 + ex), g_w * ex / (1.0 + ex)

    rows = [slice(hf * hm, (hf + 1) * hm) for hf in range(MERGE_SPLIT)]
    merged = [branch_products(r) for r in rows]
    split = [residual_norm(hf, r, m) for hf, (r, m) in enumerate(zip(rows, merged))]
    routed = [route(router_logits(hi, lo)) for hi, lo in split]

    @pl.when(i == 0)
    def _():
        cnt_ref[...] = jnp.zeros(cnt_ref.shape, F32)

    r_idx = lax.broadcasted_iota(jnp.int32, (hm, hm), 0)
    c_idx = lax.broadcasted_iota(jnp.int32, (hm, hm), 1)
    lower = jnp.where(c_idx < r_idx, 1.0, 0.0).astype(BF16)
    running = cnt_ref[...]
    for hf, (i1, i2, gate1, gate2) in enumerate(routed):
        hot1 = lane == i1
        hot2 = lane == i2
        onehot = jnp.where(hot1 | hot2, 1.0, 0.0)
        before = jnp.dot(lower, onehot.astype(BF16), preferred_element_type=F32) + running
        rank1 = jnp.sum(jnp.where(hot1, before, 0.0), axis=-1, keepdims=True)
        rank2 = jnp.sum(jnp.where(hot2, before, 0.0), axis=-1, keepdims=True)
        running = running + jnp.sum(onehot, axis=0, keepdims=True)
        rec = jnp.zeros((hm, LANES), F32)
        for pos, val in ((ROUTE_EID, (i1 - ROUTER_EXPERT_LANE0).astype(F32)),
                         (ROUTE_EID + 1, (i2 - ROUTER_EXPERT_LANE0).astype(F32)),
                         (ROUTE_RANK, rank1), (ROUTE_RANK + 1, rank2),
                         (ROUTE_GATE, gate1), (ROUTE_GATE + 1, gate2)):
            rec = jnp.where(lane == pos, val, rec)
        route_ref[rows[hf], :] = rec
        route_t_ref[:, hf * hm:(hf + 1) * hm] = rec.T[:route_t_ref.shape[0], :]
    cnt_ref[...] = running


def _merge_route(y_att, y_ret, gates, x2, wa, wb, wo, nw, wr_hi, wr_lo, b_r, T, S):
    tm = TM_MERGE
    nt = S // tm
    row = lambda i: (i, 0)
    const = lambda i: (0, 0)
    full = lambda arr: pl.BlockSpec(arr.shape, const)
    in_specs = ([pl.BlockSpec((1, y_att.shape[1], tm, LANES), lambda i: (i // nt, 0, i % nt, 0)),
                 pl.BlockSpec((tm, RET_V_W), row), pl.BlockSpec((tm, 2 * D_MODEL), row),
                 pl.BlockSpec((tm, D_MODEL), row),
                 full(wa), full(wb), full(wo), full(nw), full(wr_hi), full(wr_lo), full(b_r)])
    return pl.pallas_call(
        _merge_body,
        grid=(T // tm,),
        in_specs=in_specs,
        out_specs=[pl.BlockSpec((tm, D_MODEL), row),
                   pl.BlockSpec((tm * ROW_TILES, LANES), row),
                   pl.BlockSpec((tm, LANES), row), pl.BlockSpec((ROUTE_FIELDS, tm), lambda i: (0, i)),
                   pl.BlockSpec((1, LANES), const)],
        out_shape=[jax.ShapeDtypeStruct((T, D_MODEL), F32), jax.ShapeDtypeStruct((T * ROW_TILES, LANES), F32),
                   jax.ShapeDtypeStruct((T, LANES), F32), jax.ShapeDtypeStruct((ROUTE_FIELDS, T), F32),
                   jax.ShapeDtypeStruct((1, LANES), F32)],
        compiler_params=_cparams(("arbitrary",)),
        name="merge_route",
    )(y_att, y_ret, gates, x2, wa, wb, wo, nw, wr_hi, wr_lo, b_r)


ISSUE_UNROLL = 8


def _tile_rows(n):
    return pl.ds(pl.multiple_of(n * ROW_TILES, ROW_TILES), ROW_TILES)


def _dispatch_body(slot_ref, hn_ref, xs_ref, sem, *, T):
    i = pl.program_id(0)
    ch = hn_ref.shape[0] // ROW_TILES

    def row_copy(j, slot):
        return pltpu.make_async_copy(hn_ref.at[_tile_rows(j)], xs_ref.at[_tile_rows(slot)], sem)

    def issue(j, carry):
        t = i * ch + j
        row_copy(j, slot_ref[t]).start(priority=0)
        row_copy(j, slot_ref[T + t]).start(priority=1)
        return carry

    lax.fori_loop(0, ch, issue, 0, unroll=ISSUE_UNROLL)
    for _ in range(2):
        pltpu.make_async_copy(hn_ref, xs_ref.at[pl.ds(0, ch * ROW_TILES)], sem).wait()


def _dispatch(slots, hn, n_slots, T):
    ch = DISPATCH_CHUNK
    grid_spec = pltpu.PrefetchScalarGridSpec(
        num_scalar_prefetch=1,
        grid=(T // ch,),
        in_specs=[pl.BlockSpec((ch * ROW_TILES, LANES), lambda i, s: (i, 0))],
        out_specs=pl.BlockSpec(memory_space=pl.ANY),
        scratch_shapes=[pltpu.SemaphoreType.DMA(())],
    )
    return pl.pallas_call(
        functools.partial(_dispatch_body, T=T),
        grid_spec=grid_spec,
        out_shape=jax.ShapeDtypeStruct((n_slots * ROW_TILES, LANES), F32),
        compiler_params=_cparams(("arbitrary",)),
        name="moe_dispatch",
    )(slots, hn)


def _expert_body(blk_ref, eid_ref, valid_ref, fresh_ref, next_ref, x_ref, w1_ref, w3_ref, w2_ref, y_ref,
                 w1b, w3b, w2b, w1s, w3s, w2s, wsem):
    i = pl.program_id(0)
    valid = valid_ref[i]

    def weight_copies(e):
        return (pltpu.make_async_copy(w1_ref.at[e], w1s, wsem.at[0]),
                pltpu.make_async_copy(w3_ref.at[e], w3s, wsem.at[1]),
                pltpu.make_async_copy(w2_ref.at[e], w2s, wsem.at[2]))

    @pl.when(i == 0)
    def _():
        for cp in weight_copies(eid_ref[0]):
            cp.start()

    @pl.when(valid > 0)
    def _():
        @pl.when(fresh_ref[i] == 1)
        def _():
            for cp in weight_copies(eid_ref[i]):
                cp.wait()
            w1b[...] = w1s[...].astype(BF16)
            w3b[...] = w3s[...].astype(BF16)
            w2b[...] = w2s[...].astype(BF16)

            @pl.when(next_ref[i] >= 0)
            def _():
                for cp in weight_copies(next_ref[i]):
                    cp.start()

        bm = x_ref.shape[0] // ROW_TILES
        live =lax.broadcasted_iota(jnp.int32, (bm, LANES), 0) < valid
        x = jnp.concatenate(
            [jnp.where(live, x_ref[pl.ds(j, bm, stride=ROW_TILES), :], 0.0).astype(BF16)
             for j in range(ROW_TILES)], axis=1)
        a = jnp.dot(x, w1b[...], preferred_element_type=F32)
        b = jnp.dot(x, w3b[...], preferred_element_type=F32)
        hid = (a * jax.nn.sigmoid(a) * b).astype(BF16)
        y = jnp.dot(hid, w2b[...], preferred_element_type=F32)
        for j in range(ROW_TILES):
            y_ref[pl.ds(j, bm, stride=ROW_TILES), :] = y[:, j * LANES:(j + 1) * LANES]


def _experts(blk, blk_eid, blk_valid, blk_fresh, blk_next, x_slots, w1, w3, w2, n_blocks):
    bm = MOE_BM
    slot_block = lambda i, blk, eid, val, fr, nx: (blk[i], 0)
    grid_spec = pltpu.PrefetchScalarGridSpec(
        num_scalar_prefetch=5,
        grid=(n_blocks,),
        in_specs=[pl.BlockSpec((bm * ROW_TILES, LANES), slot_block),
                  pl.BlockSpec(memory_space=pl.ANY), pl.BlockSpec(memory_space=pl.ANY),
                  pl.BlockSpec(memory_space=pl.ANY)],
        out_specs=pl.BlockSpec((bm * ROW_TILES, LANES), slot_block),
        scratch_shapes=[pltpu.VMEM((D_MODEL, MOE_HIDDEN), BF16), pltpu.VMEM((D_MODEL, MOE_HIDDEN), BF16),
                        pltpu.VMEM((MOE_HIDDEN, D_MODEL), BF16),
                        pltpu.VMEM((D_MODEL, MOE_HIDDEN), F32), pltpu.VMEM((D_MODEL, MOE_HIDDEN), F32),
                        pltpu.VMEM((MOE_HIDDEN, D_MODEL), F32), pltpu.SemaphoreType.DMA((3,))],
    )
    return pl.pallas_call(
        _expert_body,
        grid_spec=grid_spec,
        out_shape=jax.ShapeDtypeStruct(x_slots.shape, F32),
        compiler_params=_cparams(("arbitrary",)),
        name="moe_experts",
    )(blk, blk_eid, blk_valid, blk_fresh, blk_next, x_slots, w1, w3, w2)


def _combine_body(slot_ref, ys_ref, h_ref, route_ref, nw_ref, o_ref, ybuf, sem, *, T):
    i = pl.program_id(0)
    n = pl.num_programs(0)
    tm = h_ref.shape[0]

    def row_copy(slot, buf, k, j):
        return pltpu.make_async_copy(ys_ref.at[_tile_rows(slot)], ybuf.at[buf, k, _tile_rows(j)], sem.at[buf])

    def issue(tile, buf):
        def one(j, carry):
            t = tile * tm + j
            row_copy(slot_ref[t], buf, 0, j).start(priority=0)
            row_copy(slot_ref[T + t], buf, 1, j).start(priority=1)
            return carry
        lax.fori_loop(0, tm, one, 0, unroll=ISSUE_UNROLL)

    @pl.when(i == 0)
    def _():
        issue(0, 0)

    @pl.when(i + 1 < n)
    def _():
        issue(i + 1, (i + 1) % 2)

    buf = i % 2
    for k in range(2):
        pltpu.make_async_copy(ys_ref.at[pl.ds(0, tm * ROW_TILES)], ybuf.at[buf, k], sem.at[buf]).wait()
    route = route_ref[...]
    g1 = route[:, ROUTE_GATE:ROUTE_GATE + 1]
    g2 = route[:, ROUTE_GATE + 1:ROUTE_GATE + 2]
    hs = []
    ss = jnp.zeros((tm, 1), F32)
    for j in range(ROW_TILES):
        tile_row = pl.ds(j, tm, stride=ROW_TILES)
        hj = h_ref[:, j * LANES:(j + 1) * LANES] + (ybuf[buf, 0, tile_row, :] * g1 + ybuf[buf, 1, tile_row, :] * g2)
        hs.append(hj)
        ss = ss + jnp.sum(hj * hj, axis=-1, keepdims=True)
    inv = lax.rsqrt(ss * (1.0 / D_MODEL) + NORM_EPS)
    for j, hj in enumerate(hs):
        cols = slice(j * LANES, (j + 1) * LANES)
        o_ref[:, cols] = hj * inv * nw_ref[:, cols]


def _combine(slots, y_slots, h, route, nw, T):
    tm = TM_COMBINE
    row = lambda i, s: (i, 0)
    grid_spec = pltpu.PrefetchScalarGridSpec(
        num_scalar_prefetch=1,
        grid=(T // tm,),
        in_specs=[pl.BlockSpec(memory_space=pl.ANY),
                  pl.BlockSpec((tm, D_MODEL), row),
                  pl.BlockSpec((tm, LANES), row),
                  pl.BlockSpec((1, D_MODEL), lambda i, s: (0, 0))],
        out_specs=pl.BlockSpec((tm, D_MODEL), row),
        scratch_shapes=[pltpu.VMEM((2, 2, tm * ROW_TILES, LANES), F32), pltpu.SemaphoreType.DMA((2,))],
    )
    return pl.pallas_call(
        functools.partial(_combine_body, T=T),
        grid_spec=grid_spec,
        out_shape=jax.ShapeDtypeStruct((T, D_MODEL), F32),
        compiler_params=_cparams(("arbitrary",)),
        name="moe_combine",
    )(slots, y_slots, h, route, nw)


def _rotary_tables(S):
    inv_freq = (1.0 / (np.float32(ROPE_THETA) ** (np.arange(0, HEAD_DIM, 2, dtype=np.float32) / HEAD_DIM))
                ).astype(np.float32)
    ang = np.arange(S, dtype=np.float32)[:, None] * inv_freq[None, :]
    cos, sin = np.cos(ang), np.sin(ang)
    reps = LANES // HEAD_DIM
    cos_t = np.tile(np.concatenate([cos, cos], axis=1), (1, reps)).astype(np.float32)
    sin_t = np.tile(np.concatenate([-sin, sin], axis=1), (1, reps)).astype(np.float32)
    return jnp.asarray(cos_t), jnp.asarray(sin_t)


def _layer(h_in, norm_mix_w, w_in, b_branch_gate, ret_decay_fwd, ret_decay_bwd, ret_gn_w, w_attn_branch,
           w_ret_branch, w_out, norm_moe_w, moe_w_group, moe_b_group, moe_w_expert, moe_b_expert,
           moe_w1, moe_w3, moe_w2, next_norm_w, B, S, cos_t, sin_t):
    T = B * S
    (qa0, ka0, va0, qa1, ka1, va1, qa2, ka2, va2, qr, kr, vr, gr, gates) = _in_projection(
        h_in, norm_mix_w[None, :], w_in.astype(BF16), b_branch_gate[None, :], cos_t, sin_t, B, S)

    unit = lambda a: a[:, :, None]
    y_att = _attention((unit(qa0), unit(ka0), unit(va0), qa1, ka1, va1, qa2, ka2, va2), B, S)

    dec = jnp.stack([ret_decay_fwd.reshape(RET_HEADS // 2, 2), ret_decay_bwd.reshape(RET_HEADS // 2, 2)], axis=1)
    dec = jnp.broadcast_to(dec.reshape(RET_HEADS // 2, 4, 1), (RET_HEADS // 2, 4, LANES)).astype(F32)
    y_ret = _retention(dec, qr, kr, vr, gr, ret_gn_w[None, :], B, S).reshape(T, RET_V_W)

    pad = LANES - MOE_GROUPS - MOE_N_EXPERTS
    w_r = jnp.concatenate([moe_w_group, moe_w_expert, jnp.zeros((D_MODEL, pad), F32)], axis=1)
    w_r_hi = w_r.astype(BF16)
    w_r_lo = (w_r - w_r_hi.astype(F32)).astype(BF16)
    b_r = jnp.concatenate([moe_b_group, moe_b_expert, jnp.zeros((pad,), F32)])[None, :]

    h_mid, hn, route, route_t, cnt = _merge_route(
        y_att, y_ret, gates, h_in, w_attn_branch.astype(BF16), w_ret_branch.astype(BF16),
        w_out.astype(BF16), norm_moe_w[None, :], w_r_hi, w_r_lo, b_r, T, S)

    bm = MOE_BM
    counts = cnt[0, ROUTER_EXPERT_LANE0:ROUTER_EXPERT_LANE0 + MOE_N_EXPERTS].astype(jnp.int32)
    nblk = (counts + bm - 1) // bm
    blk_end = jnp.cumsum(nblk)
    pstart = (blk_end - nblk) * bm
    n_blocks = (2 * T) // bm + MOE_N_EXPERTS
    n_active = blk_end[-1]
    bidx = jnp.minimum(jnp.arange(n_blocks, dtype=jnp.int32), n_active - 1)
    blk_eid = jnp.sum(bidx[:, None] >= blk_end[None, :], axis=1).astype(jnp.int32)
    mine = blk_eid[:, None] == jnp.arange(MOE_N_EXPERTS, dtype=jnp.int32)[None, :]
    seg_end = jnp.sum(jnp.where(mine, (pstart + counts)[None, :], 0), axis=1)
    blk_valid = jnp.clip(seg_end - bidx * bm, 0, bm)
    blk_valid = jnp.where(jnp.arange(n_blocks) < n_active, blk_valid, 0).astype(jnp.int32)
    blk_fresh = jnp.concatenate([jnp.ones((1,), jnp.int32), (blk_eid[1:] != blk_eid[:-1]).astype(jnp.int32)])
    ar = jnp.arange(MOE_N_EXPERTS, dtype=jnp.int32)
    later = jnp.min(jnp.where((nblk > 0)[None, :] & (ar[None, :] > ar[:, None]), ar[None, :], MOE_N_EXPERTS), axis=1)
    later = jnp.where(later < MOE_N_EXPERTS, later, -1)
    blk_next = (jnp.sum(jnp.where(mine, later[None, :] + 1, 0), axis=1) - 1).astype(jnp.int32)
    eid = route_t[ROUTE_EID:ROUTE_EID + 2].astype(jnp.int32)
    rank = route_t[ROUTE_RANK:ROUTE_RANK + 2].astype(jnp.int32)
    start = jnp.sum(jnp.where(eid[..., None] == jnp.arange(MOE_N_EXPERTS, dtype=jnp.int32),
                              pstart.astype(jnp.int32), 0), axis=-1)
    slots = (start + rank).reshape(2 * T)

    x_slots = _dispatch(slots, hn, n_blocks * bm, T)
    y_slots = _experts(bidx, blk_eid, blk_valid, blk_fresh, blk_next, x_slots, moe_w1, moe_w3, moe_w2, n_blocks)
    return _combine(slots, y_slots, h_mid, route, next_norm_w[None, :], T)


def kernel(x, norm_mix_w, w_in, b_branch_gate, ret_decay_fwd, ret_decay_bwd, ret_gn_w, w_attn_branch,
           w_ret_branch, w_out, norm_moe_w, moe_w_group, moe_b_group, moe_w_expert, moe_b_expert, moe_w1,
           moe_w3, moe_w2, norm_final_w):
    B, S, D = x.shape
    depth = norm_mix_w.shape[0]
    assert depth == 1, "the final norm is fused into the layer's combine stage"
    assert D == D_MODEL and S % TM_INPROJ == 0 and (B * S) < (1 << 24)
    cos_t, sin_t = _rotary_tables(S)
    out = _layer(x.reshape(B * S, D), norm_mix_w[0], w_in[0], b_branch_gate[0], ret_decay_fwd[0],
                 ret_decay_bwd[0], ret_gn_w[0], w_attn_branch[0], w_ret_branch[0], w_out[0], norm_moe_w[0],
                 moe_w_group[0], moe_b_group[0], moe_w_expert[0], moe_b_expert[0], moe_w1[0], moe_w3[0],
                 moe_w2[0], norm_final_w, B, S, cos_t, sin_t)
    return out.reshape(B, S, D)
```

```python
import functools

import numpy as np
import jax
import jax.numpy as jnp
from jax import lax
from jax.experimental import pallas as pl
from jax.experimental.pallas import tpu as pltpu

F32 = jnp.float32
BF16 = jnp.bfloat16

D_MODEL = 1024
HEAD_DIM = 64
ATTN_PAIRS = ((128, 1), (512, 4), (2048, 16))
ATTN_HEADS_PER_GROUP = 8
ATTN_GROUP_W = ATTN_HEADS_PER_GROUP * HEAD_DIM
ATTN_HALF = 64
ROPE_THETA = 10000.0
RET_HEADS = 8
RET_QK_DIM = 64
RET_V_DIM = 128
RET_CHUNK = 128
RET_QK_W = RET_HEADS * RET_QK_DIM
RET_V_W = RET_HEADS * RET_V_DIM
MOE_GROUPS = 8
MOE_EXPERTS_PER_GROUP = 8
MOE_N_EXPERTS = MOE_GROUPS * MOE_EXPERTS_PER_GROUP
MOE_HIDDEN = 512
NORM_EPS = 1e-6

LANES = 128
COL_CHUNKS = D_MODEL // LANES
ROW_TILES = COL_CHUNKS // 2
U32 = jnp.uint32
NEG_BIG = -1e30
LOG2_E = 1.4426950408889634

TM_INPROJ = 512
TM_MERGE = 1024
MERGE_SPLIT = 4
MERGE_COL_CHUNKS = 4
TM_COMBINE = 512
MOE_BM = 512
DISPATCH_CHUNK = 4096
ATTN_QB = 128
ATTN_UNROLL = 8
RET_UNROLL = 8

VMEM_LIMIT = 56 * 1024 * 1024

_A = 3 * ATTN_GROUP_W
OFF_QA, OFF_KA, OFF_VA = 0, _A, 2 * _A
OFF_QR = 3 * _A
OFF_KR = OFF_QR + RET_QK_W
OFF_VR = OFF_KR + RET_QK_W
OFF_GR = OFF_VR + RET_V_W
OFF_GL = OFF_GR + RET_V_W
IN_W = OFF_GL + 2 * D_MODEL


def _cparams(sem, vmem=VMEM_LIMIT):
    return pltpu.CompilerParams(dimension_semantics=sem, vmem_limit_bytes=vmem)


def _inproj_body(x_ref, nw_ref, w_ref, bg_ref, cos_ref, sin_ref,
                 qa0, ka0, va0, qa1, ka1, va1, qa2, ka2, va2, qr, kr, vr, gr, gt,
                 stage_ref):
    tm = x_ref.shape[0]
    x = x_ref[...]
    ms = jnp.mean(x * x, axis=-1, keepdims=True)
    xn = (x * lax.rsqrt(ms + NORM_EPS) * nw_ref[...]).astype(BF16)
    cos = cos_ref[...]
    sin = sin_ref[...]
    lane = lax.broadcasted_iota(jnp.int32, (tm, LANES), 1)
    first_half = (lane & (HEAD_DIM - 1)) < (HEAD_DIM // 2)

    def proj(c0, width):
        return jnp.dot(xn, w_ref[:, c0:c0 + width], preferred_element_type=F32)

    def rotary(a, scale):
        partner = jnp.where(first_half, pltpu.roll(a, LANES - HEAD_DIM // 2, 1),
                            pltpu.roll(a, HEAD_DIM // 2, 1))
        r = a * cos + partner * sin
        return r * scale if scale != 1.0 else r

    def chunks(acc):
        return [acc[:, c * LANES:(c + 1) * LANES] for c in range(acc.shape[1] // LANES)]

    def store_natural(out_ref, acc, fn):
        for c, a in enumerate(chunks(acc)):
            out_ref[:, c * LANES:(c + 1) * LANES] = fn(a).astype(out_ref.dtype)

    def store_pairs(out_ref, acc, fn):
        for c, a in enumerate(chunks(acc)):
            out_ref[0, c] = fn(a).astype(out_ref.dtype)

    def store_strided(out_ref, acc, fn, d):
        for c, a in enumerate(chunks(acc)):
            stage_ref[c] = fn(a)
        for c in range(acc.shape[1] // LANES):
            for r in range(d):
                out_ref[0, c, r] = stage_ref[c, pl.ds(r, tm // d, stride=d), :].astype(out_ref.dtype)

    ident = lambda a: a
    rot_q = lambda a: rotary(a, HEAD_DIM ** -0.5 * LOG2_E)
    rot_1 = lambda a: rotary(a, 1.0)
    rot_k = lambda a: rotary(a, RET_QK_DIM ** -0.5)

    W = ATTN_GROUP_W
    store_pairs(qa0, proj(OFF_QA, W), rot_q)
    store_pairs(ka0, proj(OFF_KA, W), rot_1)
    store_pairs(va0, proj(OFF_VA, W), ident)
    for g, (qo, ko, vo) in ((1, (qa1, ka1, va1)), (2, (qa2, ka2, va2))):
        d = ATTN_PAIRS[g][1]
        store_strided(qo, proj(OFF_QA + g * W, W), rot_q, d)
        store_strided(ko, proj(OFF_KA + g * W, W), rot_1, d)
        store_strided(vo, proj(OFF_VA + g * W, W), ident, d)
    store_natural(qr, proj(OFF_QR, RET_QK_W), rot_1)
    store_natural(kr, proj(OFF_KR, RET_QK_W), rot_k)
    for h in range(RET_V_W // W):
        vr[:, h * W:(h + 1) * W] = proj(OFF_VR + h * W, W).astype(vr.dtype)
        gr[:, h * W:(h + 1) * W] = proj(OFF_GR + h * W, W).astype(gr.dtype)
    for h in range(2 * D_MODEL // W):
        z = proj(OFF_GL + h * W, W) + bg_ref[:, h * W:(h + 1) * W]
        gt[:, h * W:(h + 1) * W] = jax.nn.sigmoid(z).astype(gt.dtype)


def _in_projection(x2, norm_w, w_bf, b_gate, cos_t, sin_t, B, S):
    T = B * S
    tm = TM_INPROJ
    nt = S // tm
    W = ATTN_GROUP_W
    row = lambda i: (i, 0)
    const = lambda i: (0, 0)
    nat = lambda width: pl.BlockSpec((tm, width), row)

    P = W // LANES

    def strided_spec(d):
        return pl.BlockSpec((1, P, d, tm // d, LANES), lambda i: (i // nt, 0, 0, i % nt, 0))

    def strided_shape(d):
        return jax.ShapeDtypeStruct((B, P, d, S // d, LANES), BF16)

    pair_spec = pl.BlockSpec((1, P, tm, LANES), lambda i: (i // nt, 0, i % nt, 0))
    pair_shape = jax.ShapeDtypeStruct((B, P, S, LANES), BF16)
    nat_shape = lambda width: jax.ShapeDtypeStruct((T, width), BF16)
    d1, d2 = ATTN_PAIRS[1][1], ATTN_PAIRS[2][1]
    out_shape = ([pair_shape] * 3 + [strided_shape(d1)] * 3 + [strided_shape(d2)] * 3
                 + [nat_shape(RET_QK_W)] * 2 + [nat_shape(RET_V_W)] * 2 + [nat_shape(2 * D_MODEL)])
    out_specs = ([pair_spec] * 3 + [strided_spec(d1)] * 3 + [strided_spec(d2)] * 3
                 + [nat(RET_QK_W)] * 2 + [nat(RET_V_W)] * 2 + [nat(2 * D_MODEL)])
    in_specs = [
        pl.BlockSpec((tm, D_MODEL), row),
        pl.BlockSpec((1, D_MODEL), const),
        pl.BlockSpec((D_MODEL, IN_W), const, pipeline_mode=pl.Buffered(1)),
        pl.BlockSpec((1, 2 * D_MODEL), const),
        pl.BlockSpec((tm, LANES), lambda i: (i % nt, 0)),
        pl.BlockSpec((tm, LANES), lambda i: (i % nt, 0)),
    ]
    return pl.pallas_call(
        _inproj_body,
        grid=(T // tm,),
        in_specs=in_specs,
        out_specs=out_specs,
        out_shape=out_shape,
        scratch_shapes=[pltpu.VMEM((W // LANES, tm, LANES), F32)],
        compiler_params=_cparams(("parallel",)),
        name="in_projection",
    )(x2, norm_w, w_bf, b_gate, cos_t, sin_t)


def _attn_body(q0_ref, k0_ref, v0_ref, q1_ref, k1_ref, v1_ref, q2_ref, k2_ref, v2_ref, o_ref,
               va_ref, vb_ref, part_ref, bias_ref, *, S):
    QB, H = ATTN_QB, ATTN_HALF
    lane = lax.broadcasted_iota(jnp.int32, (QB, LANES), 1)
    head0 = lane < HEAD_DIM
    ones = jnp.ones((QB, LANES), BF16)

    qi = lax.broadcasted_iota(jnp.int32, (QB, QB + 2 * H), 0)
    ki = lax.broadcasted_iota(jnp.int32, (QB, QB + 2 * H), 1)
    for n in range(3):
        bias_ref[n] = jnp.where(jnp.abs(ki - qi - n * H) <= H, 0.0, NEG_BIG).astype(F32)

    def scores(q_rows, k_rows, bias, h):
        qm = jnp.where(head0 if h == 0 else jnp.logical_not(head0), q_rows, jnp.zeros_like(q_rows))
        return lax.dot_general(qm, k_rows, (((1,), (1,)), ((), ())), preferred_element_type=F32) + bias

    def weights(s):
        m = jnp.max(s, axis=-1, keepdims=True)
        return m, jnp.exp2(s - m).astype(BF16)

    def heads_to_lanes(m0, a, m1, b):
        num = jnp.where(head0, a, b)
        den = pltpu.roll(jnp.where(head0, b, a), HEAD_DIM, 1)
        mx = jnp.where(head0, m0, m1)
        return num, den, mx

    def run_group(q_ref, k_ref, v_ref, d, prepare, store):
        L = S // d
        KW = min(L, QB + 2 * H)
        nb = L // QB

        def fill(t, carry):
            r = t // nb
            rows = pl.ds(pl.multiple_of((t % nb) * QB, QB), QB)
            v = v_ref[0, 0, r, rows, :]
            dst = pl.ds(pl.multiple_of(t * QB, QB), QB)
            va_ref[dst, :] = jnp.where(head0, v, ones)
            vb_ref[dst, :] = jnp.where(head0, ones, v)
            return carry

        lax.fori_loop(0, S // QB, fill, 0, unroll=4)

        def trip(i, carry):
            blocks = []
            for u in range(ATTN_UNROLL):
                t = i * ATTN_UNROLL + u
                r = t // nb
                q0 = pl.multiple_of((t % nb) * QB, QB)
                ws = pl.multiple_of(jnp.clip(q0 - H, 0, L - KW), H)
                bias = bias_ref[(q0 - ws) // H][:, :KW]
                q_rows = q_ref[0, 0, r, pl.ds(q0, QB), :]
                k_rows = k_ref[0, 0, r, pl.ds(ws, KW), :]
                vrows = pl.ds(pl.multiple_of(r * L + ws, H), KW)
                blocks.append((r, q0, vrows, [scores(q_rows, k_rows, bias, h) for h in range(2)]))
            blocks = [(r, q0, vrows, [weights(s) for s in ss]) for r, q0, vrows, ss in blocks]
            done = []
            for r, q0, vrows, ((m0, p0), (m1, p1)) in blocks:
                a = jnp.dot(p0, va_ref[vrows, :], preferred_element_type=F32)
                b = jnp.dot(p1, vb_ref[vrows, :], preferred_element_type=F32)
                done.append((r, q0, m0, a, m1, b))
            done = [(r, q0, prepare(q0, *heads_to_lanes(m0, a, m1, b))) for r, q0, m0, a, m1, b in done]
            for r, q0, vals in done:
                store(r, q0, vals)
            return carry

        lax.fori_loop(0, S // QB // ATTN_UNROLL, trip, 0)

    def store_partial(g):
        d = ATTN_PAIRS[g][1]

        def store(r, q0, vals):
            rows = pl.ds(r + q0 * d, QB, stride=d)
            for n, val in enumerate(vals):
                part_ref[3 * (g - 1) + n, rows, :] = val
        return store

    keep = lambda q0, num, den, mx: (num, den, mx)
    run_group(q1_ref, k1_ref, v1_ref, ATTN_PAIRS[1][1], keep, store_partial(1))
    run_group(q2_ref, k2_ref, v2_ref, ATTN_PAIRS[2][1], keep, store_partial(2))

    def merge(q0, num, den, mx):
        rows = pl.ds(q0, QB)
        nums = [num, part_ref[0, rows, :], part_ref[3, rows, :]]
        dens = [den, part_ref[1, rows, :], part_ref[4, rows, :]]
        mxs = [mx, part_ref[2, rows, :], part_ref[5, rows, :]]
        top = jnp.maximum(jnp.maximum(mxs[0], mxs[1]), mxs[2])
        ws = [jnp.exp2(m - top) for m in mxs]
        n = ws[0] * nums[0] + ws[1] * nums[1] + ws[2] * nums[2]
        dn = ws[0] * dens[0] + ws[1] * dens[1] + ws[2] * dens[2]
        return (n / dn).astype(o_ref.dtype)

    def store_out(r, q0, y):
        o_ref[0, 0, pl.ds(q0, QB), :] = y

    run_group(q0_ref, k0_ref, v0_ref, ATTN_PAIRS[0][1], merge, store_out)


def _attention(qkv, B, S):
    P = ATTN_GROUP_W // LANES
    in_specs = []
    for _, d in ATTN_PAIRS:
        in_specs += [pl.BlockSpec((1, 1, d, S // d, LANES), lambda b, p: (b, p, 0, 0, 0))] * 3
    return pl.pallas_call(
        functools.partial(_attn_body, S=S),
        grid=(B, P),
        in_specs=in_specs,
        out_specs=pl.BlockSpec((1, 1, S, LANES), lambda b, p: (b, p, 0, 0)),
        out_shape=jax.ShapeDtypeStruct((B, P, S, LANES), BF16),
        scratch_shapes=[pltpu.VMEM((S, LANES), BF16), pltpu.VMEM((S, LANES), BF16),
                        pltpu.VMEM((6, S, LANES), F32),
                        pltpu.VMEM((3, ATTN_QB, ATTN_QB + 2 * ATTN_HALF), F32)],
        compiler_params=_cparams(("parallel", "parallel")),
        name="attention",
    )(*qkv)


def _log_sigmoid(z):
    return jnp.minimum(z, 0.0) - jnp.log(1.0 + jnp.exp(-jnp.abs(z)))


def _ret_body(dec_ref, q_ref, k_ref, v_ref, g_ref, gnw_ref, o_ref, kt_ref, sf_ref, sb_ref, st_ref, *, S):
    C = RET_CHUNK
    nc = S // C
    lg = _log_sigmoid(dec_ref[0])
    a_row = lax.broadcasted_iota(jnp.int32, (C, LANES), 0).astype(F32)
    lane = lax.broadcasted_iota(jnp.int32, (C, LANES), 1)
    rel = (lax.broadcasted_iota(jnp.int32, (C, C), 0) - lax.broadcasted_iota(jnp.int32, (C, C), 1)).astype(F32)

    heads = []
    for h in range(2):
        lgf = lg[h:h + 1, :]
        lgb = lg[2 + h:3 + h, :]
        in_head = (lane < RET_QK_DIM) if h == 0 else (lane >= RET_QK_DIM)
        heads.append(dict(
            in_head=in_head,
            xi_f=jnp.where(in_head, jnp.exp((a_row + 1.0) * lgf), 0.0),
            xi_b=jnp.where(in_head, jnp.exp((C - a_row) * lgb), 0.0),
            zeta_f=jnp.exp((C - 1.0 - a_row) * lgf),
            zeta_b=jnp.exp(a_row * lgb),
            dloc=jnp.where(rel > 0, jnp.exp(rel * lgf), jnp.where(rel < 0, jnp.exp(-rel * lgb), 2.0)),
            cd_f=jnp.exp(C * lgf),
            cd_b=jnp.exp(C * lgb),
        ))

    def chunk_rows(n):
        return pl.ds(pl.multiple_of(n * C, C), C)

    def v_head(rows, h):
        return v_ref[0, rows, h * RET_V_DIM:(h + 1) * RET_V_DIM]

    def products(i, carry):
        items = []
        for u in range(RET_UNROLL):
            n = i * RET_UNROLL + u
            rows = chunk_rows(n)
            kt = k_ref[0, rows, :].astype(F32).T.astype(BF16)
            kt_ref[n] = kt
            for h, hd in enumerate(heads):
                vh = v_head(rows, h).astype(F32)
                items.append((n, h, kt, (vh * hd["zeta_f"]).astype(BF16), (vh * hd["zeta_b"]).astype(BF16)))
        outs = [(n, h, jnp.dot(kt, vf, preferred_element_type=F32), jnp.dot(kt, vb, preferred_element_type=F32))
                for n, h, kt, vf, vb in items]
        for n, h, f, b in outs:
            sf_ref[n, h] = f
            sb_ref[n, h] = b
        return carry

    lax.fori_loop(0, nc // RET_UNROLL, products, 0)

    def scan(ref, key, order, half):
        rows = slice(half * 2 * RET_QK_DIM, (half + 1) * 2 * RET_QK_DIM)

        def step(i, state):
            n = order(i)
            new = []
            for h, hd in enumerate(heads):
                st_ref[n, h, rows, :] = state[h].astype(BF16)
                new.append(hd[key] * state[h] + ref[n, h])
            return tuple(new)
        zero = jnp.zeros(ref.shape[2:], F32)
        lax.fori_loop(0, nc, step, (zero, zero))

    scan(sf_ref, "cd_f", lambda i: i, 0)
    scan(sb_ref, "cd_b", lambda i: nc - 1 - i, 1)

    def outputs(i, carry):
        items = []
        for u in range(RET_UNROLL):
            n = i * RET_UNROLL + u
            rows = chunk_rows(n)
            qp = q_ref[0, rows, :]
            qf = qp.astype(F32)
            kt = kt_ref[n]
            for h, hd in enumerate(heads):
                qm = jnp.where(hd["in_head"], qp, jnp.zeros_like(qp))
                qx = jnp.concatenate([(qf * hd["xi_f"]).astype(BF16), (qf * hd["xi_b"]).astype(BF16)], axis=1)
                items.append((rows, h, hd, jnp.dot(qm, kt, preferred_element_type=F32),
                              jnp.dot(qx, st_ref[n, h], preferred_element_type=F32)))
        items = [(rows, h, hd, (s * hd["dloc"]).astype(BF16), cross) for rows, h, hd, s, cross in items]
        items = [(rows, h, cross + jnp.dot(p, v_head(rows, h), preferred_element_type=F32))
                 for rows, h, hd, p, cross in items]
        for rows, h, ret in items:
            mu = jnp.mean(ret, axis=-1, keepdims=True)
            xc = ret - mu
            var = jnp.mean(xc * xc, axis=-1, keepdims=True)
            cols = slice(h * RET_V_DIM, (h + 1) * RET_V_DIM)
            gate = g_ref[0, rows, cols].astype(F32)
            y = xc * lax.rsqrt(var + NORM_EPS) * gnw_ref[:, cols] * (gate * jax.nn.sigmoid(gate))
            o_ref[0, rows, cols] = y.astype(o_ref.dtype)
        return carry

    lax.fori_loop(0, nc // RET_UNROLL, outputs, 0)


def _retention(dec, qr, kr, vr, gr, gn_w, B, S):
    nc = S // RET_CHUNK
    npairs = RET_HEADS // 2
    qk_spec = pl.BlockSpec((1, S, 2 * RET_QK_DIM), lambda b, p: (b, 0, p))
    v_spec = pl.BlockSpec((1, S, 2 * RET_V_DIM), lambda b, p: (b, 0, p))
    return pl.pallas_call(
        functools.partial(_ret_body, S=S),
        grid=(B, npairs),
        in_specs=[pl.BlockSpec((1, 4, LANES), lambda b, p: (p, 0, 0)),
                  qk_spec, qk_spec, v_spec, v_spec,
                  pl.BlockSpec((1, 2 * RET_V_DIM), lambda b, p: (0, p))],
        out_specs=v_spec,
        out_shape=jax.ShapeDtypeStruct((B, S, RET_V_W), BF16),
        scratch_shapes=[pltpu.VMEM((nc, 2 * RET_QK_DIM, RET_CHUNK), BF16),
                        pltpu.VMEM((nc, 2, 2 * RET_QK_DIM, RET_V_DIM), F32),
                        pltpu.VMEM((nc, 2, 2 * RET_QK_DIM, RET_V_DIM), F32),
                        pltpu.VMEM((nc, 2, 4 * RET_QK_DIM, RET_V_DIM), BF16)],
        compiler_params=_cparams(("parallel", "parallel")),
        name="retention",
    )(dec, qr.reshape(B, S, RET_QK_W), kr.reshape(B, S, RET_QK_W),
      vr.reshape(B, S, RET_V_W), gr.reshape(B, S, RET_V_W), gn_w)


ROUTE_EID, ROUTE_RANK, ROUTE_GATE = 0, 2, 4
ROUTE_FIELDS = 8
ROUTER_EXPERT_LANE0 = MOE_GROUPS


def _merge_body(yatt_ref, yret_ref, gt_ref, x_ref, wa_ref, wb_ref, wo_ref,
                nw_ref, wrh_ref, wrl_ref, br_ref,
                h_ref, hn_ref, route_ref, route_t_ref, cnt_ref):
    tm = x_ref.shape[0]
    hm = tm // MERGE_SPLIT
    i = pl.program_id(0)
    cw = D_MODEL // MERGE_COL_CHUNKS

    def branch_products(rows):
        y_att = jnp.concatenate([yatt_ref[0, c, rows, :] for c in range(yatt_ref.shape[1])], axis=1)
        y_ret = yret_ref[rows, :]
        chunks = []
        for c in range(MERGE_COL_CHUNKS):
            cols = slice(c * cw, (c + 1) * cw)
            a = jnp.dot(y_att, wa_ref[:, cols], preferred_element_type=F32)
            b = jnp.dot(y_ret, wb_ref[:, cols], preferred_element_type=F32)
            g_att = gt_ref[rows, c * cw:(c + 1) * cw].astype(F32)
            g_ret = gt_ref[rows, D_MODEL + c * cw:D_MODEL + (c + 1) * cw].astype(F32)
            chunks.append((g_att * a + g_ret * b).astype(BF16))
        return jnp.concatenate(chunks, axis=1)

    def residual_norm(hf, rows, merged):
        mix = jnp.dot(merged, wo_ref[...], preferred_element_type=F32)
        h = x_ref[rows, :] + mix
        h_ref[rows, :] = h
        ms = jnp.mean(h * h, axis=-1, keepdims=True)
        hn = h * lax.rsqrt(ms + NORM_EPS) * nw_ref[...]
        for j, word in enumerate(_pack_row(hn)):
            hn_ref[pl.ds(hf * hm * ROW_TILES + j, hm, stride=ROW_TILES), :] = word
        hi = hn.astype(BF16)
        lo = (hn - hi.astype(F32)).astype(BF16)
        return hi, lo

    def router_logits(hi, lo):
        return (jnp.dot(hi, wrh_ref[...], preferred_element_type=F32)
                + jnp.dot(hi, wrl_ref[...], preferred_element_type=F32)
                + jnp.dot(lo, wrh_ref[...], preferred_element_type=F32)) + br_ref[...]

    lane = lax.broadcasted_iota(jnp.int32, (hm, LANES), 1)
    far = jnp.int32(4 * LANES)

    def first_argmax(vals, vmax):
        return jnp.min(jnp.where(vals == vmax, lane, far), axis=-1, keepdims=True)

    def route(logits):
        is_group = lane < MOE_GROUPS
        gl = jnp.where(is_group, logits, NEG_BIG)
        gmax = jnp.max(gl, axis=-1, keepdims=True)
        g_w = 1.0 / jnp.sum(jnp.where(is_group, jnp.exp(gl - gmax), 0.0), axis=-1, keepdims=True)
        g_idx = first_argmax(gl, gmax)
        e_lane = lane - ROUTER_EXPERT_LANE0
        in_group = (e_lane >= 0) & (e_lane < MOE_N_EXPERTS) & (jnp.right_shift(e_lane, 3) == g_idx)
        el = jnp.where(in_group, logits, NEG_BIG)
        m1 = jnp.max(el, axis=-1, keepdims=True)
        i1 = first_argmax(el, m1)
        el2 = jnp.where(lane == i1, NEG_BIG, el)
        m2 = jnp.max(el2, axis=-1, keepdims=True)
        i2 = first_argmax(el2, m2)
        ex = jnp.exp(m2 - m1)
        return i1, i2, g_w / (1.0 + ex), g_w * ex / (1.0 + ex)

    rows = [slice(hf * hm, (hf + 1) * hm) for hf in range(MERGE_SPLIT)]
    merged = [branch_products(r) for r in rows]
    split = [residual_norm(hf, r, m) for hf, (r, m) in enumerate(zip(rows, merged))]
    routed = [route(router_logits(hi, lo)) for hi, lo in split]

    @pl.when(i == 0)
    def _():
        cnt_ref[...] = jnp.zeros(cnt_ref.shape, F32)

    r_idx = lax.broadcasted_iota(jnp.int32, (hm, hm), 0)
    c_idx = lax.broadcasted_iota(jnp.int32, (hm, hm), 1)
    lower = jnp.where(c_idx < r_idx, 1.0, 0.0).astype(BF16)
    running = cnt_ref[...]
    for hf, (i1, i2, gate1, gate2) in enumerate(routed):
        hot1 = lane == i1
        hot2 = lane == i2
        onehot = jnp.where(hot1 | hot2, 1.0, 0.0)
        before = jnp.dot(lower, onehot.astype(BF16), preferred_element_type=F32) + running
        rank1 = jnp.sum(jnp.where(hot1, before, 0.0), axis=-1, keepdims=True)
        rank2 = jnp.sum(jnp.where(hot2, before, 0.0), axis=-1, keepdims=True)
        running = running + jnp.sum(onehot, axis=0, keepdims=True)
        rec = jnp.zeros((hm, LANES), F32)
        for pos, val in ((ROUTE_EID, (i1 - ROUTER_EXPERT_LANE0).astype(F32)),
                         (ROUTE_EID + 1, (i2 - ROUTER_EXPERT_LANE0).astype(F32)),
                         (ROUTE_RANK, rank1), (ROUTE_RANK + 1, rank2),
                         (ROUTE_GATE, gate1), (ROUTE_GATE + 1, gate2)):
            rec = jnp.where(lane == pos, val, rec)
        route_ref[rows[hf], :] = rec
        route_t_ref[:, hf * hm:(hf + 1) * hm] = rec.T[:route_t_ref.shape[0], :]
    cnt_ref[...] = running


def _merge_route(y_att, y_ret, gates, x2, wa, wb, wo, nw, wr_hi, wr_lo, b_r, T, S):
    tm = TM_MERGE
    nt = S // tm
    row = lambda i: (i, 0)
    const = lambda i: (0, 0)
    full = lambda arr: pl.BlockSpec(arr.shape, const)
    in_specs = ([pl.BlockSpec((1, y_att.shape[1], tm, LANES), lambda i: (i // nt, 0, i % nt, 0)),
                 pl.BlockSpec((tm, RET_V_W), row), pl.BlockSpec((tm, 2 * D_MODEL), row),
                 pl.BlockSpec((tm, D_MODEL), row),
                 full(wa), full(wb), full(wo), full(nw), full(wr_hi), full(wr_lo), full(b_r)])
    return pl.pallas_call(
        _merge_body,
        grid=(T // tm,),
        in_specs=in_specs,
        out_specs=[pl.BlockSpec((tm, D_MODEL), row),
                   pl.BlockSpec((tm * ROW_TILES, LANES), row),
                   pl.BlockSpec((tm, LANES), row), pl.BlockSpec((ROUTE_FIELDS, tm), lambda i: (0, i)),
                   pl.BlockSpec((1, LANES), const)],
        out_shape=[jax.ShapeDtypeStruct((T, D_MODEL), F32), jax.ShapeDtypeStruct((T * ROW_TILES, LANES), U32),
                   jax.ShapeDtypeStruct((T, LANES), F32), jax.ShapeDtypeStruct((ROUTE_FIELDS, T), F32),
                   jax.ShapeDtypeStruct((1, LANES), F32)],
        compiler_params=_cparams(("arbitrary",)),
        name="merge_route",
    )(y_att, y_ret, gates, x2, wa, wb, wo, nw, wr_hi, wr_lo, b_r)


ISSUE_UNROLL = 8


def _tile_rows(n):
    return pl.ds(pl.multiple_of(n * ROW_TILES, ROW_TILES), ROW_TILES)


def _pack_row(x):
    bits = lambda c: pltpu.bitcast(x[:, c * LANES:(c + 1) * LANES].astype(BF16).astype(F32), U32)
    return [(bits(j) >> 16) | bits(j + ROW_TILES) for j in range(ROW_TILES)]


def _unpack_word(w):
    return pltpu.bitcast(w << 16, F32), pltpu.bitcast(w & jnp.uint32(0xFFFF0000), F32)


def _dispatch_body(slot_ref, hn_ref, xs_ref, sem, *, T):
    i = pl.program_id(0)
    ch = hn_ref.shape[0] // ROW_TILES

    def row_copy(j, slot):
        return pltpu.make_async_copy(hn_ref.at[_tile_rows(j)], xs_ref.at[_tile_rows(slot)], sem)

    def issue(j, carry):
        t = i * ch + j
        row_copy(j, slot_ref[t]).start(priority=0)
        row_copy(j, slot_ref[T + t]).start(priority=1)
        return carry

    lax.fori_loop(0, ch, issue, 0, unroll=ISSUE_UNROLL)
    for _ in range(2):
        pltpu.make_async_copy(hn_ref, xs_ref.at[pl.ds(0, ch * ROW_TILES)], sem).wait()


def _dispatch(slots, hn, n_slots, T):
    ch = DISPATCH_CHUNK
    grid_spec = pltpu.PrefetchScalarGridSpec(
        num_scalar_prefetch=1,
        grid=(T // ch,),
        in_specs=[pl.BlockSpec((ch * ROW_TILES, LANES), lambda i, s: (i, 0))],
        out_specs=pl.BlockSpec(memory_space=pl.ANY),
        scratch_shapes=[pltpu.SemaphoreType.DMA(())],
    )
    return pl.pallas_call(
        functools.partial(_dispatch_body, T=T),
        grid_spec=grid_spec,
        out_shape=jax.ShapeDtypeStruct((n_slots * ROW_TILES, LANES), U32),
        compiler_params=_cparams(("arbitrary",)),
        name="moe_dispatch",
    )(slots, hn)


def _expert_body(blk_ref, eid_ref, valid_ref, fresh_ref, next_ref, x_ref, w1_ref, w3_ref, w2_ref, y_ref,
                 w1b, w3b, w2b, w1s, w3s, w2s, wsem):
    i = pl.program_id(0)
    valid = valid_ref[i]

    def weight_copies(e):
        return (pltpu.make_async_copy(w1_ref.at[e], w1s, wsem.at[0]),
                pltpu.make_async_copy(w3_ref.at[e], w3s, wsem.at[1]),
                pltpu.make_async_copy(w2_ref.at[e], w2s, wsem.at[2]))

    @pl.when(i == 0)
    def _():
        for cp in weight_copies(eid_ref[0]):
            cp.start()

    @pl.when(valid > 0)
    def _():
        @pl.when(fresh_ref[i] == 1)
        def _():
            for cp in weight_copies(eid_ref[i]):
                cp.wait()
            w1b[...] = w1s[...].astype(BF16)
            w3b[...] = w3s[...].astype(BF16)
            w2b[...] = w2s[...].astype(BF16)

            @pl.when(next_ref[i] >= 0)
            def _():
                for cp in weight_copies(next_ref[i]):
                    cp.start()

        bm = x_ref.shape[0] // ROW_TILES
        live =lax.broadcasted_iota(jnp.int32, (bm, LANES), 0) < valid
        halves = [_unpack_word(x_ref[pl.ds(j, bm, stride=ROW_TILES), :]) for j in range(ROW_TILES)]
        x = jnp.concatenate(
            [jnp.where(live, c, 0.0).astype(BF16) for c in [lo for lo, _ in halves] + [hi for _, hi in halves]],
            axis=1)
        a = jnp.dot(x, w1b[...], preferred_element_type=F32)
        b = jnp.dot(x, w3b[...], preferred_element_type=F32)
        hid = (a * jax.nn.sigmoid(a) * b).astype(BF16)
        y = jnp.dot(hid, w2b[...], preferred_element_type=F32)
        for j, word in enumerate(_pack_row(y)):
            y_ref[pl.ds(j, bm, stride=ROW_TILES), :] = word


def _experts(blk, blk_eid, blk_valid, blk_fresh, blk_next, x_slots, w1, w3, w2, n_blocks):
    bm = MOE_BM
    slot_block = lambda i, blk, eid, val, fr, nx: (blk[i], 0)
    grid_spec = pltpu.PrefetchScalarGridSpec(
        num_scalar_prefetch=5,
        grid=(n_blocks,),
        in_specs=[pl.BlockSpec((bm * ROW_TILES, LANES), slot_block),
                  pl.BlockSpec(memory_space=pl.ANY), pl.BlockSpec(memory_space=pl.ANY),
                  pl.BlockSpec(memory_space=pl.ANY)],
        out_specs=pl.BlockSpec((bm * ROW_TILES, LANES), slot_block),
        scratch_shapes=[pltpu.VMEM((D_MODEL, MOE_HIDDEN), BF16), pltpu.VMEM((D_MODEL, MOE_HIDDEN), BF16),
                        pltpu.VMEM((MOE_HIDDEN, D_MODEL), BF16),
                        pltpu.VMEM((D_MODEL, MOE_HIDDEN), F32), pltpu.VMEM((D_MODEL, MOE_HIDDEN), F32),
                        pltpu.VMEM((MOE_HIDDEN, D_MODEL), F32), pltpu.SemaphoreType.DMA((3,))],
    )
    return pl.pallas_call(
        _expert_body,
        grid_spec=grid_spec,
        out_shape=jax.ShapeDtypeStruct(x_slots.shape, U32),
        compiler_params=_cparams(("arbitrary",)),
        name="moe_experts",
    )(blk, blk_eid, blk_valid, blk_fresh, blk_next, x_slots, w1, w3, w2)


def _combine_body(slot_ref, ys_ref, h_ref, route_ref, nw_ref, o_ref, ybuf, sem, *, T):
    i = pl.program_id(0)
    n = pl.num_programs(0)
    tm = h_ref.shape[0]

    def row_copy(slot, buf, k, j):
        return pltpu.make_async_copy(ys_ref.at[_tile_rows(slot)], ybuf.at[buf, k, _tile_rows(j)], sem.at[buf])

    def issue(tile, buf):
        def one(j, carry):
            t = tile * tm + j
            row_copy(slot_ref[t], buf, 0, j).start(priority=0)
            row_copy(slot_ref[T + t], buf, 1, j).start(priority=1)
            return carry
        lax.fori_loop(0, tm, one, 0, unroll=ISSUE_UNROLL)

    @pl.when(i == 0)
    def _():
        issue(0, 0)

    @pl.when(i + 1 < n)
    def _():
        issue(i + 1, (i + 1) % 2)

    buf = i % 2
    for k in range(2):
        pltpu.make_async_copy(ys_ref.at[pl.ds(0, tm * ROW_TILES)], ybuf.at[buf, k], sem.at[buf]).wait()
    route = route_ref[...]
    g1 = route[:, ROUTE_GATE:ROUTE_GATE + 1]
    g2 = route[:, ROUTE_GATE + 1:ROUTE_GATE + 2]
    hs = [None] * COL_CHUNKS
    ss = jnp.zeros((tm, 1), F32)
    for j in range(ROW_TILES):
        tile_row = pl.ds(j, tm, stride=ROW_TILES)
        first = _unpack_word(ybuf[buf, 0, tile_row, :])
        second = _unpack_word(ybuf[buf, 1, tile_row, :])
        for c, y1, y2 in ((j, first[0], second[0]), (j + ROW_TILES, first[1], second[1])):
            hc = h_ref[:, c * LANES:(c + 1) * LANES] + (y1 * g1 + y2 * g2)
            hs[c] = hc
            ss = ss + jnp.sum(hc * hc, axis=-1, keepdims=True)
    inv = lax.rsqrt(ss * (1.0 / D_MODEL) + NORM_EPS)
    for j, hj in enumerate(hs):
        cols = slice(j * LANES, (j + 1) * LANES)
        o_ref[:, cols] = hj * inv * nw_ref[:, cols]


def _combine(slots, y_slots, h, route, nw, T):
    tm = TM_COMBINE
    row = lambda i, s: (i, 0)
    grid_spec = pltpu.PrefetchScalarGridSpec(
        num_scalar_prefetch=1,
        grid=(T // tm,),
        in_specs=[pl.BlockSpec(memory_space=pl.ANY),
                  pl.BlockSpec((tm, D_MODEL), row),
                  pl.BlockSpec((tm, LANES), row),
                  pl.BlockSpec((1, D_MODEL), lambda i, s: (0, 0))],
        out_specs=pl.BlockSpec((tm, D_MODEL), row),
        scratch_shapes=[pltpu.VMEM((2, 2, tm * ROW_TILES, LANES), U32), pltpu.SemaphoreType.DMA((2,))],
    )
    return pl.pallas_call(
        functools.partial(_combine_body, T=T),
        grid_spec=grid_spec,
        out_shape=jax.ShapeDtypeStruct((T, D_MODEL), F32),
        compiler_params=_cparams(("arbitrary",)),
        name="moe_combine",
    )(slots, y_slots, h, route, nw)


def _rotary_tables(S):
    inv_freq = (1.0 / (np.float32(ROPE_THETA) ** (np.arange(0, HEAD_DIM, 2, dtype=np.float32) / HEAD_DIM))
                ).astype(np.float32)
    ang = np.arange(S, dtype=np.float32)[:, None] * inv_freq[None, :]
    cos, sin = np.cos(ang), np.sin(ang)
    reps = LANES // HEAD_DIM
    cos_t = np.tile(np.concatenate([cos, cos], axis=1), (1, reps)).astype(np.float32)
    sin_t = np.tile(np.concatenate([-sin, sin], axis=1), (1, reps)).astype(np.float32)
    return jnp.asarray(cos_t), jnp.asarray(sin_t)


def _layer(h_in, norm_mix_w, w_in, b_branch_gate, ret_decay_fwd, ret_decay_bwd, ret_gn_w, w_attn_branch,
           w_ret_branch, w_out, norm_moe_w, moe_w_group, moe_b_group, moe_w_expert, moe_b_expert,
           moe_w1, moe_w3, moe_w2, next_norm_w, B, S, cos_t, sin_t):
    T = B * S
    (qa0, ka0, va0, qa1, ka1, va1, qa2, ka2, va2, qr, kr, vr, gr, gates) = _in_projection(
        h_in, norm_mix_w[None, :], w_in.astype(BF16), b_branch_gate[None, :], cos_t, sin_t, B, S)

    unit = lambda a: a[:, :, None]
    y_att = _attention((unit(qa0), unit(ka0), unit(va0), qa1, ka1, va1, qa2, ka2, va2), B, S)

    dec = jnp.stack([ret_decay_fwd.reshape(RET_HEADS // 2, 2), ret_decay_bwd.reshape(RET_HEADS // 2, 2)], axis=1)
    dec = jnp.broadcast_to(dec.reshape(RET_HEADS // 2, 4, 1), (RET_HEADS // 2, 4, LANES)).astype(F32)
    y_ret = _retention(dec, qr, kr, vr, gr, ret_gn_w[None, :], B, S).reshape(T, RET_V_W)

    pad = LANES - MOE_GROUPS - MOE_N_EXPERTS
    w_r = jnp.concatenate([moe_w_group, moe_w_expert, jnp.zeros((D_MODEL, pad), F32)], axis=1)
    w_r_hi = w_r.astype(BF16)
    w_r_lo = (w_r - w_r_hi.astype(F32)).astype(BF16)
    b_r = jnp.concatenate([moe_b_group, moe_b_expert, jnp.zeros((pad,), F32)])[None, :]

    h_mid, hn, route, route_t, cnt = _merge_route(
        y_att, y_ret, gates, h_in, w_attn_branch.astype(BF16), w_ret_branch.astype(BF16),
        w_out.astype(BF16), norm_moe_w[None, :], w_r_hi, w_r_lo, b_r, T, S)

    bm = MOE_BM
    counts = cnt[0, ROUTER_EXPERT_LANE0:ROUTER_EXPERT_LANE0 + MOE_N_EXPERTS].astype(jnp.int32)
    nblk = (counts + bm - 1) // bm
    blk_end = jnp.cumsum(nblk)
    pstart = (blk_end - nblk) * bm
    n_blocks = (2 * T) // bm + MOE_N_EXPERTS
    n_active = blk_end[-1]
    bidx = jnp.minimum(jnp.arange(n_blocks, dtype=jnp.int32), n_active - 1)
    blk_eid = jnp.sum(bidx[:, None] >= blk_end[None, :], axis=1).astype(jnp.int32)
    mine = blk_eid[:, None] == jnp.arange(MOE_N_EXPERTS, dtype=jnp.int32)[None, :]
    seg_end = jnp.sum(jnp.where(mine, (pstart + counts)[None, :], 0), axis=1)
    blk_valid = jnp.clip(seg_end - bidx * bm, 0, bm)
    blk_valid = jnp.where(jnp.arange(n_blocks) < n_active, blk_valid, 0).astype(jnp.int32)
    blk_fresh = jnp.concatenate([jnp.ones((1,), jnp.int32), (blk_eid[1:] != blk_eid[:-1]).astype(jnp.int32)])
    ar = jnp.arange(MOE_N_EXPERTS, dtype=jnp.int32)
    later = jnp.min(jnp.where((nblk > 0)[None, :] & (ar[None, :] > ar[:, None]), ar[None, :], MOE_N_EXPERTS), axis=1)
    later = jnp.where(later < MOE_N_EXPERTS, later, -1)
    blk_next = (jnp.sum(jnp.where(mine, later[None, :] + 1, 0), axis=1) - 1).astype(jnp.int32)
    eid = route_t[ROUTE_EID:ROUTE_EID + 2].astype(jnp.int32)
    rank = route_t[ROUTE_RANK:ROUTE_RANK + 2].astype(jnp.int32)
    start = jnp.sum(jnp.where(eid[..., None] == jnp.arange(MOE_N_EXPERTS, dtype=jnp.int32),
                              pstart.astype(jnp.int32), 0), axis=-1)
    slots = (start + rank).reshape(2 * T)

    x_slots = _dispatch(slots, hn, n_blocks * bm, T)
    y_slots = _experts(bidx, blk_eid, blk_valid, blk_fresh, blk_next, x_slots, moe_w1, moe_w3, moe_w2, n_blocks)
    return _combine(slots, y_slots, h_mid, route, next_norm_w[None, :], T)


def kernel(x, norm_mix_w, w_in, b_branch_gate, ret_decay_fwd, ret_decay_bwd, ret_gn_w, w_attn_branch,
           w_ret_branch, w_out, norm_moe_w, moe_w_group, moe_b_group, moe_w_expert, moe_b_expert, moe_w1,
           moe_w3, moe_w2, norm_final_w):
    B, S, D = x.shape
    depth = norm_mix_w.shape[0]
    assert depth == 1, "the final norm is fused into the layer's combine stage"
    assert D == D_MODEL and S % TM_INPROJ == 0 and (B * S) < (1 << 24)
    cos_t, sin_t = _rotary_tables(S)
    out = _layer(x.reshape(B * S, D), norm_mix_w[0], w_in[0], b_branch_gate[0], ret_decay_fwd[0],
                 ret_decay_bwd[0], ret_gn_w[0], w_attn_branch[0], w_ret_branch[0], w_out[0], norm_moe_w[0],
                 moe_w_group[0], moe_b_group[0], moe_w_expert[0], moe_b_expert[0], moe_w1[0], moe_w3[0],
                 moe_w2[0], norm_final_w, B, S, cos_t, sin_t)
    return out.reshape(B, S, D)
```

```python
import functools

import numpy as np
import jax
import jax.numpy as jnp
from jax import lax
from jax.experimental import pallas as pl
from jax.experimental.pallas import tpu as pltpu

F32 = jnp.float32
BF16 = jnp.bfloat16

D_MODEL = 1024
HEAD_DIM = 64
ATTN_PAIRS = ((128, 1), (512, 4), (2048, 16))
ATTN_HEADS_PER_GROUP = 8
ATTN_GROUP_W = ATTN_HEADS_PER_GROUP * HEAD_DIM
ATTN_HALF = 64
ROPE_THETA = 10000.0
RET_HEADS = 8
RET_QK_DIM = 64
RET_V_DIM = 128
RET_CHUNK = 128
RET_QK_W = RET_HEADS * RET_QK_DIM
RET_V_W = RET_HEADS * RET_V_DIM
MOE_GROUPS = 8
MOE_EXPERTS_PER_GROUP = 8
MOE_N_EXPERTS = MOE_GROUPS * MOE_EXPERTS_PER_GROUP
MOE_HIDDEN = 512
NORM_EPS = 1e-6

LANES = 128
COL_CHUNKS = D_MODEL // LANES
ROW_TILES = COL_CHUNKS // 2
U32 = jnp.uint32
NEG_BIG = -1e30
LOG2_E = 1.4426950408889634

TM_INPROJ = 512
TM_MERGE = 1024
MERGE_SPLIT = 4
MERGE_COL_CHUNKS = 4
TM_COMBINE = 512
MOE_BM = 512
DISPATCH_CHUNK = 4096
ATTN_QB = 128
ATTN_UNROLL = 8
RET_UNROLL = 8

VMEM_LIMIT = 56 * 1024 * 1024

_A = 3 * ATTN_GROUP_W
OFF_QA, OFF_KA, OFF_VA = 0, _A, 2 * _A
OFF_QR = 3 * _A
OFF_KR = OFF_QR + RET_QK_W
OFF_VR = OFF_KR + RET_QK_W
OFF_GR = OFF_VR + RET_V_W
OFF_GL = OFF_GR + RET_V_W
IN_W = OFF_GL + 2 * D_MODEL


def _cparams(sem, vmem=VMEM_LIMIT):
    return pltpu.CompilerParams(dimension_semantics=sem, vmem_limit_bytes=vmem)


def _inproj_body(x_ref, nw_ref, w_ref, bg_ref, cos_ref, sin_ref,
                 qa0, ka0, va0, qa1, ka1, va1, qa2, ka2, va2, qr, kr, vr, gr, gt,
                 stage_ref):
    tm = x_ref.shape[0]
    x = x_ref[...]
    ms = jnp.mean(x * x, axis=-1, keepdims=True)
    xn = (x * lax.rsqrt(ms + NORM_EPS) * nw_ref[...]).astype(BF16)
    cos = cos_ref[...]
    sin = sin_ref[...]
    lane = lax.broadcasted_iota(jnp.int32, (tm, LANES), 1)
    first_half = (lane & (HEAD_DIM - 1)) < (HEAD_DIM // 2)

    def proj(c0, width):
        return jnp.dot(xn, w_ref[:, c0:c0 + width], preferred_element_type=F32)

    def rotary(a, scale):
        partner = jnp.where(first_half, pltpu.roll(a, LANES - HEAD_DIM // 2, 1),
                            pltpu.roll(a, HEAD_DIM // 2, 1))
        r = a * cos + partner * sin
        return r * scale if scale != 1.0 else r

    def chunks(acc):
        return [acc[:, c * LANES:(c + 1) * LANES] for c in range(acc.shape[1] // LANES)]

    def store_natural(out_ref, acc, fn):
        for c, a in enumerate(chunks(acc)):
            out_ref[:, c * LANES:(c + 1) * LANES] = fn(a).astype(out_ref.dtype)

    def store_pairs(out_ref, acc, fn):
        for c, a in enumerate(chunks(acc)):
            out_ref[0, c] = fn(a).astype(out_ref.dtype)

    def store_strided(out_ref, acc, fn, d):
        for c, a in enumerate(chunks(acc)):
            stage_ref[c] = fn(a)
        for c in range(acc.shape[1] // LANES):
            for r in range(d):
                out_ref[0, c, r] = stage_ref[c, pl.ds(r, tm // d, stride=d), :].astype(out_ref.dtype)

    ident = lambda a: a
    rot_q = lambda a: rotary(a, HEAD_DIM ** -0.5 * LOG2_E)
    rot_1 = lambda a: rotary(a, 1.0)
    rot_k = lambda a: rotary(a, RET_QK_DIM ** -0.5)

    W = ATTN_GROUP_W
    store_pairs(qa0, proj(OFF_QA, W), rot_q)
    store_pairs(ka0, proj(OFF_KA, W), rot_1)
    store_pairs(va0, proj(OFF_VA, W), ident)
    for g, (qo, ko, vo) in ((1, (qa1, ka1, va1)), (2, (qa2, ka2, va2))):
        d = ATTN_PAIRS[g][1]
        store_strided(qo, proj(OFF_QA + g * W, W), rot_q, d)
        store_strided(ko, proj(OFF_KA + g * W, W), rot_1, d)
        store_strided(vo, proj(OFF_VA + g * W, W), ident, d)
    store_natural(qr, proj(OFF_QR, RET_QK_W), rot_1)
    store_natural(kr, proj(OFF_KR, RET_QK_W), rot_k)
    for h in range(RET_V_W // W):
        vr[:, h * W:(h + 1) * W] = proj(OFF_VR + h * W, W).astype(vr.dtype)
        gr[:, h * W:(h + 1) * W] = proj(OFF_GR + h * W, W).astype(gr.dtype)
    for h in range(2 * D_MODEL // W):
        z = proj(OFF_GL + h * W, W) + bg_ref[:, h * W:(h + 1) * W]
        gt[:, h * W:(h + 1) * W] = jax.nn.sigmoid(z).astype(gt.dtype)


def _in_projection(x2, norm_w, w_bf, b_gate, cos_t, sin_t, B, S):
    T = B * S
    tm = TM_INPROJ
    nt = S // tm
    W = ATTN_GROUP_W
    row = lambda i: (i, 0)
    const = lambda i: (0, 0)
    nat = lambda width: pl.BlockSpec((tm, width), row)

    P = W // LANES

    def strided_spec(d):
        return pl.BlockSpec((1, P, d, tm // d, LANES), lambda i: (i // nt, 0, 0, i % nt, 0))

    def strided_shape(d):
        return jax.ShapeDtypeStruct((B, P, d, S // d, LANES), BF16)

    pair_spec = pl.BlockSpec((1, P, tm, LANES), lambda i: (i // nt, 0, i % nt, 0))
    pair_shape = jax.ShapeDtypeStruct((B, P, S, LANES), BF16)
    nat_shape = lambda width: jax.ShapeDtypeStruct((T, width), BF16)
    d1, d2 = ATTN_PAIRS[1][1], ATTN_PAIRS[2][1]
    out_shape = ([pair_shape] * 3 + [strided_shape(d1)] * 3 + [strided_shape(d2)] * 3
                 + [nat_shape(RET_QK_W)] * 2 + [nat_shape(RET_V_W)] * 2 + [nat_shape(2 * D_MODEL)])
    out_specs = ([pair_spec] * 3 + [strided_spec(d1)] * 3 + [strided_spec(d2)] * 3
                 + [nat(RET_QK_W)] * 2 + [nat(RET_V_W)] * 2 + [nat(2 * D_MODEL)])
    in_specs = [
        pl.BlockSpec((tm, D_MODEL), row),
        pl.BlockSpec((1, D_MODEL), const),
        pl.BlockSpec((D_MODEL, IN_W), const, pipeline_mode=pl.Buffered(1)),
        pl.BlockSpec((1, 2 * D_MODEL), const),
        pl.BlockSpec((tm, LANES), lambda i: (i % nt, 0)),
        pl.BlockSpec((tm, LANES), lambda i: (i % nt, 0)),
    ]
    return pl.pallas_call(
        _inproj_body,
        grid=(T // tm,),
        in_specs=in_specs,
        out_specs=out_specs,
        out_shape=out_shape,
        scratch_shapes=[pltpu.VMEM((W // LANES, tm, LANES), F32)],
        compiler_params=_cparams(("parallel",)),
        name="in_projection",
    )(x2, norm_w, w_bf, b_gate, cos_t, sin_t)


def _attn_body(q0_ref, k0_ref, v0_ref, q1_ref, k1_ref, v1_ref, q2_ref, k2_ref, v2_ref, o_ref,
               part_ref, bias_ref, *, S):
    QB, H = ATTN_QB, ATTN_HALF
    lane = lax.broadcasted_iota(jnp.int32, (QB, LANES), 1)
    head0 = lane < HEAD_DIM

    qi = lax.broadcasted_iota(jnp.int32, (QB, QB + 2 * H), 0)
    ki = lax.broadcasted_iota(jnp.int32, (QB, QB + 2 * H), 1)
    for n in range(3):
        bias_ref[n] = jnp.where(jnp.abs(ki - qi - n * H) <= H, 0.0, NEG_BIG).astype(F32)

    def scores(q_rows, k_rows, bias, h):
        qm = jnp.where(head0 if h == 0 else jnp.logical_not(head0), q_rows, jnp.zeros_like(q_rows))
        return lax.dot_general(qm, k_rows, (((1,), (1,)), ((), ())), preferred_element_type=F32) + bias

    def weights(s):
        m = jnp.max(s, axis=-1, keepdims=True)
        return m, jnp.exp2(s - m).astype(BF16)

    def heads_to_lanes(m0, a, m1, b):
        num = jnp.where(head0, a, b)
        den = pltpu.roll(jnp.where(head0, b, a), HEAD_DIM, 1)
        mx = jnp.where(head0, m0, m1)
        return num, den, mx

    def run_group(q_ref, k_ref, v_ref, d, prepare, store):
        L = S // d
        KW = min(L, QB + 2 * H)
        nb = L // QB

        key_head0 = lax.broadcasted_iota(jnp.int32, (KW, LANES), 1) < HEAD_DIM
        key_ones = jnp.ones((KW, LANES), BF16)

        def trip(i, carry):
            blocks = []
            for u in range(ATTN_UNROLL):
                t = i * ATTN_UNROLL + u
                r = t // nb
                q0 = pl.multiple_of((t % nb) * QB, QB)
                ws = pl.multiple_of(jnp.clip(q0 - H, 0, L - KW), H)
                bias = bias_ref[(q0 - ws) // H][:, :KW]
                q_rows = q_ref[0, 0, r, pl.ds(q0, QB), :]
                k_rows = k_ref[0, 0, r, pl.ds(ws, KW), :]
                blocks.append((r, q0, ws, [scores(q_rows, k_rows, bias, h) for h in range(2)]))
            blocks = [(r, q0, ws, [weights(s) for s in ss]) for r, q0, ws, ss in blocks]
            done = []
            for r, q0, ws, ((m0, p0), (m1, p1)) in blocks:
                v_rows = v_ref[0, 0, r, pl.ds(ws, KW), :]
                a = jnp.dot(p0, jnp.where(key_head0, v_rows, key_ones), preferred_element_type=F32)
                b = jnp.dot(p1, jnp.where(key_head0, key_ones, v_rows), preferred_element_type=F32)
                done.append((r, q0, m0, a, m1, b))
            done = [(r, q0, prepare(q0, *heads_to_lanes(m0, a, m1, b))) for r, q0, m0, a, m1, b in done]
            for r, q0, vals in done:
                store(r, q0, vals)
            return carry

        lax.fori_loop(0, S // QB // ATTN_UNROLL, trip, 0)

    def store_partial(g):
        d = ATTN_PAIRS[g][1]

        def store(r, q0, vals):
            rows = pl.ds(r + q0 * d, QB, stride=d)
            for n, val in enumerate(vals):
                part_ref[3 * (g - 1) + n, rows, :] = val
        return store

    keep = lambda q0, num, den, mx: (num, den, mx)
    run_group(q1_ref, k1_ref, v1_ref, ATTN_PAIRS[1][1], keep, store_partial(1))
    run_group(q2_ref, k2_ref, v2_ref, ATTN_PAIRS[2][1], keep, store_partial(2))

    def merge(q0, num, den, mx):
        rows = pl.ds(q0, QB)
        nums = [num, part_ref[0, rows, :], part_ref[3, rows, :]]
        dens = [den, part_ref[1, rows, :], part_ref[4, rows, :]]
        mxs = [mx, part_ref[2, rows, :], part_ref[5, rows, :]]
        top = jnp.maximum(jnp.maximum(mxs[0], mxs[1]), mxs[2])
        ws = [jnp.exp2(m - top) for m in mxs]
        n = ws[0] * nums[0] + ws[1] * nums[1] + ws[2] * nums[2]
        dn = ws[0] * dens[0] + ws[1] * dens[1] + ws[2] * dens[2]
        return (n / dn).astype(o_ref.dtype)

    def store_out(r, q0, y):
        o_ref[0, 0, pl.ds(q0, QB), :] = y

    run_group(q0_ref, k0_ref, v0_ref, ATTN_PAIRS[0][1], merge, store_out)


def _attention(qkv, B, S):
    P = ATTN_GROUP_W // LANES
    in_specs = []
    for _, d in ATTN_PAIRS:
        in_specs += [pl.BlockSpec((1, 1, d, S // d, LANES), lambda b, p: (b, p, 0, 0, 0))] * 3
    return pl.pallas_call(
        functools.partial(_attn_body, S=S),
        grid=(B, P),
        in_specs=in_specs,
        out_specs=pl.BlockSpec((1, 1, S, LANES), lambda b, p: (b, p, 0, 0)),
        out_shape=jax.ShapeDtypeStruct((B, P, S, LANES), BF16),
        scratch_shapes=[pltpu.VMEM((6, S, LANES), F32),
                        pltpu.VMEM((3, ATTN_QB, ATTN_QB + 2 * ATTN_HALF), F32)],
        compiler_params=_cparams(("parallel", "parallel")),
        name="attention",
    )(*qkv)


def _log_sigmoid(z):
    return jnp.minimum(z, 0.0) - jnp.log(1.0 + jnp.exp(-jnp.abs(z)))


def _ret_body(dec_ref, q_ref, k_ref, v_ref, g_ref, gnw_ref, o_ref, kt_ref, sf_ref, sb_ref, st_ref, *, S):
    C = RET_CHUNK
    nc = S // C
    lg = _log_sigmoid(dec_ref[0])
    a_row = lax.broadcasted_iota(jnp.int32, (C, LANES), 0).astype(F32)
    lane = lax.broadcasted_iota(jnp.int32, (C, LANES), 1)
    rel = (lax.broadcasted_iota(jnp.int32, (C, C), 0) - lax.broadcasted_iota(jnp.int32, (C, C), 1)).astype(F32)

    heads = []
    for h in range(2):
        lgf = lg[h:h + 1, :]
        lgb = lg[2 + h:3 + h, :]
        in_head = (lane < RET_QK_DIM) if h == 0 else (lane >= RET_QK_DIM)
        heads.append(dict(
            in_head=in_head,
            xi_f=jnp.where(in_head, jnp.exp((a_row + 1.0) * lgf), 0.0),
            xi_b=jnp.where(in_head, jnp.exp((C - a_row) * lgb), 0.0),
            zeta_f=jnp.exp((C - 1.0 - a_row) * lgf),
            zeta_b=jnp.exp(a_row * lgb),
            dloc=jnp.where(rel > 0, jnp.exp(rel * lgf), jnp.where(rel < 0, jnp.exp(-rel * lgb), 2.0)),
            cd_f=jnp.exp(C * lgf),
            cd_b=jnp.exp(C * lgb),
        ))

    def chunk_rows(n):
        return pl.ds(pl.multiple_of(n * C, C), C)

    def v_head(rows, h):
        return v_ref[0, rows, h * RET_V_DIM:(h + 1) * RET_V_DIM]

    def products(i, carry):
        items = []
        for u in range(RET_UNROLL):
            n = i * RET_UNROLL + u
            rows = chunk_rows(n)
            kt = k_ref[0, rows, :].astype(F32).T.astype(BF16)
            kt_ref[n] = kt
            for h, hd in enumerate(heads):
                vh = v_head(rows, h).astype(F32)
                items.append((n, h, kt, (vh * hd["zeta_f"]).astype(BF16), (vh * hd["zeta_b"]).astype(BF16)))
        outs = [(n, h, jnp.dot(kt, vf, preferred_element_type=F32), jnp.dot(kt, vb, preferred_element_type=F32))
                for n, h, kt, vf, vb in items]
        for n, h, f, b in outs:
            sf_ref[n, h] = f
            sb_ref[n, h] = b
        return carry

    lax.fori_loop(0, nc // RET_UNROLL, products, 0)

    def scan(ref, key, order, half):
        rows = slice(half * 2 * RET_QK_DIM, (half + 1) * 2 * RET_QK_DIM)

        def step(i, state):
            n = order(i)
            new = []
            for h, hd in enumerate(heads):
                st_ref[n, h, rows, :] = state[h].astype(BF16)
                new.append(hd[key] * state[h] + ref[n, h])
            return tuple(new)
        zero = jnp.zeros(ref.shape[2:], F32)
        lax.fori_loop(0, nc, step, (zero, zero))

    scan(sf_ref, "cd_f", lambda i: i, 0)
    scan(sb_ref, "cd_b", lambda i: nc - 1 - i, 1)

    def outputs(i, carry):
        items = []
        for u in range(RET_UNROLL):
            n = i * RET_UNROLL + u
            rows = chunk_rows(n)
            qp = q_ref[0, rows, :]
            qf = qp.astype(F32)
            kt = kt_ref[n]
            for h, hd in enumerate(heads):
                qm = jnp.where(hd["in_head"], qp, jnp.zeros_like(qp))
                qx = jnp.concatenate([(qf * hd["xi_f"]).astype(BF16), (qf * hd["xi_b"]).astype(BF16)], axis=1)
                items.append((rows, h, hd, jnp.dot(qm, kt, preferred_element_type=F32),
                              jnp.dot(qx, st_ref[n, h], preferred_element_type=F32)))
        items = [(rows, h, hd, (s * hd["dloc"]).astype(BF16), cross) for rows, h, hd, s, cross in items]
        items = [(rows, h, cross + jnp.dot(p, v_head(rows, h), preferred_element_type=F32))
                 for rows, h, hd, p, cross in items]
        for rows, h, ret in items:
            mu = jnp.mean(ret, axis=-1, keepdims=True)
            xc = ret - mu
            var = jnp.mean(xc * xc, axis=-1, keepdims=True)
            cols = slice(h * RET_V_DIM, (h + 1) * RET_V_DIM)
            gate = g_ref[0, rows, cols].astype(F32)
            y = xc * lax.rsqrt(var + NORM_EPS) * gnw_ref[:, cols] * (gate * jax.nn.sigmoid(gate))
            o_ref[0, rows, cols] = y.astype(o_ref.dtype)
        return carry

    lax.fori_loop(0, nc // RET_UNROLL, outputs, 0)


def _retention(dec, qr, kr, vr, gr, gn_w, B, S):
    nc = S // RET_CHUNK
    npairs = RET_HEADS // 2
    qk_spec = pl.BlockSpec((1, S, 2 * RET_QK_DIM), lambda b, p: (b, 0, p))
    v_spec = pl.BlockSpec((1, S, 2 * RET_V_DIM), lambda b, p: (b, 0, p))
    return pl.pallas_call(
        functools.partial(_ret_body, S=S),
        grid=(B, npairs),
        in_specs=[pl.BlockSpec((1, 4, LANES), lambda b, p: (p, 0, 0)),
                  qk_spec, qk_spec, v_spec, v_spec,
                  pl.BlockSpec((1, 2 * RET_V_DIM), lambda b, p: (0, p))],
        out_specs=v_spec,
        out_shape=jax.ShapeDtypeStruct((B, S, RET_V_W), BF16),
        scratch_shapes=[pltpu.VMEM((nc, 2 * RET_QK_DIM, RET_CHUNK), BF16),
                        pltpu.VMEM((nc, 2, 2 * RET_QK_DIM, RET_V_DIM), F32),
                        pltpu.VMEM((nc, 2, 2 * RET_QK_DIM, RET_V_DIM), F32),
                        pltpu.VMEM((nc, 2, 4 * RET_QK_DIM, RET_V_DIM), BF16)],
        compiler_params=_cparams(("parallel", "parallel")),
        name="retention",
    )(dec, qr.reshape(B, S, RET_QK_W), kr.reshape(B, S, RET_QK_W),
      vr.reshape(B, S, RET_V_W), gr.reshape(B, S, RET_V_W), gn_w)


ROUTE_EID, ROUTE_RANK, ROUTE_GATE = 0, 2, 4
ROUTE_FIELDS = 8
ROUTER_EXPERT_LANE0 = MOE_GROUPS


def _merge_body(yatt_ref, yret_ref, gt_ref, x_ref, wa_ref, wb_ref, wo_ref,
                nw_ref, wrh_ref, wrl_ref, br_ref,
                h_ref, hn_ref, route_ref, route_t_ref, cnt_ref):
    tm = x_ref.shape[0]
    hm = tm // MERGE_SPLIT
    i = pl.program_id(0)
    cw = D_MODEL // MERGE_COL_CHUNKS

    def branch_products(rows):
        y_att = jnp.concatenate([yatt_ref[0, c, rows, :] for c in range(yatt_ref.shape[1])], axis=1)
        y_ret = yret_ref[rows, :]
        chunks = []
        for c in range(MERGE_COL_CHUNKS):
            cols = slice(c * cw, (c + 1) * cw)
            a = jnp.dot(y_att, wa_ref[:, cols], preferred_element_type=F32)
            b = jnp.dot(y_ret, wb_ref[:, cols], preferred_element_type=F32)
            g_att = gt_ref[rows, c * cw:(c + 1) * cw].astype(F32)
            g_ret = gt_ref[rows, D_MODEL + c * cw:D_MODEL + (c + 1) * cw].astype(F32)
            chunks.append((g_att * a + g_ret * b).astype(BF16))
        return jnp.concatenate(chunks, axis=1)

    def residual_norm(hf, rows, merged):
        mix = jnp.dot(merged, wo_ref[...], preferred_element_type=F32)
        h = x_ref[rows, :] + mix
        h_ref[rows, :] = h
        ms = jnp.mean(h * h, axis=-1, keepdims=True)
        hn = h * lax.rsqrt(ms + NORM_EPS) * nw_ref[...]
        for j, word in enumerate(_pack_row(hn)):
            hn_ref[pl.ds(hf * hm * ROW_TILES + j, hm, stride=ROW_TILES), :] = word
        hi = hn.astype(BF16)
        lo = (hn - hi.astype(F32)).astype(BF16)
        return hi, lo

    def router_logits(hi, lo):
        return (jnp.dot(hi, wrh_ref[...], preferred_element_type=F32)
                + jnp.dot(hi, wrl_ref[...], preferred_element_type=F32)
                + jnp.dot(lo, wrh_ref[...], preferred_element_type=F32)) + br_ref[...]

    lane = lax.broadcasted_iota(jnp.int32, (hm, LANES), 1)
    far = jnp.int32(4 * LANES)

    def first_argmax(vals, vmax):
        return jnp.min(jnp.where(vals == vmax, lane, far), axis=-1, keepdims=True)

    def route(logits):
        is_group = lane < MOE_GROUPS
        gl = jnp.where(is_group, logits, NEG_BIG)
        gmax = jnp.max(gl, axis=-1, keepdims=True)
        g_w = 1.0 / jnp.sum(jnp.where(is_group, jnp.exp(gl - gmax), 0.0), axis=-1, keepdims=True)
        g_idx = first_argmax(gl, gmax)
        e_lane = lane - ROUTER_EXPERT_LANE0
        in_group = (e_lane >= 0) & (e_lane < MOE_N_EXPERTS) & (jnp.right_shift(e_lane, 3) == g_idx)
        el = jnp.where(in_group, logits, NEG_BIG)
        m1 = jnp.max(el, axis=-1, keepdims=True)
        i1 = first_argmax(el, m1)
        el2 = jnp.where(lane == i1, NEG_BIG, el)
        m2 = jnp.max(el2, axis=-1, keepdims=True)
        i2 = first_argmax(el2, m2)
        ex = jnp.exp(m2 - m1)
        return i1, i2, g_w / (1.0 + ex), g_w * ex / (1.0 + ex)

    rows = [slice(hf * hm, (hf + 1) * hm) for hf in range(MERGE_SPLIT)]
    merged = [branch_products(r) for r in rows]
    split = [residual_norm(hf, r, m) for hf, (r, m) in enumerate(zip(rows, merged))]
    routed = [route(router_logits(hi, lo)) for hi, lo in split]

    @pl.when(i == 0)
    def _():
        cnt_ref[...] = jnp.zeros(cnt_ref.shape, F32)

    r_idx = lax.broadcasted_iota(jnp.int32, (hm, hm), 0)
    c_idx = lax.broadcasted_iota(jnp.int32, (hm, hm), 1)
    lower = jnp.where(c_idx < r_idx, 1.0, 0.0).astype(BF16)
    running = cnt_ref[...]
    for hf, (i1, i2, gate1, gate2) in enumerate(routed):
        hot1 = lane == i1
        hot2 = lane == i2
        onehot = jnp.where(hot1 | hot2, 1.0, 0.0)
        before = jnp.dot(lower, onehot.astype(BF16), preferred_element_type=F32) + running
        rank1 = jnp.sum(jnp.where(hot1, before, 0.0), axis=-1, keepdims=True)
        rank2 = jnp.sum(jnp.where(hot2, before, 0.0), axis=-1, keepdims=True)
        running = running + jnp.sum(onehot, axis=0, keepdims=True)
        rec = jnp.zeros((hm, LANES), F32)
        for pos, val in ((ROUTE_EID, (i1 - ROUTER_EXPERT_LANE0).astype(F32)),
                         (ROUTE_EID + 1, (i2 - ROUTER_EXPERT_LANE0).astype(F32)),
                         (ROUTE_RANK, rank1), (ROUTE_RANK + 1, rank2),
                         (ROUTE_GATE, gate1), (ROUTE_GATE + 1, gate2)):
            rec = jnp.where(lane == pos, val, rec)
        route_ref[rows[hf], :] = rec
        route_t_ref[:, hf * hm:(hf + 1) * hm] = rec.T[:route_t_ref.shape[0], :]
    cnt_ref[...] = running


def _merge_route(y_att, y_ret, gates, x2, wa, wb, wo, nw, wr_hi, wr_lo, b_r, T, S):
    tm = TM_MERGE
    nt = S // tm
    row = lambda i: (i, 0)
    const = lambda i: (0, 0)
    full = lambda arr: pl.BlockSpec(arr.shape, const)
    in_specs = ([pl.BlockSpec((1, y_att.shape[1], tm, LANES), lambda i: (i // nt, 0, i % nt, 0)),
                 pl.BlockSpec((tm, RET_V_W), row), pl.BlockSpec((tm, 2 * D_MODEL), row),
                 pl.BlockSpec((tm, D_MODEL), row),
                 full(wa), full(wb), full(wo), full(nw), full(wr_hi), full(wr_lo), full(b_r)])
    return pl.pallas_call(
        _merge_body,
        grid=(T // tm,),
        in_specs=in_specs,
        out_specs=[pl.BlockSpec((tm, D_MODEL), row),
                   pl.BlockSpec((tm * ROW_TILES, LANES), row),
                   pl.BlockSpec((tm, LANES), row), pl.BlockSpec((ROUTE_FIELDS, tm), lambda i: (0, i)),
                   pl.BlockSpec((1, LANES), const)],
        out_shape=[jax.ShapeDtypeStruct((T, D_MODEL), F32), jax.ShapeDtypeStruct((T * ROW_TILES, LANES), U32),
                   jax.ShapeDtypeStruct((T, LANES), F32), jax.ShapeDtypeStruct((ROUTE_FIELDS, T), F32),
                   jax.ShapeDtypeStruct((1, LANES), F32)],
        compiler_params=_cparams(("arbitrary",)),
        name="merge_route",
    )(y_att, y_ret, gates, x2, wa, wb, wo, nw, wr_hi, wr_lo, b_r)


ISSUE_UNROLL = 8


def _tile_rows(n):
    return pl.ds(pl.multiple_of(n * ROW_TILES, ROW_TILES), ROW_TILES)


def _pack_row(x):
    bits = lambda c: pltpu.bitcast(x[:, c * LANES:(c + 1) * LANES].astype(BF16).astype(F32), U32)
    return [(bits(j) >> 16) | bits(j + ROW_TILES) for j in range(ROW_TILES)]


def _unpack_word(w):
    return pltpu.bitcast(w << 16, F32), pltpu.bitcast(w & jnp.uint32(0xFFFF0000), F32)


def _dispatch_body(slot_ref, hn_ref, xs_ref, sem, *, T):
    i = pl.program_id(0)
    ch = hn_ref.shape[0] // ROW_TILES

    def row_copy(j, slot):
        return pltpu.make_async_copy(hn_ref.at[_tile_rows(j)], xs_ref.at[_tile_rows(slot)], sem)

    def issue(j, carry):
        t = i * ch + j
        row_copy(j, slot_ref[t]).start(priority=0)
        row_copy(j, slot_ref[T + t]).start(priority=1)
        return carry

    lax.fori_loop(0, ch, issue, 0, unroll=ISSUE_UNROLL)
    for _ in range(2):
        pltpu.make_async_copy(hn_ref, xs_ref.at[pl.ds(0, ch * ROW_TILES)], sem).wait()


def _dispatch(slots, hn, n_slots, T):
    ch = DISPATCH_CHUNK
    grid_spec = pltpu.PrefetchScalarGridSpec(
        num_scalar_prefetch=1,
        grid=(T // ch,),
        in_specs=[pl.BlockSpec((ch * ROW_TILES, LANES), lambda i, s: (i, 0))],
        out_specs=pl.BlockSpec(memory_space=pl.ANY),
        scratch_shapes=[pltpu.SemaphoreType.DMA(())],
    )
    return pl.pallas_call(
        functools.partial(_dispatch_body, T=T),
        grid_spec=grid_spec,
        out_shape=jax.ShapeDtypeStruct((n_slots * ROW_TILES, LANES), U32),
        compiler_params=_cparams(("arbitrary",)),
        name="moe_dispatch",
    )(slots, hn)


def _expert_body(blk_ref, eid_ref, valid_ref, fresh_ref, next_ref, x_ref, w1_ref, w3_ref, w2_ref, y_ref,
                 w1b, w3b, w2b, w1s, w3s, w2s, wsem):
    i = pl.program_id(0)
    valid = valid_ref[i]

    def weight_copies(e):
        return (pltpu.make_async_copy(w1_ref.at[e], w1s, wsem.at[0]),
                pltpu.make_async_copy(w3_ref.at[e], w3s, wsem.at[1]),
                pltpu.make_async_copy(w2_ref.at[e], w2s, wsem.at[2]))

    @pl.when(i == 0)
    def _():
        for cp in weight_copies(eid_ref[0]):
            cp.start()

    @pl.when(valid > 0)
    def _():
        @pl.when(fresh_ref[i] == 1)
        def _():
            for cp in weight_copies(eid_ref[i]):
                cp.wait()
            w1b[...] = w1s[...].astype(BF16)
            w3b[...] = w3s[...].astype(BF16)
            w2b[...] = w2s[...].astype(BF16)

            @pl.when(next_ref[i] >= 0)
            def _():
                for cp in weight_copies(next_ref[i]):
                    cp.start()

        bm = x_ref.shape[0] // ROW_TILES
        live =lax.broadcasted_iota(jnp.int32, (bm, LANES), 0) < valid
        halves = [_unpack_word(x_ref[pl.ds(j, bm, stride=ROW_TILES), :]) for j in range(ROW_TILES)]
        x = jnp.concatenate(
            [jnp.where(live, c, 0.0).astype(BF16) for c in [lo for lo, _ in halves] + [hi for _, hi in halves]],
            axis=1)
        a = jnp.dot(x, w1b[...], preferred_element_type=F32)
        b = jnp.dot(x, w3b[...], preferred_element_type=F32)
        hid = (a * jax.nn.sigmoid(a) * b).astype(BF16)
        y = jnp.dot(hid, w2b[...], preferred_element_type=F32)
        for j, word in enumerate(_pack_row(y)):
            y_ref[pl.ds(j, bm, stride=ROW_TILES), :] = word


def _experts(blk, blk_eid, blk_valid, blk_fresh, blk_next, x_slots, w1, w3, w2, n_blocks):
    bm = MOE_BM
    slot_block = lambda i, blk, eid, val, fr, nx: (blk[i], 0)
    grid_spec = pltpu.PrefetchScalarGridSpec(
        num_scalar_prefetch=5,
        grid=(n_blocks,),
        in_specs=[pl.BlockSpec((bm * ROW_TILES, LANES), slot_block),
                  pl.BlockSpec(memory_space=pl.ANY), pl.BlockSpec(memory_space=pl.ANY),
                  pl.BlockSpec(memory_space=pl.ANY)],
        out_specs=pl.BlockSpec((bm * ROW_TILES, LANES), slot_block),
        scratch_shapes=[pltpu.VMEM((D_MODEL, MOE_HIDDEN), BF16), pltpu.VMEM((D_MODEL, MOE_HIDDEN), BF16),
                        pltpu.VMEM((MOE_HIDDEN, D_MODEL), BF16),
                        pltpu.VMEM((D_MODEL, MOE_HIDDEN), F32), pltpu.VMEM((D_MODEL, MOE_HIDDEN), F32),
                        pltpu.VMEM((MOE_HIDDEN, D_MODEL), F32), pltpu.SemaphoreType.DMA((3,))],
    )
    return pl.pallas_call(
        _expert_body,
        grid_spec=grid_spec,
        out_shape=jax.ShapeDtypeStruct(x_slots.shape, U32),
        compiler_params=_cparams(("arbitrary",)),
        name="moe_experts",
    )(blk, blk_eid, blk_valid, blk_fresh, blk_next, x_slots, w1, w3, w2)


def _combine_body(slot_ref, ys_ref, h_ref, route_ref, nw_ref, o_ref, ybuf, sem, *, T):
    i = pl.program_id(0)
    n = pl.num_programs(0)
    tm = h_ref.shape[0]

    def row_copy(slot, buf, k, j):
        return pltpu.make_async_copy(ys_ref.at[_tile_rows(slot)], ybuf.at[buf, k, _tile_rows(j)], sem.at[buf])

    def issue(tile, buf):
        def one(j, carry):
            t = tile * tm + j
            row_copy(slot_ref[t], buf, 0, j).start(priority=0)
            row_copy(slot_ref[T + t], buf, 1, j).start(priority=1)
            return carry
        lax.fori_loop(0, tm, one, 0, unroll=ISSUE_UNROLL)

    @pl.when(i == 0)
    def _():
        issue(0, 0)

    @pl.when(i + 1 < n)
    def _():
        issue(i + 1, (i + 1) % 2)

    buf = i % 2
    for k in range(2):
        pltpu.make_async_copy(ys_ref.at[pl.ds(0, tm * ROW_TILES)], ybuf.at[buf, k], sem.at[buf]).wait()
    route = route_ref[...]
    g1 = route[:, ROUTE_GATE:ROUTE_GATE + 1]
    g2 = route[:, ROUTE_GATE + 1:ROUTE_GATE + 2]
    hs = [None] * COL_CHUNKS
    ss = jnp.zeros((tm, 1), F32)
    for j in range(ROW_TILES):
        tile_row = pl.ds(j, tm, stride=ROW_TILES)
        first = _unpack_word(ybuf[buf, 0, tile_row, :])
        second = _unpack_word(ybuf[buf, 1, tile_row, :])
        for c, y1, y2 in ((j, first[0], second[0]), (j + ROW_TILES, first[1], second[1])):
            hc = h_ref[:, c * LANES:(c + 1) * LANES] + (y1 * g1 + y2 * g2)
            hs[c] = hc
            ss = ss + jnp.sum(hc * hc, axis=-1, keepdims=True)
    inv = lax.rsqrt(ss * (1.0 / D_MODEL) + NORM_EPS)
    for j, hj in enumerate(hs):
        cols = slice(j * LANES, (j + 1) * LANES)
        o_ref[:, cols] = hj * inv * nw_ref[:, cols]


def _combine(slots, y_slots, h, route, nw, T):
    tm = TM_COMBINE
    row = lambda i, s: (i, 0)
    grid_spec = pltpu.PrefetchScalarGridSpec(
        num_scalar_prefetch=1,
        grid=(T // tm,),
        in_specs=[pl.BlockSpec(memory_space=pl.ANY),
                  pl.BlockSpec((tm, D_MODEL), row),
                  pl.BlockSpec((tm, LANES), row),
                  pl.BlockSpec((1, D_MODEL), lambda i, s: (0, 0))],
        out_specs=pl.BlockSpec((tm, D_MODEL), row),
        scratch_shapes=[pltpu.VMEM((2, 2, tm * ROW_TILES, LANES), U32), pltpu.SemaphoreType.DMA((2,))],
    )
    return pl.pallas_call(
        functools.partial(_combine_body, T=T),
        grid_spec=grid_spec,
        out_shape=jax.ShapeDtypeStruct((T, D_MODEL), F32),
        compiler_params=_cparams(("arbitrary",)),
        name="moe_combine",
    )(slots, y_slots, h, route, nw)


def _rotary_tables(S):
    inv_freq = (1.0 / (np.float32(ROPE_THETA) ** (np.arange(0, HEAD_DIM, 2, dtype=np.float32) / HEAD_DIM))
                ).astype(np.float32)
    ang = np.arange(S, dtype=np.float32)[:, None] * inv_freq[None, :]
    cos, sin = np.cos(ang), np.sin(ang)
    reps = LANES // HEAD_DIM
    cos_t = np.tile(np.concatenate([cos, cos], axis=1), (1, reps)).astype(np.float32)
    sin_t = np.tile(np.concatenate([-sin, sin], axis=1), (1, reps)).astype(np.float32)
    return jnp.asarray(cos_t), jnp.asarray(sin_t)


def _layer(h_in, norm_mix_w, w_in, b_branch_gate, ret_decay_fwd, ret_decay_bwd, ret_gn_w, w_attn_branch,
           w_ret_branch, w_out, norm_moe_w, moe_w_group, moe_b_group, moe_w_expert, moe_b_expert,
           moe_w1, moe_w3, moe_w2, next_norm_w, B, S, cos_t, sin_t):
    T = B * S
    (qa0, ka0, va0, qa1, ka1, va1, qa2, ka2, va2, qr, kr, vr, gr, gates) = _in_projection(
        h_in, norm_mix_w[None, :], w_in.astype(BF16), b_branch_gate[None, :], cos_t, sin_t, B, S)

    unit = lambda a: a[:, :, None]
    y_att = _attention((unit(qa0), unit(ka0), unit(va0), qa1, ka1, va1, qa2, ka2, va2), B, S)

    dec = jnp.stack([ret_decay_fwd.reshape(RET_HEADS // 2, 2), ret_decay_bwd.reshape(RET_HEADS // 2, 2)], axis=1)
    dec = jnp.broadcast_to(dec.reshape(RET_HEADS // 2, 4, 1), (RET_HEADS // 2, 4, LANES)).astype(F32)
    y_ret = _retention(dec, qr, kr, vr, gr, ret_gn_w[None, :], B, S).reshape(T, RET_V_W)

    pad = LANES - MOE_GROUPS - MOE_N_EXPERTS
    w_r = jnp.concatenate([moe_w_group, moe_w_expert, jnp.zeros((D_MODEL, pad), F32)], axis=1)
    w_r_hi = w_r.astype(BF16)
    w_r_lo = (w_r - w_r_hi.astype(F32)).astype(BF16)
    b_r = jnp.concatenate([moe_b_group, moe_b_expert, jnp.zeros((pad,), F32)])[None, :]

    h_mid, hn, route, route_t, cnt = _merge_route(
        y_att, y_ret, gates, h_in, w_attn_branch.astype(BF16), w_ret_branch.astype(BF16),
        w_out.astype(BF16), norm_moe_w[None, :], w_r_hi, w_r_lo, b_r, T, S)

    bm = MOE_BM
    counts = cnt[0, ROUTER_EXPERT_LANE0:ROUTER_EXPERT_LANE0 + MOE_N_EXPERTS].astype(jnp.int32)
    nblk = (counts + bm - 1) // bm
    blk_end = jnp.cumsum(nblk)
    pstart = (blk_end - nblk) * bm
    n_blocks = (2 * T) // bm + MOE_N_EXPERTS
    n_active = blk_end[-1]
    bidx = jnp.minimum(jnp.arange(n_blocks, dtype=jnp.int32), n_active - 1)
    blk_eid = jnp.sum(bidx[:, None] >= blk_end[None, :], axis=1).astype(jnp.int32)
    mine = blk_eid[:, None] == jnp.arange(MOE_N_EXPERTS, dtype=jnp.int32)[None, :]
    seg_end = jnp.sum(jnp.where(mine, (pstart + counts)[None, :], 0), axis=1)
    blk_valid = jnp.clip(seg_end - bidx * bm, 0, bm)
    blk_valid = jnp.where(jnp.arange(n_blocks) < n_active, blk_valid, 0).astype(jnp.int32)
    blk_fresh = jnp.concatenate([jnp.ones((1,), jnp.int32), (blk_eid[1:] != blk_eid[:-1]).astype(jnp.int32)])
    ar = jnp.arange(MOE_N_EXPERTS, dtype=jnp.int32)
    later = jnp.min(jnp.where((nblk > 0)[None, :] & (ar[None, :] > ar[:, None]), ar[None, :], MOE_N_EXPERTS), axis=1)
    later = jnp.where(later < MOE_N_EXPERTS, later, -1)
    blk_next = (jnp.sum(jnp.where(mine, later[None, :] + 1, 0), axis=1) - 1).astype(jnp.int32)
    eid = route_t[ROUTE_EID:ROUTE_EID + 2].astype(jnp.int32)
    rank = route_t[ROUTE_RANK:ROUTE_RANK + 2].astype(jnp.int32)
    start = jnp.sum(jnp.where(eid[..., None] == jnp.arange(MOE_N_EXPERTS, dtype=jnp.int32),
                              pstart.astype(jnp.int32), 0), axis=-1)
    slots = (start + rank).reshape(2 * T)

    x_slots = _dispatch(slots, hn, n_blocks * bm, T)
    y_slots = _experts(bidx, blk_eid, blk_valid, blk_fresh, blk_next, x_slots, moe_w1, moe_w3, moe_w2, n_blocks)
    return _combine(slots, y_slots, h_mid, route, next_norm_w[None, :], T)


def kernel(x, norm_mix_w, w_in, b_branch_gate, ret_decay_fwd, ret_decay_bwd, ret_gn_w, w_attn_branch,
           w_ret_branch, w_out, norm_moe_w, moe_w_group, moe_b_group, moe_w_expert, moe_b_expert, moe_w1,
           moe_w3, moe_w2, norm_final_w):
    B, S, D = x.shape
    depth = norm_mix_w.shape[0]
    assert depth == 1, "the final norm is fused into the layer's combine stage"
    assert D == D_MODEL and S % TM_INPROJ == 0 and (B * S) < (1 << 24)
    cos_t, sin_t = _rotary_tables(S)
    out = _layer(x.reshape(B * S, D), norm_mix_w[0], w_in[0], b_branch_gate[0], ret_decay_fwd[0],
                 ret_decay_bwd[0], ret_gn_w[0], w_attn_branch[0], w_ret_branch[0], w_out[0], norm_moe_w[0],
                 moe_w_group[0], moe_b_group[0], moe_w_expert[0], moe_b_expert[0], moe_w1[0], moe_w3[0],
                 moe_w2[0], norm_final_w, B, S, cos_t, sin_t)
    return out.reshape(B, S, D)
```

```python
import functools

import numpy as np
import jax
import jax.numpy as jnp
from jax import lax
from jax.experimental import pallas as pl
from jax.experimental.pallas import tpu as pltpu

F32 = jnp.float32
BF16 = jnp.bfloat16

D_MODEL = 1024
HEAD_DIM = 64
ATTN_PAIRS = ((128, 1), (512, 4), (2048, 16))
ATTN_HEADS_PER_GROUP = 8
ATTN_GROUP_W = ATTN_HEADS_PER_GROUP * HEAD_DIM
ATTN_HALF = 64
ROPE_THETA = 10000.0
RET_HEADS = 8
RET_QK_DIM = 64
RET_V_DIM = 128
RET_CHUNK = 128
RET_QK_W = RET_HEADS * RET_QK_DIM
RET_V_W = RET_HEADS * RET_V_DIM
MOE_GROUPS = 8
MOE_EXPERTS_PER_GROUP = 8
MOE_N_EXPERTS = MOE_GROUPS * MOE_EXPERTS_PER_GROUP
MOE_HIDDEN = 512
NORM_EPS = 1e-6

LANES = 128
COL_CHUNKS = D_MODEL // LANES
ROW_TILES = COL_CHUNKS // 2
U32 = jnp.uint32
BF16_BITS = 16
NEG_BIG = -1e30
LOG2_E = 1.4426950408889634

TM_INPROJ = 512
TM_MERGE = 1024
MERGE_SPLIT = 4
MERGE_COL_CHUNKS = 4
TM_COMBINE = 512
MOE_BM = 512
DISPATCH_CHUNK = 4096
ATTN_QB = 128
ATTN_UNROLL = 8
RET_UNROLL = 8

V7X_VMEM_BYTES = 64 * 1024 * 1024
VMEM_LIMIT = V7X_VMEM_BYTES * 7 // 8

_A = 3 * ATTN_GROUP_W
OFF_QA, OFF_KA, OFF_VA = 0, _A, 2 * _A
OFF_QR = 3 * _A
OFF_KR = OFF_QR + RET_QK_W
OFF_VR = OFF_KR + RET_QK_W
OFF_GR = OFF_VR + RET_V_W
OFF_GL = OFF_GR + RET_V_W
IN_W = OFF_GL + 2 * D_MODEL


def _cparams(sem, vmem=VMEM_LIMIT):
    return pltpu.CompilerParams(dimension_semantics=sem, vmem_limit_bytes=vmem)


def _inproj_body(x_ref, nw_ref, w_ref, bg_ref, cos_ref, sin_ref,
                 qa0, ka0, va0, qa1, ka1, va1, qa2, ka2, va2, qr, kr, vr, gr, gt,
                 stage_ref):
    tm = x_ref.shape[0]
    x = x_ref[...]
    ms = jnp.mean(x * x, axis=-1, keepdims=True)
    xn = (x * lax.rsqrt(ms + NORM_EPS) * nw_ref[...]).astype(BF16)
    cos = cos_ref[...]
    sin = sin_ref[...]
    lane = lax.broadcasted_iota(jnp.int32, (tm, LANES), 1)
    first_half = (lane & (HEAD_DIM - 1)) < (HEAD_DIM // 2)

    def proj(c0, width):
        return jnp.dot(xn, w_ref[:, c0:c0 + width], preferred_element_type=F32)

    def rotary(a, scale):
        partner = jnp.where(first_half, pltpu.roll(a, LANES - HEAD_DIM // 2, 1),
                            pltpu.roll(a, HEAD_DIM // 2, 1))
        r = a * cos + partner * sin
        return r * scale if scale != 1.0 else r

    def chunks(acc):
        return [acc[:, c * LANES:(c + 1) * LANES] for c in range(acc.shape[1] // LANES)]

    def store_natural(out_ref, acc, fn):
        for c, a in enumerate(chunks(acc)):
            out_ref[:, c * LANES:(c + 1) * LANES] = fn(a).astype(out_ref.dtype)

    def store_pairs(out_ref, acc, fn):
        for c, a in enumerate(chunks(acc)):
            out_ref[0, c] = fn(a).astype(out_ref.dtype)

    def store_strided(out_ref, acc, fn, d):
        for c, a in enumerate(chunks(acc)):
            stage_ref[c] = fn(a)
        for c in range(acc.shape[1] // LANES):
            for r in range(d):
                out_ref[0, c, r] = stage_ref[c, pl.ds(r, tm // d, stride=d), :].astype(out_ref.dtype)

    ident = lambda a: a
    rot_q = lambda a: rotary(a, HEAD_DIM ** -0.5 * LOG2_E)
    rot_1 = lambda a: rotary(a, 1.0)
    rot_k = lambda a: rotary(a, RET_QK_DIM ** -0.5)

    W = ATTN_GROUP_W
    store_pairs(qa0, proj(OFF_QA, W), rot_q)
    store_pairs(ka0, proj(OFF_KA, W), rot_1)
    store_pairs(va0, proj(OFF_VA, W), ident)
    for g, (qo, ko, vo) in ((1, (qa1, ka1, va1)), (2, (qa2, ka2, va2))):
        d = ATTN_PAIRS[g][1]
        store_strided(qo, proj(OFF_QA + g * W, W), rot_q, d)
        store_strided(ko, proj(OFF_KA + g * W, W), rot_1, d)
        store_strided(vo, proj(OFF_VA + g * W, W), ident, d)
    store_natural(qr, proj(OFF_QR, RET_QK_W), rot_1)
    store_natural(kr, proj(OFF_KR, RET_QK_W), rot_k)
    for h in range(RET_V_W // W):
        vr[:, h * W:(h + 1) * W] = proj(OFF_VR + h * W, W).astype(vr.dtype)
        gr[:, h * W:(h + 1) * W] = proj(OFF_GR + h * W, W).astype(gr.dtype)
    for h in range(2 * D_MODEL // W):
        z = proj(OFF_GL + h * W, W) + bg_ref[:, h * W:(h + 1) * W]
        gt[:, h * W:(h + 1) * W] = jax.nn.sigmoid(z).astype(gt.dtype)


def _in_projection(x2, norm_w, w_bf, b_gate, cos_t, sin_t, B, S):
    T = B * S
    tm = TM_INPROJ
    nt = S // tm
    W = ATTN_GROUP_W
    row = lambda i: (i, 0)
    const = lambda i: (0, 0)
    nat = lambda width: pl.BlockSpec((tm, width), row)

    P = W // LANES

    def strided_spec(d):
        return pl.BlockSpec((1, P, d, tm // d, LANES), lambda i: (i // nt, 0, 0, i % nt, 0))

    def strided_shape(d):
        return jax.ShapeDtypeStruct((B, P, d, S // d, LANES), BF16)

    pair_spec = pl.BlockSpec((1, P, tm, LANES), lambda i: (i // nt, 0, i % nt, 0))
    pair_shape = jax.ShapeDtypeStruct((B, P, S, LANES), BF16)
    nat_shape = lambda width: jax.ShapeDtypeStruct((T, width), BF16)
    d1, d2 = ATTN_PAIRS[1][1], ATTN_PAIRS[2][1]
    out_shape = ([pair_shape] * 3 + [strided_shape(d1)] * 3 + [strided_shape(d2)] * 3
                 + [nat_shape(RET_QK_W)] * 2 + [nat_shape(RET_V_W)] * 2 + [nat_shape(2 * D_MODEL)])
    out_specs = ([pair_spec] * 3 + [strided_spec(d1)] * 3 + [strided_spec(d2)] * 3
                 + [nat(RET_QK_W)] * 2 + [nat(RET_V_W)] * 2 + [nat(2 * D_MODEL)])
    in_specs = [
        pl.BlockSpec((tm, D_MODEL), row),
        pl.BlockSpec((1, D_MODEL), const),
        pl.BlockSpec((D_MODEL, IN_W), const, pipeline_mode=pl.Buffered(1)),
        pl.BlockSpec((1, 2 * D_MODEL), const),
        pl.BlockSpec((tm, LANES), lambda i: (i % nt, 0)),
        pl.BlockSpec((tm, LANES), lambda i: (i % nt, 0)),
    ]
    return pl.pallas_call(
        _inproj_body,
        grid=(T // tm,),
        in_specs=in_specs,
        out_specs=out_specs,
        out_shape=out_shape,
        scratch_shapes=[pltpu.VMEM((W // LANES, tm, LANES), F32)],
        compiler_params=_cparams(("parallel",)),
        name="in_projection",
    )(x2, norm_w, w_bf, b_gate, cos_t, sin_t)


def _attn_body(q0_ref, k0_ref, v0_ref, q1_ref, k1_ref, v1_ref, q2_ref, k2_ref, v2_ref, o_ref,
               part_ref, bias_ref, *, S):
    QB, H = ATTN_QB, ATTN_HALF
    lane = lax.broadcasted_iota(jnp.int32, (QB, LANES), 1)
    head0 = lane < HEAD_DIM

    qi = lax.broadcasted_iota(jnp.int32, (QB, QB + 2 * H), 0)
    ki = lax.broadcasted_iota(jnp.int32, (QB, QB + 2 * H), 1)
    for n in range(3):
        bias_ref[n] = jnp.where(jnp.abs(ki - qi - n * H) <= H, 0.0, NEG_BIG).astype(F32)

    def scores(q_rows, k_rows, bias, h):
        qm = jnp.where(head0 if h == 0 else jnp.logical_not(head0), q_rows, jnp.zeros_like(q_rows))
        return lax.dot_general(qm, k_rows, (((1,), (1,)), ((), ())), preferred_element_type=F32) + bias

    def weights(s):
        m = jnp.max(s, axis=-1, keepdims=True)
        return m, jnp.exp2(s - m).astype(BF16)

    def heads_to_lanes(m0, a, m1, b):
        num = jnp.where(head0, a, b)
        den = pltpu.roll(jnp.where(head0, b, a), HEAD_DIM, 1)
        mx = jnp.where(head0, m0, m1)
        return num, den, mx

    def run_group(q_ref, k_ref, v_ref, d, prepare, store):
        L = S // d
        KW = min(L, QB + 2 * H)
        nb = L // QB

        key_head0 = lax.broadcasted_iota(jnp.int32, (KW, LANES), 1) < HEAD_DIM
        key_ones = jnp.ones((KW, LANES), BF16)

        def trip(i, carry):
            blocks = []
            for u in range(ATTN_UNROLL):
                t = i * ATTN_UNROLL + u
                r = t // nb
                q0 = pl.multiple_of((t % nb) * QB, QB)
                ws = pl.multiple_of(jnp.clip(q0 - H, 0, L - KW), H)
                bias = bias_ref[(q0 - ws) // H][:, :KW]
                q_rows = q_ref[0, 0, r, pl.ds(q0, QB), :]
                k_rows = k_ref[0, 0, r, pl.ds(ws, KW), :]
                blocks.append((r, q0, ws, [scores(q_rows, k_rows, bias, h) for h in range(2)]))
            blocks = [(r, q0, ws, [weights(s) for s in ss]) for r, q0, ws, ss in blocks]
            done = []
            for r, q0, ws, ((m0, p0), (m1, p1)) in blocks:
                v_rows = v_ref[0, 0, r, pl.ds(ws, KW), :]
                a = jnp.dot(p0, jnp.where(key_head0, v_rows, key_ones), preferred_element_type=F32)
                b = jnp.dot(p1, jnp.where(key_head0, key_ones, v_rows), preferred_element_type=F32)
                done.append((r, q0, m0, a, m1, b))
            done = [(r, q0, prepare(q0, *heads_to_lanes(m0, a, m1, b))) for r, q0, m0, a, m1, b in done]
            for r, q0, vals in done:
                store(r, q0, vals)
            return carry

        lax.fori_loop(0, S // QB // ATTN_UNROLL, trip, 0)

    def store_partial(g):
        d = ATTN_PAIRS[g][1]

        def store(r, q0, vals):
            rows = pl.ds(r + q0 * d, QB, stride=d)
            for n, val in enumerate(vals):
                part_ref[3 * (g - 1) + n, rows, :] = val
        return store

    keep = lambda q0, num, den, mx: (num, den, mx)
    run_group(q1_ref, k1_ref, v1_ref, ATTN_PAIRS[1][1], keep, store_partial(1))
    run_group(q2_ref, k2_ref, v2_ref, ATTN_PAIRS[2][1], keep, store_partial(2))

    def merge(q0, num, den, mx):
        rows = pl.ds(q0, QB)
        nums = [num, part_ref[0, rows, :], part_ref[3, rows, :]]
        dens = [den, part_ref[1, rows, :], part_ref[4, rows, :]]
        mxs = [mx, part_ref[2, rows, :], part_ref[5, rows, :]]
        top = jnp.maximum(jnp.maximum(mxs[0], mxs[1]), mxs[2])
        ws = [jnp.exp2(m - top) for m in mxs]
        n = ws[0] * nums[0] + ws[1] * nums[1] + ws[2] * nums[2]
        dn = ws[0] * dens[0] + ws[1] * dens[1] + ws[2] * dens[2]
        return (n / dn).astype(o_ref.dtype)

    def store_out(r, q0, y):
        o_ref[0, 0, pl.ds(q0, QB), :] = y

    run_group(q0_ref, k0_ref, v0_ref, ATTN_PAIRS[0][1], merge, store_out)


def _attention(qkv, B, S):
    P = ATTN_GROUP_W // LANES
    in_specs = []
    for _, d in ATTN_PAIRS:
        in_specs += [pl.BlockSpec((1, 1, d, S // d, LANES), lambda b, p: (b, p, 0, 0, 0))] * 3
    return pl.pallas_call(
        functools.partial(_attn_body, S=S),
        grid=(B, P),
        in_specs=in_specs,
        out_specs=pl.BlockSpec((1, 1, S, LANES), lambda b, p: (b, p, 0, 0)),
        out_shape=jax.ShapeDtypeStruct((B, P, S, LANES), BF16),
        scratch_shapes=[pltpu.VMEM((6, S, LANES), F32),
                        pltpu.VMEM((3, ATTN_QB, ATTN_QB + 2 * ATTN_HALF), F32)],
        compiler_params=_cparams(("parallel", "parallel")),
        name="attention",
    )(*qkv)


def _log_sigmoid(z):
    return jnp.minimum(z, 0.0) - jnp.log(1.0 + jnp.exp(-jnp.abs(z)))


def _ret_body(dec_ref, q_ref, k_ref, v_ref, g_ref, gnw_ref, o_ref, kt_ref, sf_ref, sb_ref, st_ref, *, S):
    C = RET_CHUNK
    nc = S // C
    lg = _log_sigmoid(dec_ref[0])
    a_row = lax.broadcasted_iota(jnp.int32, (C, LANES), 0).astype(F32)
    lane = lax.broadcasted_iota(jnp.int32, (C, LANES), 1)
    rel = (lax.broadcasted_iota(jnp.int32, (C, C), 0) - lax.broadcasted_iota(jnp.int32, (C, C), 1)).astype(F32)

    heads = []
    for h in range(2):
        lgf = lg[h:h + 1, :]
        lgb = lg[2 + h:3 + h, :]
        in_head = (lane < RET_QK_DIM) if h == 0 else (lane >= RET_QK_DIM)
        heads.append(dict(
            in_head=in_head,
            xi_f=jnp.where(in_head, jnp.exp((a_row + 1.0) * lgf), 0.0),
            xi_b=jnp.where(in_head, jnp.exp((C - a_row) * lgb), 0.0),
            zeta_f=jnp.exp((C - 1.0 - a_row) * lgf),
            zeta_b=jnp.exp(a_row * lgb),
            dloc=jnp.where(rel > 0, jnp.exp(rel * lgf), jnp.where(rel < 0, jnp.exp(-rel * lgb), 2.0)),
            cd_f=jnp.exp(C * lgf),
            cd_b=jnp.exp(C * lgb),
        ))

    def chunk_rows(n):
        return pl.ds(pl.multiple_of(n * C, C), C)

    def v_head(rows, h):
        return v_ref[0, rows, h * RET_V_DIM:(h + 1) * RET_V_DIM]

    def products(i, carry):
        items = []
        for u in range(RET_UNROLL):
            n = i * RET_UNROLL + u
            rows = chunk_rows(n)
            kt = k_ref[0, rows, :].astype(F32).T.astype(BF16)
            kt_ref[n] = kt
            for h, hd in enumerate(heads):
                vh = v_head(rows, h).astype(F32)
                items.append((n, h, kt, (vh * hd["zeta_f"]).astype(BF16), (vh * hd["zeta_b"]).astype(BF16)))
        outs = [(n, h, jnp.dot(kt, vf, preferred_element_type=F32), jnp.dot(kt, vb, preferred_element_type=F32))
                for n, h, kt, vf, vb in items]
        for n, h, f, b in outs:
            sf_ref[n, h] = f
            sb_ref[n, h] = b
        return carry

    lax.fori_loop(0, nc // RET_UNROLL, products, 0)

    def scan(ref, key, order, half):
        rows = slice(half * 2 * RET_QK_DIM, (half + 1) * 2 * RET_QK_DIM)

        def step(i, state):
            n = order(i)
            new = []
            for h, hd in enumerate(heads):
                st_ref[n, h, rows, :] = state[h].astype(BF16)
                new.append(hd[key] * state[h] + ref[n, h])
            return tuple(new)
        zero = jnp.zeros(ref.shape[2:], F32)
        lax.fori_loop(0, nc, step, (zero, zero))

    scan(sf_ref, "cd_f", lambda i: i, 0)
    scan(sb_ref, "cd_b", lambda i: nc - 1 - i, 1)

    def outputs(i, carry):
        items = []
        for u in range(RET_UNROLL):
            n = i * RET_UNROLL + u
            rows = chunk_rows(n)
            qp = q_ref[0, rows, :]
            qf = qp.astype(F32)
            kt = kt_ref[n]
            for h, hd in enumerate(heads):
                qm = jnp.where(hd["in_head"], qp, jnp.zeros_like(qp))
                qx = jnp.concatenate([(qf * hd["xi_f"]).astype(BF16), (qf * hd["xi_b"]).astype(BF16)], axis=1)
                items.append((rows, h, hd, jnp.dot(qm, kt, preferred_element_type=F32),
                              jnp.dot(qx, st_ref[n, h], preferred_element_type=F32)))
        items = [(rows, h, hd, (s * hd["dloc"]).astype(BF16), cross) for rows, h, hd, s, cross in items]
        items = [(rows, h, cross + jnp.dot(p, v_head(rows, h), preferred_element_type=F32))
                 for rows, h, hd, p, cross in items]
        for rows, h, ret in items:
            mu = jnp.mean(ret, axis=-1, keepdims=True)
            xc = ret - mu
            var = jnp.mean(xc * xc, axis=-1, keepdims=True)
            cols = slice(h * RET_V_DIM, (h + 1) * RET_V_DIM)
            gate = g_ref[0, rows, cols].astype(F32)
            y = xc * lax.rsqrt(var + NORM_EPS) * gnw_ref[:, cols] * (gate * jax.nn.sigmoid(gate))
            o_ref[0, rows, cols] = y.astype(o_ref.dtype)
        return carry

    lax.fori_loop(0, nc // RET_UNROLL, outputs, 0)


def _retention(dec, qr, kr, vr, gr, gn_w, B, S):
    nc = S // RET_CHUNK
    npairs = RET_HEADS // 2
    qk_spec = pl.BlockSpec((1, S, 2 * RET_QK_DIM), lambda b, p: (b, 0, p))
    v_spec = pl.BlockSpec((1, S, 2 * RET_V_DIM), lambda b, p: (b, 0, p))
    return pl.pallas_call(
        functools.partial(_ret_body, S=S),
        grid=(B, npairs),
        in_specs=[pl.BlockSpec((1, 4, LANES), lambda b, p: (p, 0, 0)),
                  qk_spec, qk_spec, v_spec, v_spec,
                  pl.BlockSpec((1, 2 * RET_V_DIM), lambda b, p: (0, p))],
        out_specs=v_spec,
        out_shape=jax.ShapeDtypeStruct((B, S, RET_V_W), BF16),
        scratch_shapes=[pltpu.VMEM((nc, 2 * RET_QK_DIM, RET_CHUNK), BF16),
                        pltpu.VMEM((nc, 2, 2 * RET_QK_DIM, RET_V_DIM), F32),
                        pltpu.VMEM((nc, 2, 2 * RET_QK_DIM, RET_V_DIM), F32),
                        pltpu.VMEM((nc, 2, 4 * RET_QK_DIM, RET_V_DIM), BF16)],
        compiler_params=_cparams(("parallel", "parallel")),
        name="retention",
    )(dec, qr.reshape(B, S, RET_QK_W), kr.reshape(B, S, RET_QK_W),
      vr.reshape(B, S, RET_V_W), gr.reshape(B, S, RET_V_W), gn_w)


ROUTE_EID, ROUTE_RANK, ROUTE_GATE = 0, 2, 4
ROUTE_FIELDS = 8
ROUTER_EXPERT_LANE0 = MOE_GROUPS
GROUP_SHIFT = MOE_EXPERTS_PER_GROUP.bit_length() - 1


def _merge_body(yatt_ref, yret_ref, gt_ref, x_ref, wa_ref, wb_ref, wo_ref,
                nw_ref, wrh_ref, wrl_ref, br_ref,
                h_ref, hn_ref, route_ref, route_t_ref, cnt_ref):
    tm = x_ref.shape[0]
    hm = tm // MERGE_SPLIT
    i = pl.program_id(0)
    cw = D_MODEL // MERGE_COL_CHUNKS

    def branch_products(rows):
        y_att = jnp.concatenate([yatt_ref[0, c, rows, :] for c in range(yatt_ref.shape[1])], axis=1)
        y_ret = yret_ref[rows, :]
        chunks = []
        for c in range(MERGE_COL_CHUNKS):
            cols = slice(c * cw, (c + 1) * cw)
            a = jnp.dot(y_att, wa_ref[:, cols], preferred_element_type=F32)
            b = jnp.dot(y_ret, wb_ref[:, cols], preferred_element_type=F32)
            g_att = gt_ref[rows, c * cw:(c + 1) * cw].astype(F32)
            g_ret = gt_ref[rows, D_MODEL + c * cw:D_MODEL + (c + 1) * cw].astype(F32)
            chunks.append((g_att * a + g_ret * b).astype(BF16))
        return jnp.concatenate(chunks, axis=1)

    def residual_norm(hf, rows, merged):
        mix = jnp.dot(merged, wo_ref[...], preferred_element_type=F32)
        h = x_ref[rows, :] + mix
        h_ref[rows, :] = h
        ms = jnp.mean(h * h, axis=-1, keepdims=True)
        hn = h * lax.rsqrt(ms + NORM_EPS) * nw_ref[...]
        for j, word in enumerate(_pack_row(hn)):
            hn_ref[pl.ds(hf * hm * ROW_TILES + j, hm, stride=ROW_TILES), :] = word
        hi = hn.astype(BF16)
        lo = (hn - hi.astype(F32)).astype(BF16)
        return hi, lo

    def router_logits(hi, lo):
        return (jnp.dot(hi, wrh_ref[...], preferred_element_type=F32)
                + jnp.dot(hi, wrl_ref[...], preferred_element_type=F32)
                + jnp.dot(lo, wrh_ref[...], preferred_element_type=F32)) + br_ref[...]

    lane = lax.broadcasted_iota(jnp.int32, (hm, LANES), 1)
    far = jnp.int32(LANES)

    def first_argmax(vals, vmax):
        return jnp.min(jnp.where(vals == vmax, lane, far), axis=-1, keepdims=True)

    def route(logits):
        is_group = lane < MOE_GROUPS
        gl = jnp.where(is_group, logits, NEG_BIG)
        gmax = jnp.max(gl, axis=-1, keepdims=True)
        g_w = 1.0 / jnp.sum(jnp.where(is_group, jnp.exp(gl - gmax), 0.0), axis=-1, keepdims=True)
        g_idx = first_argmax(gl, gmax)
        e_lane = lane - ROUTER_EXPERT_LANE0
        in_group = (e_lane >= 0) & (e_lane < MOE_N_EXPERTS) & (jnp.right_shift(e_lane, GROUP_SHIFT) == g_idx)
        el = jnp.where(in_group, logits, NEG_BIG)
        m1 = jnp.max(el, axis=-1, keepdims=True)
        i1 = first_argmax(el, m1)
        el2 = jnp.where(lane == i1, NEG_BIG, el)
        m2 = jnp.max(el2, axis=-1, keepdims=True)
        i2 = first_argmax(el2, m2)
        ex = jnp.exp(m2 - m1)
        return i1, i2, g_w / (1.0 + ex), g_w * ex / (1.0 + ex)

    rows = [slice(hf * hm, (hf + 1) * hm) for hf in range(MERGE_SPLIT)]
    merged = [branch_products(r) for r in rows]
    split = [residual_norm(hf, r, m) for hf, (r, m) in enumerate(zip(rows, merged))]
    routed = [route(router_logits(hi, lo)) for hi, lo in split]

    @pl.when(i == 0)
    def _():
        cnt_ref[...] = jnp.zeros(cnt_ref.shape, F32)

    r_idx = lax.broadcasted_iota(jnp.int32, (hm, hm), 0)
    c_idx = lax.broadcasted_iota(jnp.int32, (hm, hm), 1)
    lower = jnp.where(c_idx < r_idx, 1.0, 0.0).astype(BF16)
    running = cnt_ref[...]
    for hf, (i1, i2, gate1, gate2) in enumerate(routed):
        hot1 = lane == i1
        hot2 = lane == i2
        onehot = jnp.where(hot1 | hot2, 1.0, 0.0)
        before = jnp.dot(lower, onehot.astype(BF16), preferred_element_type=F32) + running
        rank1 = jnp.sum(jnp.where(hot1, before, 0.0), axis=-1, keepdims=True)
        rank2 = jnp.sum(jnp.where(hot2, before, 0.0), axis=-1, keepdims=True)
        running = running + jnp.sum(onehot, axis=0, keepdims=True)
        rec = jnp.zeros((hm, LANES), F32)
        for pos, val in ((ROUTE_EID, (i1 - ROUTER_EXPERT_LANE0).astype(F32)),
                         (ROUTE_EID + 1, (i2 - ROUTER_EXPERT_LANE0).astype(F32)),
                         (ROUTE_RANK, rank1), (ROUTE_RANK + 1, rank2),
                         (ROUTE_GATE, gate1), (ROUTE_GATE + 1, gate2)):
            rec = jnp.where(lane == pos, val, rec)
        route_ref[rows[hf], :] = rec
        route_t_ref[:, hf * hm:(hf + 1) * hm] = rec.T[:route_t_ref.shape[0], :]
    cnt_ref[...] = running


def _merge_route(y_att, y_ret, gates, x2, wa, wb, wo, nw, wr_hi, wr_lo, b_r, T, S):
    tm = TM_MERGE
    nt = S // tm
    row = lambda i: (i, 0)
    const = lambda i: (0, 0)
    full = lambda arr: pl.BlockSpec(arr.shape, const)
    in_specs = ([pl.BlockSpec((1, y_att.shape[1], tm, LANES), lambda i: (i // nt, 0, i % nt, 0)),
                 pl.BlockSpec((tm, RET_V_W), row), pl.BlockSpec((tm, 2 * D_MODEL), row),
                 pl.BlockSpec((tm, D_MODEL), row),
                 full(wa), full(wb), full(wo), full(nw), full(wr_hi), full(wr_lo), full(b_r)])
    return pl.pallas_call(
        _merge_body,
        grid=(T // tm,),
        in_specs=in_specs,
        out_specs=[pl.BlockSpec((tm, D_MODEL), row),
                   pl.BlockSpec((tm * ROW_TILES, LANES), row),
                   pl.BlockSpec((tm, LANES), row), pl.BlockSpec((ROUTE_FIELDS, tm), lambda i: (0, i)),
                   pl.BlockSpec((1, LANES), const)],
        out_shape=[jax.ShapeDtypeStruct((T, D_MODEL), F32), jax.ShapeDtypeStruct((T * ROW_TILES, LANES), U32),
                   jax.ShapeDtypeStruct((T, LANES), F32), jax.ShapeDtypeStruct((ROUTE_FIELDS, T), F32),
                   jax.ShapeDtypeStruct((1, LANES), F32)],
        compiler_params=_cparams(("arbitrary",)),
        name="merge_route",
    )(y_att, y_ret, gates, x2, wa, wb, wo, nw, wr_hi, wr_lo, b_r)


ISSUE_UNROLL = 8


def _tile_rows(n):
    return pl.ds(pl.multiple_of(n * ROW_TILES, ROW_TILES), ROW_TILES)


def _pack_row(x):
    bits = lambda c: pltpu.bitcast(x[:, c * LANES:(c + 1) * LANES].astype(BF16).astype(F32), U32)
    return [(bits(j) >> BF16_BITS) | bits(j + ROW_TILES) for j in range(ROW_TILES)]


def _unpack_word(w):
    high = jnp.uint32(((1 << BF16_BITS) - 1) << BF16_BITS)
    return pltpu.bitcast(w << BF16_BITS, F32), pltpu.bitcast(w & high, F32)


def _dispatch_body(slot_ref, hn_ref, xs_ref, sem, *, T):
    i = pl.program_id(0)
    ch = hn_ref.shape[0] // ROW_TILES

    def row_copy(j, slot):
        return pltpu.make_async_copy(hn_ref.at[_tile_rows(j)], xs_ref.at[_tile_rows(slot)], sem)

    def issue(j, carry):
        t = i * ch + j
        row_copy(j, slot_ref[t]).start(priority=0)
        row_copy(j, slot_ref[T + t]).start(priority=1)
        return carry

    lax.fori_loop(0, ch, issue, 0, unroll=ISSUE_UNROLL)
    for _ in range(2):
        pltpu.make_async_copy(hn_ref, xs_ref.at[pl.ds(0, ch * ROW_TILES)], sem).wait()


def _dispatch(slots, hn, n_slots, T):
    ch = DISPATCH_CHUNK
    grid_spec = pltpu.PrefetchScalarGridSpec(
        num_scalar_prefetch=1,
        grid=(T // ch,),
        in_specs=[pl.BlockSpec((ch * ROW_TILES, LANES), lambda i, s: (i, 0))],
        out_specs=pl.BlockSpec(memory_space=pl.ANY),
        scratch_shapes=[pltpu.SemaphoreType.DMA(())],
    )
    return pl.pallas_call(
        functools.partial(_dispatch_body, T=T),
        grid_spec=grid_spec,
        out_shape=jax.ShapeDtypeStruct((n_slots * ROW_TILES, LANES), U32),
        compiler_params=_cparams(("arbitrary",)),
        name="moe_dispatch",
    )(slots, hn)


def _expert_body(blk_ref, eid_ref, valid_ref, fresh_ref, next_ref, x_ref, w1_ref, w3_ref, w2_ref, y_ref,
                 w1b, w3b, w2b, w1s, w3s, w2s, wsem):
    i = pl.program_id(0)
    valid = valid_ref[i]

    def weight_copies(e):
        return (pltpu.make_async_copy(w1_ref.at[e], w1s, wsem.at[0]),
                pltpu.make_async_copy(w3_ref.at[e], w3s, wsem.at[1]),
                pltpu.make_async_copy(w2_ref.at[e], w2s, wsem.at[2]))

    @pl.when(i == 0)
    def _():
        for cp in weight_copies(eid_ref[0]):
            cp.start()

    @pl.when(valid > 0)
    def _():
        @pl.when(fresh_ref[i] == 1)
        def _():
            for cp in weight_copies(eid_ref[i]):
                cp.wait()
            w1b[...] = w1s[...].astype(BF16)
            w3b[...] = w3s[...].astype(BF16)
            w2b[...] = w2s[...].astype(BF16)

            @pl.when(next_ref[i] >= 0)
            def _():
                for cp in weight_copies(next_ref[i]):
                    cp.start()

        bm = x_ref.shape[0] // ROW_TILES
        live =lax.broadcasted_iota(jnp.int32, (bm, LANES), 0) < valid
        halves = [_unpack_word(x_ref[pl.ds(j, bm, stride=ROW_TILES), :]) for j in range(ROW_TILES)]
        x = jnp.concatenate(
            [jnp.where(live, c, 0.0).astype(BF16) for c in [lo for lo, _ in halves] + [hi for _, hi in halves]],
            axis=1)
        a = jnp.dot(x, w1b[...], preferred_element_type=F32)
        b = jnp.dot(x, w3b[...], preferred_element_type=F32)
        hid = (a * jax.nn.sigmoid(a) * b).astype(BF16)
        y = jnp.dot(hid, w2b[...], preferred_element_type=F32)
        for j, word in enumerate(_pack_row(y)):
            y_ref[pl.ds(j, bm, stride=ROW_TILES), :] = word


def _experts(blk, blk_eid, blk_valid, blk_fresh, blk_next, x_slots, w1, w3, w2, n_blocks):
    bm = MOE_BM
    slot_block = lambda i, blk, eid, val, fr, nx: (blk[i], 0)
    grid_spec = pltpu.PrefetchScalarGridSpec(
        num_scalar_prefetch=5,
        grid=(n_blocks,),
        in_specs=[pl.BlockSpec((bm * ROW_TILES, LANES), slot_block),
                  pl.BlockSpec(memory_space=pl.ANY), pl.BlockSpec(memory_space=pl.ANY),
                  pl.BlockSpec(memory_space=pl.ANY)],
        out_specs=pl.BlockSpec((bm * ROW_TILES, LANES), slot_block),
        scratch_shapes=[pltpu.VMEM((D_MODEL, MOE_HIDDEN), BF16), pltpu.VMEM((D_MODEL, MOE_HIDDEN), BF16),
                        pltpu.VMEM((MOE_HIDDEN, D_MODEL), BF16),
                        pltpu.VMEM((D_MODEL, MOE_HIDDEN), F32), pltpu.VMEM((D_MODEL, MOE_HIDDEN), F32),
                        pltpu.VMEM((MOE_HIDDEN, D_MODEL), F32), pltpu.SemaphoreType.DMA((3,))],
    )
    return pl.pallas_call(
        _expert_body,
        grid_spec=grid_spec,
        out_shape=jax.ShapeDtypeStruct(x_slots.shape, U32),
        compiler_params=_cparams(("arbitrary",)),
        name="moe_experts",
    )(blk, blk_eid, blk_valid, blk_fresh, blk_next, x_slots, w1, w3, w2)


def _combine_body(slot_ref, ys_ref, h_ref, route_ref, nw_ref, o_ref, ybuf, sem, *, T):
    i = pl.program_id(0)
    n = pl.num_programs(0)
    tm = h_ref.shape[0]

    def row_copy(slot, buf, k, j):
        return pltpu.make_async_copy(ys_ref.at[_tile_rows(slot)], ybuf.at[buf, k, _tile_rows(j)], sem.at[buf])

    def issue(tile, buf):
        def one(j, carry):
            t = tile * tm + j
            row_copy(slot_ref[t], buf, 0, j).start(priority=0)
            row_copy(slot_ref[T + t], buf, 1, j).start(priority=1)
            return carry
        lax.fori_loop(0, tm, one, 0, unroll=ISSUE_UNROLL)

    @pl.when(i == 0)
    def _():
        issue(0, 0)

    @pl.when(i + 1 < n)
    def _():
        issue(i + 1, (i + 1) % 2)

    buf = i % 2
    for k in range(2):
        pltpu.make_async_copy(ys_ref.at[pl.ds(0, tm * ROW_TILES)], ybuf.at[buf, k], sem.at[buf]).wait()
    route = route_ref[...]
    g1 = route[:, ROUTE_GATE:ROUTE_GATE + 1]
    g2 = route[:, ROUTE_GATE + 1:ROUTE_GATE + 2]
    hs = [None] * COL_CHUNKS
    ss = jnp.zeros((tm, 1), F32)
    for j in range(ROW_TILES):
        tile_row = pl.ds(j, tm, stride=ROW_TILES)
        first = _unpack_word(ybuf[buf, 0, tile_row, :])
        second = _unpack_word(ybuf[buf, 1, tile_row, :])
        for c, y1, y2 in ((j, first[0], second[0]), (j + ROW_TILES, first[1], second[1])):
            hc = h_ref[:, c * LANES:(c + 1) * LANES] + (y1 * g1 + y2 * g2)
            hs[c] = hc
            ss = ss + jnp.sum(hc * hc, axis=-1, keepdims=True)
    inv = lax.rsqrt(ss * (1.0 / D_MODEL) + NORM_EPS)
    for j, hj in enumerate(hs):
        cols = slice(j * LANES, (j + 1) * LANES)
        o_ref[:, cols] = hj * inv * nw_ref[:, cols]


def _combine(slots, y_slots, h, route, nw, T):
    tm = TM_COMBINE
    row = lambda i, s: (i, 0)
    grid_spec = pltpu.PrefetchScalarGridSpec(
        num_scalar_prefetch=1,
        grid=(T // tm,),
        in_specs=[pl.BlockSpec(memory_space=pl.ANY),
                  pl.BlockSpec((tm, D_MODEL), row),
                  pl.BlockSpec((tm, LANES), row),
                  pl.BlockSpec((1, D_MODEL), lambda i, s: (0, 0))],
        out_specs=pl.BlockSpec((tm, D_MODEL), row),
        scratch_shapes=[pltpu.VMEM((2, 2, tm * ROW_TILES, LANES), U32), pltpu.SemaphoreType.DMA((2,))],
    )
    return pl.pallas_call(
        functools.partial(_combine_body, T=T),
        grid_spec=grid_spec,
        out_shape=jax.ShapeDtypeStruct((T, D_MODEL), F32),
        compiler_params=_cparams(("arbitrary",)),
        name="moe_combine",
    )(slots, y_slots, h, route, nw)


def _rotary_tables(S):
    inv_freq = (1.0 / (np.float32(ROPE_THETA) ** (np.arange(0, HEAD_DIM, 2, dtype=np.float32) / HEAD_DIM))
                ).astype(np.float32)
    ang = np.arange(S, dtype=np.float32)[:, None] * inv_freq[None, :]
    cos, sin = np.cos(ang), np.sin(ang)
    reps = LANES // HEAD_DIM
    cos_t = np.tile(np.concatenate([cos, cos], axis=1), (1, reps)).astype(np.float32)
    sin_t = np.tile(np.concatenate([-sin, sin], axis=1), (1, reps)).astype(np.float32)
    return jnp.asarray(cos_t), jnp.asarray(sin_t)


def _layer(h_in, norm_mix_w, w_in, b_branch_gate, ret_decay_fwd, ret_decay_bwd, ret_gn_w, w_attn_branch,
           w_ret_branch, w_out, norm_moe_w, moe_w_group, moe_b_group, moe_w_expert, moe_b_expert,
           moe_w1, moe_w3, moe_w2, next_norm_w, B, S, cos_t, sin_t):
    T = B * S
    (qa0, ka0, va0, qa1, ka1, va1, qa2, ka2, va2, qr, kr, vr, gr, gates) = _in_projection(
        h_in, norm_mix_w[None, :], w_in.astype(BF16), b_branch_gate[None, :], cos_t, sin_t, B, S)

    unit = lambda a: a[:, :, None]
    y_att = _attention((unit(qa0), unit(ka0), unit(va0), qa1, ka1, va1, qa2, ka2, va2), B, S)

    dec = jnp.stack([ret_decay_fwd.reshape(RET_HEADS // 2, 2), ret_decay_bwd.reshape(RET_HEADS // 2, 2)], axis=1)
    dec = jnp.broadcast_to(dec.reshape(RET_HEADS // 2, 4, 1), (RET_HEADS // 2, 4, LANES)).astype(F32)
    y_ret = _retention(dec, qr, kr, vr, gr, ret_gn_w[None, :], B, S).reshape(T, RET_V_W)

    pad = LANES - MOE_GROUPS - MOE_N_EXPERTS
    w_r = jnp.concatenate([moe_w_group, moe_w_expert, jnp.zeros((D_MODEL, pad), F32)], axis=1)
    w_r_hi = w_r.astype(BF16)
    w_r_lo = (w_r - w_r_hi.astype(F32)).astype(BF16)
    b_r = jnp.concatenate([moe_b_group, moe_b_expert, jnp.zeros((pad,), F32)])[None, :]

    h_mid, hn, route, route_t, cnt = _merge_route(
        y_att, y_ret, gates, h_in, w_attn_branch.astype(BF16), w_ret_branch.astype(BF16),
        w_out.astype(BF16), norm_moe_w[None, :], w_r_hi, w_r_lo, b_r, T, S)

    bm = MOE_BM
    counts = cnt[0, ROUTER_EXPERT_LANE0:ROUTER_EXPERT_LANE0 + MOE_N_EXPERTS].astype(jnp.int32)
    nblk = (counts + bm - 1) // bm
    blk_end = jnp.cumsum(nblk)
    pstart = (blk_end - nblk) * bm
    n_blocks = (2 * T) // bm + MOE_N_EXPERTS
    n_active = blk_end[-1]
    bidx = jnp.minimum(jnp.arange(n_blocks, dtype=jnp.int32), n_active - 1)
    blk_eid = jnp.sum(bidx[:, None] >= blk_end[None, :], axis=1).astype(jnp.int32)
    mine = blk_eid[:, None] == jnp.arange(MOE_N_EXPERTS, dtype=jnp.int32)[None, :]
    seg_end = jnp.sum(jnp.where(mine, (pstart + counts)[None, :], 0), axis=1)
    blk_valid = jnp.clip(seg_end - bidx * bm, 0, bm)
    blk_valid = jnp.where(jnp.arange(n_blocks) < n_active, blk_valid, 0).astype(jnp.int32)
    blk_fresh = jnp.concatenate([jnp.ones((1,), jnp.int32), (blk_eid[1:] != blk_eid[:-1]).astype(jnp.int32)])
    ar = jnp.arange(MOE_N_EXPERTS, dtype=jnp.int32)
    later = jnp.min(jnp.where((nblk > 0)[None, :] & (ar[None, :] > ar[:, None]), ar[None, :], MOE_N_EXPERTS), axis=1)
    later = jnp.where(later < MOE_N_EXPERTS, later, -1)
    blk_next = (jnp.sum(jnp.where(mine, later[None, :] + 1, 0), axis=1) - 1).astype(jnp.int32)
    eid = route_t[ROUTE_EID:ROUTE_EID + 2].astype(jnp.int32)
    rank = route_t[ROUTE_RANK:ROUTE_RANK + 2].astype(jnp.int32)
    start = jnp.sum(jnp.where(eid[..., None] == jnp.arange(MOE_N_EXPERTS, dtype=jnp.int32),
                              pstart.astype(jnp.int32), 0), axis=-1)
    slots = (start + rank).reshape(2 * T)

    x_slots = _dispatch(slots, hn, n_blocks * bm, T)
    y_slots = _experts(bidx, blk_eid, blk_valid, blk_fresh, blk_next, x_slots, moe_w1, moe_w3, moe_w2, n_blocks)
    return _combine(slots, y_slots, h_mid, route, next_norm_w[None, :], T)


def kernel(x, norm_mix_w, w_in, b_branch_gate, ret_decay_fwd, ret_decay_bwd, ret_gn_w, w_attn_branch,
           w_ret_branch, w_out, norm_moe_w, moe_w_group, moe_b_group, moe_w_expert, moe_b_expert, moe_w1,
           moe_w3, moe_w2, norm_final_w):
    B, S, D = x.shape
    depth = norm_mix_w.shape[0]
    assert depth == 1, "the final norm is fused into the layer's combine stage"
    assert D == D_MODEL and S % TM_INPROJ == 0 and (B * S) < (1 << 24)
    cos_t, sin_t = _rotary_tables(S)
    out = _layer(x.reshape(B * S, D), norm_mix_w[0], w_in[0], b_branch_gate[0], ret_decay_fwd[0],
                 ret_decay_bwd[0], ret_gn_w[0], w_attn_branch[0], w_ret_branch[0], w_out[0], norm_moe_w[0],
                 moe_w_group[0], moe_b_group[0], moe_w_expert[0], moe_b_expert[0], moe_w1[0], moe_w3[0],
                 moe_w2[0], norm_final_w, B, S, cos_t, sin_t)
    return out.reshape(B, S, D)
```

```python
import functools

import numpy as np
import jax
import jax.numpy as jnp
from jax import lax
from jax.experimental import pallas as pl
from jax.experimental.pallas import tpu as pltpu

F32 = jnp.float32
BF16 = jnp.bfloat16

D_MODEL = 1024
HEAD_DIM = 64
ATTN_PAIRS = ((128, 1), (512, 4), (2048, 16))
ATTN_HEADS_PER_GROUP = 8
ATTN_GROUP_W = ATTN_HEADS_PER_GROUP * HEAD_DIM
ATTN_HALF = 64
ROPE_THETA = 10000.0
RET_HEADS = 8
RET_QK_DIM = 64
RET_V_DIM = 128
RET_CHUNK = 128
RET_QK_W = RET_HEADS * RET_QK_DIM
RET_V_W = RET_HEADS * RET_V_DIM
MOE_GROUPS = 8
MOE_EXPERTS_PER_GROUP = 8
MOE_N_EXPERTS = MOE_GROUPS * MOE_EXPERTS_PER_GROUP
MOE_HIDDEN = 512
NORM_EPS = 1e-6

LANES = 128
COL_CHUNKS = D_MODEL // LANES
ROW_TILES = COL_CHUNKS // 2
U32 = jnp.uint32
BF16_BITS = 16
NEG_BIG = -1e30
LOG2_E = 1.4426950408889634

TM_INPROJ = 512
TM_MERGE = 1024
MERGE_SPLIT = 4
MERGE_COL_CHUNKS = 4
TM_COMBINE = 1024
MOE_BM = 512
DISPATCH_CHUNK = 4096
ATTN_QB = 128
ATTN_UNROLL = 8
RET_UNROLL = 8

V7X_VMEM_BYTES = 64 * 1024 * 1024
VMEM_LIMIT = V7X_VMEM_BYTES * 7 // 8

_A = 3 * ATTN_GROUP_W
OFF_QA, OFF_KA, OFF_VA = 0, _A, 2 * _A
OFF_QR = 3 * _A
OFF_KR = OFF_QR + RET_QK_W
OFF_VR = OFF_KR + RET_QK_W
OFF_GR = OFF_VR + RET_V_W
OFF_GL = OFF_GR + RET_V_W
IN_W = OFF_GL + 2 * D_MODEL


def _cparams(sem, vmem=VMEM_LIMIT):
    return pltpu.CompilerParams(dimension_semantics=sem, vmem_limit_bytes=vmem)


def _inproj_body(x_ref, nw_ref, w_ref, bg_ref, cos_ref, sin_ref,
                 qa0, ka0, va0, qa1, ka1, va1, qa2, ka2, va2, qr, kr, vr, gr, gt,
                 stage_ref):
    tm = x_ref.shape[0]
    x = x_ref[...]
    ms = jnp.mean(x * x, axis=-1, keepdims=True)
    xn = (x * lax.rsqrt(ms + NORM_EPS) * nw_ref[...]).astype(BF16)
    cos = cos_ref[...]
    sin = sin_ref[...]
    lane = lax.broadcasted_iota(jnp.int32, (tm, LANES), 1)
    first_half = (lane & (HEAD_DIM - 1)) < (HEAD_DIM // 2)

    def proj(c0, width):
        return jnp.dot(xn, w_ref[:, c0:c0 + width], preferred_element_type=F32)

    def rotary(a, scale):
        partner = jnp.where(first_half, pltpu.roll(a, LANES - HEAD_DIM // 2, 1),
                            pltpu.roll(a, HEAD_DIM // 2, 1))
        r = a * cos + partner * sin
        return r * scale if scale != 1.0 else r

    def chunks(acc):
        return [acc[:, c * LANES:(c + 1) * LANES] for c in range(acc.shape[1] // LANES)]

    def store_natural(out_ref, acc, fn):
        for c, a in enumerate(chunks(acc)):
            out_ref[:, c * LANES:(c + 1) * LANES] = fn(a).astype(out_ref.dtype)

    def store_pairs(out_ref, acc, fn):
        for c, a in enumerate(chunks(acc)):
            out_ref[0, c] = fn(a).astype(out_ref.dtype)

    def store_strided(out_ref, acc, fn, d):
        for c, a in enumerate(chunks(acc)):
            stage_ref[c] = fn(a)
        for c in range(acc.shape[1] // LANES):
            for r in range(d):
                out_ref[0, c, r] = stage_ref[c, pl.ds(r, tm // d, stride=d), :].astype(out_ref.dtype)

    ident = lambda a: a
    rot_q = lambda a: rotary(a, HEAD_DIM ** -0.5 * LOG2_E)
    rot_1 = lambda a: rotary(a, 1.0)
    rot_k = lambda a: rotary(a, RET_QK_DIM ** -0.5)

    W = ATTN_GROUP_W
    store_pairs(qa0, proj(OFF_QA, W), rot_q)
    store_pairs(ka0, proj(OFF_KA, W), rot_1)
    store_pairs(va0, proj(OFF_VA, W), ident)
    for g, (qo, ko, vo) in ((1, (qa1, ka1, va1)), (2, (qa2, ka2, va2))):
        d = ATTN_PAIRS[g][1]
        store_strided(qo, proj(OFF_QA + g * W, W), rot_q, d)
        store_strided(ko, proj(OFF_KA + g * W, W), rot_1, d)
        store_strided(vo, proj(OFF_VA + g * W, W), ident, d)
    store_natural(qr, proj(OFF_QR, RET_QK_W), rot_1)
    store_natural(kr, proj(OFF_KR, RET_QK_W), rot_k)
    for h in range(RET_V_W // W):
        vr[:, h * W:(h + 1) * W] = proj(OFF_VR + h * W, W).astype(vr.dtype)
        gr[:, h * W:(h + 1) * W] = proj(OFF_GR + h * W, W).astype(gr.dtype)
    for h in range(2 * D_MODEL // W):
        z = proj(OFF_GL + h * W, W) + bg_ref[:, h * W:(h + 1) * W]
        gt[:, h * W:(h + 1) * W] = jax.nn.sigmoid(z).astype(gt.dtype)


def _in_projection(x2, norm_w, w_bf, b_gate, cos_t, sin_t, B, S):
    T = B * S
    tm = TM_INPROJ
    nt = S // tm
    W = ATTN_GROUP_W
    row = lambda i: (i, 0)
    const = lambda i: (0, 0)
    nat = lambda width: pl.BlockSpec((tm, width), row)

    P = W // LANES

    def strided_spec(d):
        return pl.BlockSpec((1, P, d, tm // d, LANES), lambda i: (i // nt, 0, 0, i % nt, 0))

    def strided_shape(d):
        return jax.ShapeDtypeStruct((B, P, d, S // d, LANES), BF16)

    pair_spec = pl.BlockSpec((1, P, tm, LANES), lambda i: (i // nt, 0, i % nt, 0))
    pair_shape = jax.ShapeDtypeStruct((B, P, S, LANES), BF16)
    nat_shape = lambda width: jax.ShapeDtypeStruct((T, width), BF16)
    d1, d2 = ATTN_PAIRS[1][1], ATTN_PAIRS[2][1]
    out_shape = ([pair_shape] * 3 + [strided_shape(d1)] * 3 + [strided_shape(d2)] * 3
                 + [nat_shape(RET_QK_W)] * 2 + [nat_shape(RET_V_W)] * 2 + [nat_shape(2 * D_MODEL)])
    out_specs = ([pair_spec] * 3 + [strided_spec(d1)] * 3 + [strided_spec(d2)] * 3
                 + [nat(RET_QK_W)] * 2 + [nat(RET_V_W)] * 2 + [nat(2 * D_MODEL)])
    in_specs = [
        pl.BlockSpec((tm, D_MODEL), row),
        pl.BlockSpec((1, D_MODEL), const),
        pl.BlockSpec((D_MODEL, IN_W), const, pipeline_mode=pl.Buffered(1)),
        pl.BlockSpec((1, 2 * D_MODEL), const),
        pl.BlockSpec((tm, LANES), lambda i: (i % nt, 0)),
        pl.BlockSpec((tm, LANES), lambda i: (i % nt, 0)),
    ]
    return pl.pallas_call(
        _inproj_body,
        grid=(T // tm,),
        in_specs=in_specs,
        out_specs=out_specs,
        out_shape=out_shape,
        scratch_shapes=[pltpu.VMEM((W // LANES, tm, LANES), F32)],
        compiler_params=_cparams(("parallel",)),
        name="in_projection",
    )(x2, norm_w, w_bf, b_gate, cos_t, sin_t)


def _attn_body(q0_ref, k0_ref, v0_ref, q1_ref, k1_ref, v1_ref, q2_ref, k2_ref, v2_ref, o_ref,
               part_ref, bias_ref, *, S):
    QB, H = ATTN_QB, ATTN_HALF
    lane = lax.broadcasted_iota(jnp.int32, (QB, LANES), 1)
    head0 = lane < HEAD_DIM

    qi = lax.broadcasted_iota(jnp.int32, (QB, QB + 2 * H), 0)
    ki = lax.broadcasted_iota(jnp.int32, (QB, QB + 2 * H), 1)
    for n in range(3):
        bias_ref[n] = jnp.where(jnp.abs(ki - qi - n * H) <= H, 0.0, NEG_BIG).astype(F32)

    def scores(q_rows, k_rows, bias, h):
        qm = jnp.where(head0 if h == 0 else jnp.logical_not(head0), q_rows, jnp.zeros_like(q_rows))
        return lax.dot_general(qm, k_rows, (((1,), (1,)), ((), ())), preferred_element_type=F32) + bias

    def weights(s):
        m = jnp.max(s, axis=-1, keepdims=True)
        return m, jnp.exp2(s - m).astype(BF16)

    def heads_to_lanes(m0, a, m1, b):
        num = jnp.where(head0, a, b)
        den = pltpu.roll(jnp.where(head0, b, a), HEAD_DIM, 1)
        mx = jnp.where(head0, m0, m1)
        return num, den, mx

    def run_group(q_ref, k_ref, v_ref, d, prepare, store):
        L = S // d
        KW = min(L, QB + 2 * H)
        nb = L // QB

        key_head0 = lax.broadcasted_iota(jnp.int32, (KW, LANES), 1) < HEAD_DIM
        key_ones = jnp.ones((KW, LANES), BF16)

        def trip(i, carry):
            blocks = []
            for u in range(ATTN_UNROLL):
                t = i * ATTN_UNROLL + u
                r = t // nb
                q0 = pl.multiple_of((t % nb) * QB, QB)
                ws = pl.multiple_of(jnp.clip(q0 - H, 0, L - KW), H)
                bias = bias_ref[(q0 - ws) // H][:, :KW]
                q_rows = q_ref[0, 0, r, pl.ds(q0, QB), :]
                k_rows = k_ref[0, 0, r, pl.ds(ws, KW), :]
                blocks.append((r, q0, ws, [scores(q_rows, k_rows, bias, h) for h in range(2)]))
            blocks = [(r, q0, ws, [weights(s) for s in ss]) for r, q0, ws, ss in blocks]
            done = []
            for r, q0, ws, ((m0, p0), (m1, p1)) in blocks:
                v_rows = v_ref[0, 0, r, pl.ds(ws, KW), :]
                a = jnp.dot(p0, jnp.where(key_head0, v_rows, key_ones), preferred_element_type=F32)
                b = jnp.dot(p1, jnp.where(key_head0, key_ones, v_rows), preferred_element_type=F32)
                done.append((r, q0, m0, a, m1, b))
            done = [(r, q0, prepare(q0, *heads_to_lanes(m0, a, m1, b))) for r, q0, m0, a, m1, b in done]
            for r, q0, vals in done:
                store(r, q0, vals)
            return carry

        lax.fori_loop(0, S // QB // ATTN_UNROLL, trip, 0)

    def store_partial(g):
        d = ATTN_PAIRS[g][1]

        def store(r, q0, vals):
            rows = pl.ds(r + q0 * d, QB, stride=d)
            for n, val in enumerate(vals):
                part_ref[3 * (g - 1) + n, rows, :] = val
        return store

    keep = lambda q0, num, den, mx: (num, den, mx)
    run_group(q1_ref, k1_ref, v1_ref, ATTN_PAIRS[1][1], keep, store_partial(1))
    run_group(q2_ref, k2_ref, v2_ref, ATTN_PAIRS[2][1], keep, store_partial(2))

    def merge(q0, num, den, mx):
        rows = pl.ds(q0, QB)
        nums = [num, part_ref[0, rows, :], part_ref[3, rows, :]]
        dens = [den, part_ref[1, rows, :], part_ref[4, rows, :]]
        mxs = [mx, part_ref[2, rows, :], part_ref[5, rows, :]]
        top = jnp.maximum(jnp.maximum(mxs[0], mxs[1]), mxs[2])
        ws = [jnp.exp2(m - top) for m in mxs]
        n = ws[0] * nums[0] + ws[1] * nums[1] + ws[2] * nums[2]
        dn = ws[0] * dens[0] + ws[1] * dens[1] + ws[2] * dens[2]
        return (n / dn).astype(o_ref.dtype)

    def store_out(r, q0, y):
        o_ref[0, 0, pl.ds(q0, QB), :] = y

    run_group(q0_ref, k0_ref, v0_ref, ATTN_PAIRS[0][1], merge, store_out)


def _attention(qkv, B, S):
    P = ATTN_GROUP_W // LANES
    in_specs = []
    for _, d in ATTN_PAIRS:
        in_specs += [pl.BlockSpec((1, 1, d, S // d, LANES), lambda b, p: (b, p, 0, 0, 0))] * 3
    return pl.pallas_call(
        functools.partial(_attn_body, S=S),
        grid=(B, P),
        in_specs=in_specs,
        out_specs=pl.BlockSpec((1, 1, S, LANES), lambda b, p: (b, p, 0, 0)),
        out_shape=jax.ShapeDtypeStruct((B, P, S, LANES), BF16),
        scratch_shapes=[pltpu.VMEM((6, S, LANES), F32),
                        pltpu.VMEM((3, ATTN_QB, ATTN_QB + 2 * ATTN_HALF), F32)],
        compiler_params=_cparams(("parallel", "parallel")),
        name="attention",
    )(*qkv)


def _log_sigmoid(z):
    return jnp.minimum(z, 0.0) - jnp.log(1.0 + jnp.exp(-jnp.abs(z)))


def _ret_body(dec_ref, q_ref, k_ref, v_ref, g_ref, gnw_ref, o_ref, kt_ref, sf_ref, sb_ref, st_ref, *, S):
    C = RET_CHUNK
    nc = S // C
    lg = _log_sigmoid(dec_ref[0])
    a_row = lax.broadcasted_iota(jnp.int32, (C, LANES), 0).astype(F32)
    lane = lax.broadcasted_iota(jnp.int32, (C, LANES), 1)
    rel = (lax.broadcasted_iota(jnp.int32, (C, C), 0) - lax.broadcasted_iota(jnp.int32, (C, C), 1)).astype(F32)

    heads = []
    for h in range(2):
        lgf = lg[h:h + 1, :]
        lgb = lg[2 + h:3 + h, :]
        in_head = (lane < RET_QK_DIM) if h == 0 else (lane >= RET_QK_DIM)
        heads.append(dict(
            in_head=in_head,
            xi_f=jnp.where(in_head, jnp.exp((a_row + 1.0) * lgf), 0.0),
            xi_b=jnp.where(in_head, jnp.exp((C - a_row) * lgb), 0.0),
            zeta_f=jnp.exp((C - 1.0 - a_row) * lgf),
            zeta_b=jnp.exp(a_row * lgb),
            dloc=jnp.where(rel > 0, jnp.exp(rel * lgf), jnp.where(rel < 0, jnp.exp(-rel * lgb), 2.0)),
            cd_f=jnp.exp(C * lgf),
            cd_b=jnp.exp(C * lgb),
        ))

    def chunk_rows(n):
        return pl.ds(pl.multiple_of(n * C, C), C)

    def v_head(rows, h):
        return v_ref[0, rows, h * RET_V_DIM:(h + 1) * RET_V_DIM]

    def products(i, carry):
        items = []
        for u in range(RET_UNROLL):
            n = i * RET_UNROLL + u
            rows = chunk_rows(n)
            kt = k_ref[0, rows, :].astype(F32).T.astype(BF16)
            kt_ref[n] = kt
            for h, hd in enumerate(heads):
                vh = v_head(rows, h).astype(F32)
                items.append((n, h, kt, (vh * hd["zeta_f"]).astype(BF16), (vh * hd["zeta_b"]).astype(BF16)))
        outs = [(n, h, jnp.dot(kt, vf, preferred_element_type=F32), jnp.dot(kt, vb, preferred_element_type=F32))
                for n, h, kt, vf, vb in items]
        for n, h, f, b in outs:
            sf_ref[n, h] = f
            sb_ref[n, h] = b
        return carry

    lax.fori_loop(0, nc // RET_UNROLL, products, 0)

    def scan(ref, key, order, half):
        rows = slice(half * 2 * RET_QK_DIM, (half + 1) * 2 * RET_QK_DIM)

        def step(i, state):
            n = order(i)
            new = []
            for h, hd in enumerate(heads):
                st_ref[n, h, rows, :] = state[h].astype(BF16)
                new.append(hd[key] * state[h] + ref[n, h])
            return tuple(new)
        zero = jnp.zeros(ref.shape[2:], F32)
        lax.fori_loop(0, nc, step, (zero, zero))

    scan(sf_ref, "cd_f", lambda i: i, 0)
    scan(sb_ref, "cd_b", lambda i: nc - 1 - i, 1)

    def outputs(i, carry):
        items = []
        for u in range(RET_UNROLL):
            n = i * RET_UNROLL + u
            rows = chunk_rows(n)
            qp = q_ref[0, rows, :]
            qf = qp.astype(F32)
            kt = kt_ref[n]
            for h, hd in enumerate(heads):
                qm = jnp.where(hd["in_head"], qp, jnp.zeros_like(qp))
                qx = jnp.concatenate([(qf * hd["xi_f"]).astype(BF16), (qf * hd["xi_b"]).astype(BF16)], axis=1)
                items.append((rows, h, hd, jnp.dot(qm, kt, preferred_element_type=F32),
                              jnp.dot(qx, st_ref[n, h], preferred_element_type=F32)))
        items = [(rows, h, hd, (s * hd["dloc"]).astype(BF16), cross) for rows, h, hd, s, cross in items]
        items = [(rows, h, cross + jnp.dot(p, v_head(rows, h), preferred_element_type=F32))
                 for rows, h, hd, p, cross in items]
        for rows, h, ret in items:
            mu = jnp.mean(ret, axis=-1, keepdims=True)
            xc = ret - mu
            var = jnp.mean(xc * xc, axis=-1, keepdims=True)
            cols = slice(h * RET_V_DIM, (h + 1) * RET_V_DIM)
            gate = g_ref[0, rows, cols].astype(F32)
            y = xc * lax.rsqrt(var + NORM_EPS) * gnw_ref[:, cols] * (gate * jax.nn.sigmoid(gate))
            o_ref[0, rows, cols] = y.astype(o_ref.dtype)
        return carry

    lax.fori_loop(0, nc // RET_UNROLL, outputs, 0)


def _retention(dec, qr, kr, vr, gr, gn_w, B, S):
    nc = S // RET_CHUNK
    npairs = RET_HEADS // 2
    qk_spec = pl.BlockSpec((1, S, 2 * RET_QK_DIM), lambda b, p: (b, 0, p))
    v_spec = pl.BlockSpec((1, S, 2 * RET_V_DIM), lambda b, p: (b, 0, p))
    return pl.pallas_call(
        functools.partial(_ret_body, S=S),
        grid=(B, npairs),
        in_specs=[pl.BlockSpec((1, 4, LANES), lambda b, p: (p, 0, 0)),
                  qk_spec, qk_spec, v_spec, v_spec,
                  pl.BlockSpec((1, 2 * RET_V_DIM), lambda b, p: (0, p))],
        out_specs=v_spec,
        out_shape=jax.ShapeDtypeStruct((B, S, RET_V_W), BF16),
        scratch_shapes=[pltpu.VMEM((nc, 2 * RET_QK_DIM, RET_CHUNK), BF16),
                        pltpu.VMEM((nc, 2, 2 * RET_QK_DIM, RET_V_DIM), F32),
                        pltpu.VMEM((nc, 2, 2 * RET_QK_DIM, RET_V_DIM), F32),
                        pltpu.VMEM((nc, 2, 4 * RET_QK_DIM, RET_V_DIM), BF16)],
        compiler_params=_cparams(("parallel", "parallel")),
        name="retention",
    )(dec, qr.reshape(B, S, RET_QK_W), kr.reshape(B, S, RET_QK_W),
      vr.reshape(B, S, RET_V_W), gr.reshape(B, S, RET_V_W), gn_w)


ROUTE_EID, ROUTE_RANK, ROUTE_GATE = 0, 2, 4
ROUTE_FIELDS = 8
ROUTER_EXPERT_LANE0 = MOE_GROUPS
GROUP_SHIFT = MOE_EXPERTS_PER_GROUP.bit_length() - 1


def _merge_body(yatt_ref, yret_ref, gt_ref, x_ref, wa_ref, wb_ref, wo_ref,
                nw_ref, wrh_ref, wrl_ref, br_ref,
                h_ref, hn_ref, route_ref, route_t_ref, cnt_ref):
    tm = x_ref.shape[0]
    hm = tm // MERGE_SPLIT
    i = pl.program_id(0)
    cw = D_MODEL // MERGE_COL_CHUNKS

    def branch_products(rows):
        y_att = jnp.concatenate([yatt_ref[0, c, rows, :] for c in range(yatt_ref.shape[1])], axis=1)
        y_ret = yret_ref[rows, :]
        chunks = []
        for c in range(MERGE_COL_CHUNKS):
            cols = slice(c * cw, (c + 1) * cw)
            a = jnp.dot(y_att, wa_ref[:, cols], preferred_element_type=F32)
            b = jnp.dot(y_ret, wb_ref[:, cols], preferred_element_type=F32)
            g_att = gt_ref[rows, c * cw:(c + 1) * cw].astype(F32)
            g_ret = gt_ref[rows, D_MODEL + c * cw:D_MODEL + (c + 1) * cw].astype(F32)
            chunks.append((g_att * a + g_ret * b).astype(BF16))
        return jnp.concatenate(chunks, axis=1)

    def residual_norm(hf, rows, merged):
        mix = jnp.dot(merged, wo_ref[...], preferred_element_type=F32)
        h = x_ref[rows, :] + mix
        h_ref[rows, :] = h
        ms = jnp.mean(h * h, axis=-1, keepdims=True)
        hn = h * lax.rsqrt(ms + NORM_EPS) * nw_ref[...]
        for j, word in enumerate(_pack_row(hn)):
            hn_ref[pl.ds(hf * hm * ROW_TILES + j, hm, stride=ROW_TILES), :] = word
        hi = hn.astype(BF16)
        lo = (hn - hi.astype(F32)).astype(BF16)
        return hi, lo

    def router_logits(hi, lo):
        return (jnp.dot(hi, wrh_ref[...], preferred_element_type=F32)
                + jnp.dot(hi, wrl_ref[...], preferred_element_type=F32)
                + jnp.dot(lo, wrh_ref[...], preferred_element_type=F32)) + br_ref[...]

    lane = lax.broadcasted_iota(jnp.int32, (hm, LANES), 1)
    far = jnp.int32(LANES)

    def first_argmax(vals, vmax):
        return jnp.min(jnp.where(vals == vmax, lane, far), axis=-1, keepdims=True)

    def route(logits):
        is_group = lane < MOE_GROUPS
        gl = jnp.where(is_group, logits, NEG_BIG)
        gmax = jnp.max(gl, axis=-1, keepdims=True)
        g_w = 1.0 / jnp.sum(jnp.where(is_group, jnp.exp(gl - gmax), 0.0), axis=-1, keepdims=True)
        g_idx = first_argmax(gl, gmax)
        e_lane = lane - ROUTER_EXPERT_LANE0
        in_group = (e_lane >= 0) & (e_lane < MOE_N_EXPERTS) & (jnp.right_shift(e_lane, GROUP_SHIFT) == g_idx)
        el = jnp.where(in_group, logits, NEG_BIG)
        m1 = jnp.max(el, axis=-1, keepdims=True)
        i1 = first_argmax(el, m1)
        el2 = jnp.where(lane == i1, NEG_BIG, el)
        m2 = jnp.max(el2, axis=-1, keepdims=True)
        i2 = first_argmax(el2, m2)
        ex = jnp.exp(m2 - m1)
        return i1, i2, g_w / (1.0 + ex), g_w * ex / (1.0 + ex)

    rows = [slice(hf * hm, (hf + 1) * hm) for hf in range(MERGE_SPLIT)]
    merged = [branch_products(r) for r in rows]
    split = [residual_norm(hf, r, m) for hf, (r, m) in enumerate(zip(rows, merged))]
    routed = [route(router_logits(hi, lo)) for hi, lo in split]

    @pl.when(i == 0)
    def _():
        cnt_ref[...] = jnp.zeros(cnt_ref.shape, F32)

    r_idx = lax.broadcasted_iota(jnp.int32, (hm, hm), 0)
    c_idx = lax.broadcasted_iota(jnp.int32, (hm, hm), 1)
    lower = jnp.where(c_idx < r_idx, 1.0, 0.0).astype(BF16)
    running = cnt_ref[...]
    for hf, (i1, i2, gate1, gate2) in enumerate(routed):
        hot1 = lane == i1
        hot2 = lane == i2
        onehot = jnp.where(hot1 | hot2, 1.0, 0.0)
        before = jnp.dot(lower, onehot.astype(BF16), preferred_element_type=F32) + running
        rank1 = jnp.sum(jnp.where(hot1, before, 0.0), axis=-1, keepdims=True)
        rank2 = jnp.sum(jnp.where(hot2, before, 0.0), axis=-1, keepdims=True)
        running = running + jnp.sum(onehot, axis=0, keepdims=True)
        rec = jnp.zeros((hm, LANES), F32)
        for pos, val in ((ROUTE_EID, (i1 - ROUTER_EXPERT_LANE0).astype(F32)),
                         (ROUTE_EID + 1, (i2 - ROUTER_EXPERT_LANE0).astype(F32)),
                         (ROUTE_RANK, rank1), (ROUTE_RANK + 1, rank2),
                         (ROUTE_GATE, gate1), (ROUTE_GATE + 1, gate2)):
            rec = jnp.where(lane == pos, val, rec)
        route_ref[rows[hf], :] = rec
        route_t_ref[:, hf * hm:(hf + 1) * hm] = rec.T[:route_t_ref.shape[0], :]
    cnt_ref[...] = running


def _merge_route(y_att, y_ret, gates, x2, wa, wb, wo, nw, wr_hi, wr_lo, b_r, T, S):
    tm = TM_MERGE
    nt = S // tm
    row = lambda i: (i, 0)
    const = lambda i: (0, 0)
    full = lambda arr: pl.BlockSpec(arr.shape, const)
    in_specs = ([pl.BlockSpec((1, y_att.shape[1], tm, LANES), lambda i: (i // nt, 0, i % nt, 0)),
                 pl.BlockSpec((tm, RET_V_W), row), pl.BlockSpec((tm, 2 * D_MODEL), row),
                 pl.BlockSpec((tm, D_MODEL), row),
                 full(wa), full(wb), full(wo), full(nw), full(wr_hi), full(wr_lo), full(b_r)])
    return pl.pallas_call(
        _merge_body,
        grid=(T // tm,),
        in_specs=in_specs,
        out_specs=[pl.BlockSpec((tm, D_MODEL), row),
                   pl.BlockSpec((tm * ROW_TILES, LANES), row),
                   pl.BlockSpec((tm, LANES), row), pl.BlockSpec((ROUTE_FIELDS, tm), lambda i: (0, i)),
                   pl.BlockSpec((1, LANES), const)],
        out_shape=[jax.ShapeDtypeStruct((T, D_MODEL), F32), jax.ShapeDtypeStruct((T * ROW_TILES, LANES), U32),
                   jax.ShapeDtypeStruct((T, LANES), F32), jax.ShapeDtypeStruct((ROUTE_FIELDS, T), F32),
                   jax.ShapeDtypeStruct((1, LANES), F32)],
        compiler_params=_cparams(("arbitrary",)),
        name="merge_route",
    )(y_att, y_ret, gates, x2, wa, wb, wo, nw, wr_hi, wr_lo, b_r)


ISSUE_UNROLL = 8


def _tile_rows(n):
    return pl.ds(pl.multiple_of(n * ROW_TILES, ROW_TILES), ROW_TILES)


def _pack_row(x):
    bits = lambda c: pltpu.bitcast(x[:, c * LANES:(c + 1) * LANES].astype(BF16).astype(F32), U32)
    return [(bits(j) >> BF16_BITS) | bits(j + ROW_TILES) for j in range(ROW_TILES)]


def _unpack_word(w):
    high = jnp.uint32(((1 << BF16_BITS) - 1) << BF16_BITS)
    return pltpu.bitcast(w << BF16_BITS, F32), pltpu.bitcast(w & high, F32)


def _dispatch_body(slot_ref, hn_ref, xs_ref, sem, *, T):
    i = pl.program_id(0)
    ch = hn_ref.shape[0] // ROW_TILES

    def row_copy(j, slot):
        return pltpu.make_async_copy(hn_ref.at[_tile_rows(j)], xs_ref.at[_tile_rows(slot)], sem)

    def issue(j, carry):
        t = i * ch + j
        row_copy(j, slot_ref[t]).start(priority=0)
        row_copy(j, slot_ref[T + t]).start(priority=1)
        return carry

    lax.fori_loop(0, ch, issue, 0, unroll=ISSUE_UNROLL)
    for _ in range(2):
        pltpu.make_async_copy(hn_ref, xs_ref.at[pl.ds(0, ch * ROW_TILES)], sem).wait()


def _dispatch(slots, hn, n_slots, T):
    ch = DISPATCH_CHUNK
    grid_spec = pltpu.PrefetchScalarGridSpec(
        num_scalar_prefetch=1,
        grid=(T // ch,),
        in_specs=[pl.BlockSpec((ch * ROW_TILES, LANES), lambda i, s: (i, 0))],
        out_specs=pl.BlockSpec(memory_space=pl.ANY),
        scratch_shapes=[pltpu.SemaphoreType.DMA(())],
    )
    return pl.pallas_call(
        functools.partial(_dispatch_body, T=T),
        grid_spec=grid_spec,
        out_shape=jax.ShapeDtypeStruct((n_slots * ROW_TILES, LANES), U32),
        compiler_params=_cparams(("arbitrary",)),
        name="moe_dispatch",
    )(slots, hn)


def _expert_body(blk_ref, eid_ref, valid_ref, fresh_ref, next_ref, x_ref, w1_ref, w3_ref, w2_ref, y_ref,
                 w1b, w3b, w2b, w1s, w3s, w2s, wsem):
    i = pl.program_id(0)
    valid = valid_ref[i]

    def weight_copies(e):
        return (pltpu.make_async_copy(w1_ref.at[e], w1s, wsem.at[0]),
                pltpu.make_async_copy(w3_ref.at[e], w3s, wsem.at[1]),
                pltpu.make_async_copy(w2_ref.at[e], w2s, wsem.at[2]))

    @pl.when(i == 0)
    def _():
        for cp in weight_copies(eid_ref[0]):
            cp.start()

    @pl.when(valid > 0)
    def _():
        @pl.when(fresh_ref[i] == 1)
        def _():
            for cp in weight_copies(eid_ref[i]):
                cp.wait()
            w1b[...] = w1s[...].astype(BF16)
            w3b[...] = w3s[...].astype(BF16)
            w2b[...] = w2s[...].astype(BF16)

            @pl.when(next_ref[i] >= 0)
            def _():
                for cp in weight_copies(next_ref[i]):
                    cp.start()

        bm = x_ref.shape[0] // ROW_TILES
        live =lax.broadcasted_iota(jnp.int32, (bm, LANES), 0) < valid
        halves = [_unpack_word(x_ref[pl.ds(j, bm, stride=ROW_TILES), :]) for j in range(ROW_TILES)]
        x = jnp.concatenate(
            [jnp.where(live, c, 0.0).astype(BF16) for c in [lo for lo, _ in halves] + [hi for _, hi in halves]],
            axis=1)
        a = jnp.dot(x, w1b[...], preferred_element_type=F32)
        b = jnp.dot(x, w3b[...], preferred_element_type=F32)
        hid = (a * jax.nn.sigmoid(a) * b).astype(BF16)
        y = jnp.dot(hid, w2b[...], preferred_element_type=F32)
        for j, word in enumerate(_pack_row(y)):
            y_ref[pl.ds(j, bm, stride=ROW_TILES), :] = word


def _experts(blk, blk_eid, blk_valid, blk_fresh, blk_next, x_slots, w1, w3, w2, n_blocks):
    bm = MOE_BM
    slot_block = lambda i, blk, eid, val, fr, nx: (blk[i], 0)
    grid_spec = pltpu.PrefetchScalarGridSpec(
        num_scalar_prefetch=5,
        grid=(n_blocks,),
        in_specs=[pl.BlockSpec((bm * ROW_TILES, LANES), slot_block),
                  pl.BlockSpec(memory_space=pl.ANY), pl.BlockSpec(memory_space=pl.ANY),
                  pl.BlockSpec(memory_space=pl.ANY)],
        out_specs=pl.BlockSpec((bm * ROW_TILES, LANES), slot_block),
        scratch_shapes=[pltpu.VMEM((D_MODEL, MOE_HIDDEN), BF16), pltpu.VMEM((D_MODEL, MOE_HIDDEN), BF16),
                        pltpu.VMEM((MOE_HIDDEN, D_MODEL), BF16),
                        pltpu.VMEM((D_MODEL, MOE_HIDDEN), F32), pltpu.VMEM((D_MODEL, MOE_HIDDEN), F32),
                        pltpu.VMEM((MOE_HIDDEN, D_MODEL), F32), pltpu.SemaphoreType.DMA((3,))],
    )
    return pl.pallas_call(
        _expert_body,
        grid_spec=grid_spec,
        out_shape=jax.ShapeDtypeStruct(x_slots.shape, U32),
        compiler_params=_cparams(("arbitrary",)),
        name="moe_experts",
    )(blk, blk_eid, blk_valid, blk_fresh, blk_next, x_slots, w1, w3, w2)


def _combine_body(slot_ref, ys_ref, h_ref, route_ref, nw_ref, o_ref, ybuf, sem, *, T):
    i = pl.program_id(0)
    n = pl.num_programs(0)
    tm = h_ref.shape[0]

    def row_copy(slot, buf, k, j):
        return pltpu.make_async_copy(ys_ref.at[_tile_rows(slot)], ybuf.at[buf, k, _tile_rows(j)], sem.at[buf])

    def issue(tile, buf):
        def one(j, carry):
            t = tile * tm + j
            row_copy(slot_ref[t], buf, 0, j).start(priority=0)
            row_copy(slot_ref[T + t], buf, 1, j).start(priority=1)
            return carry
        lax.fori_loop(0, tm, one, 0, unroll=ISSUE_UNROLL)

    @pl.when(i == 0)
    def _():
        issue(0, 0)

    @pl.when(i + 1 < n)
    def _():
        issue(i + 1, (i + 1) % 2)

    buf = i % 2
    for k in range(2):
        pltpu.make_async_copy(ys_ref.at[pl.ds(0, tm * ROW_TILES)], ybuf.at[buf, k], sem.at[buf]).wait()
    route = route_ref[...]
    g1 = route[:, ROUTE_GATE:ROUTE_GATE + 1]
    g2 = route[:, ROUTE_GATE + 1:ROUTE_GATE + 2]
    hs = [None] * COL_CHUNKS
    ss = jnp.zeros((tm, 1), F32)
    for j in range(ROW_TILES):
        tile_row = pl.ds(j, tm, stride=ROW_TILES)
        first = _unpack_word(ybuf[buf, 0, tile_row, :])
        second = _unpack_word(ybuf[buf, 1, tile_row, :])
        for c, y1, y2 in ((j, first[0], second[0]), (j + ROW_TILES, first[1], second[1])):
            hc = h_ref[:, c * LANES:(c + 1) * LANES] + (y1 * g1 + y2 * g2)
            hs[c] = hc
            ss = ss + jnp.sum(hc * hc, axis=-1, keepdims=True)
    inv = lax.rsqrt(ss * (1.0 / D_MODEL) + NORM_EPS)
    for j, hj in enumerate(hs):
        cols = slice(j * LANES, (j + 1) * LANES)
        o_ref[:, cols] = hj * inv * nw_ref[:, cols]


def _combine(slots, y_slots, h, route, nw, T):
    tm = TM_COMBINE
    row = lambda i, s: (i, 0)
    grid_spec = pltpu.PrefetchScalarGridSpec(
        num_scalar_prefetch=1,
        grid=(T // tm,),
        in_specs=[pl.BlockSpec(memory_space=pl.ANY),
                  pl.BlockSpec((tm, D_MODEL), row),
                  pl.BlockSpec((tm, LANES), row),
                  pl.BlockSpec((1, D_MODEL), lambda i, s: (0, 0))],
        out_specs=pl.BlockSpec((tm, D_MODEL), row),
        scratch_shapes=[pltpu.VMEM((2, 2, tm * ROW_TILES, LANES), U32), pltpu.SemaphoreType.DMA((2,))],
    )
    return pl.pallas_call(
        functools.partial(_combine_body, T=T),
        grid_spec=grid_spec,
        out_shape=jax.ShapeDtypeStruct((T, D_MODEL), F32),
        compiler_params=_cparams(("arbitrary",)),
        name="moe_combine",
    )(slots, y_slots, h, route, nw)


def _rotary_tables(S):
    inv_freq = (1.0 / (np.float32(ROPE_THETA) ** (np.arange(0, HEAD_DIM, 2, dtype=np.float32) / HEAD_DIM))
                ).astype(np.float32)
    ang = np.arange(S, dtype=np.float32)[:, None] * inv_freq[None, :]
    cos, sin = np.cos(ang), np.sin(ang)
    reps = LANES // HEAD_DIM
    cos_t = np.tile(np.concatenate([cos, cos], axis=1), (1, reps)).astype(np.float32)
    sin_t = np.tile(np.concatenate([-sin, sin], axis=1), (1, reps)).astype(np.float32)
    return jnp.asarray(cos_t), jnp.asarray(sin_t)


def _layer(h_in, norm_mix_w, w_in, b_branch_gate, ret_decay_fwd, ret_decay_bwd, ret_gn_w, w_attn_branch,
           w_ret_branch, w_out, norm_moe_w, moe_w_group, moe_b_group, moe_w_expert, moe_b_expert,
           moe_w1, moe_w3, moe_w2, next_norm_w, B, S, cos_t, sin_t):
    T = B * S
    (qa0, ka0, va0, qa1, ka1, va1, qa2, ka2, va2, qr, kr, vr, gr, gates) = _in_projection(
        h_in, norm_mix_w[None, :], w_in.astype(BF16), b_branch_gate[None, :], cos_t, sin_t, B, S)

    unit = lambda a: a[:, :, None]
    y_att = _attention((unit(qa0), unit(ka0), unit(va0), qa1, ka1, va1, qa2, ka2, va2), B, S)

    dec = jnp.stack([ret_decay_fwd.reshape(RET_HEADS // 2, 2), ret_decay_bwd.reshape(RET_HEADS // 2, 2)], axis=1)
    dec = jnp.broadcast_to(dec.reshape(RET_HEADS // 2, 4, 1), (RET_HEADS // 2, 4, LANES)).astype(F32)
    y_ret = _retention(dec, qr, kr, vr, gr, ret_gn_w[None, :], B, S).reshape(T, RET_V_W)

    pad = LANES - MOE_GROUPS - MOE_N_EXPERTS
    w_r = jnp.concatenate([moe_w_group, moe_w_expert, jnp.zeros((D_MODEL, pad), F32)], axis=1)
    w_r_hi = w_r.astype(BF16)
    w_r_lo = (w_r - w_r_hi.astype(F32)).astype(BF16)
    b_r = jnp.concatenate([moe_b_group, moe_b_expert, jnp.zeros((pad,), F32)])[None, :]

    h_mid, hn, route, route_t, cnt = _merge_route(
        y_att, y_ret, gates, h_in, w_attn_branch.astype(BF16), w_ret_branch.astype(BF16),
        w_out.astype(BF16), norm_moe_w[None, :], w_r_hi, w_r_lo, b_r, T, S)

    bm = MOE_BM
    counts = cnt[0, ROUTER_EXPERT_LANE0:ROUTER_EXPERT_LANE0 + MOE_N_EXPERTS].astype(jnp.int32)
    nblk = (counts + bm - 1) // bm
    blk_end = jnp.cumsum(nblk)
    pstart = (blk_end - nblk) * bm
    n_blocks = (2 * T) // bm + MOE_N_EXPERTS
    n_active = blk_end[-1]
    bidx = jnp.minimum(jnp.arange(n_blocks, dtype=jnp.int32), n_active - 1)
    blk_eid = jnp.sum(bidx[:, None] >= blk_end[None, :], axis=1).astype(jnp.int32)
    mine = blk_eid[:, None] == jnp.arange(MOE_N_EXPERTS, dtype=jnp.int32)[None, :]
    seg_end = jnp.sum(jnp.where(mine, (pstart + counts)[None, :], 0), axis=1)
    blk_valid = jnp.clip(seg_end - bidx * bm, 0, bm)
    blk_valid = jnp.where(jnp.arange(n_blocks) < n_active, blk_valid, 0).astype(jnp.int32)
    blk_fresh = jnp.concatenate([jnp.ones((1,), jnp.int32), (blk_eid[1:] != blk_eid[:-1]).astype(jnp.int32)])
    ar = jnp.arange(MOE_N_EXPERTS, dtype=jnp.int32)
    later = jnp.min(jnp.where((nblk > 0)[None, :] & (ar[None, :] > ar[:, None]), ar[None, :], MOE_N_EXPERTS), axis=1)
    later = jnp.where(later < MOE_N_EXPERTS, later, -1)
    blk_next = (jnp.sum(jnp.where(mine, later[None, :] + 1, 0), axis=1) - 1).astype(jnp.int32)
    eid = route_t[ROUTE_EID:ROUTE_EID + 2].astype(jnp.int32)
    rank = route_t[ROUTE_RANK:ROUTE_RANK + 2].astype(jnp.int32)
    start = jnp.sum(jnp.where(eid[..., None] == jnp.arange(MOE_N_EXPERTS, dtype=jnp.int32),
                              pstart.astype(jnp.int32), 0), axis=-1)
    slots = (start + rank).reshape(2 * T)

    x_slots = _dispatch(slots, hn, n_blocks * bm, T)
    y_slots = _experts(bidx, blk_eid, blk_valid, blk_fresh, blk_next, x_slots, moe_w1, moe_w3, moe_w2, n_blocks)
    return _combine(slots, y_slots, h_mid, route, next_norm_w[None, :], T)


def kernel(x, norm_mix_w, w_in, b_branch_gate, ret_decay_fwd, ret_decay_bwd, ret_gn_w, w_attn_branch,
           w_ret_branch, w_out, norm_moe_w, moe_w_group, moe_b_group, moe_w_expert, moe_b_expert, moe_w1,
           moe_w3, moe_w2, norm_final_w):
    B, S, D = x.shape
    depth = norm_mix_w.shape[0]
    assert depth == 1, "the final norm is fused into the layer's combine stage"
    assert D == D_MODEL and S % TM_INPROJ == 0 and (B * S) < (1 << 24)
    cos_t, sin_t = _rotary_tables(S)
    out = _layer(x.reshape(B * S, D), norm_mix_w[0], w_in[0], b_branch_gate[0], ret_decay_fwd[0],
                 ret_decay_bwd[0], ret_gn_w[0], w_attn_branch[0], w_ret_branch[0], w_out[0], norm_moe_w[0],
                 moe_w_group[0], moe_b_group[0], moe_w_expert[0], moe_b_expert[0], moe_w1[0], moe_w3[0],
                 moe_w2[0], norm_final_w, B, S, cos_t, sin_t)
    return out.reshape(B, S, D)
```

```python
import functools

import numpy as np
import jax
import jax.numpy as jnp
from jax import lax
from jax.experimental import pallas as pl
from jax.experimental.pallas import tpu as pltpu

F32 = jnp.float32
BF16 = jnp.bfloat16

D_MODEL = 1024
HEAD_DIM = 64
ATTN_PAIRS = ((128, 1), (512, 4), (2048, 16))
ATTN_HEADS_PER_GROUP = 8
ATTN_GROUP_W = ATTN_HEADS_PER_GROUP * HEAD_DIM
ATTN_HALF = 64
ROPE_THETA = 10000.0
RET_HEADS = 8
RET_QK_DIM = 64
RET_V_DIM = 128
RET_CHUNK = 128
RET_QK_W = RET_HEADS * RET_QK_DIM
RET_V_W = RET_HEADS * RET_V_DIM
MOE_GROUPS = 8
MOE_EXPERTS_PER_GROUP = 8
MOE_N_EXPERTS = MOE_GROUPS * MOE_EXPERTS_PER_GROUP
MOE_HIDDEN = 512
NORM_EPS = 1e-6

LANES = 128
COL_CHUNKS = D_MODEL // LANES
ROW_TILES = COL_CHUNKS // 2
U32 = jnp.uint32
BF16_BITS = 16
NEG_BIG = -1e30
LOG2_E = 1.4426950408889634

TM_INPROJ = 512
TM_MERGE = 1024
MERGE_SPLIT = 4
MERGE_COL_CHUNKS = 4
TM_COMBINE = 256
MOE_BM = 512
DISPATCH_CHUNK = 4096
ATTN_QB = 128
ATTN_UNROLL = 8
RET_UNROLL = 8

V7X_VMEM_BYTES = 64 * 1024 * 1024
VMEM_LIMIT = V7X_VMEM_BYTES * 7 // 8

_A = 3 * ATTN_GROUP_W
OFF_QA, OFF_KA, OFF_VA = 0, _A, 2 * _A
OFF_QR = 3 * _A
OFF_KR = OFF_QR + RET_QK_W
OFF_VR = OFF_KR + RET_QK_W
OFF_GR = OFF_VR + RET_V_W
OFF_GL = OFF_GR + RET_V_W
IN_W = OFF_GL + 2 * D_MODEL


def _cparams(sem, vmem=VMEM_LIMIT):
    return pltpu.CompilerParams(dimension_semantics=sem, vmem_limit_bytes=vmem)


def _inproj_body(x_ref, nw_ref, w_ref, bg_ref, cos_ref, sin_ref,
                 qa0, ka0, va0, qa1, ka1, va1, qa2, ka2, va2, qr, kr, vr, gr, gt,
                 stage_ref):
    tm = x_ref.shape[0]
    x = x_ref[...]
    ms = jnp.mean(x * x, axis=-1, keepdims=True)
    xn = (x * lax.rsqrt(ms + NORM_EPS) * nw_ref[...]).astype(BF16)
    cos = cos_ref[...]
    sin = sin_ref[...]
    lane = lax.broadcasted_iota(jnp.int32, (tm, LANES), 1)
    first_half = (lane & (HEAD_DIM - 1)) < (HEAD_DIM // 2)

    def proj(c0, width):
        return jnp.dot(xn, w_ref[:, c0:c0 + width], preferred_element_type=F32)

    def rotary(a, scale):
        partner = jnp.where(first_half, pltpu.roll(a, LANES - HEAD_DIM // 2, 1),
                            pltpu.roll(a, HEAD_DIM // 2, 1))
        r = a * cos + partner * sin
        return r * scale if scale != 1.0 else r

    def chunks(acc):
        return [acc[:, c * LANES:(c + 1) * LANES] for c in range(acc.shape[1] // LANES)]

    def store_natural(out_ref, acc, fn):
        for c, a in enumerate(chunks(acc)):
            out_ref[:, c * LANES:(c + 1) * LANES] = fn(a).astype(out_ref.dtype)

    def store_pairs(out_ref, acc, fn):
        for c, a in enumerate(chunks(acc)):
            out_ref[0, c] = fn(a).astype(out_ref.dtype)

    def store_strided(out_ref, acc, fn, d):
        for c, a in enumerate(chunks(acc)):
            stage_ref[c] = fn(a)
        for c in range(acc.shape[1] // LANES):
            for r in range(d):
                out_ref[0, c, r] = stage_ref[c, pl.ds(r, tm // d, stride=d), :].astype(out_ref.dtype)

    ident = lambda a: a
    rot_q = lambda a: rotary(a, HEAD_DIM ** -0.5 * LOG2_E)
    rot_1 = lambda a: rotary(a, 1.0)
    rot_k = lambda a: rotary(a, RET_QK_DIM ** -0.5)

    W = ATTN_GROUP_W
    store_pairs(qa0, proj(OFF_QA, W), rot_q)
    store_pairs(ka0, proj(OFF_KA, W), rot_1)
    store_pairs(va0, proj(OFF_VA, W), ident)
    for g, (qo, ko, vo) in ((1, (qa1, ka1, va1)), (2, (qa2, ka2, va2))):
        d = ATTN_PAIRS[g][1]
        store_strided(qo, proj(OFF_QA + g * W, W), rot_q, d)
        store_strided(ko, proj(OFF_KA + g * W, W), rot_1, d)
        store_strided(vo, proj(OFF_VA + g * W, W), ident, d)
    store_natural(qr, proj(OFF_QR, RET_QK_W), rot_1)
    store_natural(kr, proj(OFF_KR, RET_QK_W), rot_k)
    for h in range(RET_V_W // W):
        vr[:, h * W:(h + 1) * W] = proj(OFF_VR + h * W, W).astype(vr.dtype)
        gr[:, h * W:(h + 1) * W] = proj(OFF_GR + h * W, W).astype(gr.dtype)
    for h in range(2 * D_MODEL // W):
        z = proj(OFF_GL + h * W, W) + bg_ref[:, h * W:(h + 1) * W]
        gt[:, h * W:(h + 1) * W] = jax.nn.sigmoid(z).astype(gt.dtype)


def _in_projection(x2, norm_w, w_bf, b_gate, cos_t, sin_t, B, S):
    T = B * S
    tm = TM_INPROJ
    nt = S // tm
    W = ATTN_GROUP_W
    row = lambda i: (i, 0)
    const = lambda i: (0, 0)
    nat = lambda width: pl.BlockSpec((tm, width), row)

    P = W // LANES

    def strided_spec(d):
        return pl.BlockSpec((1, P, d, tm // d, LANES), lambda i: (i // nt, 0, 0, i % nt, 0))

    def strided_shape(d):
        return jax.ShapeDtypeStruct((B, P, d, S // d, LANES), BF16)

    pair_spec = pl.BlockSpec((1, P, tm, LANES), lambda i: (i // nt, 0, i % nt, 0))
    pair_shape = jax.ShapeDtypeStruct((B, P, S, LANES), BF16)
    nat_shape = lambda width: jax.ShapeDtypeStruct((T, width), BF16)
    d1, d2 = ATTN_PAIRS[1][1], ATTN_PAIRS[2][1]
    out_shape = ([pair_shape] * 3 + [strided_shape(d1)] * 3 + [strided_shape(d2)] * 3
                 + [nat_shape(RET_QK_W)] * 2 + [nat_shape(RET_V_W)] * 2 + [nat_shape(2 * D_MODEL)])
    out_specs = ([pair_spec] * 3 + [strided_spec(d1)] * 3 + [strided_spec(d2)] * 3
                 + [nat(RET_QK_W)] * 2 + [nat(RET_V_W)] * 2 + [nat(2 * D_MODEL)])
    in_specs = [
        pl.BlockSpec((tm, D_MODEL), row),
        pl.BlockSpec((1, D_MODEL), const),
        pl.BlockSpec((D_MODEL, IN_W), const, pipeline_mode=pl.Buffered(1)),
        pl.BlockSpec((1, 2 * D_MODEL), const),
        pl.BlockSpec((tm, LANES), lambda i: (i % nt, 0)),
        pl.BlockSpec((tm, LANES), lambda i: (i % nt, 0)),
    ]
    return pl.pallas_call(
        _inproj_body,
        grid=(T // tm,),
        in_specs=in_specs,
        out_specs=out_specs,
        out_shape=out_shape,
        scratch_shapes=[pltpu.VMEM((W // LANES, tm, LANES), F32)],
        compiler_params=_cparams(("parallel",)),
        name="in_projection",
    )(x2, norm_w, w_bf, b_gate, cos_t, sin_t)


def _attn_body(q0_ref, k0_ref, v0_ref, q1_ref, k1_ref, v1_ref, q2_ref, k2_ref, v2_ref, o_ref,
               part_ref, bias_ref, *, S):
    QB, H = ATTN_QB, ATTN_HALF
    lane = lax.broadcasted_iota(jnp.int32, (QB, LANES), 1)
    head0 = lane < HEAD_DIM

    qi = lax.broadcasted_iota(jnp.int32, (QB, QB + 2 * H), 0)
    ki = lax.broadcasted_iota(jnp.int32, (QB, QB + 2 * H), 1)
    for n in range(3):
        bias_ref[n] = jnp.where(jnp.abs(ki - qi - n * H) <= H, 0.0, NEG_BIG).astype(F32)

    def scores(q_rows, k_rows, bias, h):
        qm = jnp.where(head0 if h == 0 else jnp.logical_not(head0), q_rows, jnp.zeros_like(q_rows))
        return lax.dot_general(qm, k_rows, (((1,), (1,)), ((), ())), preferred_element_type=F32) + bias

    def weights(s):
        m = jnp.max(s, axis=-1, keepdims=True)
        return m, jnp.exp2(s - m).astype(BF16)

    def heads_to_lanes(m0, a, m1, b):
        num = jnp.where(head0, a, b)
        den = pltpu.roll(jnp.where(head0, b, a), HEAD_DIM, 1)
        mx = jnp.where(head0, m0, m1)
        return num, den, mx

    def run_group(q_ref, k_ref, v_ref, d, prepare, store):
        L = S // d
        KW = min(L, QB + 2 * H)
        nb = L // QB

        key_head0 = lax.broadcasted_iota(jnp.int32, (KW, LANES), 1) < HEAD_DIM
        key_ones = jnp.ones((KW, LANES), BF16)

        def trip(i, carry):
            blocks = []
            for u in range(ATTN_UNROLL):
                t = i * ATTN_UNROLL + u
                r = t // nb
                q0 = pl.multiple_of((t % nb) * QB, QB)
                ws = pl.multiple_of(jnp.clip(q0 - H, 0, L - KW), H)
                bias = bias_ref[(q0 - ws) // H][:, :KW]
                q_rows = q_ref[0, 0, r, pl.ds(q0, QB), :]
                k_rows = k_ref[0, 0, r, pl.ds(ws, KW), :]
                blocks.append((r, q0, ws, [scores(q_rows, k_rows, bias, h) for h in range(2)]))
            blocks = [(r, q0, ws, [weights(s) for s in ss]) for r, q0, ws, ss in blocks]
            done = []
            for r, q0, ws, ((m0, p0), (m1, p1)) in blocks:
                v_rows = v_ref[0, 0, r, pl.ds(ws, KW), :]
                a = jnp.dot(p0, jnp.where(key_head0, v_rows, key_ones), preferred_element_type=F32)
                b = jnp.dot(p1, jnp.where(key_head0, key_ones, v_rows), preferred_element_type=F32)
                done.append((r, q0, m0, a, m1, b))
            done = [(r, q0, prepare(q0, *heads_to_lanes(m0, a, m1, b))) for r, q0, m0, a, m1, b in done]
            for r, q0, vals in done:
                store(r, q0, vals)
            return carry

        lax.fori_loop(0, S // QB // ATTN_UNROLL, trip, 0)

    def store_partial(g):
        d = ATTN_PAIRS[g][1]

        def store(r, q0, vals):
            rows = pl.ds(r + q0 * d, QB, stride=d)
            for n, val in enumerate(vals):
                part_ref[3 * (g - 1) + n, rows, :] = val
        return store

    keep = lambda q0, num, den, mx: (num, den, mx)
    run_group(q1_ref, k1_ref, v1_ref, ATTN_PAIRS[1][1], keep, store_partial(1))
    run_group(q2_ref, k2_ref, v2_ref, ATTN_PAIRS[2][1], keep, store_partial(2))

    def merge(q0, num, den, mx):
        rows = pl.ds(q0, QB)
        nums = [num, part_ref[0, rows, :], part_ref[3, rows, :]]
        dens = [den, part_ref[1, rows, :], part_ref[4, rows, :]]
        mxs = [mx, part_ref[2, rows, :], part_ref[5, rows, :]]
        top = jnp.maximum(jnp.maximum(mxs[0], mxs[1]), mxs[2])
        ws = [jnp.exp2(m - top) for m in mxs]
        n = ws[0] * nums[0] + ws[1] * nums[1] + ws[2] * nums[2]
        dn = ws[0] * dens[0] + ws[1] * dens[1] + ws[2] * dens[2]
        return (n / dn).astype(o_ref.dtype)

    def store_out(r, q0, y):
        o_ref[0, 0, pl.ds(q0, QB), :] = y

    run_group(q0_ref, k0_ref, v0_ref, ATTN_PAIRS[0][1], merge, store_out)


def _attention(qkv, B, S):
    P = ATTN_GROUP_W // LANES
    in_specs = []
    for _, d in ATTN_PAIRS:
        in_specs += [pl.BlockSpec((1, 1, d, S // d, LANES), lambda b, p: (b, p, 0, 0, 0))] * 3
    return pl.pallas_call(
        functools.partial(_attn_body, S=S),
        grid=(B, P),
        in_specs=in_specs,
        out_specs=pl.BlockSpec((1, 1, S, LANES), lambda b, p: (b, p, 0, 0)),
        out_shape=jax.ShapeDtypeStruct((B, P, S, LANES), BF16),
        scratch_shapes=[pltpu.VMEM((6, S, LANES), F32),
                        pltpu.VMEM((3, ATTN_QB, ATTN_QB + 2 * ATTN_HALF), F32)],
        compiler_params=_cparams(("parallel", "parallel")),
        name="attention",
    )(*qkv)


def _log_sigmoid(z):
    return jnp.minimum(z, 0.0) - jnp.log(1.0 + jnp.exp(-jnp.abs(z)))


def _ret_body(dec_ref, q_ref, k_ref, v_ref, g_ref, gnw_ref, o_ref, kt_ref, sf_ref, sb_ref, st_ref, *, S):
    C = RET_CHUNK
    nc = S // C
    lg = _log_sigmoid(dec_ref[0])
    a_row = lax.broadcasted_iota(jnp.int32, (C, LANES), 0).astype(F32)
    lane = lax.broadcasted_iota(jnp.int32, (C, LANES), 1)
    rel = (lax.broadcasted_iota(jnp.int32, (C, C), 0) - lax.broadcasted_iota(jnp.int32, (C, C), 1)).astype(F32)

    heads = []
    for h in range(2):
        lgf = lg[h:h + 1, :]
        lgb = lg[2 + h:3 + h, :]
        in_head = (lane < RET_QK_DIM) if h == 0 else (lane >= RET_QK_DIM)
        heads.append(dict(
            in_head=in_head,
            xi_f=jnp.where(in_head, jnp.exp((a_row + 1.0) * lgf), 0.0),
            xi_b=jnp.where(in_head, jnp.exp((C - a_row) * lgb), 0.0),
            zeta_f=jnp.exp((C - 1.0 - a_row) * lgf),
            zeta_b=jnp.exp(a_row * lgb),
            dloc=jnp.where(rel > 0, jnp.exp(rel * lgf), jnp.where(rel < 0, jnp.exp(-rel * lgb), 2.0)),
            cd_f=jnp.exp(C * lgf),
            cd_b=jnp.exp(C * lgb),
        ))

    def chunk_rows(n):
        return pl.ds(pl.multiple_of(n * C, C), C)

    def v_head(rows, h):
        return v_ref[0, rows, h * RET_V_DIM:(h + 1) * RET_V_DIM]

    def products(i, carry):
        items = []
        for u in range(RET_UNROLL):
            n = i * RET_UNROLL + u
            rows = chunk_rows(n)
            kt = k_ref[0, rows, :].astype(F32).T.astype(BF16)
            kt_ref[n] = kt
            for h, hd in enumerate(heads):
                vh = v_head(rows, h).astype(F32)
                items.append((n, h, kt, (vh * hd["zeta_f"]).astype(BF16), (vh * hd["zeta_b"]).astype(BF16)))
        outs = [(n, h, jnp.dot(kt, vf, preferred_element_type=F32), jnp.dot(kt, vb, preferred_element_type=F32))
                for n, h, kt, vf, vb in items]
        for n, h, f, b in outs:
            sf_ref[n, h] = f
            sb_ref[n, h] = b
        return carry

    lax.fori_loop(0, nc // RET_UNROLL, products, 0)

    def scan(ref, key, order, half):
        rows = slice(half * 2 * RET_QK_DIM, (half + 1) * 2 * RET_QK_DIM)

        def step(i, state):
            n = order(i)
            new = []
            for h, hd in enumerate(heads):
                st_ref[n, h, rows, :] = state[h].astype(BF16)
                new.append(hd[key] * state[h] + ref[n, h])
            return tuple(new)
        zero = jnp.zeros(ref.shape[2:], F32)
        lax.fori_loop(0, nc, step, (zero, zero))

    scan(sf_ref, "cd_f", lambda i: i, 0)
    scan(sb_ref, "cd_b", lambda i: nc - 1 - i, 1)

    def outputs(i, carry):
        items = []
        for u in range(RET_UNROLL):
            n = i * RET_UNROLL + u
            rows = chunk_rows(n)
            qp = q_ref[0, rows, :]
            qf = qp.astype(F32)
            kt = kt_ref[n]
            for h, hd in enumerate(heads):
                qm = jnp.where(hd["in_head"], qp, jnp.zeros_like(qp))
                qx = jnp.concatenate([(qf * hd["xi_f"]).astype(BF16), (qf * hd["xi_b"]).astype(BF16)], axis=1)
                items.append((rows, h, hd, jnp.dot(qm, kt, preferred_element_type=F32),
                              jnp.dot(qx, st_ref[n, h], preferred_element_type=F32)))
        items = [(rows, h, hd, (s * hd["dloc"]).astype(BF16), cross) for rows, h, hd, s, cross in items]
        items = [(rows, h, cross + jnp.dot(p, v_head(rows, h), preferred_element_type=F32))
                 for rows, h, hd, p, cross in items]
        for rows, h, ret in items:
            mu = jnp.mean(ret, axis=-1, keepdims=True)
            xc = ret - mu
            var = jnp.mean(xc * xc, axis=-1, keepdims=True)
            cols = slice(h * RET_V_DIM, (h + 1) * RET_V_DIM)
            gate = g_ref[0, rows, cols].astype(F32)
            y = xc * lax.rsqrt(var + NORM_EPS) * gnw_ref[:, cols] * (gate * jax.nn.sigmoid(gate))
            o_ref[0, rows, cols] = y.astype(o_ref.dtype)
        return carry

    lax.fori_loop(0, nc // RET_UNROLL, outputs, 0)


def _retention(dec, qr, kr, vr, gr, gn_w, B, S):
    nc = S // RET_CHUNK
    npairs = RET_HEADS // 2
    qk_spec = pl.BlockSpec((1, S, 2 * RET_QK_DIM), lambda b, p: (b, 0, p))
    v_spec = pl.BlockSpec((1, S, 2 * RET_V_DIM), lambda b, p: (b, 0, p))
    return pl.pallas_call(
        functools.partial(_ret_body, S=S),
        grid=(B, npairs),
        in_specs=[pl.BlockSpec((1, 4, LANES), lambda b, p: (p, 0, 0)),
                  qk_spec, qk_spec, v_spec, v_spec,
                  pl.BlockSpec((1, 2 * RET_V_DIM), lambda b, p: (0, p))],
        out_specs=v_spec,
        out_shape=jax.ShapeDtypeStruct((B, S, RET_V_W), BF16),
        scratch_shapes=[pltpu.VMEM((nc, 2 * RET_QK_DIM, RET_CHUNK), BF16),
                        pltpu.VMEM((nc, 2, 2 * RET_QK_DIM, RET_V_DIM), F32),
                        pltpu.VMEM((nc, 2, 2 * RET_QK_DIM, RET_V_DIM), F32),
                        pltpu.VMEM((nc, 2, 4 * RET_QK_DIM, RET_V_DIM), BF16)],
        compiler_params=_cparams(("parallel", "parallel")),
        name="retention",
    )(dec, qr.reshape(B, S, RET_QK_W), kr.reshape(B, S, RET_QK_W),
      vr.reshape(B, S, RET_V_W), gr.reshape(B, S, RET_V_W), gn_w)


ROUTE_EID, ROUTE_RANK, ROUTE_GATE = 0, 2, 4
ROUTE_FIELDS = 8
ROUTER_EXPERT_LANE0 = MOE_GROUPS
GROUP_SHIFT = MOE_EXPERTS_PER_GROUP.bit_length() - 1


def _merge_body(yatt_ref, yret_ref, gt_ref, x_ref, wa_ref, wb_ref, wo_ref,
                nw_ref, wrh_ref, wrl_ref, br_ref,
                h_ref, hn_ref, route_ref, route_t_ref, cnt_ref):
    tm = x_ref.shape[0]
    hm = tm // MERGE_SPLIT
    i = pl.program_id(0)
    cw = D_MODEL // MERGE_COL_CHUNKS

    def branch_products(rows):
        y_att = jnp.concatenate([yatt_ref[0, c, rows, :] for c in range(yatt_ref.shape[1])], axis=1)
        y_ret = yret_ref[rows, :]
        chunks = []
        for c in range(MERGE_COL_CHUNKS):
            cols = slice(c * cw, (c + 1) * cw)
            a = jnp.dot(y_att, wa_ref[:, cols], preferred_element_type=F32)
            b = jnp.dot(y_ret, wb_ref[:, cols], preferred_element_type=F32)
            g_att = gt_ref[rows, c * cw:(c + 1) * cw].astype(F32)
            g_ret = gt_ref[rows, D_MODEL + c * cw:D_MODEL + (c + 1) * cw].astype(F32)
            chunks.append((g_att * a + g_ret * b).astype(BF16))
        return jnp.concatenate(chunks, axis=1)

    def residual_norm(hf, rows, merged):
        mix = jnp.dot(merged, wo_ref[...], preferred_element_type=F32)
        h = x_ref[rows, :] + mix
        h_ref[rows, :] = h
        ms = jnp.mean(h * h, axis=-1, keepdims=True)
        hn = h * lax.rsqrt(ms + NORM_EPS) * nw_ref[...]
        for j, word in enumerate(_pack_row(hn)):
            hn_ref[pl.ds(hf * hm * ROW_TILES + j, hm, stride=ROW_TILES), :] = word
        hi = hn.astype(BF16)
        lo = (hn - hi.astype(F32)).astype(BF16)
        return hi, lo

    def router_logits(hi, lo):
        return (jnp.dot(hi, wrh_ref[...], preferred_element_type=F32)
                + jnp.dot(hi, wrl_ref[...], preferred_element_type=F32)
                + jnp.dot(lo, wrh_ref[...], preferred_element_type=F32)) + br_ref[...]

    lane = lax.broadcasted_iota(jnp.int32, (hm, LANES), 1)
    far = jnp.int32(LANES)

    def first_argmax(vals, vmax):
        return jnp.min(jnp.where(vals == vmax, lane, far), axis=-1, keepdims=True)

    def route(logits):
        is_group = lane < MOE_GROUPS
        gl = jnp.where(is_group, logits, NEG_BIG)
        gmax = jnp.max(gl, axis=-1, keepdims=True)
        g_w = 1.0 / jnp.sum(jnp.where(is_group, jnp.exp(gl - gmax), 0.0), axis=-1, keepdims=True)
        g_idx = first_argmax(gl, gmax)
        e_lane = lane - ROUTER_EXPERT_LANE0
        in_group = (e_lane >= 0) & (e_lane < MOE_N_EXPERTS) & (jnp.right_shift(e_lane, GROUP_SHIFT) == g_idx)
        el = jnp.where(in_group, logits, NEG_BIG)
        m1 = jnp.max(el, axis=-1, keepdims=True)
        i1 = first_argmax(el, m1)
        el2 = jnp.where(lane == i1, NEG_BIG, el)
        m2 = jnp.max(el2, axis=-1, keepdims=True)
        i2 = first_argmax(el2, m2)
        ex = jnp.exp(m2 - m1)
        return i1, i2, g_w / (1.0 + ex), g_w * ex / (1.0 + ex)

    rows = [slice(hf * hm, (hf + 1) * hm) for hf in range(MERGE_SPLIT)]
    merged = [branch_products(r) for r in rows]
    split = [residual_norm(hf, r, m) for hf, (r, m) in enumerate(zip(rows, merged))]
    routed = [route(router_logits(hi, lo)) for hi, lo in split]

    @pl.when(i == 0)
    def _():
        cnt_ref[...] = jnp.zeros(cnt_ref.shape, F32)

    r_idx = lax.broadcasted_iota(jnp.int32, (hm, hm), 0)
    c_idx = lax.broadcasted_iota(jnp.int32, (hm, hm), 1)
    lower = jnp.where(c_idx < r_idx, 1.0, 0.0).astype(BF16)
    running = cnt_ref[...]
    for hf, (i1, i2, gate1, gate2) in enumerate(routed):
        hot1 = lane == i1
        hot2 = lane == i2
        onehot = jnp.where(hot1 | hot2, 1.0, 0.0)
        before = jnp.dot(lower, onehot.astype(BF16), preferred_element_type=F32) + running
        rank1 = jnp.sum(jnp.where(hot1, before, 0.0), axis=-1, keepdims=True)
        rank2 = jnp.sum(jnp.where(hot2, before, 0.0), axis=-1, keepdims=True)
        running = running + jnp.sum(onehot, axis=0, keepdims=True)
        rec = jnp.zeros((hm, LANES), F32)
        for pos, val in ((ROUTE_EID, (i1 - ROUTER_EXPERT_LANE0).astype(F32)),
                         (ROUTE_EID + 1, (i2 - ROUTER_EXPERT_LANE0).astype(F32)),
                         (ROUTE_RANK, rank1), (ROUTE_RANK + 1, rank2),
                         (ROUTE_GATE, gate1), (ROUTE_GATE + 1, gate2)):
            rec = jnp.where(lane == pos, val, rec)
        route_ref[rows[hf], :] = rec
        route_t_ref[:, hf * hm:(hf + 1) * hm] = rec.T[:route_t_ref.shape[0], :]
    cnt_ref[...] = running


def _merge_route(y_att, y_ret, gates, x2, wa, wb, wo, nw, wr_hi, wr_lo, b_r, T, S):
    tm = TM_MERGE
    nt = S // tm
    row = lambda i: (i, 0)
    const = lambda i: (0, 0)
    full = lambda arr: pl.BlockSpec(arr.shape, const)
    in_specs = ([pl.BlockSpec((1, y_att.shape[1], tm, LANES), lambda i: (i // nt, 0, i % nt, 0)),
                 pl.BlockSpec((tm, RET_V_W), row), pl.BlockSpec((tm, 2 * D_MODEL), row),
                 pl.BlockSpec((tm, D_MODEL), row),
                 full(wa), full(wb), full(wo), full(nw), full(wr_hi), full(wr_lo), full(b_r)])
    return pl.pallas_call(
        _merge_body,
        grid=(T // tm,),
        in_specs=in_specs,
        out_specs=[pl.BlockSpec((tm, D_MODEL), row),
                   pl.BlockSpec((tm * ROW_TILES, LANES), row),
                   pl.BlockSpec((tm, LANES), row), pl.BlockSpec((ROUTE_FIELDS, tm), lambda i: (0, i)),
                   pl.BlockSpec((1, LANES), const)],
        out_shape=[jax.ShapeDtypeStruct((T, D_MODEL), F32), jax.ShapeDtypeStruct((T * ROW_TILES, LANES), U32),
                   jax.ShapeDtypeStruct((T, LANES), F32), jax.ShapeDtypeStruct((ROUTE_FIELDS, T), F32),
                   jax.ShapeDtypeStruct((1, LANES), F32)],
        compiler_params=_cparams(("arbitrary",)),
        name="merge_route",
    )(y_att, y_ret, gates, x2, wa, wb, wo, nw, wr_hi, wr_lo, b_r)


ISSUE_UNROLL = 8


def _tile_rows(n):
    return pl.ds(pl.multiple_of(n * ROW_TILES, ROW_TILES), ROW_TILES)


def _pack_row(x):
    bits = lambda c: pltpu.bitcast(x[:, c * LANES:(c + 1) * LANES].astype(BF16).astype(F32), U32)
    return [(bits(j) >> BF16_BITS) | bits(j + ROW_TILES) for j in range(ROW_TILES)]


def _unpack_word(w):
    high = jnp.uint32(((1 << BF16_BITS) - 1) << BF16_BITS)
    return pltpu.bitcast(w << BF16_BITS, F32), pltpu.bitcast(w & high, F32)


def _dispatch_body(slot_ref, hn_ref, xs_ref, sem, *, T):
    i = pl.program_id(0)
    ch = hn_ref.shape[0] // ROW_TILES

    def row_copy(j, slot):
        return pltpu.make_async_copy(hn_ref.at[_tile_rows(j)], xs_ref.at[_tile_rows(slot)], sem)

    def issue(j, carry):
        t = i * ch + j
        row_copy(j, slot_ref[t]).start(priority=0)
        row_copy(j, slot_ref[T + t]).start(priority=1)
        return carry

    lax.fori_loop(0, ch, issue, 0, unroll=ISSUE_UNROLL)
    for _ in range(2):
        pltpu.make_async_copy(hn_ref, xs_ref.at[pl.ds(0, ch * ROW_TILES)], sem).wait()


def _dispatch(slots, hn, n_slots, T):
    ch = DISPATCH_CHUNK
    grid_spec = pltpu.PrefetchScalarGridSpec(
        num_scalar_prefetch=1,
        grid=(T // ch,),
        in_specs=[pl.BlockSpec((ch * ROW_TILES, LANES), lambda i, s: (i, 0))],
        out_specs=pl.BlockSpec(memory_space=pl.ANY),
        scratch_shapes=[pltpu.SemaphoreType.DMA(())],
    )
    return pl.pallas_call(
        functools.partial(_dispatch_body, T=T),
        grid_spec=grid_spec,
        out_shape=jax.ShapeDtypeStruct((n_slots * ROW_TILES, LANES), U32),
        compiler_params=_cparams(("arbitrary",)),
        name="moe_dispatch",
    )(slots, hn)


def _expert_body(blk_ref, eid_ref, valid_ref, fresh_ref, next_ref, x_ref, w1_ref, w3_ref, w2_ref, y_ref,
                 w1b, w3b, w2b, w1s, w3s, w2s, wsem):
    i = pl.program_id(0)
    valid = valid_ref[i]

    def weight_copies(e):
        return (pltpu.make_async_copy(w1_ref.at[e], w1s, wsem.at[0]),
                pltpu.make_async_copy(w3_ref.at[e], w3s, wsem.at[1]),
                pltpu.make_async_copy(w2_ref.at[e], w2s, wsem.at[2]))

    @pl.when(i == 0)
    def _():
        for cp in weight_copies(eid_ref[0]):
            cp.start()

    @pl.when(valid > 0)
    def _():
        @pl.when(fresh_ref[i] == 1)
        def _():
            for cp in weight_copies(eid_ref[i]):
                cp.wait()
            w1b[...] = w1s[...].astype(BF16)
            w3b[...] = w3s[...].astype(BF16)
            w2b[...] = w2s[...].astype(BF16)

            @pl.when(next_ref[i] >= 0)
            def _():
                for cp in weight_copies(next_ref[i]):
                    cp.start()

        bm = x_ref.shape[0] // ROW_TILES
        live =lax.broadcasted_iota(jnp.int32, (bm, LANES), 0) < valid
        halves = [_unpack_word(x_ref[pl.ds(j, bm, stride=ROW_TILES), :]) for j in range(ROW_TILES)]
        x = jnp.concatenate(
            [jnp.where(live, c, 0.0).astype(BF16) for c in [lo for lo, _ in halves] + [hi for _, hi in halves]],
            axis=1)
        a = jnp.dot(x, w1b[...], preferred_element_type=F32)
        b = jnp.dot(x, w3b[...], preferred_element_type=F32)
        hid = (a * jax.nn.sigmoid(a) * b).astype(BF16)
        y = jnp.dot(hid, w2b[...], preferred_element_type=F32)
        for j, word in enumerate(_pack_row(y)):
            y_ref[pl.ds(j, bm, stride=ROW_TILES), :] = word


def _experts(blk, blk_eid, blk_valid, blk_fresh, blk_next, x_slots, w1, w3, w2, n_blocks):
    bm = MOE_BM
    slot_block = lambda i, blk, eid, val, fr, nx: (blk[i], 0)
    grid_spec = pltpu.PrefetchScalarGridSpec(
        num_scalar_prefetch=5,
        grid=(n_blocks,),
        in_specs=[pl.BlockSpec((bm * ROW_TILES, LANES), slot_block),
                  pl.BlockSpec(memory_space=pl.ANY), pl.BlockSpec(memory_space=pl.ANY),
                  pl.BlockSpec(memory_space=pl.ANY)],
        out_specs=pl.BlockSpec((bm * ROW_TILES, LANES), slot_block),
        scratch_shapes=[pltpu.VMEM((D_MODEL, MOE_HIDDEN), BF16), pltpu.VMEM((D_MODEL, MOE_HIDDEN), BF16),
                        pltpu.VMEM((MOE_HIDDEN, D_MODEL), BF16),
                        pltpu.VMEM((D_MODEL, MOE_HIDDEN), F32), pltpu.VMEM((D_MODEL, MOE_HIDDEN), F32),
                        pltpu.VMEM((MOE_HIDDEN, D_MODEL), F32), pltpu.SemaphoreType.DMA((3,))],
    )
    return pl.pallas_call(
        _expert_body,
        grid_spec=grid_spec,
        out_shape=jax.ShapeDtypeStruct(x_slots.shape, U32),
        compiler_params=_cparams(("arbitrary",)),
        name="moe_experts",
    )(blk, blk_eid, blk_valid, blk_fresh, blk_next, x_slots, w1, w3, w2)


def _combine_body(slot_ref, ys_ref, h_ref, route_ref, nw_ref, o_ref, ybuf, sem, *, T):
    i = pl.program_id(0)
    n = pl.num_programs(0)
    tm = h_ref.shape[0]

    def row_copy(slot, buf, k, j):
        return pltpu.make_async_copy(ys_ref.at[_tile_rows(slot)], ybuf.at[buf, k, _tile_rows(j)], sem.at[buf])

    def issue(tile, buf):
        def one(j, carry):
            t = tile * tm + j
            row_copy(slot_ref[t], buf, 0, j).start(priority=0)
            row_copy(slot_ref[T + t], buf, 1, j).start(priority=1)
            return carry
        lax.fori_loop(0, tm, one, 0, unroll=ISSUE_UNROLL)

    @pl.when(i == 0)
    def _():
        issue(0, 0)

    @pl.when(i + 1 < n)
    def _():
        issue(i + 1, (i + 1) % 2)

    buf = i % 2
    for k in range(2):
        pltpu.make_async_copy(ys_ref.at[pl.ds(0, tm * ROW_TILES)], ybuf.at[buf, k], sem.at[buf]).wait()
    route = route_ref[...]
    g1 = route[:, ROUTE_GATE:ROUTE_GATE + 1]
    g2 = route[:, ROUTE_GATE + 1:ROUTE_GATE + 2]
    hs = [None] * COL_CHUNKS
    ss = jnp.zeros((tm, 1), F32)
    for j in range(ROW_TILES):
        tile_row = pl.ds(j, tm, stride=ROW_TILES)
        first = _unpack_word(ybuf[buf, 0, tile_row, :])
        second = _unpack_word(ybuf[buf, 1, tile_row, :])
        for c, y1, y2 in ((j, first[0], second[0]), (j + ROW_TILES, first[1], second[1])):
            hc = h_ref[:, c * LANES:(c + 1) * LANES] + (y1 * g1 + y2 * g2)
            hs[c] = hc
            ss = ss + jnp.sum(hc * hc, axis=-1, keepdims=True)
    inv = lax.rsqrt(ss * (1.0 / D_MODEL) + NORM_EPS)
    for j, hj in enumerate(hs):
        cols = slice(j * LANES, (j + 1) * LANES)
        o_ref[:, cols] = hj * inv * nw_ref[:, cols]


def _combine(slots, y_slots, h, route, nw, T):
    tm = TM_COMBINE
    row = lambda i, s: (i, 0)
    grid_spec = pltpu.PrefetchScalarGridSpec(
        num_scalar_prefetch=1,
        grid=(T // tm,),
        in_specs=[pl.BlockSpec(memory_space=pl.ANY),
                  pl.BlockSpec((tm, D_MODEL), row),
                  pl.BlockSpec((tm, LANES), row),
                  pl.BlockSpec((1, D_MODEL), lambda i, s: (0, 0))],
        out_specs=pl.BlockSpec((tm, D_MODEL), row),
        scratch_shapes=[pltpu.VMEM((2, 2, tm * ROW_TILES, LANES), U32), pltpu.SemaphoreType.DMA((2,))],
    )
    return pl.pallas_call(
        functools.partial(_combine_body, T=T),
        grid_spec=grid_spec,
        out_shape=jax.ShapeDtypeStruct((T, D_MODEL), F32),
        compiler_params=_cparams(("arbitrary",)),
        name="moe_combine",
    )(slots, y_slots, h, route, nw)


def _rotary_tables(S):
    inv_freq = (1.0 / (np.float32(ROPE_THETA) ** (np.arange(0, HEAD_DIM, 2, dtype=np.float32) / HEAD_DIM))
                ).astype(np.float32)
    ang = np.arange(S, dtype=np.float32)[:, None] * inv_freq[None, :]
    cos, sin = np.cos(ang), np.sin(ang)
    reps = LANES // HEAD_DIM
    cos_t = np.tile(np.concatenate([cos, cos], axis=1), (1, reps)).astype(np.float32)
    sin_t = np.tile(np.concatenate([-sin, sin], axis=1), (1, reps)).astype(np.float32)
    return jnp.asarray(cos_t), jnp.asarray(sin_t)


def _layer(h_in, norm_mix_w, w_in, b_branch_gate, ret_decay_fwd, ret_decay_bwd, ret_gn_w, w_attn_branch,
           w_ret_branch, w_out, norm_moe_w, moe_w_group, moe_b_group, moe_w_expert, moe_b_expert,
           moe_w1, moe_w3, moe_w2, next_norm_w, B, S, cos_t, sin_t):
    T = B * S
    (qa0, ka0, va0, qa1, ka1, va1, qa2, ka2, va2, qr, kr, vr, gr, gates) = _in_projection(
        h_in, norm_mix_w[None, :], w_in.astype(BF16), b_branch_gate[None, :], cos_t, sin_t, B, S)

    unit = lambda a: a[:, :, None]
    y_att = _attention((unit(qa0), unit(ka0), unit(va0), qa1, ka1, va1, qa2, ka2, va2), B, S)

    dec = jnp.stack([ret_decay_fwd.reshape(RET_HEADS // 2, 2), ret_decay_bwd.reshape(RET_HEADS // 2, 2)], axis=1)
    dec = jnp.broadcast_to(dec.reshape(RET_HEADS // 2, 4, 1), (RET_HEADS // 2, 4, LANES)).astype(F32)
    y_ret = _retention(dec, qr, kr, vr, gr, ret_gn_w[None, :], B, S).reshape(T, RET_V_W)

    pad = LANES - MOE_GROUPS - MOE_N_EXPERTS
    w_r = jnp.concatenate([moe_w_group, moe_w_expert, jnp.zeros((D_MODEL, pad), F32)], axis=1)
    w_r_hi = w_r.astype(BF16)
    w_r_lo = (w_r - w_r_hi.astype(F32)).astype(BF16)
    b_r = jnp.concatenate([moe_b_group, moe_b_expert, jnp.zeros((pad,), F32)])[None, :]

    h_mid, hn, route, route_t, cnt = _merge_route(
        y_att, y_ret, gates, h_in, w_attn_branch.astype(BF16), w_ret_branch.astype(BF16),
        w_out.astype(BF16), norm_moe_w[None, :], w_r_hi, w_r_lo, b_r, T, S)

    bm = MOE_BM
    counts = cnt[0, ROUTER_EXPERT_LANE0:ROUTER_EXPERT_LANE0 + MOE_N_EXPERTS].astype(jnp.int32)
    nblk = (counts + bm - 1) // bm
    blk_end = jnp.cumsum(nblk)
    pstart = (blk_end - nblk) * bm
    n_blocks = (2 * T) // bm + MOE_N_EXPERTS
    n_active = blk_end[-1]
    bidx = jnp.minimum(jnp.arange(n_blocks, dtype=jnp.int32), n_active - 1)
    blk_eid = jnp.sum(bidx[:, None] >= blk_end[None, :], axis=1).astype(jnp.int32)
    mine = blk_eid[:, None] == jnp.arange(MOE_N_EXPERTS, dtype=jnp.int32)[None, :]
    seg_end = jnp.sum(jnp.where(mine, (pstart + counts)[None, :], 0), axis=1)
    blk_valid = jnp.clip(seg_end - bidx * bm, 0, bm)
    blk_valid = jnp.where(jnp.arange(n_blocks) < n_active, blk_valid, 0).astype(jnp.int32)
    blk_fresh = jnp.concatenate([jnp.ones((1,), jnp.int32), (blk_eid[1:] != blk_eid[:-1]).astype(jnp.int32)])
    ar = jnp.arange(MOE_N_EXPERTS, dtype=jnp.int32)
    later = jnp.min(jnp.where((nblk > 0)[None, :] & (ar[None, :] > ar[:, None]), ar[None, :], MOE_N_EXPERTS), axis=1)
    later = jnp.where(later < MOE_N_EXPERTS, later, -1)
    blk_next = (jnp.sum(jnp.where(mine, later[None, :] + 1, 0), axis=1) - 1).astype(jnp.int32)
    eid = route_t[ROUTE_EID:ROUTE_EID + 2].astype(jnp.int32)
    rank = route_t[ROUTE_RANK:ROUTE_RANK + 2].astype(jnp.int32)
    start = jnp.sum(jnp.where(eid[..., None] == jnp.arange(MOE_N_EXPERTS, dtype=jnp.int32),
                              pstart.astype(jnp.int32), 0), axis=-1)
    slots = (start + rank).reshape(2 * T)

    x_slots = _dispatch(slots, hn, n_blocks * bm, T)
    y_slots = _experts(bidx, blk_eid, blk_valid, blk_fresh, blk_next, x_slots, moe_w1, moe_w3, moe_w2, n_blocks)
    return _combine(slots, y_slots, h_mid, route, next_norm_w[None, :], T)


def kernel(x, norm_mix_w, w_in, b_branch_gate, ret_decay_fwd, ret_decay_bwd, ret_gn_w, w_attn_branch,
           w_ret_branch, w_out, norm_moe_w, moe_w_group, moe_b_group, moe_w_expert, moe_b_expert, moe_w1,
           moe_w3, moe_w2, norm_final_w):
    B, S, D = x.shape
    depth = norm_mix_w.shape[0]
    assert depth == 1, "the final norm is fused into the layer's combine stage"
    assert D == D_MODEL and S % TM_INPROJ == 0 and (B * S) < (1 << 24)
    cos_t, sin_t = _rotary_tables(S)
    out = _layer(x.reshape(B * S, D), norm_mix_w[0], w_in[0], b_branch_gate[0], ret_decay_fwd[0],
                 ret_decay_bwd[0], ret_gn_w[0], w_attn_branch[0], w_ret_branch[0], w_out[0], norm_moe_w[0],
                 moe_w_group[0], moe_b_group[0], moe_w_expert[0], moe_b_expert[0], moe_w1[0], moe_w3[0],
                 moe_w2[0], norm_final_w, B, S, cos_t, sin_t)
    return out.reshape(B, S, D)
```

```python
import functools

import numpy as np
import jax
import jax.numpy as jnp
from jax import lax
from jax.experimental import pallas as pl
from jax.experimental.pallas import tpu as pltpu

F32 = jnp.float32
BF16 = jnp.bfloat16

D_MODEL = 1024
HEAD_DIM = 64
ATTN_PAIRS = ((128, 1), (512, 4), (2048, 16))
ATTN_HEADS_PER_GROUP = 8
ATTN_GROUP_W = ATTN_HEADS_PER_GROUP * HEAD_DIM
ATTN_HALF = 64
ROPE_THETA = 10000.0
RET_HEADS = 8
RET_QK_DIM = 64
RET_V_DIM = 128
RET_CHUNK = 128
RET_QK_W = RET_HEADS * RET_QK_DIM
RET_V_W = RET_HEADS * RET_V_DIM
MOE_GROUPS = 8
MOE_EXPERTS_PER_GROUP = 8
MOE_N_EXPERTS = MOE_GROUPS * MOE_EXPERTS_PER_GROUP
MOE_HIDDEN = 512
NORM_EPS = 1e-6

LANES = 128
COL_CHUNKS = D_MODEL // LANES
ROW_TILES = COL_CHUNKS // 2
U32 = jnp.uint32
BF16_BITS = 16
NEG_BIG = -1e30
LOG2_E = 1.4426950408889634

TM_INPROJ = 512
TM_MERGE = 1024
MERGE_SPLIT = 4
MERGE_COL_CHUNKS = 4
TM_COMBINE = 512
MOE_BM = 512
DISPATCH_CHUNK = 4096
ATTN_QB = 128
ATTN_UNROLL = 8
RET_UNROLL = 8

V7X_VMEM_BYTES = 64 * 1024 * 1024
VMEM_LIMIT = V7X_VMEM_BYTES * 7 // 8

_A = 3 * ATTN_GROUP_W
OFF_QA, OFF_KA, OFF_VA = 0, _A, 2 * _A
OFF_QR = 3 * _A
OFF_KR = OFF_QR + RET_QK_W
OFF_VR = OFF_KR + RET_QK_W
OFF_GR = OFF_VR + RET_V_W
OFF_GL = OFF_GR + RET_V_W
IN_W = OFF_GL + 2 * D_MODEL


def _cparams(sem, vmem=VMEM_LIMIT):
    return pltpu.CompilerParams(dimension_semantics=sem, vmem_limit_bytes=vmem)


def _inproj_body(x_ref, nw_ref, w_ref, bg_ref, cos_ref, sin_ref,
                 qa0, ka0, va0, qa1, ka1, va1, qa2, ka2, va2, qr, kr, vr, gr, gt,
                 stage_ref):
    tm = x_ref.shape[0]
    x = x_ref[...]
    ms = jnp.mean(x * x, axis=-1, keepdims=True)
    xn = (x * lax.rsqrt(ms + NORM_EPS) * nw_ref[...]).astype(BF16)
    cos = cos_ref[...]
    sin = sin_ref[...]
    lane = lax.broadcasted_iota(jnp.int32, (tm, LANES), 1)
    first_half = (lane & (HEAD_DIM - 1)) < (HEAD_DIM // 2)

    def proj(c0, width):
        return jnp.dot(xn, w_ref[:, c0:c0 + width], preferred_element_type=F32)

    def rotary(a, scale):
        partner = jnp.where(first_half, pltpu.roll(a, LANES - HEAD_DIM // 2, 1),
                            pltpu.roll(a, HEAD_DIM // 2, 1))
        r = a * cos + partner * sin
        return r * scale if scale != 1.0 else r

    def chunks(acc):
        return [acc[:, c * LANES:(c + 1) * LANES] for c in range(acc.shape[1] // LANES)]

    def store_natural(out_ref, acc, fn):
        for c, a in enumerate(chunks(acc)):
            out_ref[:, c * LANES:(c + 1) * LANES] = fn(a).astype(out_ref.dtype)

    def store_pairs(out_ref, acc, fn):
        for c, a in enumerate(chunks(acc)):
            out_ref[0, c] = fn(a).astype(out_ref.dtype)

    def store_strided(out_ref, acc, fn, d):
        for c, a in enumerate(chunks(acc)):
            stage_ref[c] = fn(a)
        for c in range(acc.shape[1] // LANES):
            for r in range(d):
                out_ref[0, c, r] = stage_ref[c, pl.ds(r, tm // d, stride=d), :].astype(out_ref.dtype)

    ident = lambda a: a
    rot_q = lambda a: rotary(a, HEAD_DIM ** -0.5 * LOG2_E)
    rot_1 = lambda a: rotary(a, 1.0)
    rot_k = lambda a: rotary(a, RET_QK_DIM ** -0.5)

    W = ATTN_GROUP_W
    store_pairs(qa0, proj(OFF_QA, W), rot_q)
    store_pairs(ka0, proj(OFF_KA, W), rot_1)
    store_pairs(va0, proj(OFF_VA, W), ident)
    for g, (qo, ko, vo) in ((1, (qa1, ka1, va1)), (2, (qa2, ka2, va2))):
        d = ATTN_PAIRS[g][1]
        store_strided(qo, proj(OFF_QA + g * W, W), rot_q, d)
        store_strided(ko, proj(OFF_KA + g * W, W), rot_1, d)
        store_strided(vo, proj(OFF_VA + g * W, W), ident, d)
    store_natural(qr, proj(OFF_QR, RET_QK_W), rot_1)
    store_natural(kr, proj(OFF_KR, RET_QK_W), rot_k)
    for h in range(RET_V_W // W):
        vr[:, h * W:(h + 1) * W] = proj(OFF_VR + h * W, W).astype(vr.dtype)
        gr[:, h * W:(h + 1) * W] = proj(OFF_GR + h * W, W).astype(gr.dtype)
    for h in range(2 * D_MODEL // W):
        z = proj(OFF_GL + h * W, W) + bg_ref[:, h * W:(h + 1) * W]
        gt[:, h * W:(h + 1) * W] = jax.nn.sigmoid(z).astype(gt.dtype)


def _in_projection(x2, norm_w, w_bf, b_gate, cos_t, sin_t, B, S):
    T = B * S
    tm = TM_INPROJ
    nt = S // tm
    W = ATTN_GROUP_W
    row = lambda i: (i, 0)
    const = lambda i: (0, 0)
    nat = lambda width: pl.BlockSpec((tm, width), row)

    P = W // LANES

    def strided_spec(d):
        return pl.BlockSpec((1, P, d, tm // d, LANES), lambda i: (i // nt, 0, 0, i % nt, 0))

    def strided_shape(d):
        return jax.ShapeDtypeStruct((B, P, d, S // d, LANES), BF16)

    pair_spec = pl.BlockSpec((1, P, tm, LANES), lambda i: (i // nt, 0, i % nt, 0))
    pair_shape = jax.ShapeDtypeStruct((B, P, S, LANES), BF16)
    nat_shape = lambda width: jax.ShapeDtypeStruct((T, width), BF16)
    d1, d2 = ATTN_PAIRS[1][1], ATTN_PAIRS[2][1]
    out_shape = ([pair_shape] * 3 + [strided_shape(d1)] * 3 + [strided_shape(d2)] * 3
                 + [nat_shape(RET_QK_W)] * 2 + [nat_shape(RET_V_W)] * 2 + [nat_shape(2 * D_MODEL)])
    out_specs = ([pair_spec] * 3 + [strided_spec(d1)] * 3 + [strided_spec(d2)] * 3
                 + [nat(RET_QK_W)] * 2 + [nat(RET_V_W)] * 2 + [nat(2 * D_MODEL)])
    in_specs = [
        pl.BlockSpec((tm, D_MODEL), row),
        pl.BlockSpec((1, D_MODEL), const),
        pl.BlockSpec((D_MODEL, IN_W), const, pipeline_mode=pl.Buffered(1)),
        pl.BlockSpec((1, 2 * D_MODEL), const),
        pl.BlockSpec((tm, LANES), lambda i: (i % nt, 0)),
        pl.BlockSpec((tm, LANES), lambda i: (i % nt, 0)),
    ]
    return pl.pallas_call(
        _inproj_body,
        grid=(T // tm,),
        in_specs=in_specs,
        out_specs=out_specs,
        out_shape=out_shape,
        scratch_shapes=[pltpu.VMEM((W // LANES, tm, LANES), F32)],
        compiler_params=_cparams(("parallel",)),
        name="in_projection",
    )(x2, norm_w, w_bf, b_gate, cos_t, sin_t)


def _attn_body(q0_ref, k0_ref, v0_ref, q1_ref, k1_ref, v1_ref, q2_ref, k2_ref, v2_ref, o_ref,
               part_ref, bias_ref, *, S):
    QB, H = ATTN_QB, ATTN_HALF
    lane = lax.broadcasted_iota(jnp.int32, (QB, LANES), 1)
    head0 = lane < HEAD_DIM

    qi = lax.broadcasted_iota(jnp.int32, (QB, QB + 2 * H), 0)
    ki = lax.broadcasted_iota(jnp.int32, (QB, QB + 2 * H), 1)
    for n in range(3):
        bias_ref[n] = jnp.where(jnp.abs(ki - qi - n * H) <= H, 0.0, NEG_BIG).astype(F32)

    def scores(q_rows, k_rows, bias, h):
        qm = jnp.where(head0 if h == 0 else jnp.logical_not(head0), q_rows, jnp.zeros_like(q_rows))
        return lax.dot_general(qm, k_rows, (((1,), (1,)), ((), ())), preferred_element_type=F32) + bias

    def weights(s):
        m = jnp.max(s, axis=-1, keepdims=True)
        return m, jnp.exp2(s - m).astype(BF16)

    def heads_to_lanes(m0, a, m1, b):
        num = jnp.where(head0, a, b)
        den = pltpu.roll(jnp.where(head0, b, a), HEAD_DIM, 1)
        mx = jnp.where(head0, m0, m1)
        return num, den, mx

    def run_group(q_ref, k_ref, v_ref, d, prepare, store):
        L = S // d
        KW = min(L, QB + 2 * H)
        nb = L // QB

        key_head0 = lax.broadcasted_iota(jnp.int32, (KW, LANES), 1) < HEAD_DIM
        key_ones = jnp.ones((KW, LANES), BF16)

        def trip(i, carry):
            blocks = []
            for u in range(ATTN_UNROLL):
                t = i * ATTN_UNROLL + u
                r = t // nb
                q0 = pl.multiple_of((t % nb) * QB, QB)
                ws = pl.multiple_of(jnp.clip(q0 - H, 0, L - KW), H)
                bias = bias_ref[(q0 - ws) // H][:, :KW]
                q_rows = q_ref[0, 0, r, pl.ds(q0, QB), :]
                k_rows = k_ref[0, 0, r, pl.ds(ws, KW), :]
                blocks.append((r, q0, ws, [scores(q_rows, k_rows, bias, h) for h in range(2)]))
            blocks = [(r, q0, ws, [weights(s) for s in ss]) for r, q0, ws, ss in blocks]
            done = []
            for r, q0, ws, ((m0, p0), (m1, p1)) in blocks:
                v_rows = v_ref[0, 0, r, pl.ds(ws, KW), :]
                a = jnp.dot(p0, jnp.where(key_head0, v_rows, key_ones), preferred_element_type=F32)
                b = jnp.dot(p1, jnp.where(key_head0, key_ones, v_rows), preferred_element_type=F32)
                done.append((r, q0, m0, a, m1, b))
            done = [(r, q0, prepare(q0, *heads_to_lanes(m0, a, m1, b))) for r, q0, m0, a, m1, b in done]
            for r, q0, vals in done:
                store(r, q0, vals)
            return carry

        lax.fori_loop(0, S // QB // ATTN_UNROLL, trip, 0)

    def store_partial(g):
        d = ATTN_PAIRS[g][1]

        def store(r, q0, vals):
            rows = pl.ds(r + q0 * d, QB, stride=d)
            for n, val in enumerate(vals):
                part_ref[3 * (g - 1) + n, rows, :] = val
        return store

    keep = lambda q0, num, den, mx: (num, den, mx)
    run_group(q1_ref, k1_ref, v1_ref, ATTN_PAIRS[1][1], keep, store_partial(1))
    run_group(q2_ref, k2_ref, v2_ref, ATTN_PAIRS[2][1], keep, store_partial(2))

    def merge(q0, num, den, mx):
        rows = pl.ds(q0, QB)
        nums = [num, part_ref[0, rows, :], part_ref[3, rows, :]]
        dens = [den, part_ref[1, rows, :], part_ref[4, rows, :]]
        mxs = [mx, part_ref[2, rows, :], part_ref[5, rows, :]]
        top = jnp.maximum(jnp.maximum(mxs[0], mxs[1]), mxs[2])
        ws = [jnp.exp2(m - top) for m in mxs]
        n = ws[0] * nums[0] + ws[1] * nums[1] + ws[2] * nums[2]
        dn = ws[0] * dens[0] + ws[1] * dens[1] + ws[2] * dens[2]
        return (n / dn).astype(o_ref.dtype)

    def store_out(r, q0, y):
        o_ref[0, 0, pl.ds(q0, QB), :] = y

    run_group(q0_ref, k0_ref, v0_ref, ATTN_PAIRS[0][1], merge, store_out)


def _attention(qkv, B, S):
    P = ATTN_GROUP_W // LANES
    in_specs = []
    for _, d in ATTN_PAIRS:
        in_specs += [pl.BlockSpec((1, 1, d, S // d, LANES), lambda b, p: (b, p, 0, 0, 0))] * 3
    return pl.pallas_call(
        functools.partial(_attn_body, S=S),
        grid=(B, P),
        in_specs=in_specs,
        out_specs=pl.BlockSpec((1, 1, S, LANES), lambda b, p: (b, p, 0, 0)),
        out_shape=jax.ShapeDtypeStruct((B, P, S, LANES), BF16),
        scratch_shapes=[pltpu.VMEM((6, S, LANES), F32),
                        pltpu.VMEM((3, ATTN_QB, ATTN_QB + 2 * ATTN_HALF), F32)],
        compiler_params=_cparams(("parallel", "parallel")),
        name="attention",
    )(*qkv)


def _log_sigmoid(z):
    return jnp.minimum(z, 0.0) - jnp.log(1.0 + jnp.exp(-jnp.abs(z)))


def _ret_body(dec_ref, q_ref, k_ref, v_ref, g_ref, gnw_ref, o_ref, kt_ref, sf_ref, sb_ref, st_ref, *, S):
    C = RET_CHUNK
    nc = S // C
    lg = _log_sigmoid(dec_ref[0])
    a_row = lax.broadcasted_iota(jnp.int32, (C, LANES), 0).astype(F32)
    lane = lax.broadcasted_iota(jnp.int32, (C, LANES), 1)
    rel = (lax.broadcasted_iota(jnp.int32, (C, C), 0) - lax.broadcasted_iota(jnp.int32, (C, C), 1)).astype(F32)

    heads = []
    for h in range(2):
        lgf = lg[h:h + 1, :]
        lgb = lg[2 + h:3 + h, :]
        in_head = (lane < RET_QK_DIM) if h == 0 else (lane >= RET_QK_DIM)
        heads.append(dict(
            in_head=in_head,
            xi_f=jnp.where(in_head, jnp.exp((a_row + 1.0) * lgf), 0.0),
            xi_b=jnp.where(in_head, jnp.exp((C - a_row) * lgb), 0.0),
            zeta_f=jnp.exp((C - 1.0 - a_row) * lgf),
            zeta_b=jnp.exp(a_row * lgb),
            dloc=jnp.where(rel > 0, jnp.exp(rel * lgf), jnp.where(rel < 0, jnp.exp(-rel * lgb), 2.0)),
            cd_f=jnp.exp(C * lgf),
            cd_b=jnp.exp(C * lgb),
        ))

    def chunk_rows(n):
        return pl.ds(pl.multiple_of(n * C, C), C)

    def v_head(rows, h):
        return v_ref[0, rows, h * RET_V_DIM:(h + 1) * RET_V_DIM]

    def products(i, carry):
        items = []
        for u in range(RET_UNROLL):
            n = i * RET_UNROLL + u
            rows = chunk_rows(n)
            kt = k_ref[0, rows, :].astype(F32).T.astype(BF16)
            kt_ref[n] = kt
            for h, hd in enumerate(heads):
                vh = v_head(rows, h).astype(F32)
                items.append((n, h, kt, (vh * hd["zeta_f"]).astype(BF16), (vh * hd["zeta_b"]).astype(BF16)))
        outs = [(n, h, jnp.dot(kt, vf, preferred_element_type=F32), jnp.dot(kt, vb, preferred_element_type=F32))
                for n, h, kt, vf, vb in items]
        for n, h, f, b in outs:
            sf_ref[n, h] = f
            sb_ref[n, h] = b
        return carry

    lax.fori_loop(0, nc // RET_UNROLL, products, 0)

    def scan(ref, key, order, half):
        rows = slice(half * 2 * RET_QK_DIM, (half + 1) * 2 * RET_QK_DIM)

        def step(i, state):
            n = order(i)
            new = []
            for h, hd in enumerate(heads):
                st_ref[n, h, rows, :] = state[h].astype(BF16)
                new.append(hd[key] * state[h] + ref[n, h])
            return tuple(new)
        zero = jnp.zeros(ref.shape[2:], F32)
        lax.fori_loop(0, nc, step, (zero, zero))

    scan(sf_ref, "cd_f", lambda i: i, 0)
    scan(sb_ref, "cd_b", lambda i: nc - 1 - i, 1)

    def outputs(i, carry):
        items = []
        for u in range(RET_UNROLL):
            n = i * RET_UNROLL + u
            rows = chunk_rows(n)
            qp = q_ref[0, rows, :]
            qf = qp.astype(F32)
            kt = kt_ref[n]
            for h, hd in enumerate(heads):
                qm = jnp.where(hd["in_head"], qp, jnp.zeros_like(qp))
                qx = jnp.concatenate([(qf * hd["xi_f"]).astype(BF16), (qf * hd["xi_b"]).astype(BF16)], axis=1)
                items.append((rows, h, hd, jnp.dot(qm, kt, preferred_element_type=F32),
                              jnp.dot(qx, st_ref[n, h], preferred_element_type=F32)))
        items = [(rows, h, hd, (s * hd["dloc"]).astype(BF16), cross) for rows, h, hd, s, cross in items]
        items = [(rows, h, cross + jnp.dot(p, v_head(rows, h), preferred_element_type=F32))
                 for rows, h, hd, p, cross in items]
        for rows, h, ret in items:
            mu = jnp.mean(ret, axis=-1, keepdims=True)
            xc = ret - mu
            var = jnp.mean(xc * xc, axis=-1, keepdims=True)
            cols = slice(h * RET_V_DIM, (h + 1) * RET_V_DIM)
            gate = g_ref[0, rows, cols].astype(F32)
            y = xc * lax.rsqrt(var + NORM_EPS) * gnw_ref[:, cols] * (gate * jax.nn.sigmoid(gate))
            o_ref[0, rows, cols] = y.astype(o_ref.dtype)
        return carry

    lax.fori_loop(0, nc // RET_UNROLL, outputs, 0)


def _retention(dec, qr, kr, vr, gr, gn_w, B, S):
    nc = S // RET_CHUNK
    npairs = RET_HEADS // 2
    qk_spec = pl.BlockSpec((1, S, 2 * RET_QK_DIM), lambda b, p: (b, 0, p))
    v_spec = pl.BlockSpec((1, S, 2 * RET_V_DIM), lambda b, p: (b, 0, p))
    return pl.pallas_call(
        functools.partial(_ret_body, S=S),
        grid=(B, npairs),
        in_specs=[pl.BlockSpec((1, 4, LANES), lambda b, p: (p, 0, 0)),
                  qk_spec, qk_spec, v_spec, v_spec,
                  pl.BlockSpec((1, 2 * RET_V_DIM), lambda b, p: (0, p))],
        out_specs=v_spec,
        out_shape=jax.ShapeDtypeStruct((B, S, RET_V_W), BF16),
        scratch_shapes=[pltpu.VMEM((nc, 2 * RET_QK_DIM, RET_CHUNK), BF16),
                        pltpu.VMEM((nc, 2, 2 * RET_QK_DIM, RET_V_DIM), F32),
                        pltpu.VMEM((nc, 2, 2 * RET_QK_DIM, RET_V_DIM), F32),
                        pltpu.VMEM((nc, 2, 4 * RET_QK_DIM, RET_V_DIM), BF16)],
        compiler_params=_cparams(("parallel", "parallel")),
        name="retention",
    )(dec, qr.reshape(B, S, RET_QK_W), kr.reshape(B, S, RET_QK_W),
      vr.reshape(B, S, RET_V_W), gr.reshape(B, S, RET_V_W), gn_w)


ROUTE_EID, ROUTE_RANK, ROUTE_GATE = 0, 2, 4
ROUTE_FIELDS = 8
ROUTER_EXPERT_LANE0 = MOE_GROUPS
GROUP_SHIFT = MOE_EXPERTS_PER_GROUP.bit_length() - 1


def _merge_body(yatt_ref, yret_ref, gt_ref, x_ref, wa_ref, wb_ref, wo_ref,
                nw_ref, wrh_ref, wrl_ref, br_ref,
                h_ref, hn_ref, route_ref, route_t_ref, cnt_ref):
    tm = x_ref.shape[0]
    hm = tm // MERGE_SPLIT
    i = pl.program_id(0)
    cw = D_MODEL // MERGE_COL_CHUNKS

    def branch_products(rows):
        y_att = jnp.concatenate([yatt_ref[0, c, rows, :] for c in range(yatt_ref.shape[1])], axis=1)
        y_ret = yret_ref[rows, :]
        chunks = []
        for c in range(MERGE_COL_CHUNKS):
            cols = slice(c * cw, (c + 1) * cw)
            a = jnp.dot(y_att, wa_ref[:, cols], preferred_element_type=F32)
            b = jnp.dot(y_ret, wb_ref[:, cols], preferred_element_type=F32)
            g_att = gt_ref[rows, c * cw:(c + 1) * cw].astype(F32)
            g_ret = gt_ref[rows, D_MODEL + c * cw:D_MODEL + (c + 1) * cw].astype(F32)
            chunks.append((g_att * a + g_ret * b).astype(BF16))
        return jnp.concatenate(chunks, axis=1)

    def residual_norm(hf, rows, merged):
        mix = jnp.dot(merged, wo_ref[...], preferred_element_type=F32)
        h = x_ref[rows, :] + mix
        h_ref[rows, :] = h
        ms = jnp.mean(h * h, axis=-1, keepdims=True)
        hn = h * lax.rsqrt(ms + NORM_EPS) * nw_ref[...]
        for j, word in enumerate(_pack_row(hn)):
            hn_ref[pl.ds(hf * hm * ROW_TILES + j, hm, stride=ROW_TILES), :] = word
        hi = hn.astype(BF16)
        lo = (hn - hi.astype(F32)).astype(BF16)
        return hi, lo

    def router_logits(hi, lo):
        return (jnp.dot(hi, wrh_ref[...], preferred_element_type=F32)
                + jnp.dot(hi, wrl_ref[...], preferred_element_type=F32)
                + jnp.dot(lo, wrh_ref[...], preferred_element_type=F32)) + br_ref[...]

    lane = lax.broadcasted_iota(jnp.int32, (hm, LANES), 1)
    far = jnp.int32(LANES)

    def first_argmax(vals, vmax):
        return jnp.min(jnp.where(vals == vmax, lane, far), axis=-1, keepdims=True)

    def route(logits):
        is_group = lane < MOE_GROUPS
        gl = jnp.where(is_group, logits, NEG_BIG)
        gmax = jnp.max(gl, axis=-1, keepdims=True)
        g_w = 1.0 / jnp.sum(jnp.where(is_group, jnp.exp(gl - gmax), 0.0), axis=-1, keepdims=True)
        g_idx = first_argmax(gl, gmax)
        e_lane = lane - ROUTER_EXPERT_LANE0
        in_group = (e_lane >= 0) & (e_lane < MOE_N_EXPERTS) & (jnp.right_shift(e_lane, GROUP_SHIFT) == g_idx)
        el = jnp.where(in_group, logits, NEG_BIG)
        m1 = jnp.max(el, axis=-1, keepdims=True)
        i1 = first_argmax(el, m1)
        el2 = jnp.where(lane == i1, NEG_BIG, el)
        m2 = jnp.max(el2, axis=-1, keepdims=True)
        i2 = first_argmax(el2, m2)
        ex = jnp.exp(m2 - m1)
        return i1, i2, g_w / (1.0 + ex), g_w * ex / (1.0 + ex)

    rows = [slice(hf * hm, (hf + 1) * hm) for hf in range(MERGE_SPLIT)]
    merged = [branch_products(r) for r in rows]
    split = [residual_norm(hf, r, m) for hf, (r, m) in enumerate(zip(rows, merged))]
    routed = [route(router_logits(hi, lo)) for hi, lo in split]

    @pl.when(i == 0)
    def _():
        cnt_ref[...] = jnp.zeros(cnt_ref.shape, F32)

    r_idx = lax.broadcasted_iota(jnp.int32, (hm, hm), 0)
    c_idx = lax.broadcasted_iota(jnp.int32, (hm, hm), 1)
    lower = jnp.where(c_idx < r_idx, 1.0, 0.0).astype(BF16)
    running = cnt_ref[...]
    for hf, (i1, i2, gate1, gate2) in enumerate(routed):
        hot1 = lane == i1
        hot2 = lane == i2
        onehot = jnp.where(hot1 | hot2, 1.0, 0.0)
        before = jnp.dot(lower, onehot.astype(BF16), preferred_element_type=F32) + running
        rank1 = jnp.sum(jnp.where(hot1, before, 0.0), axis=-1, keepdims=True)
        rank2 = jnp.sum(jnp.where(hot2, before, 0.0), axis=-1, keepdims=True)
        running = running + jnp.sum(onehot, axis=0, keepdims=True)
        rec = jnp.zeros((hm, LANES), F32)
        for pos, val in ((ROUTE_EID, (i1 - ROUTER_EXPERT_LANE0).astype(F32)),
                         (ROUTE_EID + 1, (i2 - ROUTER_EXPERT_LANE0).astype(F32)),
                         (ROUTE_RANK, rank1), (ROUTE_RANK + 1, rank2),
                         (ROUTE_GATE, gate1), (ROUTE_GATE + 1, gate2)):
            rec = jnp.where(lane == pos, val, rec)
        route_ref[rows[hf], :] = rec
        route_t_ref[:, hf * hm:(hf + 1) * hm] = rec.T[:route_t_ref.shape[0], :]
    cnt_ref[...] = running


def _merge_route(y_att, y_ret, gates, x2, wa, wb, wo, nw, wr_hi, wr_lo, b_r, T, S):
    tm = TM_MERGE
    nt = S // tm
    row = lambda i: (i, 0)
    const = lambda i: (0, 0)
    full = lambda arr: pl.BlockSpec(arr.shape, const)
    in_specs = ([pl.BlockSpec((1, y_att.shape[1], tm, LANES), lambda i: (i // nt, 0, i % nt, 0)),
                 pl.BlockSpec((tm, RET_V_W), row), pl.BlockSpec((tm, 2 * D_MODEL), row),
                 pl.BlockSpec((tm, D_MODEL), row),
                 full(wa), full(wb), full(wo), full(nw), full(wr_hi), full(wr_lo), full(b_r)])
    return pl.pallas_call(
        _merge_body,
        grid=(T // tm,),
        in_specs=in_specs,
        out_specs=[pl.BlockSpec((tm, D_MODEL), row),
                   pl.BlockSpec((tm * ROW_TILES, LANES), row),
                   pl.BlockSpec((tm, LANES), row), pl.BlockSpec((ROUTE_FIELDS, tm), lambda i: (0, i)),
                   pl.BlockSpec((1, LANES), const)],
        out_shape=[jax.ShapeDtypeStruct((T, D_MODEL), F32), jax.ShapeDtypeStruct((T * ROW_TILES, LANES), U32),
                   jax.ShapeDtypeStruct((T, LANES), F32), jax.ShapeDtypeStruct((ROUTE_FIELDS, T), F32),
                   jax.ShapeDtypeStruct((1, LANES), F32)],
        compiler_params=_cparams(("arbitrary",)),
        name="merge_route",
    )(y_att, y_ret, gates, x2, wa, wb, wo, nw, wr_hi, wr_lo, b_r)


ISSUE_UNROLL = 8


def _tile_rows(n):
    return pl.ds(pl.multiple_of(n * ROW_TILES, ROW_TILES), ROW_TILES)


def _pack_row(x):
    bits = lambda c: pltpu.bitcast(x[:, c * LANES:(c + 1) * LANES].astype(BF16).astype(F32), U32)
    return [(bits(j) >> BF16_BITS) | bits(j + ROW_TILES) for j in range(ROW_TILES)]


def _unpack_word(w):
    high = jnp.uint32(((1 << BF16_BITS) - 1) << BF16_BITS)
    return pltpu.bitcast(w << BF16_BITS, F32), pltpu.bitcast(w & high, F32)


def _dispatch_body(slot_ref, hn_ref, xs_ref, sem, *, T):
    i = pl.program_id(0)
    ch = hn_ref.shape[0] // ROW_TILES

    def row_copy(j, slot):
        return pltpu.make_async_copy(hn_ref.at[_tile_rows(j)], xs_ref.at[_tile_rows(slot)], sem)

    def issue(j, carry):
        t = i * ch + j
        row_copy(j, slot_ref[t]).start(priority=0)
        row_copy(j, slot_ref[T + t]).start(priority=1)
        return carry

    lax.fori_loop(0, ch, issue, 0, unroll=ISSUE_UNROLL)
    for _ in range(2):
        pltpu.make_async_copy(hn_ref, xs_ref.at[pl.ds(0, ch * ROW_TILES)], sem).wait()


def _dispatch(slots, hn, n_slots, T):
    ch = DISPATCH_CHUNK
    grid_spec = pltpu.PrefetchScalarGridSpec(
        num_scalar_prefetch=1,
        grid=(T // ch,),
        in_specs=[pl.BlockSpec((ch * ROW_TILES, LANES), lambda i, s: (i, 0))],
        out_specs=pl.BlockSpec(memory_space=pl.ANY),
        scratch_shapes=[pltpu.SemaphoreType.DMA(())],
    )
    return pl.pallas_call(
        functools.partial(_dispatch_body, T=T),
        grid_spec=grid_spec,
        out_shape=jax.ShapeDtypeStruct((n_slots * ROW_TILES, LANES), U32),
        compiler_params=_cparams(("arbitrary",)),
        name="moe_dispatch",
    )(slots, hn)


def _expert_body(blk_ref, eid_ref, valid_ref, fresh_ref, next_ref, x_ref, w1_ref, w3_ref, w2_ref, y_ref,
                 w1b, w3b, w2b, w1s, w3s, w2s, wsem):
    i = pl.program_id(0)
    valid = valid_ref[i]

    def weight_copies(e):
        return (pltpu.make_async_copy(w1_ref.at[e], w1s, wsem.at[0]),
                pltpu.make_async_copy(w3_ref.at[e], w3s, wsem.at[1]),
                pltpu.make_async_copy(w2_ref.at[e], w2s, wsem.at[2]))

    @pl.when(i == 0)
    def _():
        for cp in weight_copies(eid_ref[0]):
            cp.start()

    @pl.when(valid > 0)
    def _():
        @pl.when(fresh_ref[i] == 1)
        def _():
            for cp in weight_copies(eid_ref[i]):
                cp.wait()
            w1b[...] = w1s[...].astype(BF16)
            w3b[...] = w3s[...].astype(BF16)
            w2b[...] = w2s[...].astype(BF16)

            @pl.when(next_ref[i] >= 0)
            def _():
                for cp in weight_copies(next_ref[i]):
                    cp.start()

        bm = x_ref.shape[0] // ROW_TILES

        def swiglu(nrows):
            live = lax.broadcasted_iota(jnp.int32, (nrows, LANES), 0) < valid
            halves = [_unpack_word(x_ref[pl.ds(j, nrows, stride=ROW_TILES), :]) for j in range(ROW_TILES)]
            x = jnp.concatenate(
                [jnp.where(live, c, 0.0).astype(BF16) for c in [lo for lo, _ in halves] + [hi for _, hi in halves]],
                axis=1)
            a = jnp.dot(x, w1b[...], preferred_element_type=F32)
            b = jnp.dot(x, w3b[...], preferred_element_type=F32)
            hid = (a * jax.nn.sigmoid(a) * b).astype(BF16)
            y = jnp.dot(hid, w2b[...], preferred_element_type=F32)
            for j, word in enumerate(_pack_row(y)):
                y_ref[pl.ds(j, nrows, stride=ROW_TILES), :] = word

        @pl.when(valid > bm // 2)
        def _():
            swiglu(bm)

        @pl.when(valid <= bm // 2)
        def _():
            swiglu(bm // 2)


def _experts(blk, blk_eid, blk_valid, blk_fresh, blk_next, x_slots, w1, w3, w2, n_blocks):
    bm = MOE_BM
    slot_block = lambda i, blk, eid, val, fr, nx: (blk[i], 0)
    grid_spec = pltpu.PrefetchScalarGridSpec(
        num_scalar_prefetch=5,
        grid=(n_blocks,),
        in_specs=[pl.BlockSpec((bm * ROW_TILES, LANES), slot_block),
                  pl.BlockSpec(memory_space=pl.ANY), pl.BlockSpec(memory_space=pl.ANY),
                  pl.BlockSpec(memory_space=pl.ANY)],
        out_specs=pl.BlockSpec((bm * ROW_TILES, LANES), slot_block),
        scratch_shapes=[pltpu.VMEM((D_MODEL, MOE_HIDDEN), BF16), pltpu.VMEM((D_MODEL, MOE_HIDDEN), BF16),
                        pltpu.VMEM((MOE_HIDDEN, D_MODEL), BF16),
                        pltpu.VMEM((D_MODEL, MOE_HIDDEN), F32), pltpu.VMEM((D_MODEL, MOE_HIDDEN), F32),
                        pltpu.VMEM((MOE_HIDDEN, D_MODEL), F32), pltpu.SemaphoreType.DMA((3,))],
    )
    return pl.pallas_call(
        _expert_body,
        grid_spec=grid_spec,
        out_shape=jax.ShapeDtypeStruct(x_slots.shape, U32),
        compiler_params=_cparams(("arbitrary",)),
        name="moe_experts",
    )(blk, blk_eid, blk_valid, blk_fresh, blk_next, x_slots, w1, w3, w2)


def _combine_body(slot_ref, ys_ref, h_ref, route_ref, nw_ref, o_ref, ybuf, sem, *, T):
    i = pl.program_id(0)
    n = pl.num_programs(0)
    tm = h_ref.shape[0]

    def row_copy(slot, buf, k, j):
        return pltpu.make_async_copy(ys_ref.at[_tile_rows(slot)], ybuf.at[buf, k, _tile_rows(j)], sem.at[buf])

    def issue(tile, buf):
        def one(j, carry):
            t = tile * tm + j
            row_copy(slot_ref[t], buf, 0, j).start(priority=0)
            row_copy(slot_ref[T + t], buf, 1, j).start(priority=1)
            return carry
        lax.fori_loop(0, tm, one, 0, unroll=ISSUE_UNROLL)

    @pl.when(i == 0)
    def _():
        issue(0, 0)

    @pl.when(i + 1 < n)
    def _():
        issue(i + 1, (i + 1) % 2)

    buf = i % 2
    for k in range(2):
        pltpu.make_async_copy(ys_ref.at[pl.ds(0, tm * ROW_TILES)], ybuf.at[buf, k], sem.at[buf]).wait()
    route = route_ref[...]
    g1 = route[:, ROUTE_GATE:ROUTE_GATE + 1]
    g2 = route[:, ROUTE_GATE + 1:ROUTE_GATE + 2]
    hs = [None] * COL_CHUNKS
    ss = jnp.zeros((tm, 1), F32)
    for j in range(ROW_TILES):
        tile_row = pl.ds(j, tm, stride=ROW_TILES)
        first = _unpack_word(ybuf[buf, 0, tile_row, :])
        second = _unpack_word(ybuf[buf, 1, tile_row, :])
        for c, y1, y2 in ((j, first[0], second[0]), (j + ROW_TILES, first[1], second[1])):
            hc = h_ref[:, c * LANES:(c + 1) * LANES] + (y1 * g1 + y2 * g2)
            hs[c] = hc
            ss = ss + jnp.sum(hc * hc, axis=-1, keepdims=True)
    inv = lax.rsqrt(ss * (1.0 / D_MODEL) + NORM_EPS)
    for j, hj in enumerate(hs):
        cols = slice(j * LANES, (j + 1) * LANES)
        o_ref[:, cols] = hj * inv * nw_ref[:, cols]


def _combine(slots, y_slots, h, route, nw, T):
    tm = TM_COMBINE
    row = lambda i, s: (i, 0)
    grid_spec = pltpu.PrefetchScalarGridSpec(
        num_scalar_prefetch=1,
        grid=(T // tm,),
        in_specs=[pl.BlockSpec(memory_space=pl.ANY),
                  pl.BlockSpec((tm, D_MODEL), row),
                  pl.BlockSpec((tm, LANES), row),
                  pl.BlockSpec((1, D_MODEL), lambda i, s: (0, 0))],
        out_specs=pl.BlockSpec((tm, D_MODEL), row),
        scratch_shapes=[pltpu.VMEM((2, 2, tm * ROW_TILES, LANES), U32), pltpu.SemaphoreType.DMA((2,))],
    )
    return pl.pallas_call(
        functools.partial(_combine_body, T=T),
        grid_spec=grid_spec,
        out_shape=jax.ShapeDtypeStruct((T, D_MODEL), F32),
        compiler_params=_cparams(("arbitrary",)),
        name="moe_combine",
    )(slots, y_slots, h, route, nw)


def _rotary_tables(S):
    inv_freq = (1.0 / (np.float32(ROPE_THETA) ** (np.arange(0, HEAD_DIM, 2, dtype=np.float32) / HEAD_DIM))
                ).astype(np.float32)
    ang = np.arange(S, dtype=np.float32)[:, None] * inv_freq[None, :]
    cos, sin = np.cos(ang), np.sin(ang)
    reps = LANES // HEAD_DIM
    cos_t = np.tile(np.concatenate([cos, cos], axis=1), (1, reps)).astype(np.float32)
    sin_t = np.tile(np.concatenate([-sin, sin], axis=1), (1, reps)).astype(np.float32)
    return jnp.asarray(cos_t), jnp.asarray(sin_t)


def _layer(h_in, norm_mix_w, w_in, b_branch_gate, ret_decay_fwd, ret_decay_bwd, ret_gn_w, w_attn_branch,
           w_ret_branch, w_out, norm_moe_w, moe_w_group, moe_b_group, moe_w_expert, moe_b_expert,
           moe_w1, moe_w3, moe_w2, next_norm_w, B, S, cos_t, sin_t):
    T = B * S
    (qa0, ka0, va0, qa1, ka1, va1, qa2, ka2, va2, qr, kr, vr, gr, gates) = _in_projection(
        h_in, norm_mix_w[None, :], w_in.astype(BF16), b_branch_gate[None, :], cos_t, sin_t, B, S)

    unit = lambda a: a[:, :, None]
    y_att = _attention((unit(qa0), unit(ka0), unit(va0), qa1, ka1, va1, qa2, ka2, va2), B, S)

    dec = jnp.stack([ret_decay_fwd.reshape(RET_HEADS // 2, 2), ret_decay_bwd.reshape(RET_HEADS // 2, 2)], axis=1)
    dec = jnp.broadcast_to(dec.reshape(RET_HEADS // 2, 4, 1), (RET_HEADS // 2, 4, LANES)).astype(F32)
    y_ret = _retention(dec, qr, kr, vr, gr, ret_gn_w[None, :], B, S).reshape(T, RET_V_W)

    pad = LANES - MOE_GROUPS - MOE_N_EXPERTS
    w_r = jnp.concatenate([moe_w_group, moe_w_expert, jnp.zeros((D_MODEL, pad), F32)], axis=1)
    w_r_hi = w_r.astype(BF16)
    w_r_lo = (w_r - w_r_hi.astype(F32)).astype(BF16)
    b_r = jnp.concatenate([moe_b_group, moe_b_expert, jnp.zeros((pad,), F32)])[None, :]

    h_mid, hn, route, route_t, cnt = _merge_route(
        y_att, y_ret, gates, h_in, w_attn_branch.astype(BF16), w_ret_branch.astype(BF16),
        w_out.astype(BF16), norm_moe_w[None, :], w_r_hi, w_r_lo, b_r, T, S)

    bm = MOE_BM
    counts = cnt[0, ROUTER_EXPERT_LANE0:ROUTER_EXPERT_LANE0 + MOE_N_EXPERTS].astype(jnp.int32)
    nblk = (counts + bm - 1) // bm
    blk_end = jnp.cumsum(nblk)
    pstart = (blk_end - nblk) * bm
    n_blocks = (2 * T) // bm + MOE_N_EXPERTS
    n_active = blk_end[-1]
    bidx = jnp.minimum(jnp.arange(n_blocks, dtype=jnp.int32), n_active - 1)
    blk_eid = jnp.sum(bidx[:, None] >= blk_end[None, :], axis=1).astype(jnp.int32)
    mine = blk_eid[:, None] == jnp.arange(MOE_N_EXPERTS, dtype=jnp.int32)[None, :]
    seg_end = jnp.sum(jnp.where(mine, (pstart + counts)[None, :], 0), axis=1)
    blk_valid = jnp.clip(seg_end - bidx * bm, 0, bm)
    blk_valid = jnp.where(jnp.arange(n_blocks) < n_active, blk_valid, 0).astype(jnp.int32)
    blk_fresh = jnp.concatenate([jnp.ones((1,), jnp.int32), (blk_eid[1:] != blk_eid[:-1]).astype(jnp.int32)])
    ar = jnp.arange(MOE_N_EXPERTS, dtype=jnp.int32)
    later = jnp.min(jnp.where((nblk > 0)[None, :] & (ar[None, :] > ar[:, None]), ar[None, :], MOE_N_EXPERTS), axis=1)
    later = jnp.where(later < MOE_N_EXPERTS, later, -1)
    blk_next = (jnp.sum(jnp.where(mine, later[None, :] + 1, 0), axis=1) - 1).astype(jnp.int32)
    eid = route_t[ROUTE_EID:ROUTE_EID + 2].astype(jnp.int32)
    rank = route_t[ROUTE_RANK:ROUTE_RANK + 2].astype(jnp.int32)
    start = jnp.sum(jnp.where(eid[..., None] == jnp.arange(MOE_N_EXPERTS, dtype=jnp.int32),
                              pstart.astype(jnp.int32), 0), axis=-1)
    slots = (start + rank).reshape(2 * T)

    x_slots = _dispatch(slots, hn, n_blocks * bm, T)
    y_slots = _experts(bidx, blk_eid, blk_valid, blk_fresh, blk_next, x_slots, moe_w1, moe_w3, moe_w2, n_blocks)
    return _combine(slots, y_slots, h_mid, route, next_norm_w[None, :], T)


def kernel(x, norm_mix_w, w_in, b_branch_gate, ret_decay_fwd, ret_decay_bwd, ret_gn_w, w_attn_branch,
           w_ret_branch, w_out, norm_moe_w, moe_w_group, moe_b_group, moe_w_expert, moe_b_expert, moe_w1,
           moe_w3, moe_w2, norm_final_w):
    B, S, D = x.shape
    depth = norm_mix_w.shape[0]
    assert depth == 1, "the final norm is fused into the layer's combine stage"
    assert D == D_MODEL and S % TM_INPROJ == 0 and (B * S) < (1 << 24)
    cos_t, sin_t = _rotary_tables(S)
    out = _layer(x.reshape(B * S, D), norm_mix_w[0], w_in[0], b_branch_gate[0], ret_decay_fwd[0],
                 ret_decay_bwd[0], ret_gn_w[0], w_attn_branch[0], w_ret_branch[0], w_out[0], norm_moe_w[0],
                 moe_w_group[0], moe_b_group[0], moe_w_expert[0], moe_b_expert[0], moe_w1[0], moe_w3[0],
                 moe_w2[0], norm_final_w, B, S, cos_t, sin_t)
    return out.reshape(B, S, D)
```

```python
import functools

import numpy as np
import jax
import jax.numpy as jnp
from jax import lax
from jax.experimental import pallas as pl
from jax.experimental.pallas import tpu as pltpu

F32 = jnp.float32
BF16 = jnp.bfloat16

D_MODEL = 1024
HEAD_DIM = 64
ATTN_PAIRS = ((128, 1), (512, 4), (2048, 16))
ATTN_HEADS_PER_GROUP = 8
ATTN_GROUP_W = ATTN_HEADS_PER_GROUP * HEAD_DIM
ATTN_HALF = 64
ROPE_THETA = 10000.0
RET_HEADS = 8
RET_QK_DIM = 64
RET_V_DIM = 128
RET_CHUNK = 128
RET_QK_W = RET_HEADS * RET_QK_DIM
RET_V_W = RET_HEADS * RET_V_DIM
MOE_GROUPS = 8
MOE_EXPERTS_PER_GROUP = 8
MOE_N_EXPERTS = MOE_GROUPS * MOE_EXPERTS_PER_GROUP
MOE_HIDDEN = 512
NORM_EPS = 1e-6

LANES = 128
COL_CHUNKS = D_MODEL // LANES
ROW_TILES = COL_CHUNKS // 2
U32 = jnp.uint32
BF16_BITS = 16
NEG_BIG = -1e30
LOG2_E = 1.4426950408889634

TM_INPROJ = 512
TM_MERGE = 1024
MERGE_SPLIT = 4
MERGE_COL_CHUNKS = 4
TM_COMBINE = 512
MOE_BM = 1024
MOE_TAIL = 256
DISPATCH_CHUNK = 4096
ATTN_QB = 128
ATTN_UNROLL = 8
RET_UNROLL = 8

V7X_VMEM_BYTES = 64 * 1024 * 1024
VMEM_LIMIT = V7X_VMEM_BYTES * 7 // 8

_A = 3 * ATTN_GROUP_W
OFF_QA, OFF_KA, OFF_VA = 0, _A, 2 * _A
OFF_QR = 3 * _A
OFF_KR = OFF_QR + RET_QK_W
OFF_VR = OFF_KR + RET_QK_W
OFF_GR = OFF_VR + RET_V_W
OFF_GL = OFF_GR + RET_V_W
IN_W = OFF_GL + 2 * D_MODEL


def _cparams(sem, vmem=VMEM_LIMIT):
    return pltpu.CompilerParams(dimension_semantics=sem, vmem_limit_bytes=vmem)


def _inproj_body(x_ref, nw_ref, w_ref, bg_ref, cos_ref, sin_ref,
                 qa0, ka0, va0, qa1, ka1, va1, qa2, ka2, va2, qr, kr, vr, gr, gt,
                 stage_ref):
    tm = x_ref.shape[0]
    x = x_ref[...]
    ms = jnp.mean(x * x, axis=-1, keepdims=True)
    xn = (x * lax.rsqrt(ms + NORM_EPS) * nw_ref[...]).astype(BF16)
    cos = cos_ref[...]
    sin = sin_ref[...]
    lane = lax.broadcasted_iota(jnp.int32, (tm, LANES), 1)
    first_half = (lane & (HEAD_DIM - 1)) < (HEAD_DIM // 2)

    def proj(c0, width):
        return jnp.dot(xn, w_ref[:, c0:c0 + width], preferred_element_type=F32)

    def rotary(a, scale):
        partner = jnp.where(first_half, pltpu.roll(a, LANES - HEAD_DIM // 2, 1),
                            pltpu.roll(a, HEAD_DIM // 2, 1))
        r = a * cos + partner * sin
        return r * scale if scale != 1.0 else r

    def chunks(acc):
        return [acc[:, c * LANES:(c + 1) * LANES] for c in range(acc.shape[1] // LANES)]

    def store_natural(out_ref, acc, fn):
        for c, a in enumerate(chunks(acc)):
            out_ref[:, c * LANES:(c + 1) * LANES] = fn(a).astype(out_ref.dtype)

    def store_pairs(out_ref, acc, fn):
        for c, a in enumerate(chunks(acc)):
            out_ref[0, c] = fn(a).astype(out_ref.dtype)

    def store_strided(out_ref, acc, fn, d):
        for c, a in enumerate(chunks(acc)):
            stage_ref[c] = fn(a)
        for c in range(acc.shape[1] // LANES):
            for r in range(d):
                out_ref[0, c, r] = stage_ref[c, pl.ds(r, tm // d, stride=d), :].astype(out_ref.dtype)

    ident = lambda a: a
    rot_q = lambda a: rotary(a, HEAD_DIM ** -0.5 * LOG2_E)
    rot_1 = lambda a: rotary(a, 1.0)
    rot_k = lambda a: rotary(a, RET_QK_DIM ** -0.5)

    W = ATTN_GROUP_W
    store_pairs(qa0, proj(OFF_QA, W), rot_q)
    store_pairs(ka0, proj(OFF_KA, W), rot_1)
    store_pairs(va0, proj(OFF_VA, W), ident)
    for g, (qo, ko, vo) in ((1, (qa1, ka1, va1)), (2, (qa2, ka2, va2))):
        d = ATTN_PAIRS[g][1]
        store_strided(qo, proj(OFF_QA + g * W, W), rot_q, d)
        store_strided(ko, proj(OFF_KA + g * W, W), rot_1, d)
        store_strided(vo, proj(OFF_VA + g * W, W), ident, d)
    store_natural(qr, proj(OFF_QR, RET_QK_W), rot_1)
    store_natural(kr, proj(OFF_KR, RET_QK_W), rot_k)
    for h in range(RET_V_W // W):
        vr[:, h * W:(h + 1) * W] = proj(OFF_VR + h * W, W).astype(vr.dtype)
        gr[:, h * W:(h + 1) * W] = proj(OFF_GR + h * W, W).astype(gr.dtype)
    for h in range(2 * D_MODEL // W):
        z = proj(OFF_GL + h * W, W) + bg_ref[:, h * W:(h + 1) * W]
        gt[:, h * W:(h + 1) * W] = jax.nn.sigmoid(z).astype(gt.dtype)


def _in_projection(x2, norm_w, w_bf, b_gate, cos_t, sin_t, B, S):
    T = B * S
    tm = TM_INPROJ
    nt = S // tm
    W = ATTN_GROUP_W
    row = lambda i: (i, 0)
    const = lambda i: (0, 0)
    nat = lambda width: pl.BlockSpec((tm, width), row)

    P = W // LANES

    def strided_spec(d):
        return pl.BlockSpec((1, P, d, tm // d, LANES), lambda i: (i // nt, 0, 0, i % nt, 0))

    def strided_shape(d):
        return jax.ShapeDtypeStruct((B, P, d, S // d, LANES), BF16)

    pair_spec = pl.BlockSpec((1, P, tm, LANES), lambda i: (i // nt, 0, i % nt, 0))
    pair_shape = jax.ShapeDtypeStruct((B, P, S, LANES), BF16)
    nat_shape = lambda width: jax.ShapeDtypeStruct((T, width), BF16)
    d1, d2 = ATTN_PAIRS[1][1], ATTN_PAIRS[2][1]
    out_shape = ([pair_shape] * 3 + [strided_shape(d1)] * 3 + [strided_shape(d2)] * 3
                 + [nat_shape(RET_QK_W)] * 2 + [nat_shape(RET_V_W)] * 2 + [nat_shape(2 * D_MODEL)])
    out_specs = ([pair_spec] * 3 + [strided_spec(d1)] * 3 + [strided_spec(d2)] * 3
                 + [nat(RET_QK_W)] * 2 + [nat(RET_V_W)] * 2 + [nat(2 * D_MODEL)])
    in_specs = [
        pl.BlockSpec((tm, D_MODEL), row),
        pl.BlockSpec((1, D_MODEL), const),
        pl.BlockSpec((D_MODEL, IN_W), const, pipeline_mode=pl.Buffered(1)),
        pl.BlockSpec((1, 2 * D_MODEL), const),
        pl.BlockSpec((tm, LANES), lambda i: (i % nt, 0)),
        pl.BlockSpec((tm, LANES), lambda i: (i % nt, 0)),
    ]
    return pl.pallas_call(
        _inproj_body,
        grid=(T // tm,),
        in_specs=in_specs,
        out_specs=out_specs,
        out_shape=out_shape,
        scratch_shapes=[pltpu.VMEM((W // LANES, tm, LANES), F32)],
        compiler_params=_cparams(("parallel",)),
        name="in_projection",
    )(x2, norm_w, w_bf, b_gate, cos_t, sin_t)


def _attn_body(q0_ref, k0_ref, v0_ref, q1_ref, k1_ref, v1_ref, q2_ref, k2_ref, v2_ref, o_ref,
               part_ref, bias_ref, *, S):
    QB, H = ATTN_QB, ATTN_HALF
    lane = lax.broadcasted_iota(jnp.int32, (QB, LANES), 1)
    head0 = lane < HEAD_DIM

    qi = lax.broadcasted_iota(jnp.int32, (QB, QB + 2 * H), 0)
    ki = lax.broadcasted_iota(jnp.int32, (QB, QB + 2 * H), 1)
    for n in range(3):
        bias_ref[n] = jnp.where(jnp.abs(ki - qi - n * H) <= H, 0.0, NEG_BIG).astype(F32)

    def scores(q_rows, k_rows, bias, h):
        qm = jnp.where(head0 if h == 0 else jnp.logical_not(head0), q_rows, jnp.zeros_like(q_rows))
        return lax.dot_general(qm, k_rows, (((1,), (1,)), ((), ())), preferred_element_type=F32) + bias

    def weights(s):
        m = jnp.max(s, axis=-1, keepdims=True)
        return m, jnp.exp2(s - m).astype(BF16)

    def heads_to_lanes(m0, a, m1, b):
        num = jnp.where(head0, a, b)
        den = pltpu.roll(jnp.where(head0, b, a), HEAD_DIM, 1)
        mx = jnp.where(head0, m0, m1)
        return num, den, mx

    def run_group(q_ref, k_ref, v_ref, d, prepare, store):
        L = S // d
        KW = min(L, QB + 2 * H)
        nb = L // QB

        key_head0 = lax.broadcasted_iota(jnp.int32, (KW, LANES), 1) < HEAD_DIM
        key_ones = jnp.ones((KW, LANES), BF16)

        def trip(i, carry):
            blocks = []
            for u in range(ATTN_UNROLL):
                t = i * ATTN_UNROLL + u
                r = t // nb
                q0 = pl.multiple_of((t % nb) * QB, QB)
                ws = pl.multiple_of(jnp.clip(q0 - H, 0, L - KW), H)
                bias = bias_ref[(q0 - ws) // H][:, :KW]
                q_rows = q_ref[0, 0, r, pl.ds(q0, QB), :]
                k_rows = k_ref[0, 0, r, pl.ds(ws, KW), :]
                blocks.append((r, q0, ws, [scores(q_rows, k_rows, bias, h) for h in range(2)]))
            blocks = [(r, q0, ws, [weights(s) for s in ss]) for r, q0, ws, ss in blocks]
            done = []
            for r, q0, ws, ((m0, p0), (m1, p1)) in blocks:
                v_rows = v_ref[0, 0, r, pl.ds(ws, KW), :]
                a = jnp.dot(p0, jnp.where(key_head0, v_rows, key_ones), preferred_element_type=F32)
                b = jnp.dot(p1, jnp.where(key_head0, key_ones, v_rows), preferred_element_type=F32)
                done.append((r, q0, m0, a, m1, b))
            done = [(r, q0, prepare(q0, *heads_to_lanes(m0, a, m1, b))) for r, q0, m0, a, m1, b in done]
            for r, q0, vals in done:
                store(r, q0, vals)
            return carry

        lax.fori_loop(0, S // QB // ATTN_UNROLL, trip, 0)

    def store_partial(g):
        d = ATTN_PAIRS[g][1]

        def store(r, q0, vals):
            rows = pl.ds(r + q0 * d, QB, stride=d)
            for n, val in enumerate(vals):
                part_ref[3 * (g - 1) + n, rows, :] = val
        return store

    keep = lambda q0, num, den, mx: (num, den, mx)
    run_group(q1_ref, k1_ref, v1_ref, ATTN_PAIRS[1][1], keep, store_partial(1))
    run_group(q2_ref, k2_ref, v2_ref, ATTN_PAIRS[2][1], keep, store_partial(2))

    def merge(q0, num, den, mx):
        rows = pl.ds(q0, QB)
        nums = [num, part_ref[0, rows, :], part_ref[3, rows, :]]
        dens = [den, part_ref[1, rows, :], part_ref[4, rows, :]]
        mxs = [mx, part_ref[2, rows, :], part_ref[5, rows, :]]
        top = jnp.maximum(jnp.maximum(mxs[0], mxs[1]), mxs[2])
        ws = [jnp.exp2(m - top) for m in mxs]
        n = ws[0] * nums[0] + ws[1] * nums[1] + ws[2] * nums[2]
        dn = ws[0] * dens[0] + ws[1] * dens[1] + ws[2] * dens[2]
        return (n / dn).astype(o_ref.dtype)

    def store_out(r, q0, y):
        o_ref[0, 0, pl.ds(q0, QB), :] = y

    run_group(q0_ref, k0_ref, v0_ref, ATTN_PAIRS[0][1], merge, store_out)


def _attention(qkv, B, S):
    P = ATTN_GROUP_W // LANES
    in_specs = []
    for _, d in ATTN_PAIRS:
        in_specs += [pl.BlockSpec((1, 1, d, S // d, LANES), lambda b, p: (b, p, 0, 0, 0))] * 3
    return pl.pallas_call(
        functools.partial(_attn_body, S=S),
        grid=(B, P),
        in_specs=in_specs,
        out_specs=pl.BlockSpec((1, 1, S, LANES), lambda b, p: (b, p, 0, 0)),
        out_shape=jax.ShapeDtypeStruct((B, P, S, LANES), BF16),
        scratch_shapes=[pltpu.VMEM((6, S, LANES), F32),
                        pltpu.VMEM((3, ATTN_QB, ATTN_QB + 2 * ATTN_HALF), F32)],
        compiler_params=_cparams(("parallel", "parallel")),
        name="attention",
    )(*qkv)


def _log_sigmoid(z):
    return jnp.minimum(z, 0.0) - jnp.log(1.0 + jnp.exp(-jnp.abs(z)))


def _ret_body(dec_ref, q_ref, k_ref, v_ref, g_ref, gnw_ref, o_ref, kt_ref, sf_ref, sb_ref, st_ref, *, S):
    C = RET_CHUNK
    nc = S // C
    lg = _log_sigmoid(dec_ref[0])
    a_row = lax.broadcasted_iota(jnp.int32, (C, LANES), 0).astype(F32)
    lane = lax.broadcasted_iota(jnp.int32, (C, LANES), 1)
    rel = (lax.broadcasted_iota(jnp.int32, (C, C), 0) - lax.broadcasted_iota(jnp.int32, (C, C), 1)).astype(F32)

    heads = []
    for h in range(2):
        lgf = lg[h:h + 1, :]
        lgb = lg[2 + h:3 + h, :]
        in_head = (lane < RET_QK_DIM) if h == 0 else (lane >= RET_QK_DIM)
        heads.append(dict(
            in_head=in_head,
            xi_f=jnp.where(in_head, jnp.exp((a_row + 1.0) * lgf), 0.0),
            xi_b=jnp.where(in_head, jnp.exp((C - a_row) * lgb), 0.0),
            zeta_f=jnp.exp((C - 1.0 - a_row) * lgf),
            zeta_b=jnp.exp(a_row * lgb),
            dloc=jnp.where(rel > 0, jnp.exp(rel * lgf), jnp.where(rel < 0, jnp.exp(-rel * lgb), 2.0)),
            cd_f=jnp.exp(C * lgf),
            cd_b=jnp.exp(C * lgb),
        ))

    def chunk_rows(n):
        return pl.ds(pl.multiple_of(n * C, C), C)

    def v_head(rows, h):
        return v_ref[0, rows, h * RET_V_DIM:(h + 1) * RET_V_DIM]

    def products(i, carry):
        items = []
        for u in range(RET_UNROLL):
            n = i * RET_UNROLL + u
            rows = chunk_rows(n)
            kt = k_ref[0, rows, :].astype(F32).T.astype(BF16)
            kt_ref[n] = kt
            for h, hd in enumerate(heads):
                vh = v_head(rows, h).astype(F32)
                items.append((n, h, kt, (vh * hd["zeta_f"]).astype(BF16), (vh * hd["zeta_b"]).astype(BF16)))
        outs = [(n, h, jnp.dot(kt, vf, preferred_element_type=F32), jnp.dot(kt, vb, preferred_element_type=F32))
                for n, h, kt, vf, vb in items]
        for n, h, f, b in outs:
            sf_ref[n, h] = f
            sb_ref[n, h] = b
        return carry

    lax.fori_loop(0, nc // RET_UNROLL, products, 0)

    def scan(ref, key, order, half):
        rows = slice(half * 2 * RET_QK_DIM, (half + 1) * 2 * RET_QK_DIM)

        def step(i, state):
            n = order(i)
            new = []
            for h, hd in enumerate(heads):
                st_ref[n, h, rows, :] = state[h].astype(BF16)
                new.append(hd[key] * state[h] + ref[n, h])
            return tuple(new)
        zero = jnp.zeros(ref.shape[2:], F32)
        lax.fori_loop(0, nc, step, (zero, zero))

    scan(sf_ref, "cd_f", lambda i: i, 0)
    scan(sb_ref, "cd_b", lambda i: nc - 1 - i, 1)

    def outputs(i, carry):
        items = []
        for u in range(RET_UNROLL):
            n = i * RET_UNROLL + u
            rows = chunk_rows(n)
            qp = q_ref[0, rows, :]
            qf = qp.astype(F32)
            kt = kt_ref[n]
            for h, hd in enumerate(heads):
                qm = jnp.where(hd["in_head"], qp, jnp.zeros_like(qp))
                qx = jnp.concatenate([(qf * hd["xi_f"]).astype(BF16), (qf * hd["xi_b"]).astype(BF16)], axis=1)
                items.append((rows, h, hd, jnp.dot(qm, kt, preferred_element_type=F32),
                              jnp.dot(qx, st_ref[n, h], preferred_element_type=F32)))
        items = [(rows, h, hd, (s * hd["dloc"]).astype(BF16), cross) for rows, h, hd, s, cross in items]
        items = [(rows, h, cross + jnp.dot(p, v_head(rows, h), preferred_element_type=F32))
                 for rows, h, hd, p, cross in items]
        for rows, h, ret in items:
            mu = jnp.mean(ret, axis=-1, keepdims=True)
            xc = ret - mu
            var = jnp.mean(xc * xc, axis=-1, keepdims=True)
            cols = slice(h * RET_V_DIM, (h + 1) * RET_V_DIM)
            gate = g_ref[0, rows, cols].astype(F32)
            y = xc * lax.rsqrt(var + NORM_EPS) * gnw_ref[:, cols] * (gate * jax.nn.sigmoid(gate))
            o_ref[0, rows, cols] = y.astype(o_ref.dtype)
        return carry

    lax.fori_loop(0, nc // RET_UNROLL, outputs, 0)


def _retention(dec, qr, kr, vr, gr, gn_w, B, S):
    nc = S // RET_CHUNK
    npairs = RET_HEADS // 2
    qk_spec = pl.BlockSpec((1, S, 2 * RET_QK_DIM), lambda b, p: (b, 0, p))
    v_spec = pl.BlockSpec((1, S, 2 * RET_V_DIM), lambda b, p: (b, 0, p))
    return pl.pallas_call(
        functools.partial(_ret_body, S=S),
        grid=(B, npairs),
        in_specs=[pl.BlockSpec((1, 4, LANES), lambda b, p: (p, 0, 0)),
                  qk_spec, qk_spec, v_spec, v_spec,
                  pl.BlockSpec((1, 2 * RET_V_DIM), lambda b, p: (0, p))],
        out_specs=v_spec,
        out_shape=jax.ShapeDtypeStruct((B, S, RET_V_W), BF16),
        scratch_shapes=[pltpu.VMEM((nc, 2 * RET_QK_DIM, RET_CHUNK), BF16),
                        pltpu.VMEM((nc, 2, 2 * RET_QK_DIM, RET_V_DIM), F32),
                        pltpu.VMEM((nc, 2, 2 * RET_QK_DIM, RET_V_DIM), F32),
                        pltpu.VMEM((nc, 2, 4 * RET_QK_DIM, RET_V_DIM), BF16)],
        compiler_params=_cparams(("parallel", "parallel")),
        name="retention",
    )(dec, qr.reshape(B, S, RET_QK_W), kr.reshape(B, S, RET_QK_W),
      vr.reshape(B, S, RET_V_W), gr.reshape(B, S, RET_V_W), gn_w)


ROUTE_EID, ROUTE_RANK, ROUTE_GATE = 0, 2, 4
ROUTE_FIELDS = 8
ROUTER_EXPERT_LANE0 = MOE_GROUPS
GROUP_SHIFT = MOE_EXPERTS_PER_GROUP.bit_length() - 1


def _merge_body(yatt_ref, yret_ref, gt_ref, x_ref, wa_ref, wb_ref, wo_ref,
                nw_ref, wrh_ref, wrl_ref, br_ref,
                h_ref, hn_ref, route_ref, route_t_ref, cnt_ref):
    tm = x_ref.shape[0]
    hm = tm // MERGE_SPLIT
    i = pl.program_id(0)
    cw = D_MODEL // MERGE_COL_CHUNKS

    def branch_products(rows):
        y_att = jnp.concatenate([yatt_ref[0, c, rows, :] for c in range(yatt_ref.shape[1])], axis=1)
        y_ret = yret_ref[rows, :]
        chunks = []
        for c in range(MERGE_COL_CHUNKS):
            cols = slice(c * cw, (c + 1) * cw)
            a = jnp.dot(y_att, wa_ref[:, cols], preferred_element_type=F32)
            b = jnp.dot(y_ret, wb_ref[:, cols], preferred_element_type=F32)
            g_att = gt_ref[rows, c * cw:(c + 1) * cw].astype(F32)
            g_ret = gt_ref[rows, D_MODEL + c * cw:D_MODEL + (c + 1) * cw].astype(F32)
            chunks.append((g_att * a + g_ret * b).astype(BF16))
        return jnp.concatenate(chunks, axis=1)

    def residual_norm(hf, rows, merged):
        mix = jnp.dot(merged, wo_ref[...], preferred_element_type=F32)
        h = x_ref[rows, :] + mix
        h_ref[rows, :] = h
        ms = jnp.mean(h * h, axis=-1, keepdims=True)
        hn = h * lax.rsqrt(ms + NORM_EPS) * nw_ref[...]
        for j, word in enumerate(_pack_row(hn)):
            hn_ref[pl.ds(hf * hm * ROW_TILES + j, hm, stride=ROW_TILES), :] = word
        hi = hn.astype(BF16)
        lo = (hn - hi.astype(F32)).astype(BF16)
        return hi, lo

    def router_logits(hi, lo):
        return (jnp.dot(hi, wrh_ref[...], preferred_element_type=F32)
                + jnp.dot(hi, wrl_ref[...], preferred_element_type=F32)
                + jnp.dot(lo, wrh_ref[...], preferred_element_type=F32)) + br_ref[...]

    lane = lax.broadcasted_iota(jnp.int32, (hm, LANES), 1)
    far = jnp.int32(LANES)

    def first_argmax(vals, vmax):
        return jnp.min(jnp.where(vals == vmax, lane, far), axis=-1, keepdims=True)

    def route(logits):
        is_group = lane < MOE_GROUPS
        gl = jnp.where(is_group, logits, NEG_BIG)
        gmax = jnp.max(gl, axis=-1, keepdims=True)
        g_w = 1.0 / jnp.sum(jnp.where(is_group, jnp.exp(gl - gmax), 0.0), axis=-1, keepdims=True)
        g_idx = first_argmax(gl, gmax)
        e_lane = lane - ROUTER_EXPERT_LANE0
        in_group = (e_lane >= 0) & (e_lane < MOE_N_EXPERTS) & (jnp.right_shift(e_lane, GROUP_SHIFT) == g_idx)
        el = jnp.where(in_group, logits, NEG_BIG)
        m1 = jnp.max(el, axis=-1, keepdims=True)
        i1 = first_argmax(el, m1)
        el2 = jnp.where(lane == i1, NEG_BIG, el)
        m2 = jnp.max(el2, axis=-1, keepdims=True)
        i2 = first_argmax(el2, m2)
        ex = jnp.exp(m2 - m1)
        return i1, i2, g_w / (1.0 + ex), g_w * ex / (1.0 + ex)

    rows = [slice(hf * hm, (hf + 1) * hm) for hf in range(MERGE_SPLIT)]
    merged = [branch_products(r) for r in rows]
    split = [residual_norm(hf, r, m) for hf, (r, m) in enumerate(zip(rows, merged))]
    routed = [route(router_logits(hi, lo)) for hi, lo in split]

    @pl.when(i == 0)
    def _():
        cnt_ref[...] = jnp.zeros(cnt_ref.shape, F32)

    r_idx = lax.broadcasted_iota(jnp.int32, (hm, hm), 0)
    c_idx = lax.broadcasted_iota(jnp.int32, (hm, hm), 1)
    lower = jnp.where(c_idx < r_idx, 1.0, 0.0).astype(BF16)
    running = cnt_ref[...]
    for hf, (i1, i2, gate1, gate2) in enumerate(routed):
        hot1 = lane == i1
        hot2 = lane == i2
        onehot = jnp.where(hot1 | hot2, 1.0, 0.0)
        before = jnp.dot(lower, onehot.astype(BF16), preferred_element_type=F32) + running
        rank1 = jnp.sum(jnp.where(hot1, before, 0.0), axis=-1, keepdims=True)
        rank2 = jnp.sum(jnp.where(hot2, before, 0.0), axis=-1, keepdims=True)
        running = running + jnp.sum(onehot, axis=0, keepdims=True)
        rec = jnp.zeros((hm, LANES), F32)
        for pos, val in ((ROUTE_EID, (i1 - ROUTER_EXPERT_LANE0).astype(F32)),
                         (ROUTE_EID + 1, (i2 - ROUTER_EXPERT_LANE0).astype(F32)),
                         (ROUTE_RANK, rank1), (ROUTE_RANK + 1, rank2),
                         (ROUTE_GATE, gate1), (ROUTE_GATE + 1, gate2)):
            rec = jnp.where(lane == pos, val, rec)
        route_ref[rows[hf], :] = rec
        route_t_ref[:, hf * hm:(hf + 1) * hm] = rec.T[:route_t_ref.shape[0], :]
    cnt_ref[...] = running


def _merge_route(y_att, y_ret, gates, x2, wa, wb, wo, nw, wr_hi, wr_lo, b_r, T, S):
    tm = TM_MERGE
    nt = S // tm
    row = lambda i: (i, 0)
    const = lambda i: (0, 0)
    full = lambda arr: pl.BlockSpec(arr.shape, const)
    in_specs = ([pl.BlockSpec((1, y_att.shape[1], tm, LANES), lambda i: (i // nt, 0, i % nt, 0)),
                 pl.BlockSpec((tm, RET_V_W), row), pl.BlockSpec((tm, 2 * D_MODEL), row),
                 pl.BlockSpec((tm, D_MODEL), row),
                 full(wa), full(wb), full(wo), full(nw), full(wr_hi), full(wr_lo), full(b_r)])
    return pl.pallas_call(
        _merge_body,
        grid=(T // tm,),
        in_specs=in_specs,
        out_specs=[pl.BlockSpec((tm, D_MODEL), row),
                   pl.BlockSpec((tm * ROW_TILES, LANES), row),
                   pl.BlockSpec((tm, LANES), row), pl.BlockSpec((ROUTE_FIELDS, tm), lambda i: (0, i)),
                   pl.BlockSpec((1, LANES), const)],
        out_shape=[jax.ShapeDtypeStruct((T, D_MODEL), F32), jax.ShapeDtypeStruct((T * ROW_TILES, LANES), U32),
                   jax.ShapeDtypeStruct((T, LANES), F32), jax.ShapeDtypeStruct((ROUTE_FIELDS, T), F32),
                   jax.ShapeDtypeStruct((1, LANES), F32)],
        compiler_params=_cparams(("arbitrary",)),
        name="merge_route",
    )(y_att, y_ret, gates, x2, wa, wb, wo, nw, wr_hi, wr_lo, b_r)


ISSUE_UNROLL = 8


def _tile_rows(n):
    return pl.ds(pl.multiple_of(n * ROW_TILES, ROW_TILES), ROW_TILES)


def _pack_row(x):
    bits = lambda c: pltpu.bitcast(x[:, c * LANES:(c + 1) * LANES].astype(BF16).astype(F32), U32)
    return [(bits(j) >> BF16_BITS) | bits(j + ROW_TILES) for j in range(ROW_TILES)]


def _unpack_word(w):
    high = jnp.uint32(((1 << BF16_BITS) - 1) << BF16_BITS)
    return pltpu.bitcast(w << BF16_BITS, F32), pltpu.bitcast(w & high, F32)


def _dispatch_body(slot_ref, hn_ref, xs_ref, sem, *, T):
    i = pl.program_id(0)
    ch = hn_ref.shape[0] // ROW_TILES

    def row_copy(j, slot):
        return pltpu.make_async_copy(hn_ref.at[_tile_rows(j)], xs_ref.at[_tile_rows(slot)], sem)

    def issue(j, carry):
        t = i * ch + j
        row_copy(j, slot_ref[t]).start(priority=0)
        row_copy(j, slot_ref[T + t]).start(priority=1)
        return carry

    lax.fori_loop(0, ch, issue, 0, unroll=ISSUE_UNROLL)
    for _ in range(2):
        pltpu.make_async_copy(hn_ref, xs_ref.at[pl.ds(0, ch * ROW_TILES)], sem).wait()


def _dispatch(slots, hn, n_slots, T):
    ch = DISPATCH_CHUNK
    grid_spec = pltpu.PrefetchScalarGridSpec(
        num_scalar_prefetch=1,
        grid=(T // ch,),
        in_specs=[pl.BlockSpec((ch * ROW_TILES, LANES), lambda i, s: (i, 0))],
        out_specs=pl.BlockSpec(memory_space=pl.ANY),
        scratch_shapes=[pltpu.SemaphoreType.DMA(())],
    )
    return pl.pallas_call(
        functools.partial(_dispatch_body, T=T),
        grid_spec=grid_spec,
        out_shape=jax.ShapeDtypeStruct((n_slots * ROW_TILES, LANES), U32),
        compiler_params=_cparams(("arbitrary",)),
        name="moe_dispatch",
    )(slots, hn)


def _expert_body(blk_ref, eid_ref, valid_ref, fresh_ref, next_ref, x_ref, w1_ref, w3_ref, w2_ref, y_ref,
                 w1b, w3b, w2b, w1s, w3s, w2s, wsem):
    i = pl.program_id(0)
    valid = valid_ref[i]

    def weight_copies(e):
        return (pltpu.make_async_copy(w1_ref.at[e], w1s, wsem.at[0]),
                pltpu.make_async_copy(w3_ref.at[e], w3s, wsem.at[1]),
                pltpu.make_async_copy(w2_ref.at[e], w2s, wsem.at[2]))

    @pl.when(i == 0)
    def _():
        for cp in weight_copies(eid_ref[0]):
            cp.start()

    @pl.when(valid > 0)
    def _():
        @pl.when(fresh_ref[i] == 1)
        def _():
            for cp in weight_copies(eid_ref[i]):
                cp.wait()
            w1b[...] = w1s[...].astype(BF16)
            w3b[...] = w3s[...].astype(BF16)
            w2b[...] = w2s[...].astype(BF16)

            @pl.when(next_ref[i] >= 0)
            def _():
                for cp in weight_copies(next_ref[i]):
                    cp.start()

        bm = x_ref.shape[0] // ROW_TILES

        def swiglu(nrows):
            live = lax.broadcasted_iota(jnp.int32, (nrows, LANES), 0) < valid
            halves = [_unpack_word(x_ref[pl.ds(j, nrows, stride=ROW_TILES), :]) for j in range(ROW_TILES)]
            x = jnp.concatenate(
                [jnp.where(live, c, 0.0).astype(BF16) for c in [lo for lo, _ in halves] + [hi for _, hi in halves]],
                axis=1)
            a = jnp.dot(x, w1b[...], preferred_element_type=F32)
            b = jnp.dot(x, w3b[...], preferred_element_type=F32)
            hid = (a * jax.nn.sigmoid(a) * b).astype(BF16)
            y = jnp.dot(hid, w2b[...], preferred_element_type=F32)
            for j, word in enumerate(_pack_row(y)):
                y_ref[pl.ds(j, nrows, stride=ROW_TILES), :] = word

        for rows in range(MOE_TAIL, bm + 1, MOE_TAIL):
            @pl.when((valid > rows - MOE_TAIL) & (valid <= rows))
            def _(rows=rows):
                swiglu(rows)


def _experts(blk, blk_eid, blk_valid, blk_fresh, blk_next, x_slots, w1, w3, w2, n_blocks):
    bm = MOE_BM
    slot_block = lambda i, blk, eid, val, fr, nx: (blk[i], 0)
    grid_spec = pltpu.PrefetchScalarGridSpec(
        num_scalar_prefetch=5,
        grid=(n_blocks,),
        in_specs=[pl.BlockSpec((bm * ROW_TILES, LANES), slot_block),
                  pl.BlockSpec(memory_space=pl.ANY), pl.BlockSpec(memory_space=pl.ANY),
                  pl.BlockSpec(memory_space=pl.ANY)],
        out_specs=pl.BlockSpec((bm * ROW_TILES, LANES), slot_block),
        scratch_shapes=[pltpu.VMEM((D_MODEL, MOE_HIDDEN), BF16), pltpu.VMEM((D_MODEL, MOE_HIDDEN), BF16),
                        pltpu.VMEM((MOE_HIDDEN, D_MODEL), BF16),
                        pltpu.VMEM((D_MODEL, MOE_HIDDEN), F32), pltpu.VMEM((D_MODEL, MOE_HIDDEN), F32),
                        pltpu.VMEM((MOE_HIDDEN, D_MODEL), F32), pltpu.SemaphoreType.DMA((3,))],
    )
    return pl.pallas_call(
        _expert_body,
        grid_spec=grid_spec,
        out_shape=jax.ShapeDtypeStruct(x_slots.shape, U32),
        compiler_params=_cparams(("arbitrary",)),
        name="moe_experts",
    )(blk, blk_eid, blk_valid, blk_fresh, blk_next, x_slots, w1, w3, w2)


def _combine_body(slot_ref, ys_ref, h_ref, route_ref, nw_ref, o_ref, ybuf, sem, *, T):
    i = pl.program_id(0)
    n = pl.num_programs(0)
    tm = h_ref.shape[0]

    def row_copy(slot, buf, k, j):
        return pltpu.make_async_copy(ys_ref.at[_tile_rows(slot)], ybuf.at[buf, k, _tile_rows(j)], sem.at[buf])

    def issue(tile, buf):
        def one(j, carry):
            t = tile * tm + j
            row_copy(slot_ref[t], buf, 0, j).start(priority=0)
            row_copy(slot_ref[T + t], buf, 1, j).start(priority=1)
            return carry
        lax.fori_loop(0, tm, one, 0, unroll=ISSUE_UNROLL)

    @pl.when(i == 0)
    def _():
        issue(0, 0)

    @pl.when(i + 1 < n)
    def _():
        issue(i + 1, (i + 1) % 2)

    buf = i % 2
    for k in range(2):
        pltpu.make_async_copy(ys_ref.at[pl.ds(0, tm * ROW_TILES)], ybuf.at[buf, k], sem.at[buf]).wait()
    route = route_ref[...]
    g1 = route[:, ROUTE_GATE:ROUTE_GATE + 1]
    g2 = route[:, ROUTE_GATE + 1:ROUTE_GATE + 2]
    hs = [None] * COL_CHUNKS
    ss = jnp.zeros((tm, 1), F32)
    for j in range(ROW_TILES):
        tile_row = pl.ds(j, tm, stride=ROW_TILES)
        first = _unpack_word(ybuf[buf, 0, tile_row, :])
        second = _unpack_word(ybuf[buf, 1, tile_row, :])
        for c, y1, y2 in ((j, first[0], second[0]), (j + ROW_TILES, first[1], second[1])):
            hc = h_ref[:, c * LANES:(c + 1) * LANES] + (y1 * g1 + y2 * g2)
            hs[c] = hc
            ss = ss + jnp.sum(hc * hc, axis=-1, keepdims=True)
    inv = lax.rsqrt(ss * (1.0 / D_MODEL) + NORM_EPS)
    for j, hj in enumerate(hs):
        cols = slice(j * LANES, (j + 1) * LANES)
        o_ref[:, cols] = hj * inv * nw_ref[:, cols]


def _combine(slots, y_slots, h, route, nw, T):
    tm = TM_COMBINE
    row = lambda i, s: (i, 0)
    grid_spec = pltpu.PrefetchScalarGridSpec(
        num_scalar_prefetch=1,
        grid=(T // tm,),
        in_specs=[pl.BlockSpec(memory_space=pl.ANY),
                  pl.BlockSpec((tm, D_MODEL), row),
                  pl.BlockSpec((tm, LANES), row),
                  pl.BlockSpec((1, D_MODEL), lambda i, s: (0, 0))],
        out_specs=pl.BlockSpec((tm, D_MODEL), row),
        scratch_shapes=[pltpu.VMEM((2, 2, tm * ROW_TILES, LANES), U32), pltpu.SemaphoreType.DMA((2,))],
    )
    return pl.pallas_call(
        functools.partial(_combine_body, T=T),
        grid_spec=grid_spec,
        out_shape=jax.ShapeDtypeStruct((T, D_MODEL), F32),
        compiler_params=_cparams(("arbitrary",)),
        name="moe_combine",
    )(slots, y_slots, h, route, nw)


def _rotary_tables(S):
    inv_freq = (1.0 / (np.float32(ROPE_THETA) ** (np.arange(0, HEAD_DIM, 2, dtype=np.float32) / HEAD_DIM))
                ).astype(np.float32)
    ang = np.arange(S, dtype=np.float32)[:, None] * inv_freq[None, :]
    cos, sin = np.cos(ang), np.sin(ang)
    reps = LANES // HEAD_DIM
    cos_t = np.tile(np.concatenate([cos, cos], axis=1), (1, reps)).astype(np.float32)
    sin_t = np.tile(np.concatenate([-sin, sin], axis=1), (1, reps)).astype(np.float32)
    return jnp.asarray(cos_t), jnp.asarray(sin_t)


def _layer(h_in, norm_mix_w, w_in, b_branch_gate, ret_decay_fwd, ret_decay_bwd, ret_gn_w, w_attn_branch,
           w_ret_branch, w_out, norm_moe_w, moe_w_group, moe_b_group, moe_w_expert, moe_b_expert,
           moe_w1, moe_w3, moe_w2, next_norm_w, B, S, cos_t, sin_t):
    T = B * S
    (qa0, ka0, va0, qa1, ka1, va1, qa2, ka2, va2, qr, kr, vr, gr, gates) = _in_projection(
        h_in, norm_mix_w[None, :], w_in.astype(BF16), b_branch_gate[None, :], cos_t, sin_t, B, S)

    unit = lambda a: a[:, :, None]
    y_att = _attention((unit(qa0), unit(ka0), unit(va0), qa1, ka1, va1, qa2, ka2, va2), B, S)

    dec = jnp.stack([ret_decay_fwd.reshape(RET_HEADS // 2, 2), ret_decay_bwd.reshape(RET_HEADS // 2, 2)], axis=1)
    dec = jnp.broadcast_to(dec.reshape(RET_HEADS // 2, 4, 1), (RET_HEADS // 2, 4, LANES)).astype(F32)
    y_ret = _retention(dec, qr, kr, vr, gr, ret_gn_w[None, :], B, S).reshape(T, RET_V_W)

    pad = LANES - MOE_GROUPS - MOE_N_EXPERTS
    w_r = jnp.concatenate([moe_w_group, moe_w_expert, jnp.zeros((D_MODEL, pad), F32)], axis=1)
    w_r_hi = w_r.astype(BF16)
    w_r_lo = (w_r - w_r_hi.astype(F32)).astype(BF16)
    b_r = jnp.concatenate([moe_b_group, moe_b_expert, jnp.zeros((pad,), F32)])[None, :]

    h_mid, hn, route, route_t, cnt = _merge_route(
        y_att, y_ret, gates, h_in, w_attn_branch.astype(BF16), w_ret_branch.astype(BF16),
        w_out.astype(BF16), norm_moe_w[None, :], w_r_hi, w_r_lo, b_r, T, S)

    bm = MOE_BM
    counts = cnt[0, ROUTER_EXPERT_LANE0:ROUTER_EXPERT_LANE0 + MOE_N_EXPERTS].astype(jnp.int32)
    nblk = (counts + bm - 1) // bm
    blk_end = jnp.cumsum(nblk)
    pstart = (blk_end - nblk) * bm
    n_blocks = (2 * T) // bm + MOE_N_EXPERTS
    n_active = blk_end[-1]
    bidx = jnp.minimum(jnp.arange(n_blocks, dtype=jnp.int32), n_active - 1)
    blk_eid = jnp.sum(bidx[:, None] >= blk_end[None, :], axis=1).astype(jnp.int32)
    mine = blk_eid[:, None] == jnp.arange(MOE_N_EXPERTS, dtype=jnp.int32)[None, :]
    seg_end = jnp.sum(jnp.where(mine, (pstart + counts)[None, :], 0), axis=1)
    blk_valid = jnp.clip(seg_end - bidx * bm, 0, bm)
    blk_valid = jnp.where(jnp.arange(n_blocks) < n_active, blk_valid, 0).astype(jnp.int32)
    blk_fresh = jnp.concatenate([jnp.ones((1,), jnp.int32), (blk_eid[1:] != blk_eid[:-1]).astype(jnp.int32)])
    ar = jnp.arange(MOE_N_EXPERTS, dtype=jnp.int32)
    later = jnp.min(jnp.where((nblk > 0)[None, :] & (ar[None, :] > ar[:, None]), ar[None, :], MOE_N_EXPERTS), axis=1)
    later = jnp.where(later < MOE_N_EXPERTS, later, -1)
    blk_next = (jnp.sum(jnp.where(mine, later[None, :] + 1, 0), axis=1) - 1).astype(jnp.int32)
    eid = route_t[ROUTE_EID:ROUTE_EID + 2].astype(jnp.int32)
    rank = route_t[ROUTE_RANK:ROUTE_RANK + 2].astype(jnp.int32)
    start = jnp.sum(jnp.where(eid[..., None] == jnp.arange(MOE_N_EXPERTS, dtype=jnp.int32),
                              pstart.astype(jnp.int32), 0), axis=-1)
    slots = (start + rank).reshape(2 * T)

    x_slots = _dispatch(slots, hn, n_blocks * bm, T)
    y_slots = _experts(bidx, blk_eid, blk_valid, blk_fresh, blk_next, x_slots, moe_w1, moe_w3, moe_w2, n_blocks)
    return _combine(slots, y_slots, h_mid, route, next_norm_w[None, :], T)


def kernel(x, norm_mix_w, w_in, b_branch_gate, ret_decay_fwd, ret_decay_bwd, ret_gn_w, w_attn_branch,
           w_ret_branch, w_out, norm_moe_w, moe_w_group, moe_b_group, moe_w_expert, moe_b_expert, moe_w1,
           moe_w3, moe_w2, norm_final_w):
    B, S, D = x.shape
    depth = norm_mix_w.shape[0]
    assert depth == 1, "the final norm is fused into the layer's combine stage"
    assert D == D_MODEL and S % TM_INPROJ == 0 and (B * S) < (1 << 24)
    cos_t, sin_t = _rotary_tables(S)
    out = _layer(x.reshape(B * S, D), norm_mix_w[0], w_in[0], b_branch_gate[0], ret_decay_fwd[0],
                 ret_decay_bwd[0], ret_gn_w[0], w_attn_branch[0], w_ret_branch[0], w_out[0], norm_moe_w[0],
                 moe_w_group[0], moe_b_group[0], moe_w_expert[0], moe_b_expert[0], moe_w1[0], moe_w3[0],
                 moe_w2[0], norm_final_w, B, S, cos_t, sin_t)
    return out.reshape(B, S, D)
```

```python
import functools

import numpy as np
import jax
import jax.numpy as jnp
from jax import lax
from jax.experimental import pallas as pl
from jax.experimental.pallas import tpu as pltpu

F32 = jnp.float32
BF16 = jnp.bfloat16

D_MODEL = 1024
HEAD_DIM = 64
ATTN_PAIRS = ((128, 1), (512, 4), (2048, 16))
ATTN_HEADS_PER_GROUP = 8
ATTN_GROUP_W = ATTN_HEADS_PER_GROUP * HEAD_DIM
ATTN_HALF = 64
ROPE_THETA = 10000.0
RET_HEADS = 8
RET_QK_DIM = 64
RET_V_DIM = 128
RET_CHUNK = 128
RET_QK_W = RET_HEADS * RET_QK_DIM
RET_V_W = RET_HEADS * RET_V_DIM
MOE_GROUPS = 8
MOE_EXPERTS_PER_GROUP = 8
MOE_N_EXPERTS = MOE_GROUPS * MOE_EXPERTS_PER_GROUP
MOE_HIDDEN = 512
NORM_EPS = 1e-6

LANES = 128
COL_CHUNKS = D_MODEL // LANES
ROW_TILES = COL_CHUNKS // 2
U32 = jnp.uint32
BF16_BITS = 16
NEG_BIG = -1e30
LOG2_E = 1.4426950408889634

TM_INPROJ = 512
TM_MERGE = 1024
MERGE_SPLIT = 4
MERGE_COL_CHUNKS = 4
TM_COMBINE = 512
MOE_BM = 512
DISPATCH_CHUNK = 4096
ATTN_QB = 128
ATTN_UNROLL = 8
RET_UNROLL = 16

V7X_VMEM_BYTES = 64 * 1024 * 1024
VMEM_LIMIT = V7X_VMEM_BYTES * 7 // 8

_A = 3 * ATTN_GROUP_W
OFF_QA, OFF_KA, OFF_VA = 0, _A, 2 * _A
OFF_QR = 3 * _A
OFF_KR = OFF_QR + RET_QK_W
OFF_VR = OFF_KR + RET_QK_W
OFF_GR = OFF_VR + RET_V_W
OFF_GL = OFF_GR + RET_V_W
IN_W = OFF_GL + 2 * D_MODEL


def _cparams(sem, vmem=VMEM_LIMIT):
    return pltpu.CompilerParams(dimension_semantics=sem, vmem_limit_bytes=vmem)


def _inproj_body(x_ref, nw_ref, w_ref, bg_ref, cos_ref, sin_ref,
                 qa0, ka0, va0, qa1, ka1, va1, qa2, ka2, va2, qr, kr, vr, gr, gt,
                 stage_ref):
    tm = x_ref.shape[0]
    x = x_ref[...]
    ms = jnp.mean(x * x, axis=-1, keepdims=True)
    xn = (x * lax.rsqrt(ms + NORM_EPS) * nw_ref[...]).astype(BF16)
    cos = cos_ref[...]
    sin = sin_ref[...]
    lane = lax.broadcasted_iota(jnp.int32, (tm, LANES), 1)
    first_half = (lane & (HEAD_DIM - 1)) < (HEAD_DIM // 2)

    def proj(c0, width):
        return jnp.dot(xn, w_ref[:, c0:c0 + width], preferred_element_type=F32)

    def rotary(a, scale):
        partner = jnp.where(first_half, pltpu.roll(a, LANES - HEAD_DIM // 2, 1),
                            pltpu.roll(a, HEAD_DIM // 2, 1))
        r = a * cos + partner * sin
        return r * scale if scale != 1.0 else r

    def chunks(acc):
        return [acc[:, c * LANES:(c + 1) * LANES] for c in range(acc.shape[1] // LANES)]

    def store_natural(out_ref, acc, fn):
        for c, a in enumerate(chunks(acc)):
            out_ref[:, c * LANES:(c + 1) * LANES] = fn(a).astype(out_ref.dtype)

    def store_pairs(out_ref, acc, fn):
        for c, a in enumerate(chunks(acc)):
            out_ref[0, c] = fn(a).astype(out_ref.dtype)

    def store_strided(out_ref, acc, fn, d):
        for c, a in enumerate(chunks(acc)):
            stage_ref[c] = fn(a)
        for c in range(acc.shape[1] // LANES):
            for r in range(d):
                out_ref[0, c, r] = stage_ref[c, pl.ds(r, tm // d, stride=d), :].astype(out_ref.dtype)

    ident = lambda a: a
    rot_q = lambda a: rotary(a, HEAD_DIM ** -0.5 * LOG2_E)
    rot_1 = lambda a: rotary(a, 1.0)
    rot_k = lambda a: rotary(a, RET_QK_DIM ** -0.5)

    W = ATTN_GROUP_W
    store_pairs(qa0, proj(OFF_QA, W), rot_q)
    store_pairs(ka0, proj(OFF_KA, W), rot_1)
    store_pairs(va0, proj(OFF_VA, W), ident)
    for g, (qo, ko, vo) in ((1, (qa1, ka1, va1)), (2, (qa2, ka2, va2))):
        d = ATTN_PAIRS[g][1]
        store_strided(qo, proj(OFF_QA + g * W, W), rot_q, d)
        store_strided(ko, proj(OFF_KA + g * W, W), rot_1, d)
        store_strided(vo, proj(OFF_VA + g * W, W), ident, d)
    store_natural(qr, proj(OFF_QR, RET_QK_W), rot_1)
    store_natural(kr, proj(OFF_KR, RET_QK_W), rot_k)
    for h in range(RET_V_W // W):
        vr[:, h * W:(h + 1) * W] = proj(OFF_VR + h * W, W).astype(vr.dtype)
        gr[:, h * W:(h + 1) * W] = proj(OFF_GR + h * W, W).astype(gr.dtype)
    for h in range(2 * D_MODEL // W):
        z = proj(OFF_GL + h * W, W) + bg_ref[:, h * W:(h + 1) * W]
        gt[:, h * W:(h + 1) * W] = jax.nn.sigmoid(z).astype(gt.dtype)


def _in_projection(x2, norm_w, w_bf, b_gate, cos_t, sin_t, B, S):
    T = B * S
    tm = TM_INPROJ
    nt = S // tm
    W = ATTN_GROUP_W
    row = lambda i: (i, 0)
    const = lambda i: (0, 0)
    nat = lambda width: pl.BlockSpec((tm, width), row)

    P = W // LANES

    def strided_spec(d):
        return pl.BlockSpec((1, P, d, tm // d, LANES), lambda i: (i // nt, 0, 0, i % nt, 0))

    def strided_shape(d):
        return jax.ShapeDtypeStruct((B, P, d, S // d, LANES), BF16)

    pair_spec = pl.BlockSpec((1, P, tm, LANES), lambda i: (i // nt, 0, i % nt, 0))
    pair_shape = jax.ShapeDtypeStruct((B, P, S, LANES), BF16)
    nat_shape = lambda width: jax.ShapeDtypeStruct((T, width), BF16)
    d1, d2 = ATTN_PAIRS[1][1], ATTN_PAIRS[2][1]
    out_shape = ([pair_shape] * 3 + [strided_shape(d1)] * 3 + [strided_shape(d2)] * 3
                 + [nat_shape(RET_QK_W)] * 2 + [nat_shape(RET_V_W)] * 2 + [nat_shape(2 * D_MODEL)])
    out_specs = ([pair_spec] * 3 + [strided_spec(d1)] * 3 + [strided_spec(d2)] * 3
                 + [nat(RET_QK_W)] * 2 + [nat(RET_V_W)] * 2 + [nat(2 * D_MODEL)])
    in_specs = [
        pl.BlockSpec((tm, D_MODEL), row),
        pl.BlockSpec((1, D_MODEL), const),
        pl.BlockSpec((D_MODEL, IN_W), const, pipeline_mode=pl.Buffered(1)),
        pl.BlockSpec((1, 2 * D_MODEL), const),
        pl.BlockSpec((tm, LANES), lambda i: (i % nt, 0)),
        pl.BlockSpec((tm, LANES), lambda i: (i % nt, 0)),
    ]
    return pl.pallas_call(
        _inproj_body,
        grid=(T // tm,),
        in_specs=in_specs,
        out_specs=out_specs,
        out_shape=out_shape,
        scratch_shapes=[pltpu.VMEM((W // LANES, tm, LANES), F32)],
        compiler_params=_cparams(("parallel",)),
        name="in_projection",
    )(x2, norm_w, w_bf, b_gate, cos_t, sin_t)


def _attn_body(q0_ref, k0_ref, v0_ref, q1_ref, k1_ref, v1_ref, q2_ref, k2_ref, v2_ref, o_ref,
               part_ref, bias_ref, *, S):
    QB, H = ATTN_QB, ATTN_HALF
    lane = lax.broadcasted_iota(jnp.int32, (QB, LANES), 1)
    head0 = lane < HEAD_DIM

    qi = lax.broadcasted_iota(jnp.int32, (QB, QB + 2 * H), 0)
    ki = lax.broadcasted_iota(jnp.int32, (QB, QB + 2 * H), 1)
    for n in range(3):
        bias_ref[n] = jnp.where(jnp.abs(ki - qi - n * H) <= H, 0.0, NEG_BIG).astype(F32)

    def scores(q_rows, k_rows, bias, h):
        qm = jnp.where(head0 if h == 0 else jnp.logical_not(head0), q_rows, jnp.zeros_like(q_rows))
        return lax.dot_general(qm, k_rows, (((1,), (1,)), ((), ())), preferred_element_type=F32) + bias

    def weights(s):
        m = jnp.max(s, axis=-1, keepdims=True)
        return m, jnp.exp2(s - m).astype(BF16)

    def heads_to_lanes(m0, a, m1, b):
        num = jnp.where(head0, a, b)
        den = pltpu.roll(jnp.where(head0, b, a), HEAD_DIM, 1)
        mx = jnp.where(head0, m0, m1)
        return num, den, mx

    def run_group(q_ref, k_ref, v_ref, d, prepare, store):
        L = S // d
        KW = min(L, QB + 2 * H)
        nb = L // QB

        key_head0 = lax.broadcasted_iota(jnp.int32, (KW, LANES), 1) < HEAD_DIM
        key_ones = jnp.ones((KW, LANES), BF16)

        def trip(i, carry):
            blocks = []
            for u in range(ATTN_UNROLL):
                t = i * ATTN_UNROLL + u
                r = t // nb
                q0 = pl.multiple_of((t % nb) * QB, QB)
                ws = pl.multiple_of(jnp.clip(q0 - H, 0, L - KW), H)
                bias = bias_ref[(q0 - ws) // H][:, :KW]
                q_rows = q_ref[0, 0, r, pl.ds(q0, QB), :]
                k_rows = k_ref[0, 0, r, pl.ds(ws, KW), :]
                blocks.append((r, q0, ws, [scores(q_rows, k_rows, bias, h) for h in range(2)]))
            blocks = [(r, q0, ws, [weights(s) for s in ss]) for r, q0, ws, ss in blocks]
            done = []
            for r, q0, ws, ((m0, p0), (m1, p1)) in blocks:
                v_rows = v_ref[0, 0, r, pl.ds(ws, KW), :]
                a = jnp.dot(p0, jnp.where(key_head0, v_rows, key_ones), preferred_element_type=F32)
                b = jnp.dot(p1, jnp.where(key_head0, key_ones, v_rows), preferred_element_type=F32)
                done.append((r, q0, m0, a, m1, b))
            done = [(r, q0, prepare(q0, *heads_to_lanes(m0, a, m1, b))) for r, q0, m0, a, m1, b in done]
            for r, q0, vals in done:
                store(r, q0, vals)
            return carry

        lax.fori_loop(0, S // QB // ATTN_UNROLL, trip, 0)

    def store_partial(g):
        d = ATTN_PAIRS[g][1]

        def store(r, q0, vals):
            rows = pl.ds(r + q0 * d, QB, stride=d)
            for n, val in enumerate(vals):
                part_ref[3 * (g - 1) + n, rows, :] = val
        return store

    keep = lambda q0, num, den, mx: (num, den, mx)
    run_group(q1_ref, k1_ref, v1_ref, ATTN_PAIRS[1][1], keep, store_partial(1))
    run_group(q2_ref, k2_ref, v2_ref, ATTN_PAIRS[2][1], keep, store_partial(2))

    def merge(q0, num, den, mx):
        rows = pl.ds(q0, QB)
        nums = [num, part_ref[0, rows, :], part_ref[3, rows, :]]
        dens = [den, part_ref[1, rows, :], part_ref[4, rows, :]]
        mxs = [mx, part_ref[2, rows, :], part_ref[5, rows, :]]
        top = jnp.maximum(jnp.maximum(mxs[0], mxs[1]), mxs[2])
        ws = [jnp.exp2(m - top) for m in mxs]
        n = ws[0] * nums[0] + ws[1] * nums[1] + ws[2] * nums[2]
        dn = ws[0] * dens[0] + ws[1] * dens[1] + ws[2] * dens[2]
        return (n / dn).astype(o_ref.dtype)

    def store_out(r, q0, y):
        o_ref[0, 0, pl.ds(q0, QB), :] = y

    run_group(q0_ref, k0_ref, v0_ref, ATTN_PAIRS[0][1], merge, store_out)


def _attention(qkv, B, S):
    P = ATTN_GROUP_W // LANES
    in_specs = []
    for _, d in ATTN_PAIRS:
        in_specs += [pl.BlockSpec((1, 1, d, S // d, LANES), lambda b, p: (b, p, 0, 0, 0))] * 3
    return pl.pallas_call(
        functools.partial(_attn_body, S=S),
        grid=(B, P),
        in_specs=in_specs,
        out_specs=pl.BlockSpec((1, 1, S, LANES), lambda b, p: (b, p, 0, 0)),
        out_shape=jax.ShapeDtypeStruct((B, P, S, LANES), BF16),
        scratch_shapes=[pltpu.VMEM((6, S, LANES), F32),
                        pltpu.VMEM((3, ATTN_QB, ATTN_QB + 2 * ATTN_HALF), F32)],
        compiler_params=_cparams(("parallel", "parallel")),
        name="attention",
    )(*qkv)


def _log_sigmoid(z):
    return jnp.minimum(z, 0.0) - jnp.log(1.0 + jnp.exp(-jnp.abs(z)))


def _ret_body(dec_ref, q_ref, k_ref, v_ref, g_ref, gnw_ref, o_ref, kt_ref, sf_ref, sb_ref, st_ref, *, S):
    C = RET_CHUNK
    nc = S // C
    lg = _log_sigmoid(dec_ref[0])
    a_row = lax.broadcasted_iota(jnp.int32, (C, LANES), 0).astype(F32)
    lane = lax.broadcasted_iota(jnp.int32, (C, LANES), 1)
    rel = (lax.broadcasted_iota(jnp.int32, (C, C), 0) - lax.broadcasted_iota(jnp.int32, (C, C), 1)).astype(F32)

    heads = []
    for h in range(2):
        lgf = lg[h:h + 1, :]
        lgb = lg[2 + h:3 + h, :]
        in_head = (lane < RET_QK_DIM) if h == 0 else (lane >= RET_QK_DIM)
        heads.append(dict(
            in_head=in_head,
            xi_f=jnp.where(in_head, jnp.exp((a_row + 1.0) * lgf), 0.0),
            xi_b=jnp.where(in_head, jnp.exp((C - a_row) * lgb), 0.0),
            zeta_f=jnp.exp((C - 1.0 - a_row) * lgf),
            zeta_b=jnp.exp(a_row * lgb),
            dloc=jnp.where(rel > 0, jnp.exp(rel * lgf), jnp.where(rel < 0, jnp.exp(-rel * lgb), 2.0)),
            cd_f=jnp.exp(C * lgf),
            cd_b=jnp.exp(C * lgb),
        ))

    def chunk_rows(n):
        return pl.ds(pl.multiple_of(n * C, C), C)

    def v_head(rows, h):
        return v_ref[0, rows, h * RET_V_DIM:(h + 1) * RET_V_DIM]

    def products(i, carry):
        items = []
        for u in range(RET_UNROLL):
            n = i * RET_UNROLL + u
            rows = chunk_rows(n)
            kt = k_ref[0, rows, :].astype(F32).T.astype(BF16)
            kt_ref[n] = kt
            for h, hd in enumerate(heads):
                vh = v_head(rows, h).astype(F32)
                items.append((n, h, kt, (vh * hd["zeta_f"]).astype(BF16), (vh * hd["zeta_b"]).astype(BF16)))
        outs = [(n, h, jnp.dot(kt, vf, preferred_element_type=F32), jnp.dot(kt, vb, preferred_element_type=F32))
                for n, h, kt, vf, vb in items]
        for n, h, f, b in outs:
            sf_ref[n, h] = f
            sb_ref[n, h] = b
        return carry

    lax.fori_loop(0, nc // RET_UNROLL, products, 0)

    def scan(ref, key, order, half):
        rows = slice(half * 2 * RET_QK_DIM, (half + 1) * 2 * RET_QK_DIM)

        def step(i, state):
            n = order(i)
            new = []
            for h, hd in enumerate(heads):
                st_ref[n, h, rows, :] = state[h].astype(BF16)
                new.append(hd[key] * state[h] + ref[n, h])
            return tuple(new)
        zero = jnp.zeros(ref.shape[2:], F32)
        lax.fori_loop(0, nc, step, (zero, zero))

    scan(sf_ref, "cd_f", lambda i: i, 0)
    scan(sb_ref, "cd_b", lambda i: nc - 1 - i, 1)

    def outputs(i, carry):
        items = []
        for u in range(RET_UNROLL):
            n = i * RET_UNROLL + u
            rows = chunk_rows(n)
            qp = q_ref[0, rows, :]
            qf = qp.astype(F32)
            kt = kt_ref[n]
            for h, hd in enumerate(heads):
                qm = jnp.where(hd["in_head"], qp, jnp.zeros_like(qp))
                qx = jnp.concatenate([(qf * hd["xi_f"]).astype(BF16), (qf * hd["xi_b"]).astype(BF16)], axis=1)
                items.append((rows, h, hd, jnp.dot(qm, kt, preferred_element_type=F32),
                              jnp.dot(qx, st_ref[n, h], preferred_element_type=F32)))
        items = [(rows, h, hd, (s * hd["dloc"]).astype(BF16), cross) for rows, h, hd, s, cross in items]
        items = [(rows, h, cross + jnp.dot(p, v_head(rows, h), preferred_element_type=F32))
                 for rows, h, hd, p, cross in items]
        for rows, h, ret in items:
            mu = jnp.mean(ret, axis=-1, keepdims=True)
            xc = ret - mu
            var = jnp.mean(xc * xc, axis=-1, keepdims=True)
            cols = slice(h * RET_V_DIM, (h + 1) * RET_V_DIM)
            gate = g_ref[0, rows, cols].astype(F32)
            y = xc * lax.rsqrt(var + NORM_EPS) * gnw_ref[:, cols] * (gate * jax.nn.sigmoid(gate))
            o_ref[0, rows, cols] = y.astype(o_ref.dtype)
        return carry

    lax.fori_loop(0, nc // RET_UNROLL, outputs, 0)


def _retention(dec, qr, kr, vr, gr, gn_w, B, S):
    nc = S // RET_CHUNK
    npairs = RET_HEADS // 2
    qk_spec = pl.BlockSpec((1, S, 2 * RET_QK_DIM), lambda b, p: (b, 0, p))
    v_spec = pl.BlockSpec((1, S, 2 * RET_V_DIM), lambda b, p: (b, 0, p))
    return pl.pallas_call(
        functools.partial(_ret_body, S=S),
        grid=(B, npairs),
        in_specs=[pl.BlockSpec((1, 4, LANES), lambda b, p: (p, 0, 0)),
                  qk_spec, qk_spec, v_spec, v_spec,
                  pl.BlockSpec((1, 2 * RET_V_DIM), lambda b, p: (0, p))],
        out_specs=v_spec,
        out_shape=jax.ShapeDtypeStruct((B, S, RET_V_W), BF16),
        scratch_shapes=[pltpu.VMEM((nc, 2 * RET_QK_DIM, RET_CHUNK), BF16),
                        pltpu.VMEM((nc, 2, 2 * RET_QK_DIM, RET_V_DIM), F32),
                        pltpu.VMEM((nc, 2, 2 * RET_QK_DIM, RET_V_DIM), F32),
                        pltpu.VMEM((nc, 2, 4 * RET_QK_DIM, RET_V_DIM), BF16)],
        compiler_params=_cparams(("parallel", "parallel")),
        name="retention",
    )(dec, qr.reshape(B, S, RET_QK_W), kr.reshape(B, S, RET_QK_W),
      vr.reshape(B, S, RET_V_W), gr.reshape(B, S, RET_V_W), gn_w)


ROUTE_EID, ROUTE_RANK, ROUTE_GATE = 0, 2, 4
ROUTE_FIELDS = 8
ROUTER_EXPERT_LANE0 = MOE_GROUPS
GROUP_SHIFT = MOE_EXPERTS_PER_GROUP.bit_length() - 1


def _merge_body(yatt_ref, yret_ref, gt_ref, x_ref, wa_ref, wb_ref, wo_ref,
                nw_ref, wrh_ref, wrl_ref, br_ref,
                h_ref, hn_ref, route_ref, route_t_ref, cnt_ref):
    tm = x_ref.shape[0]
    hm = tm // MERGE_SPLIT
    i = pl.program_id(0)
    cw = D_MODEL // MERGE_COL_CHUNKS

    def branch_products(rows):
        y_att = jnp.concatenate([yatt_ref[0, c, rows, :] for c in range(yatt_ref.shape[1])], axis=1)
        y_ret = yret_ref[rows, :]
        chunks = []
        for c in range(MERGE_COL_CHUNKS):
            cols = slice(c * cw, (c + 1) * cw)
            a = jnp.dot(y_att, wa_ref[:, cols], preferred_element_type=F32)
            b = jnp.dot(y_ret, wb_ref[:, cols], preferred_element_type=F32)
            g_att = gt_ref[rows, c * cw:(c + 1) * cw].astype(F32)
            g_ret = gt_ref[rows, D_MODEL + c * cw:D_MODEL + (c + 1) * cw].astype(F32)
            chunks.append((g_att * a + g_ret * b).astype(BF16))
        return jnp.concatenate(chunks, axis=1)

    def residual_norm(hf, rows, merged):
        mix = jnp.dot(merged, wo_ref[...], preferred_element_type=F32)
        h = x_ref[rows, :] + mix
        h_ref[rows, :] = h
        ms = jnp.mean(h * h, axis=-1, keepdims=True)
        hn = h * lax.rsqrt(ms + NORM_EPS) * nw_ref[...]
        for j, word in enumerate(_pack_row(hn)):
            hn_ref[pl.ds(hf * hm * ROW_TILES + j, hm, stride=ROW_TILES), :] = word
        hi = hn.astype(BF16)
        lo = (hn - hi.astype(F32)).astype(BF16)
        return hi, lo

    def router_logits(hi, lo):
        return (jnp.dot(hi, wrh_ref[...], preferred_element_type=F32)
                + jnp.dot(hi, wrl_ref[...], preferred_element_type=F32)
                + jnp.dot(lo, wrh_ref[...], preferred_element_type=F32)) + br_ref[...]

    lane = lax.broadcasted_iota(jnp.int32, (hm, LANES), 1)
    far = jnp.int32(LANES)

    def first_argmax(vals, vmax):
        return jnp.min(jnp.where(vals == vmax, lane, far), axis=-1, keepdims=True)

    def route(logits):
        is_group = lane < MOE_GROUPS
        gl = jnp.where(is_group, logits, NEG_BIG)
        gmax = jnp.max(gl, axis=-1, keepdims=True)
        g_w = 1.0 / jnp.sum(jnp.where(is_group, jnp.exp(gl - gmax), 0.0), axis=-1, keepdims=True)
        g_idx = first_argmax(gl, gmax)
        e_lane = lane - ROUTER_EXPERT_LANE0
        in_group = (e_lane >= 0) & (e_lane < MOE_N_EXPERTS) & (jnp.right_shift(e_lane, GROUP_SHIFT) == g_idx)
        el = jnp.where(in_group, logits, NEG_BIG)
        m1 = jnp.max(el, axis=-1, keepdims=True)
        i1 = first_argmax(el, m1)
        el2 = jnp.where(lane == i1, NEG_BIG, el)
        m2 = jnp.max(el2, axis=-1, keepdims=True)
        i2 = first_argmax(el2, m2)
        ex = jnp.exp(m2 - m1)
        return i1, i2, g_w / (1.0 + ex), g_w * ex / (1.0 + ex)

    rows = [slice(hf * hm, (hf + 1) * hm) for hf in range(MERGE_SPLIT)]
    merged = [branch_products(r) for r in rows]
    split = [residual_norm(hf, r, m) for hf, (r, m) in enumerate(zip(rows, merged))]
    routed = [route(router_logits(hi, lo)) for hi, lo in split]

    @pl.when(i == 0)
    def _():
        cnt_ref[...] = jnp.zeros(cnt_ref.shape, F32)

    r_idx = lax.broadcasted_iota(jnp.int32, (hm, hm), 0)
    c_idx = lax.broadcasted_iota(jnp.int32, (hm, hm), 1)
    lower = jnp.where(c_idx < r_idx, 1.0, 0.0).astype(BF16)
    running = cnt_ref[...]
    for hf, (i1, i2, gate1, gate2) in enumerate(routed):
        hot1 = lane == i1
        hot2 = lane == i2
        onehot = jnp.where(hot1 | hot2, 1.0, 0.0)
        before = jnp.dot(lower, onehot.astype(BF16), preferred_element_type=F32) + running
        rank1 = jnp.sum(jnp.where(hot1, before, 0.0), axis=-1, keepdims=True)
        rank2 = jnp.sum(jnp.where(hot2, before, 0.0), axis=-1, keepdims=True)
        running = running + jnp.sum(onehot, axis=0, keepdims=True)
        rec = jnp.zeros((hm, LANES), F32)
        for pos, val in ((ROUTE_EID, (i1 - ROUTER_EXPERT_LANE0).astype(F32)),
                         (ROUTE_EID + 1, (i2 - ROUTER_EXPERT_LANE0).astype(F32)),
                         (ROUTE_RANK, rank1), (ROUTE_RANK + 1, rank2),
                         (ROUTE_GATE, gate1), (ROUTE_GATE + 1, gate2)):
            rec = jnp.where(lane == pos, val, rec)
        route_ref[rows[hf], :] = rec
        route_t_ref[:, hf * hm:(hf + 1) * hm] = rec.T[:route_t_ref.shape[0], :]
    cnt_ref[...] = running


def _merge_route(y_att, y_ret, gates, x2, wa, wb, wo, nw, wr_hi, wr_lo, b_r, T, S):
    tm = TM_MERGE
    nt = S // tm
    row = lambda i: (i, 0)
    const = lambda i: (0, 0)
    full = lambda arr: pl.BlockSpec(arr.shape, const)
    in_specs = ([pl.BlockSpec((1, y_att.shape[1], tm, LANES), lambda i: (i // nt, 0, i % nt, 0)),
                 pl.BlockSpec((tm, RET_V_W), row), pl.BlockSpec((tm, 2 * D_MODEL), row),
                 pl.BlockSpec((tm, D_MODEL), row),
                 full(wa), full(wb), full(wo), full(nw), full(wr_hi), full(wr_lo), full(b_r)])
    return pl.pallas_call(
        _merge_body,
        grid=(T // tm,),
        in_specs=in_specs,
        out_specs=[pl.BlockSpec((tm, D_MODEL), row),
                   pl.BlockSpec((tm * ROW_TILES, LANES), row),
                   pl.BlockSpec((tm, LANES), row), pl.BlockSpec((ROUTE_FIELDS, tm), lambda i: (0, i)),
                   pl.BlockSpec((1, LANES), const)],
        out_shape=[jax.ShapeDtypeStruct((T, D_MODEL), F32), jax.ShapeDtypeStruct((T * ROW_TILES, LANES), U32),
                   jax.ShapeDtypeStruct((T, LANES), F32), jax.ShapeDtypeStruct((ROUTE_FIELDS, T), F32),
                   jax.ShapeDtypeStruct((1, LANES), F32)],
        compiler_params=_cparams(("arbitrary",)),
        name="merge_route",
    )(y_att, y_ret, gates, x2, wa, wb, wo, nw, wr_hi, wr_lo, b_r)


ISSUE_UNROLL = 8


def _tile_rows(n):
    return pl.ds(pl.multiple_of(n * ROW_TILES, ROW_TILES), ROW_TILES)


def _pack_row(x):
    bits = lambda c: pltpu.bitcast(x[:, c * LANES:(c + 1) * LANES].astype(BF16).astype(F32), U32)
    return [(bits(j) >> BF16_BITS) | bits(j + ROW_TILES) for j in range(ROW_TILES)]


def _unpack_word(w):
    high = jnp.uint32(((1 << BF16_BITS) - 1) << BF16_BITS)
    return pltpu.bitcast(w << BF16_BITS, F32), pltpu.bitcast(w & high, F32)


def _dispatch_body(slot_ref, hn_ref, xs_ref, sem, *, T):
    i = pl.program_id(0)
    ch = hn_ref.shape[0] // ROW_TILES

    def row_copy(j, slot):
        return pltpu.make_async_copy(hn_ref.at[_tile_rows(j)], xs_ref.at[_tile_rows(slot)], sem)

    def issue(j, carry):
        t = i * ch + j
        row_copy(j, slot_ref[t]).start(priority=0)
        row_copy(j, slot_ref[T + t]).start(priority=1)
        return carry

    lax.fori_loop(0, ch, issue, 0, unroll=ISSUE_UNROLL)
    for _ in range(2):
        pltpu.make_async_copy(hn_ref, xs_ref.at[pl.ds(0, ch * ROW_TILES)], sem).wait()


def _dispatch(slots, hn, n_slots, T):
    ch = DISPATCH_CHUNK
    grid_spec = pltpu.PrefetchScalarGridSpec(
        num_scalar_prefetch=1,
        grid=(T // ch,),
        in_specs=[pl.BlockSpec((ch * ROW_TILES, LANES), lambda i, s: (i, 0))],
        out_specs=pl.BlockSpec(memory_space=pl.ANY),
        scratch_shapes=[pltpu.SemaphoreType.DMA(())],
    )
    return pl.pallas_call(
        functools.partial(_dispatch_body, T=T),
        grid_spec=grid_spec,
        out_shape=jax.ShapeDtypeStruct((n_slots * ROW_TILES, LANES), U32),
        compiler_params=_cparams(("arbitrary",)),
        name="moe_dispatch",
    )(slots, hn)


def _expert_body(blk_ref, eid_ref, valid_ref, fresh_ref, next_ref, x_ref, w1_ref, w3_ref, w2_ref, y_ref,
                 w1b, w3b, w2b, w1s, w3s, w2s, wsem):
    i = pl.program_id(0)
    valid = valid_ref[i]

    def weight_copies(e):
        return (pltpu.make_async_copy(w1_ref.at[e], w1s, wsem.at[0]),
                pltpu.make_async_copy(w3_ref.at[e], w3s, wsem.at[1]),
                pltpu.make_async_copy(w2_ref.at[e], w2s, wsem.at[2]))

    @pl.when(i == 0)
    def _():
        for cp in weight_copies(eid_ref[0]):
            cp.start()

    @pl.when(valid > 0)
    def _():
        @pl.when(fresh_ref[i] == 1)
        def _():
            for cp in weight_copies(eid_ref[i]):
                cp.wait()
            w1b[...] = w1s[...].astype(BF16)
            w3b[...] = w3s[...].astype(BF16)
            w2b[...] = w2s[...].astype(BF16)

            @pl.when(next_ref[i] >= 0)
            def _():
                for cp in weight_copies(next_ref[i]):
                    cp.start()

        bm = x_ref.shape[0] // ROW_TILES

        def swiglu(nrows):
            live = lax.broadcasted_iota(jnp.int32, (nrows, LANES), 0) < valid
            halves = [_unpack_word(x_ref[pl.ds(j, nrows, stride=ROW_TILES), :]) for j in range(ROW_TILES)]
            x = jnp.concatenate(
                [jnp.where(live, c, 0.0).astype(BF16) for c in [lo for lo, _ in halves] + [hi for _, hi in halves]],
                axis=1)
            a = jnp.dot(x, w1b[...], preferred_element_type=F32)
            b = jnp.dot(x, w3b[...], preferred_element_type=F32)
            hid = (a * jax.nn.sigmoid(a) * b).astype(BF16)
            y = jnp.dot(hid, w2b[...], preferred_element_type=F32)
            for j, word in enumerate(_pack_row(y)):
                y_ref[pl.ds(j, nrows, stride=ROW_TILES), :] = word

        @pl.when(valid > bm // 2)
        def _():
            swiglu(bm)

        @pl.when(valid <= bm // 2)
        def _():
            swiglu(bm // 2)


def _experts(blk, blk_eid, blk_valid, blk_fresh, blk_next, x_slots, w1, w3, w2, n_blocks):
    bm = MOE_BM
    slot_block = lambda i, blk, eid, val, fr, nx: (blk[i], 0)
    grid_spec = pltpu.PrefetchScalarGridSpec(
        num_scalar_prefetch=5,
        grid=(n_blocks,),
        in_specs=[pl.BlockSpec((bm * ROW_TILES, LANES), slot_block),
                  pl.BlockSpec(memory_space=pl.ANY), pl.BlockSpec(memory_space=pl.ANY),
                  pl.BlockSpec(memory_space=pl.ANY)],
        out_specs=pl.BlockSpec((bm * ROW_TILES, LANES), slot_block),
        scratch_shapes=[pltpu.VMEM((D_MODEL, MOE_HIDDEN), BF16), pltpu.VMEM((D_MODEL, MOE_HIDDEN), BF16),
                        pltpu.VMEM((MOE_HIDDEN, D_MODEL), BF16),
                        pltpu.VMEM((D_MODEL, MOE_HIDDEN), F32), pltpu.VMEM((D_MODEL, MOE_HIDDEN), F32),
                        pltpu.VMEM((MOE_HIDDEN, D_MODEL), F32), pltpu.SemaphoreType.DMA((3,))],
    )
    return pl.pallas_call(
        _expert_body,
        grid_spec=grid_spec,
        out_shape=jax.ShapeDtypeStruct(x_slots.shape, U32),
        compiler_params=_cparams(("arbitrary",)),
        name="moe_experts",
    )(blk, blk_eid, blk_valid, blk_fresh, blk_next, x_slots, w1, w3, w2)


def _combine_body(slot_ref, ys_ref, h_ref, route_ref, nw_ref, o_ref, ybuf, sem, *, T):
    i = pl.program_id(0)
    n = pl.num_programs(0)
    tm = h_ref.shape[0]

    def row_copy(slot, buf, k, j):
        return pltpu.make_async_copy(ys_ref.at[_tile_rows(slot)], ybuf.at[buf, k, _tile_rows(j)], sem.at[buf])

    def issue(tile, buf):
        def one(j, carry):
            t = tile * tm + j
            row_copy(slot_ref[t], buf, 0, j).start(priority=0)
            row_copy(slot_ref[T + t], buf, 1, j).start(priority=1)
            return carry
        lax.fori_loop(0, tm, one, 0, unroll=ISSUE_UNROLL)

    @pl.when(i == 0)
    def _():
        issue(0, 0)

    @pl.when(i + 1 < n)
    def _():
        issue(i + 1, (i + 1) % 2)

    buf = i % 2
    for k in range(2):
        pltpu.make_async_copy(ys_ref.at[pl.ds(0, tm * ROW_TILES)], ybuf.at[buf, k], sem.at[buf]).wait()
    route = route_ref[...]
    g1 = route[:, ROUTE_GATE:ROUTE_GATE + 1]
    g2 = route[:, ROUTE_GATE + 1:ROUTE_GATE + 2]
    hs = [None] * COL_CHUNKS
    ss = jnp.zeros((tm, 1), F32)
    for j in range(ROW_TILES):
        tile_row = pl.ds(j, tm, stride=ROW_TILES)
        first = _unpack_word(ybuf[buf, 0, tile_row, :])
        second = _unpack_word(ybuf[buf, 1, tile_row, :])
        for c, y1, y2 in ((j, first[0], second[0]), (j + ROW_TILES, first[1], second[1])):
            hc = h_ref[:, c * LANES:(c + 1) * LANES] + (y1 * g1 + y2 * g2)
            hs[c] = hc
            ss = ss + jnp.sum(hc * hc, axis=-1, keepdims=True)
    inv = lax.rsqrt(ss * (1.0 / D_MODEL) + NORM_EPS)
    for j, hj in enumerate(hs):
        cols = slice(j * LANES, (j + 1) * LANES)
        o_ref[:, cols] = hj * inv * nw_ref[:, cols]


def _combine(slots, y_slots, h, route, nw, T):
    tm = TM_COMBINE
    row = lambda i, s: (i, 0)
    grid_spec = pltpu.PrefetchScalarGridSpec(
        num_scalar_prefetch=1,
        grid=(T // tm,),
        in_specs=[pl.BlockSpec(memory_space=pl.ANY),
                  pl.BlockSpec((tm, D_MODEL), row),
                  pl.BlockSpec((tm, LANES), row),
                  pl.BlockSpec((1, D_MODEL), lambda i, s: (0, 0))],
        out_specs=pl.BlockSpec((tm, D_MODEL), row),
        scratch_shapes=[pltpu.VMEM((2, 2, tm * ROW_TILES, LANES), U32), pltpu.SemaphoreType.DMA((2,))],
    )
    return pl.pallas_call(
        functools.partial(_combine_body, T=T),
        grid_spec=grid_spec,
        out_shape=jax.ShapeDtypeStruct((T, D_MODEL), F32),
        compiler_params=_cparams(("arbitrary",)),
        name="moe_combine",
    )(slots, y_slots, h, route, nw)


def _rotary_tables(S):
    inv_freq = (1.0 / (np.float32(ROPE_THETA) ** (np.arange(0, HEAD_DIM, 2, dtype=np.float32) / HEAD_DIM))
                ).astype(np.float32)
    ang = np.arange(S, dtype=np.float32)[:, None] * inv_freq[None, :]
    cos, sin = np.cos(ang), np.sin(ang)
    reps = LANES // HEAD_DIM
    cos_t = np.tile(np.concatenate([cos, cos], axis=1), (1, reps)).astype(np.float32)
    sin_t = np.tile(np.concatenate([-sin, sin], axis=1), (1, reps)).astype(np.float32)
    return jnp.asarray(cos_t), jnp.asarray(sin_t)


def _layer(h_in, norm_mix_w, w_in, b_branch_gate, ret_decay_fwd, ret_decay_bwd, ret_gn_w, w_attn_branch,
           w_ret_branch, w_out, norm_moe_w, moe_w_group, moe_b_group, moe_w_expert, moe_b_expert,
           moe_w1, moe_w3, moe_w2, next_norm_w, B, S, cos_t, sin_t):
    T = B * S
    (qa0, ka0, va0, qa1, ka1, va1, qa2, ka2, va2, qr, kr, vr, gr, gates) = _in_projection(
        h_in, norm_mix_w[None, :], w_in.astype(BF16), b_branch_gate[None, :], cos_t, sin_t, B, S)

    unit = lambda a: a[:, :, None]
    y_att = _attention((unit(qa0), unit(ka0), unit(va0), qa1, ka1, va1, qa2, ka2, va2), B, S)

    dec = jnp.stack([ret_decay_fwd.reshape(RET_HEADS // 2, 2), ret_decay_bwd.reshape(RET_HEADS // 2, 2)], axis=1)
    dec = jnp.broadcast_to(dec.reshape(RET_HEADS // 2, 4, 1), (RET_HEADS // 2, 4, LANES)).astype(F32)
    y_ret = _retention(dec, qr, kr, vr, gr, ret_gn_w[None, :], B, S).reshape(T, RET_V_W)

    pad = LANES - MOE_GROUPS - MOE_N_EXPERTS
    w_r = jnp.concatenate([moe_w_group, moe_w_expert, jnp.zeros((D_MODEL, pad), F32)], axis=1)
    w_r_hi = w_r.astype(BF16)
    w_r_lo = (w_r - w_r_hi.astype(F32)).astype(BF16)
    b_r = jnp.concatenate([moe_b_group, moe_b_expert, jnp.zeros((pad,), F32)])[None, :]

    h_mid, hn, route, route_t, cnt = _merge_route(
        y_att, y_ret, gates, h_in, w_attn_branch.astype(BF16), w_ret_branch.astype(BF16),
        w_out.astype(BF16), norm_moe_w[None, :], w_r_hi, w_r_lo, b_r, T, S)

    bm = MOE_BM
    counts = cnt[0, ROUTER_EXPERT_LANE0:ROUTER_EXPERT_LANE0 + MOE_N_EXPERTS].astype(jnp.int32)
    nblk = (counts + bm - 1) // bm
    blk_end = jnp.cumsum(nblk)
    pstart = (blk_end - nblk) * bm
    n_blocks = (2 * T) // bm + MOE_N_EXPERTS
    n_active = blk_end[-1]
    bidx = jnp.minimum(jnp.arange(n_blocks, dtype=jnp.int32), n_active - 1)
    blk_eid = jnp.sum(bidx[:, None] >= blk_end[None, :], axis=1).astype(jnp.int32)
    mine = blk_eid[:, None] == jnp.arange(MOE_N_EXPERTS, dtype=jnp.int32)[None, :]
    seg_end = jnp.sum(jnp.where(mine, (pstart + counts)[None, :], 0), axis=1)
    blk_valid = jnp.clip(seg_end - bidx * bm, 0, bm)
    blk_valid = jnp.where(jnp.arange(n_blocks) < n_active, blk_valid, 0).astype(jnp.int32)
    blk_fresh = jnp.concatenate([jnp.ones((1,), jnp.int32), (blk_eid[1:] != blk_eid[:-1]).astype(jnp.int32)])
    ar = jnp.arange(MOE_N_EXPERTS, dtype=jnp.int32)
    later = jnp.min(jnp.where((nblk > 0)[None, :] & (ar[None, :] > ar[:, None]), ar[None, :], MOE_N_EXPERTS), axis=1)
    later = jnp.where(later < MOE_N_EXPERTS, later, -1)
    blk_next = (jnp.sum(jnp.where(mine, later[None, :] + 1, 0), axis=1) - 1).astype(jnp.int32)
    eid = route_t[ROUTE_EID:ROUTE_EID + 2].astype(jnp.int32)
    rank = route_t[ROUTE_RANK:ROUTE_RANK + 2].astype(jnp.int32)
    start = jnp.sum(jnp.where(eid[..., None] == jnp.arange(MOE_N_EXPERTS, dtype=jnp.int32),
                              pstart.astype(jnp.int32), 0), axis=-1)
    slots = (start + rank).reshape(2 * T)

    x_slots = _dispatch(slots, hn, n_blocks * bm, T)
    y_slots = _experts(bidx, blk_eid, blk_valid, blk_fresh, blk_next, x_slots, moe_w1, moe_w3, moe_w2, n_blocks)
    return _combine(slots, y_slots, h_mid, route, next_norm_w[None, :], T)


def kernel(x, norm_mix_w, w_in, b_branch_gate, ret_decay_fwd, ret_decay_bwd, ret_gn_w, w_attn_branch,
           w_ret_branch, w_out, norm_moe_w, moe_w_group, moe_b_group, moe_w_expert, moe_b_expert, moe_w1,
           moe_w3, moe_w2, norm_final_w):
    B, S, D = x.shape
    depth = norm_mix_w.shape[0]
    assert depth == 1, "the final norm is fused into the layer's combine stage"
    assert D == D_MODEL and S % TM_INPROJ == 0 and (B * S) < (1 << 24)
    cos_t, sin_t = _rotary_tables(S)
    out = _layer(x.reshape(B * S, D), norm_mix_w[0], w_in[0], b_branch_gate[0], ret_decay_fwd[0],
                 ret_decay_bwd[0], ret_gn_w[0], w_attn_branch[0], w_ret_branch[0], w_out[0], norm_moe_w[0],
                 moe_w_group[0], moe_b_group[0], moe_w_expert[0], moe_b_expert[0], moe_w1[0], moe_w3[0],
                 moe_w2[0], norm_final_w, B, S, cos_t, sin_t)
    return out.reshape(B, S, D)
```

```python
import functools

import numpy as np
import jax
import jax.numpy as jnp
from jax import lax
from jax.experimental import pallas as pl
from jax.experimental.pallas import tpu as pltpu

F32 = jnp.float32
BF16 = jnp.bfloat16

D_MODEL = 1024
HEAD_DIM = 64
ATTN_PAIRS = ((128, 1), (512, 4), (2048, 16))
ATTN_HEADS_PER_GROUP = 8
ATTN_GROUP_W = ATTN_HEADS_PER_GROUP * HEAD_DIM
ATTN_HALF = 64
ROPE_THETA = 10000.0
RET_HEADS = 8
RET_QK_DIM = 64
RET_V_DIM = 128
RET_CHUNK = 128
RET_QK_W = RET_HEADS * RET_QK_DIM
RET_V_W = RET_HEADS * RET_V_DIM
MOE_GROUPS = 8
MOE_EXPERTS_PER_GROUP = 8
MOE_N_EXPERTS = MOE_GROUPS * MOE_EXPERTS_PER_GROUP
MOE_HIDDEN = 512
NORM_EPS = 1e-6

LANES = 128
COL_CHUNKS = D_MODEL // LANES
ROW_TILES = COL_CHUNKS // 2
U32 = jnp.uint32
BF16_BITS = 16
NEG_BIG = -1e30
LOG2_E = 1.4426950408889634

TM_INPROJ = 512
TM_MERGE = 1024
MERGE_SPLIT = 4
MERGE_COL_CHUNKS = 4
TM_COMBINE = 512
MOE_BM = 512
DISPATCH_CHUNK = 4096
ATTN_QB = 128
ATTN_UNROLL = 16
RET_UNROLL = 16

V7X_VMEM_BYTES = 64 * 1024 * 1024
VMEM_LIMIT = V7X_VMEM_BYTES * 7 // 8

_A = 3 * ATTN_GROUP_W
OFF_QA, OFF_KA, OFF_VA = 0, _A, 2 * _A
OFF_QR = 3 * _A
OFF_KR = OFF_QR + RET_QK_W
OFF_VR = OFF_KR + RET_QK_W
OFF_GR = OFF_VR + RET_V_W
OFF_GL = OFF_GR + RET_V_W
IN_W = OFF_GL + 2 * D_MODEL


def _cparams(sem, vmem=VMEM_LIMIT):
    return pltpu.CompilerParams(dimension_semantics=sem, vmem_limit_bytes=vmem)


def _inproj_body(x_ref, nw_ref, w_ref, bg_ref, cos_ref, sin_ref,
                 qa0, ka0, va0, qa1, ka1, va1, qa2, ka2, va2, qr, kr, vr, gr, gt,
                 stage_ref):
    tm = x_ref.shape[0]
    x = x_ref[...]
    ms = jnp.mean(x * x, axis=-1, keepdims=True)
    xn = (x * lax.rsqrt(ms + NORM_EPS) * nw_ref[...]).astype(BF16)
    cos = cos_ref[...]
    sin = sin_ref[...]
    lane = lax.broadcasted_iota(jnp.int32, (tm, LANES), 1)
    first_half = (lane & (HEAD_DIM - 1)) < (HEAD_DIM // 2)

    def proj(c0, width):
        return jnp.dot(xn, w_ref[:, c0:c0 + width], preferred_element_type=F32)

    def rotary(a, scale):
        partner = jnp.where(first_half, pltpu.roll(a, LANES - HEAD_DIM // 2, 1),
                            pltpu.roll(a, HEAD_DIM // 2, 1))
        r = a * cos + partner * sin
        return r * scale if scale != 1.0 else r

    def chunks(acc):
        return [acc[:, c * LANES:(c + 1) * LANES] for c in range(acc.shape[1] // LANES)]

    def store_natural(out_ref, acc, fn):
        for c, a in enumerate(chunks(acc)):
            out_ref[:, c * LANES:(c + 1) * LANES] = fn(a).astype(out_ref.dtype)

    def store_pairs(out_ref, acc, fn):
        for c, a in enumerate(chunks(acc)):
            out_ref[0, c] = fn(a).astype(out_ref.dtype)

    def store_strided(out_ref, acc, fn, d):
        for c, a in enumerate(chunks(acc)):
            stage_ref[c] = fn(a)
        for c in range(acc.shape[1] // LANES):
            for r in range(d):
                out_ref[0, c, r] = stage_ref[c, pl.ds(r, tm // d, stride=d), :].astype(out_ref.dtype)

    ident = lambda a: a
    rot_q = lambda a: rotary(a, HEAD_DIM ** -0.5 * LOG2_E)
    rot_1 = lambda a: rotary(a, 1.0)
    rot_k = lambda a: rotary(a, RET_QK_DIM ** -0.5)

    W = ATTN_GROUP_W
    store_pairs(qa0, proj(OFF_QA, W), rot_q)
    store_pairs(ka0, proj(OFF_KA, W), rot_1)
    store_pairs(va0, proj(OFF_VA, W), ident)
    for g, (qo, ko, vo) in ((1, (qa1, ka1, va1)), (2, (qa2, ka2, va2))):
        d = ATTN_PAIRS[g][1]
        store_strided(qo, proj(OFF_QA + g * W, W), rot_q, d)
        store_strided(ko, proj(OFF_KA + g * W, W), rot_1, d)
        store_strided(vo, proj(OFF_VA + g * W, W), ident, d)
    store_natural(qr, proj(OFF_QR, RET_QK_W), rot_1)
    store_natural(kr, proj(OFF_KR, RET_QK_W), rot_k)
    for h in range(RET_V_W // W):
        vr[:, h * W:(h + 1) * W] = proj(OFF_VR + h * W, W).astype(vr.dtype)
        gr[:, h * W:(h + 1) * W] = proj(OFF_GR + h * W, W).astype(gr.dtype)
    for h in range(2 * D_MODEL // W):
        z = proj(OFF_GL + h * W, W) + bg_ref[:, h * W:(h + 1) * W]
        gt[:, h * W:(h + 1) * W] = jax.nn.sigmoid(z).astype(gt.dtype)


def _in_projection(x2, norm_w, w_bf, b_gate, cos_t, sin_t, B, S):
    T = B * S
    tm = TM_INPROJ
    nt = S // tm
    W = ATTN_GROUP_W
    row = lambda i: (i, 0)
    const = lambda i: (0, 0)
    nat = lambda width: pl.BlockSpec((tm, width), row)

    P = W // LANES

    def strided_spec(d):
        return pl.BlockSpec((1, P, d, tm // d, LANES), lambda i: (i // nt, 0, 0, i % nt, 0))

    def strided_shape(d):
        return jax.ShapeDtypeStruct((B, P, d, S // d, LANES), BF16)

    pair_spec = pl.BlockSpec((1, P, tm, LANES), lambda i: (i // nt, 0, i % nt, 0))
    pair_shape = jax.ShapeDtypeStruct((B, P, S, LANES), BF16)
    nat_shape = lambda width: jax.ShapeDtypeStruct((T, width), BF16)
    d1, d2 = ATTN_PAIRS[1][1], ATTN_PAIRS[2][1]
    out_shape = ([pair_shape] * 3 + [strided_shape(d1)] * 3 + [strided_shape(d2)] * 3
                 + [nat_shape(RET_QK_W)] * 2 + [nat_shape(RET_V_W)] * 2 + [nat_shape(2 * D_MODEL)])
    out_specs = ([pair_spec] * 3 + [strided_spec(d1)] * 3 + [strided_spec(d2)] * 3
                 + [nat(RET_QK_W)] * 2 + [nat(RET_V_W)] * 2 + [nat(2 * D_MODEL)])
    in_specs = [
        pl.BlockSpec((tm, D_MODEL), row),
        pl.BlockSpec((1, D_MODEL), const),
        pl.BlockSpec((D_MODEL, IN_W), const, pipeline_mode=pl.Buffered(1)),
        pl.BlockSpec((1, 2 * D_MODEL), const),
        pl.BlockSpec((tm, LANES), lambda i: (i % nt, 0)),
        pl.BlockSpec((tm, LANES), lambda i: (i % nt, 0)),
    ]
    return pl.pallas_call(
        _inproj_body,
        grid=(T // tm,),
        in_specs=in_specs,
        out_specs=out_specs,
        out_shape=out_shape,
        scratch_shapes=[pltpu.VMEM((W // LANES, tm, LANES), F32)],
        compiler_params=_cparams(("parallel",)),
        name="in_projection",
    )(x2, norm_w, w_bf, b_gate, cos_t, sin_t)


def _attn_body(q0_ref, k0_ref, v0_ref, q1_ref, k1_ref, v1_ref, q2_ref, k2_ref, v2_ref, o_ref,
               part_ref, bias_ref, *, S):
    QB, H = ATTN_QB, ATTN_HALF
    lane = lax.broadcasted_iota(jnp.int32, (QB, LANES), 1)
    head0 = lane < HEAD_DIM

    qi = lax.broadcasted_iota(jnp.int32, (QB, QB + 2 * H), 0)
    ki = lax.broadcasted_iota(jnp.int32, (QB, QB + 2 * H), 1)
    for n in range(3):
        bias_ref[n] = jnp.where(jnp.abs(ki - qi - n * H) <= H, 0.0, NEG_BIG).astype(F32)

    def scores(q_rows, k_rows, bias, h):
        qm = jnp.where(head0 if h == 0 else jnp.logical_not(head0), q_rows, jnp.zeros_like(q_rows))
        return lax.dot_general(qm, k_rows, (((1,), (1,)), ((), ())), preferred_element_type=F32) + bias

    def weights(s):
        m = jnp.max(s, axis=-1, keepdims=True)
        return m, jnp.exp2(s - m).astype(BF16)

    def heads_to_lanes(m0, a, m1, b):
        num = jnp.where(head0, a, b)
        den = pltpu.roll(jnp.where(head0, b, a), HEAD_DIM, 1)
        mx = jnp.where(head0, m0, m1)
        return num, den, mx

    def run_group(q_ref, k_ref, v_ref, d, prepare, store):
        L = S // d
        KW = min(L, QB + 2 * H)
        nb = L // QB

        key_head0 = lax.broadcasted_iota(jnp.int32, (KW, LANES), 1) < HEAD_DIM
        key_ones = jnp.ones((KW, LANES), BF16)

        def trip(i, carry):
            blocks = []
            for u in range(ATTN_UNROLL):
                t = i * ATTN_UNROLL + u
                r = t // nb
                q0 = pl.multiple_of((t % nb) * QB, QB)
                ws = pl.multiple_of(jnp.clip(q0 - H, 0, L - KW), H)
                bias = bias_ref[(q0 - ws) // H][:, :KW]
                q_rows = q_ref[0, 0, r, pl.ds(q0, QB), :]
                k_rows = k_ref[0, 0, r, pl.ds(ws, KW), :]
                blocks.append((r, q0, ws, [scores(q_rows, k_rows, bias, h) for h in range(2)]))
            blocks = [(r, q0, ws, [weights(s) for s in ss]) for r, q0, ws, ss in blocks]
            done = []
            for r, q0, ws, ((m0, p0), (m1, p1)) in blocks:
                v_rows = v_ref[0, 0, r, pl.ds(ws, KW), :]
                a = jnp.dot(p0, jnp.where(key_head0, v_rows, key_ones), preferred_element_type=F32)
                b = jnp.dot(p1, jnp.where(key_head0, key_ones, v_rows), preferred_element_type=F32)
                done.append((r, q0, m0, a, m1, b))
            done = [(r, q0, prepare(q0, *heads_to_lanes(m0, a, m1, b))) for r, q0, m0, a, m1, b in done]
            for r, q0, vals in done:
                store(r, q0, vals)
            return carry

        lax.fori_loop(0, S // QB // ATTN_UNROLL, trip, 0)

    def store_partial(g):
        d = ATTN_PAIRS[g][1]

        def store(r, q0, vals):
            rows = pl.ds(r + q0 * d, QB, stride=d)
            for n, val in enumerate(vals):
                part_ref[3 * (g - 1) + n, rows, :] = val
        return store

    keep = lambda q0, num, den, mx: (num, den, mx)
    run_group(q1_ref, k1_ref, v1_ref, ATTN_PAIRS[1][1], keep, store_partial(1))
    run_group(q2_ref, k2_ref, v2_ref, ATTN_PAIRS[2][1], keep, store_partial(2))

    def merge(q0, num, den, mx):
        rows = pl.ds(q0, QB)
        nums = [num, part_ref[0, rows, :], part_ref[3, rows, :]]
        dens = [den, part_ref[1, rows, :], part_ref[4, rows, :]]
        mxs = [mx, part_ref[2, rows, :], part_ref[5, rows, :]]
        top = jnp.maximum(jnp.maximum(mxs[0], mxs[1]), mxs[2])
        ws = [jnp.exp2(m - top) for m in mxs]
        n = ws[0] * nums[0] + ws[1] * nums[1] + ws[2] * nums[2]
        dn = ws[0] * dens[0] + ws[1] * dens[1] + ws[2] * dens[2]
        return (n / dn).astype(o_ref.dtype)

    def store_out(r, q0, y):
        o_ref[0, 0, pl.ds(q0, QB), :] = y

    run_group(q0_ref, k0_ref, v0_ref, ATTN_PAIRS[0][1], merge, store_out)


def _attention(qkv, B, S):
    P = ATTN_GROUP_W // LANES
    in_specs = []
    for _, d in ATTN_PAIRS:
        in_specs += [pl.BlockSpec((1, 1, d, S // d, LANES), lambda b, p: (b, p, 0, 0, 0))] * 3
    return pl.pallas_call(
        functools.partial(_attn_body, S=S),
        grid=(B, P),
        in_specs=in_specs,
        out_specs=pl.BlockSpec((1, 1, S, LANES), lambda b, p: (b, p, 0, 0)),
        out_shape=jax.ShapeDtypeStruct((B, P, S, LANES), BF16),
        scratch_shapes=[pltpu.VMEM((6, S, LANES), F32),
                        pltpu.VMEM((3, ATTN_QB, ATTN_QB + 2 * ATTN_HALF), F32)],
        compiler_params=_cparams(("parallel", "parallel")),
        name="attention",
    )(*qkv)


def _log_sigmoid(z):
    return jnp.minimum(z, 0.0) - jnp.log(1.0 + jnp.exp(-jnp.abs(z)))


def _ret_body(dec_ref, q_ref, k_ref, v_ref, g_ref, gnw_ref, o_ref, kt_ref, sf_ref, sb_ref, st_ref, *, S):
    C = RET_CHUNK
    nc = S // C
    lg = _log_sigmoid(dec_ref[0])
    a_row = lax.broadcasted_iota(jnp.int32, (C, LANES), 0).astype(F32)
    lane = lax.broadcasted_iota(jnp.int32, (C, LANES), 1)
    rel = (lax.broadcasted_iota(jnp.int32, (C, C), 0) - lax.broadcasted_iota(jnp.int32, (C, C), 1)).astype(F32)

    heads = []
    for h in range(2):
        lgf = lg[h:h + 1, :]
        lgb = lg[2 + h:3 + h, :]
        in_head = (lane < RET_QK_DIM) if h == 0 else (lane >= RET_QK_DIM)
        heads.append(dict(
            in_head=in_head,
            xi_f=jnp.where(in_head, jnp.exp((a_row + 1.0) * lgf), 0.0),
            xi_b=jnp.where(in_head, jnp.exp((C - a_row) * lgb), 0.0),
            zeta_f=jnp.exp((C - 1.0 - a_row) * lgf),
            zeta_b=jnp.exp(a_row * lgb),
            dloc=jnp.where(rel > 0, jnp.exp(rel * lgf), jnp.where(rel < 0, jnp.exp(-rel * lgb), 2.0)),
            cd_f=jnp.exp(C * lgf),
            cd_b=jnp.exp(C * lgb),
        ))

    def chunk_rows(n):
        return pl.ds(pl.multiple_of(n * C, C), C)

    def v_head(rows, h):
        return v_ref[0, rows, h * RET_V_DIM:(h + 1) * RET_V_DIM]

    def products(i, carry):
        items = []
        for u in range(RET_UNROLL):
            n = i * RET_UNROLL + u
            rows = chunk_rows(n)
            kt = k_ref[0, rows, :].astype(F32).T.astype(BF16)
            kt_ref[n] = kt
            for h, hd in enumerate(heads):
                vh = v_head(rows, h).astype(F32)
                items.append((n, h, kt, (vh * hd["zeta_f"]).astype(BF16), (vh * hd["zeta_b"]).astype(BF16)))
        outs = [(n, h, jnp.dot(kt, vf, preferred_element_type=F32), jnp.dot(kt, vb, preferred_element_type=F32))
                for n, h, kt, vf, vb in items]
        for n, h, f, b in outs:
            sf_ref[n, h] = f
            sb_ref[n, h] = b
        return carry

    lax.fori_loop(0, nc // RET_UNROLL, products, 0)

    def scan(ref, key, order, half):
        rows = slice(half * 2 * RET_QK_DIM, (half + 1) * 2 * RET_QK_DIM)

        def step(i, state):
            n = order(i)
            new = []
            for h, hd in enumerate(heads):
                st_ref[n, h, rows, :] = state[h].astype(BF16)
                new.append(hd[key] * state[h] + ref[n, h])
            return tuple(new)
        zero = jnp.zeros(ref.shape[2:], F32)
        lax.fori_loop(0, nc, step, (zero, zero))

    scan(sf_ref, "cd_f", lambda i: i, 0)
    scan(sb_ref, "cd_b", lambda i: nc - 1 - i, 1)

    def outputs(i, carry):
        items = []
        for u in range(RET_UNROLL):
            n = i * RET_UNROLL + u
            rows = chunk_rows(n)
            qp = q_ref[0, rows, :]
            qf = qp.astype(F32)
            kt = kt_ref[n]
            for h, hd in enumerate(heads):
                qm = jnp.where(hd["in_head"], qp, jnp.zeros_like(qp))
                qx = jnp.concatenate([(qf * hd["xi_f"]).astype(BF16), (qf * hd["xi_b"]).astype(BF16)], axis=1)
                items.append((rows, h, hd, jnp.dot(qm, kt, preferred_element_type=F32),
                              jnp.dot(qx, st_ref[n, h], preferred_element_type=F32)))
        items = [(rows, h, hd, (s * hd["dloc"]).astype(BF16), cross) for rows, h, hd, s, cross in items]
        items = [(rows, h, cross + jnp.dot(p, v_head(rows, h), preferred_element_type=F32))
                 for rows, h, hd, p, cross in items]
        for rows, h, ret in items:
            mu = jnp.mean(ret, axis=-1, keepdims=True)
            xc = ret - mu
            var = jnp.mean(xc * xc, axis=-1, keepdims=True)
            cols = slice(h * RET_V_DIM, (h + 1) * RET_V_DIM)
            gate = g_ref[0, rows, cols].astype(F32)
            y = xc * lax.rsqrt(var + NORM_EPS) * gnw_ref[:, cols] * (gate * jax.nn.sigmoid(gate))
            o_ref[0, rows, cols] = y.astype(o_ref.dtype)
        return carry

    lax.fori_loop(0, nc // RET_UNROLL, outputs, 0)


def _retention(dec, qr, kr, vr, gr, gn_w, B, S):
    nc = S // RET_CHUNK
    npairs = RET_HEADS // 2
    qk_spec = pl.BlockSpec((1, S, 2 * RET_QK_DIM), lambda b, p: (b, 0, p))
    v_spec = pl.BlockSpec((1, S, 2 * RET_V_DIM), lambda b, p: (b, 0, p))
    return pl.pallas_call(
        functools.partial(_ret_body, S=S),
        grid=(B, npairs),
        in_specs=[pl.BlockSpec((1, 4, LANES), lambda b, p: (p, 0, 0)),
                  qk_spec, qk_spec, v_spec, v_spec,
                  pl.BlockSpec((1, 2 * RET_V_DIM), lambda b, p: (0, p))],
        out_specs=v_spec,
        out_shape=jax.ShapeDtypeStruct((B, S, RET_V_W), BF16),
        scratch_shapes=[pltpu.VMEM((nc, 2 * RET_QK_DIM, RET_CHUNK), BF16),
                        pltpu.VMEM((nc, 2, 2 * RET_QK_DIM, RET_V_DIM), F32),
                        pltpu.VMEM((nc, 2, 2 * RET_QK_DIM, RET_V_DIM), F32),
                        pltpu.VMEM((nc, 2, 4 * RET_QK_DIM, RET_V_DIM), BF16)],
        compiler_params=_cparams(("parallel", "parallel")),
        name="retention",
    )(dec, qr.reshape(B, S, RET_QK_W), kr.reshape(B, S, RET_QK_W),
      vr.reshape(B, S, RET_V_W), gr.reshape(B, S, RET_V_W), gn_w)


ROUTE_EID, ROUTE_RANK, ROUTE_GATE = 0, 2, 4
ROUTE_FIELDS = 8
ROUTER_EXPERT_LANE0 = MOE_GROUPS
GROUP_SHIFT = MOE_EXPERTS_PER_GROUP.bit_length() - 1


def _merge_body(yatt_ref, yret_ref, gt_ref, x_ref, wa_ref, wb_ref, wo_ref,
                nw_ref, wrh_ref, wrl_ref, br_ref,
                h_ref, hn_ref, route_ref, route_t_ref, cnt_ref):
    tm = x_ref.shape[0]
    hm = tm // MERGE_SPLIT
    i = pl.program_id(0)
    cw = D_MODEL // MERGE_COL_CHUNKS

    def branch_products(rows):
        y_att = jnp.concatenate([yatt_ref[0, c, rows, :] for c in range(yatt_ref.shape[1])], axis=1)
        y_ret = yret_ref[rows, :]
        chunks = []
        for c in range(MERGE_COL_CHUNKS):
            cols = slice(c * cw, (c + 1) * cw)
            a = jnp.dot(y_att, wa_ref[:, cols], preferred_element_type=F32)
            b = jnp.dot(y_ret, wb_ref[:, cols], preferred_element_type=F32)
            g_att = gt_ref[rows, c * cw:(c + 1) * cw].astype(F32)
            g_ret = gt_ref[rows, D_MODEL + c * cw:D_MODEL + (c + 1) * cw].astype(F32)
            chunks.append((g_att * a + g_ret * b).astype(BF16))
        return jnp.concatenate(chunks, axis=1)

    def residual_norm(hf, rows, merged):
        mix = jnp.dot(merged, wo_ref[...], preferred_element_type=F32)
        h = x_ref[rows, :] + mix
        h_ref[rows, :] = h
        ms = jnp.mean(h * h, axis=-1, keepdims=True)
        hn = h * lax.rsqrt(ms + NORM_EPS) * nw_ref[...]
        for j, word in enumerate(_pack_row(hn)):
            hn_ref[pl.ds(hf * hm * ROW_TILES + j, hm, stride=ROW_TILES), :] = word
        hi = hn.astype(BF16)
        lo = (hn - hi.astype(F32)).astype(BF16)
        return hi, lo

    def router_logits(hi, lo):
        return (jnp.dot(hi, wrh_ref[...], preferred_element_type=F32)
                + jnp.dot(hi, wrl_ref[...], preferred_element_type=F32)
                + jnp.dot(lo, wrh_ref[...], preferred_element_type=F32)) + br_ref[...]

    lane = lax.broadcasted_iota(jnp.int32, (hm, LANES), 1)
    far = jnp.int32(LANES)

    def first_argmax(vals, vmax):
        return jnp.min(jnp.where(vals == vmax, lane, far), axis=-1, keepdims=True)

    def route(logits):
        is_group = lane < MOE_GROUPS
        gl = jnp.where(is_group, logits, NEG_BIG)
        gmax = jnp.max(gl, axis=-1, keepdims=True)
        g_w = 1.0 / jnp.sum(jnp.where(is_group, jnp.exp(gl - gmax), 0.0), axis=-1, keepdims=True)
        g_idx = first_argmax(gl, gmax)
        e_lane = lane - ROUTER_EXPERT_LANE0
        in_group = (e_lane >= 0) & (e_lane < MOE_N_EXPERTS) & (jnp.right_shift(e_lane, GROUP_SHIFT) == g_idx)
        el = jnp.where(in_group, logits, NEG_BIG)
        m1 = jnp.max(el, axis=-1, keepdims=True)
        i1 = first_argmax(el, m1)
        el2 = jnp.where(lane == i1, NEG_BIG, el)
        m2 = jnp.max(el2, axis=-1, keepdims=True)
        i2 = first_argmax(el2, m2)
        ex = jnp.exp(m2 - m1)
        return i1, i2, g_w / (1.0 + ex), g_w * ex / (1.0 + ex)

    rows = [slice(hf * hm, (hf + 1) * hm) for hf in range(MERGE_SPLIT)]
    merged = [branch_products(r) for r in rows]
    split = [residual_norm(hf, r, m) for hf, (r, m) in enumerate(zip(rows, merged))]
    routed = [route(router_logits(hi, lo)) for hi, lo in split]

    @pl.when(i == 0)
    def _():
        cnt_ref[...] = jnp.zeros(cnt_ref.shape, F32)

    r_idx = lax.broadcasted_iota(jnp.int32, (hm, hm), 0)
    c_idx = lax.broadcasted_iota(jnp.int32, (hm, hm), 1)
    lower = jnp.where(c_idx < r_idx, 1.0, 0.0).astype(BF16)
    running = cnt_ref[...]
    for hf, (i1, i2, gate1, gate2) in enumerate(routed):
        hot1 = lane == i1
        hot2 = lane == i2
        onehot = jnp.where(hot1 | hot2, 1.0, 0.0)
        before = jnp.dot(lower, onehot.astype(BF16), preferred_element_type=F32) + running
        rank1 = jnp.sum(jnp.where(hot1, before, 0.0), axis=-1, keepdims=True)
        rank2 = jnp.sum(jnp.where(hot2, before, 0.0), axis=-1, keepdims=True)
        running = running + jnp.sum(onehot, axis=0, keepdims=True)
        rec = jnp.zeros((hm, LANES), F32)
        for pos, val in ((ROUTE_EID, (i1 - ROUTER_EXPERT_LANE0).astype(F32)),
                         (ROUTE_EID + 1, (i2 - ROUTER_EXPERT_LANE0).astype(F32)),
                         (ROUTE_RANK, rank1), (ROUTE_RANK + 1, rank2),
                         (ROUTE_GATE, gate1), (ROUTE_GATE + 1, gate2)):
            rec = jnp.where(lane == pos, val, rec)
        route_ref[rows[hf], :] = rec
        route_t_ref[:, hf * hm:(hf + 1) * hm] = rec.T[:route_t_ref.shape[0], :]
    cnt_ref[...] = running


def _merge_route(y_att, y_ret, gates, x2, wa, wb, wo, nw, wr_hi, wr_lo, b_r, T, S):
    tm = TM_MERGE
    nt = S // tm
    row = lambda i: (i, 0)
    const = lambda i: (0, 0)
    full = lambda arr: pl.BlockSpec(arr.shape, const)
    in_specs = ([pl.BlockSpec((1, y_att.shape[1], tm, LANES), lambda i: (i // nt, 0, i % nt, 0)),
                 pl.BlockSpec((tm, RET_V_W), row), pl.BlockSpec((tm, 2 * D_MODEL), row),
                 pl.BlockSpec((tm, D_MODEL), row),
                 full(wa), full(wb), full(wo), full(nw), full(wr_hi), full(wr_lo), full(b_r)])
    return pl.pallas_call(
        _merge_body,
        grid=(T // tm,),
        in_specs=in_specs,
        out_specs=[pl.BlockSpec((tm, D_MODEL), row),
                   pl.BlockSpec((tm * ROW_TILES, LANES), row),
                   pl.BlockSpec((tm, LANES), row), pl.BlockSpec((ROUTE_FIELDS, tm), lambda i: (0, i)),
                   pl.BlockSpec((1, LANES), const)],
        out_shape=[jax.ShapeDtypeStruct((T, D_MODEL), F32), jax.ShapeDtypeStruct((T * ROW_TILES, LANES), U32),
                   jax.ShapeDtypeStruct((T, LANES), F32), jax.ShapeDtypeStruct((ROUTE_FIELDS, T), F32),
                   jax.ShapeDtypeStruct((1, LANES), F32)],
        compiler_params=_cparams(("arbitrary",)),
        name="merge_route",
    )(y_att, y_ret, gates, x2, wa, wb, wo, nw, wr_hi, wr_lo, b_r)


ISSUE_UNROLL = 8


def _tile_rows(n):
    return pl.ds(pl.multiple_of(n * ROW_TILES, ROW_TILES), ROW_TILES)


def _pack_row(x):
    bits = lambda c: pltpu.bitcast(x[:, c * LANES:(c + 1) * LANES].astype(BF16).astype(F32), U32)
    return [(bits(j) >> BF16_BITS) | bits(j + ROW_TILES) for j in range(ROW_TILES)]


def _unpack_word(w):
    high = jnp.uint32(((1 << BF16_BITS) - 1) << BF16_BITS)
    return pltpu.bitcast(w << BF16_BITS, F32), pltpu.bitcast(w & high, F32)


def _dispatch_body(slot_ref, hn_ref, xs_ref, sem, *, T):
    i = pl.program_id(0)
    ch = hn_ref.shape[0] // ROW_TILES

    def row_copy(j, slot):
        return pltpu.make_async_copy(hn_ref.at[_tile_rows(j)], xs_ref.at[_tile_rows(slot)], sem)

    def issue(j, carry):
        t = i * ch + j
        row_copy(j, slot_ref[t]).start(priority=0)
        row_copy(j, slot_ref[T + t]).start(priority=1)
        return carry

    lax.fori_loop(0, ch, issue, 0, unroll=ISSUE_UNROLL)
    for _ in range(2):
        pltpu.make_async_copy(hn_ref, xs_ref.at[pl.ds(0, ch * ROW_TILES)], sem).wait()


def _dispatch(slots, hn, n_slots, T):
    ch = DISPATCH_CHUNK
    grid_spec = pltpu.PrefetchScalarGridSpec(
        num_scalar_prefetch=1,
        grid=(T // ch,),
        in_specs=[pl.BlockSpec((ch * ROW_TILES, LANES), lambda i, s: (i, 0))],
        out_specs=pl.BlockSpec(memory_space=pl.ANY),
        scratch_shapes=[pltpu.SemaphoreType.DMA(())],
    )
    return pl.pallas_call(
        functools.partial(_dispatch_body, T=T),
        grid_spec=grid_spec,
        out_shape=jax.ShapeDtypeStruct((n_slots * ROW_TILES, LANES), U32),
        compiler_params=_cparams(("arbitrary",)),
        name="moe_dispatch",
    )(slots, hn)


def _expert_body(blk_ref, eid_ref, valid_ref, fresh_ref, next_ref, x_ref, w1_ref, w3_ref, w2_ref, y_ref,
                 w1b, w3b, w2b, w1s, w3s, w2s, wsem):
    i = pl.program_id(0)
    valid = valid_ref[i]

    def weight_copies(e):
        return (pltpu.make_async_copy(w1_ref.at[e], w1s, wsem.at[0]),
                pltpu.make_async_copy(w3_ref.at[e], w3s, wsem.at[1]),
                pltpu.make_async_copy(w2_ref.at[e], w2s, wsem.at[2]))

    @pl.when(i == 0)
    def _():
        for cp in weight_copies(eid_ref[0]):
            cp.start()

    @pl.when(valid > 0)
    def _():
        @pl.when(fresh_ref[i] == 1)
        def _():
            for cp in weight_copies(eid_ref[i]):
                cp.wait()
            w1b[...] = w1s[...].astype(BF16)
            w3b[...] = w3s[...].astype(BF16)
            w2b[...] = w2s[...].astype(BF16)

            @pl.when(next_ref[i] >= 0)
            def _():
                for cp in weight_copies(next_ref[i]):
                    cp.start()

        bm = x_ref.shape[0] // ROW_TILES

        def swiglu(nrows):
            live = lax.broadcasted_iota(jnp.int32, (nrows, LANES), 0) < valid
            halves = [_unpack_word(x_ref[pl.ds(j, nrows, stride=ROW_TILES), :]) for j in range(ROW_TILES)]
            x = jnp.concatenate(
                [jnp.where(live, c, 0.0).astype(BF16) for c in [lo for lo, _ in halves] + [hi for _, hi in halves]],
                axis=1)
            a = jnp.dot(x, w1b[...], preferred_element_type=F32)
            b = jnp.dot(x, w3b[...], preferred_element_type=F32)
            hid = (a * jax.nn.sigmoid(a) * b).astype(BF16)
            y = jnp.dot(hid, w2b[...], preferred_element_type=F32)
            for j, word in enumerate(_pack_row(y)):
                y_ref[pl.ds(j, nrows, stride=ROW_TILES), :] = word

        @pl.when(valid > bm // 2)
        def _():
            swiglu(bm)

        @pl.when(valid <= bm // 2)
        def _():
            swiglu(bm // 2)


def _experts(blk, blk_eid, blk_valid, blk_fresh, blk_next, x_slots, w1, w3, w2, n_blocks):
    bm = MOE_BM
    slot_block = lambda i, blk, eid, val, fr, nx: (blk[i], 0)
    grid_spec = pltpu.PrefetchScalarGridSpec(
        num_scalar_prefetch=5,
        grid=(n_blocks,),
        in_specs=[pl.BlockSpec((bm * ROW_TILES, LANES), slot_block),
                  pl.BlockSpec(memory_space=pl.ANY), pl.BlockSpec(memory_space=pl.ANY),
                  pl.BlockSpec(memory_space=pl.ANY)],
        out_specs=pl.BlockSpec((bm * ROW_TILES, LANES), slot_block),
        scratch_shapes=[pltpu.VMEM((D_MODEL, MOE_HIDDEN), BF16), pltpu.VMEM((D_MODEL, MOE_HIDDEN), BF16),
                        pltpu.VMEM((MOE_HIDDEN, D_MODEL), BF16),
                        pltpu.VMEM((D_MODEL, MOE_HIDDEN), F32), pltpu.VMEM((D_MODEL, MOE_HIDDEN), F32),
                        pltpu.VMEM((MOE_HIDDEN, D_MODEL), F32), pltpu.SemaphoreType.DMA((3,))],
    )
    return pl.pallas_call(
        _expert_body,
        grid_spec=grid_spec,
        out_shape=jax.ShapeDtypeStruct(x_slots.shape, U32),
        compiler_params=_cparams(("arbitrary",)),
        name="moe_experts",
    )(blk, blk_eid, blk_valid, blk_fresh, blk_next, x_slots, w1, w3, w2)


def _combine_body(slot_ref, ys_ref, h_ref, route_ref, nw_ref, o_ref, ybuf, sem, *, T):
    i = pl.program_id(0)
    n = pl.num_programs(0)
    tm = h_ref.shape[0]

    def row_copy(slot, buf, k, j):
        return pltpu.make_async_copy(ys_ref.at[_tile_rows(slot)], ybuf.at[buf, k, _tile_rows(j)], sem.at[buf])

    def issue(tile, buf):
        def one(j, carry):
            t = tile * tm + j
            row_copy(slot_ref[t], buf, 0, j).start(priority=0)
            row_copy(slot_ref[T + t], buf, 1, j).start(priority=1)
            return carry
        lax.fori_loop(0, tm, one, 0, unroll=ISSUE_UNROLL)

    @pl.when(i == 0)
    def _():
        issue(0, 0)

    @pl.when(i + 1 < n)
    def _():
        issue(i + 1, (i + 1) % 2)

    buf = i % 2
    for k in range(2):
        pltpu.make_async_copy(ys_ref.at[pl.ds(0, tm * ROW_TILES)], ybuf.at[buf, k], sem.at[buf]).wait()
    route = route_ref[...]
    g1 = route[:, ROUTE_GATE:ROUTE_GATE + 1]
    g2 = route[:, ROUTE_GATE + 1:ROUTE_GATE + 2]
    hs = [None] * COL_CHUNKS
    ss = jnp.zeros((tm, 1), F32)
    for j in range(ROW_TILES):
        tile_row = pl.ds(j, tm, stride=ROW_TILES)
        first = _unpack_word(ybuf[buf, 0, tile_row, :])
        second = _unpack_word(ybuf[buf, 1, tile_row, :])
        for c, y1, y2 in ((j, first[0], second[0]), (j + ROW_TILES, first[1], second[1])):
            hc = h_ref[:, c * LANES:(c + 1) * LANES] + (y1 * g1 + y2 * g2)
            hs[c] = hc
            ss = ss + jnp.sum(hc * hc, axis=-1, keepdims=True)
    inv = lax.rsqrt(ss * (1.0 / D_MODEL) + NORM_EPS)
    for j, hj in enumerate(hs):
        cols = slice(j * LANES, (j + 1) * LANES)
        o_ref[:, cols] = hj * inv * nw_ref[:, cols]


def _combine(slots, y_slots, h, route, nw, T):
    tm = TM_COMBINE
    row = lambda i, s: (i, 0)
    grid_spec = pltpu.PrefetchScalarGridSpec(
        num_scalar_prefetch=1,
        grid=(T // tm,),
        in_specs=[pl.BlockSpec(memory_space=pl.ANY),
                  pl.BlockSpec((tm, D_MODEL), row),
                  pl.BlockSpec((tm, LANES), row),
                  pl.BlockSpec((1, D_MODEL), lambda i, s: (0, 0))],
        out_specs=pl.BlockSpec((tm, D_MODEL), row),
        scratch_shapes=[pltpu.VMEM((2, 2, tm * ROW_TILES, LANES), U32), pltpu.SemaphoreType.DMA((2,))],
    )
    return pl.pallas_call(
        functools.partial(_combine_body, T=T),
        grid_spec=grid_spec,
        out_shape=jax.ShapeDtypeStruct((T, D_MODEL), F32),
        compiler_params=_cparams(("arbitrary",)),
        name="moe_combine",
    )(slots, y_slots, h, route, nw)


def _rotary_tables(S):
    inv_freq = (1.0 / (np.float32(ROPE_THETA) ** (np.arange(0, HEAD_DIM, 2, dtype=np.float32) / HEAD_DIM))
                ).astype(np.float32)
    ang = np.arange(S, dtype=np.float32)[:, None] * inv_freq[None, :]
    cos, sin = np.cos(ang), np.sin(ang)
    reps = LANES // HEAD_DIM
    cos_t = np.tile(np.concatenate([cos, cos], axis=1), (1, reps)).astype(np.float32)
    sin_t = np.tile(np.concatenate([-sin, sin], axis=1), (1, reps)).astype(np.float32)
    return jnp.asarray(cos_t), jnp.asarray(sin_t)


def _layer(h_in, norm_mix_w, w_in, b_branch_gate, ret_decay_fwd, ret_decay_bwd, ret_gn_w, w_attn_branch,
           w_ret_branch, w_out, norm_moe_w, moe_w_group, moe_b_group, moe_w_expert, moe_b_expert,
           moe_w1, moe_w3, moe_w2, next_norm_w, B, S, cos_t, sin_t):
    T = B * S
    (qa0, ka0, va0, qa1, ka1, va1, qa2, ka2, va2, qr, kr, vr, gr, gates) = _in_projection(
        h_in, norm_mix_w[None, :], w_in.astype(BF16), b_branch_gate[None, :], cos_t, sin_t, B, S)

    unit = lambda a: a[:, :, None]
    y_att = _attention((unit(qa0), unit(ka0), unit(va0), qa1, ka1, va1, qa2, ka2, va2), B, S)

    dec = jnp.stack([ret_decay_fwd.reshape(RET_HEADS // 2, 2), ret_decay_bwd.reshape(RET_HEADS // 2, 2)], axis=1)
    dec = jnp.broadcast_to(dec.reshape(RET_HEADS // 2, 4, 1), (RET_HEADS // 2, 4, LANES)).astype(F32)
    y_ret = _retention(dec, qr, kr, vr, gr, ret_gn_w[None, :], B, S).reshape(T, RET_V_W)

    pad = LANES - MOE_GROUPS - MOE_N_EXPERTS
    w_r = jnp.concatenate([moe_w_group, moe_w_expert, jnp.zeros((D_MODEL, pad), F32)], axis=1)
    w_r_hi = w_r.astype(BF16)
    w_r_lo = (w_r - w_r_hi.astype(F32)).astype(BF16)
    b_r = jnp.concatenate([moe_b_group, moe_b_expert, jnp.zeros((pad,), F32)])[None, :]

    h_mid, hn, route, route_t, cnt = _merge_route(
        y_att, y_ret, gates, h_in, w_attn_branch.astype(BF16), w_ret_branch.astype(BF16),
        w_out.astype(BF16), norm_moe_w[None, :], w_r_hi, w_r_lo, b_r, T, S)

    bm = MOE_BM
    counts = cnt[0, ROUTER_EXPERT_LANE0:ROUTER_EXPERT_LANE0 + MOE_N_EXPERTS].astype(jnp.int32)
    nblk = (counts + bm - 1) // bm
    blk_end = jnp.cumsum(nblk)
    pstart = (blk_end - nblk) * bm
    n_blocks = (2 * T) // bm + MOE_N_EXPERTS
    n_active = blk_end[-1]
    bidx = jnp.minimum(jnp.arange(n_blocks, dtype=jnp.int32), n_active - 1)
    blk_eid = jnp.sum(bidx[:, None] >= blk_end[None, :], axis=1).astype(jnp.int32)
    mine = blk_eid[:, None] == jnp.arange(MOE_N_EXPERTS, dtype=jnp.int32)[None, :]
    seg_end = jnp.sum(jnp.where(mine, (pstart + counts)[None, :], 0), axis=1)
    blk_valid = jnp.clip(seg_end - bidx * bm, 0, bm)
    blk_valid = jnp.where(jnp.arange(n_blocks) < n_active, blk_valid, 0).astype(jnp.int32)
    blk_fresh = jnp.concatenate([jnp.ones((1,), jnp.int32), (blk_eid[1:] != blk_eid[:-1]).astype(jnp.int32)])
    ar = jnp.arange(MOE_N_EXPERTS, dtype=jnp.int32)
    later = jnp.min(jnp.where((nblk > 0)[None, :] & (ar[None, :] > ar[:, None]), ar[None, :], MOE_N_EXPERTS), axis=1)
    later = jnp.where(later < MOE_N_EXPERTS, later, -1)
    blk_next = (jnp.sum(jnp.where(mine, later[None, :] + 1, 0), axis=1) - 1).astype(jnp.int32)
    eid = route_t[ROUTE_EID:ROUTE_EID + 2].astype(jnp.int32)
    rank = route_t[ROUTE_RANK:ROUTE_RANK + 2].astype(jnp.int32)
    start = jnp.sum(jnp.where(eid[..., None] == jnp.arange(MOE_N_EXPERTS, dtype=jnp.int32),
                              pstart.astype(jnp.int32), 0), axis=-1)
    slots = (start + rank).reshape(2 * T)

    x_slots = _dispatch(slots, hn, n_blocks * bm, T)
    y_slots = _experts(bidx, blk_eid, blk_valid, blk_fresh, blk_next, x_slots, moe_w1, moe_w3, moe_w2, n_blocks)
    return _combine(slots, y_slots, h_mid, route, next_norm_w[None, :], T)


def kernel(x, norm_mix_w, w_in, b_branch_gate, ret_decay_fwd, ret_decay_bwd, ret_gn_w, w_attn_branch,
           w_ret_branch, w_out, norm_moe_w, moe_w_group, moe_b_group, moe_w_expert, moe_b_expert, moe_w1,
           moe_w3, moe_w2, norm_final_w):
    B, S, D = x.shape
    depth = norm_mix_w.shape[0]
    assert depth == 1, "the final norm is fused into the layer's combine stage"
    assert D == D_MODEL and S % TM_INPROJ == 0 and (B * S) < (1 << 24)
    cos_t, sin_t = _rotary_tables(S)
    out = _layer(x.reshape(B * S, D), norm_mix_w[0], w_in[0], b_branch_gate[0], ret_decay_fwd[0],
                 ret_decay_bwd[0], ret_gn_w[0], w_attn_branch[0], w_ret_branch[0], w_out[0], norm_moe_w[0],
                 moe_w_group[0], moe_b_group[0], moe_w_expert[0], moe_b_expert[0], moe_w1[0], moe_w3[0],
                 moe_w2[0], norm_final_w, B, S, cos_t, sin_t)
    return out.reshape(B, S, D)
```
